```python
import math
import jax, jax.numpy as jnp
from jax import lax
import numpy as np

D_MODEL = 1024
BATCH = 8
SEQ = 8192
DEPTH = 4

HEAD_DIM = 64
A_HEADS = 4
A_CONFIGS = ((128, 1), (512, 4), (2048, 16))
A_ROPE_DIMS = HEAD_DIM // 4
ROPE_THETA = 500000.0
B_Q_HEADS = 8
B_KV_HEADS = 2
B_AXIAL_THETA = 10000.0
C_HEADS = 4
C_ROWS_MAX = 8
C_COLS = 16
GRID_W = 64
Q_BLOCK = 128
D_FF = 4 * D_MODEL
A_WIDTH = A_HEADS * HEAD_DIM
B_Q_WIDTH = B_Q_HEADS * HEAD_DIM
B_KV_WIDTH = B_KV_HEADS * HEAD_DIM
C_WIDTH = C_HEADS * HEAD_DIM
N_BRANCHES = 3
QKV_COLS = 3 * A_WIDTH + B_Q_WIDTH + 2 * B_KV_WIDTH + 3 * C_WIDTH
IN_COLS = QKV_COLS + N_BRANCHES * D_MODEL
DEEPNORM_ALPHA = (2 * DEPTH) ** 0.25
DEEPNORM_BETA = (8 * DEPTH) ** -0.25
LN_EPS = 1e-5
RMS_EPS = 1e-6
NEG_INF = -1e30

kernel_name = "hybrid_dilated_axial_neighbourhood_encoder"


def _split_points():
    sizes = [A_WIDTH, A_WIDTH, A_WIDTH, B_Q_WIDTH, B_KV_WIDTH, B_KV_WIDTH,
             C_WIDTH, C_WIDTH, C_WIDTH]
    return tuple(int(v) for v in np.cumsum(sizes))


def layer_norm(x, g, b):
    xf = x.astype(jnp.float32)
    mu = jnp.mean(xf, -1, keepdims=True)
    var = jnp.mean(jnp.square(xf - mu), -1, keepdims=True)
    y = (xf - mu) * lax.rsqrt(var + LN_EPS)
    return (y * g.astype(jnp.float32) + b.astype(jnp.float32)).astype(x.dtype)


def rms_norm(x, g):
    xf = x.astype(jnp.float32)
    y = xf * lax.rsqrt(jnp.mean(jnp.square(xf), -1, keepdims=True) + RMS_EPS)
    return (y * g.astype(jnp.float32)).astype(x.dtype)


def rotary(x, pos, theta):
    half = x.shape[-1] // 2
    inv = theta ** (-jnp.arange(half, dtype=jnp.float32) / half)
    ang = pos.astype(jnp.float32)[:, None] * inv[None, :]
    cos = jnp.cos(ang)[None, :, None, :]
    sin = jnp.sin(ang)[None, :, None, :]
    xf = x.astype(jnp.float32)
    x1, x2 = xf[..., :half], xf[..., half:]
    return jnp.concatenate([x1 * cos - x2 * sin, x2 * cos + x1 * sin], -1).astype(x.dtype)


def partial_rotary(x, pos):
    return jnp.concatenate([rotary(x[..., :A_ROPE_DIMS], pos, ROPE_THETA),
                            x[..., A_ROPE_DIMS:]], -1)


def axial_rotary(x, row, col):
    half = x.shape[-1] // 2
    return jnp.concatenate([rotary(x[..., :half], row, B_AXIAL_THETA),
                            rotary(x[..., half:], col, B_AXIAL_THETA)], -1)


def banded_window_stats(q, k, v, radius):
    L, hd = q.shape[-2], q.shape[-1]
    lead = q.shape[:-2]
    nb = -(-L // Q_BLOCK)
    lp = nb * Q_BLOCK
    pad_q = [(0, 0)] * len(lead) + [(0, lp - L), (0, 0)]
    pad_kv = [(0, 0)] * len(lead) + [(radius, lp - L + radius), (0, 0)]
    qb = jnp.pad(q, pad_q).reshape(lead + (nb, Q_BLOCK, hd))
    kp = jnp.pad(k, pad_kv)
    vp = jnp.pad(v, pad_kv)
    span = Q_BLOCK + 2 * radius
    idx = jnp.arange(nb)[:, None] * Q_BLOCK + jnp.arange(span)[None, :]
    kb = kp[..., idx, :]
    vb = vp[..., idx, :].astype(jnp.float32)
    qi = jnp.arange(lp).reshape(nb, Q_BLOCK)[:, :, None]
    kj = (idx - radius)[:, None, :]
    mask = (jnp.abs(qi - kj) <= radius) & (kj >= 0) & (kj < L)
    s = jnp.einsum('...nqd,...nkd->...nqk', qb, kb, preferred_element_type=jnp.float32)
    s = jnp.where(mask, s, NEG_INF)
    m = jnp.max(s, -1)
    p = jnp.exp(s - m[..., None])
    l = jnp.sum(p, -1)
    o = jnp.einsum('...nqk,...nkd->...nqd', p, vb)
    m = m.reshape(lead + (lp,))[..., :L]
    l = l.reshape(lead + (lp,))[..., :L]
    o = o.reshape(lead + (lp, hd))[..., :L, :]
    return m, l, o


def dilated_attention(q, k, v):
    b, s, h, hd = q.shape
    ms, ls, outs = [], [], []
    for window, dil in A_CONFIGS:
        radius = window // (2 * dil)
        L = s // dil

        def to_sub(t):
            return t.reshape(b, L, dil, h, hd).transpose(0, 2, 3, 1, 4)

        m, l, o = banded_window_stats(to_sub(q), to_sub(k), to_sub(v), radius)
        ms.append(m.transpose(0, 3, 1, 2).reshape(b, s, h))
        ls.append(l.transpose(0, 3, 1, 2).reshape(b, s, h))
        outs.append(o.transpose(0, 3, 1, 2, 4).reshape(b, s, h, hd))
    m_all = jnp.stack(ms)
    l_all = jnp.stack(ls)
    o_all = jnp.stack(outs)
    m_max = jnp.max(m_all, 0)
    w = jnp.exp(m_all - m_max)
    out = jnp.sum(w[..., None] * o_all, 0) / jnp.sum(w * l_all, 0)[..., None]
    return out.astype(q.dtype)


def axial_gqa(q, k, v):
    b, s, hq, hd = q.shape
    hkv = k.shape[2]
    g = hq // hkv
    nq = s // Q_BLOCK
    qb = q.reshape(b, nq, Q_BLOCK, hkv, g, hd).transpose(1, 0, 2, 3, 4, 5)

    def block(qblk):
        sc = jnp.einsum('bqhgd,bkhd->bhgqk', qblk, k, preferred_element_type=jnp.float32)
        p = jax.nn.softmax(sc, -1)
        return jnp.einsum('bhgqk,bkhd->bqhgd', p.astype(v.dtype), v)

    o = lax.map(block, qb)
    return o.transpose(1, 0, 2, 3, 4, 5).reshape(b, s, hq * hd)


def neighbourhood_attention(q, k, v, rpb):
    b, s, h, hd = q.shape
    rows = s // GRID_W
    kr = min(C_ROWS_MAX, rows)

    def grid(t):
        return t.reshape(b, rows, GRID_W, h, hd).transpose(0, 3, 1, 2, 4)

    qg, kg, vg = grid(q), grid(k), grid(v)
    r = jnp.arange(rows)
    r0 = jnp.clip(r - kr // 2, 0, rows - kr)
    row_idx = r0[:, None] + jnp.arange(kr)[None, :]
    kn = kg[:, :, row_idx]
    vn = vg[:, :, row_idx]
    c = jnp.arange(GRID_W)
    c0 = jnp.clip(c - C_COLS // 2, 0, GRID_W - C_COLS)
    col_mask = (c[None, :] >= c0[:, None]) & (c[None, :] < c0[:, None] + C_COLS)
    dr = row_idx - r[:, None] + (C_ROWS_MAX - 1)
    dc = jnp.clip(c[None, :] - c[:, None] + (C_COLS - 1), 0, 2 * C_COLS - 2)
    bias = rpb[:, dr[:, None, :, None], dc[None, :, None, :]]
    sc = jnp.einsum('bhrqd,bhrikd->bhrqik', qg, kn, preferred_element_type=jnp.float32)
    sc = sc + bias.astype(jnp.float32)
    sc = jnp.where(col_mask[:, None, :], sc, NEG_INF)
    p = jax.nn.softmax(sc.reshape(b, h, rows, GRID_W, kr * GRID_W), -1)
    p = p.reshape(b, h, rows, GRID_W, kr, GRID_W)
    o = jnp.einsum('bhrqik,bhrikd->bhrqd', p.astype(v.dtype), vn)
    return o.transpose(0, 2, 3, 1, 4).reshape(b, s, h * hd)


def _fwd_setup_inputs(seed: int = 0) -> dict:
    key = jax.random.key(seed)
    ks = jax.random.split(key, 16)
    f32 = jnp.float32

    def normal(k, shape, scale):
        return jax.random.normal(k, shape, f32) * scale

    x = normal(ks[0], (BATCH, SEQ, D_MODEL), 1.0)
    w_in = normal(ks[1], (DEPTH, D_MODEL, IN_COLS), D_MODEL ** -0.5)
    b_gate = normal(ks[2], (DEPTH, N_BRANCHES * D_MODEL), 0.02)
    q_norm_b = 1.0 + normal(ks[3], (DEPTH, HEAD_DIM), 0.02)
    k_norm_b = 1.0 + normal(ks[4], (DEPTH, HEAD_DIM), 0.02)
    rpb_c = normal(ks[5], (DEPTH, C_HEADS, 2 * C_ROWS_MAX - 1, 2 * C_COLS - 1), 0.1)
    w_branch_a = normal(ks[6], (DEPTH, A_WIDTH, D_MODEL), A_WIDTH ** -0.5 * DEEPNORM_BETA)
    w_branch_b = normal(ks[7], (DEPTH, B_Q_WIDTH, D_MODEL), B_Q_WIDTH ** -0.5 * DEEPNORM_BETA)
    w_branch_c = normal(ks[8], (DEPTH, C_WIDTH, D_MODEL), C_WIDTH ** -0.5 * DEEPNORM_BETA)
    w_out = normal(ks[9], (DEPTH, D_MODEL, D_MODEL), D_MODEL ** -0.5 * DEEPNORM_BETA)
    ln1_g = 1.0 + normal(ks[10], (DEPTH, D_MODEL), 0.02)
    ln1_b = normal(ks[11], (DEPTH, D_MODEL), 0.02)
    w_up = normal(ks[12], (DEPTH, D_MODEL, D_FF), D_MODEL ** -0.5)
    w_down = normal(ks[13], (DEPTH, D_FF, D_MODEL), D_FF ** -0.5 * DEEPNORM_BETA)
    ln2_g = 1.0 + normal(ks[14], (DEPTH, D_MODEL), 0.02)
    ln2_b = normal(ks[15], (DEPTH, D_MODEL), 0.02)
    return {"x": x, "w_in": w_in, "b_gate": b_gate, "q_norm_b": q_norm_b,
            "k_norm_b": k_norm_b, "rpb_c": rpb_c, "w_branch_a": w_branch_a,
            "w_branch_b": w_branch_b, "w_branch_c": w_branch_c, "w_out": w_out,
            "ln1_g": ln1_g, "ln1_b": ln1_b, "w_up": w_up, "w_down": w_down,
            "ln2_g": ln2_g, "ln2_b": ln2_b}


def _fwd_reference(x, w_in, b_gate, q_norm_b, k_norm_b, rpb_c, w_branch_a, w_branch_b,
              w_branch_c, w_out, ln1_g, ln1_b, w_up, w_down, ln2_g, ln2_b):
    b, s, _ = x.shape
    pos = jnp.arange(s)
    row = pos // GRID_W
    col = pos % GRID_W
    scale = HEAD_DIM ** -0.5
    splits = _split_points()

    def heads(t, n):
        return t.reshape(b, s, n, HEAD_DIM)

    for layer in range(DEPTH):
        h = x @ w_in[layer]
        qa, ka, va, qb, kb, vb, qc, kc, vc, gate_logits = jnp.split(h, splits, axis=-1)

        qa = partial_rotary(heads(qa, A_HEADS), pos) * scale
        ka = partial_rotary(heads(ka, A_HEADS), pos)
        oa = dilated_attention(qa, ka, heads(va, A_HEADS)).reshape(b, s, A_WIDTH)

        qb = axial_rotary(rms_norm(heads(qb, B_Q_HEADS), q_norm_b[layer]), row, col) * scale
        kb = axial_rotary(rms_norm(heads(kb, B_KV_HEADS), k_norm_b[layer]), row, col)
        ob = axial_gqa(qb, kb, heads(vb, B_KV_HEADS))

        oc = neighbourhood_attention(heads(qc, C_HEADS) * scale, heads(kc, C_HEADS),
                                     heads(vc, C_HEADS), rpb_c[layer])

        g = jax.nn.sigmoid((gate_logits + b_gate[layer]).astype(jnp.float32)).astype(x.dtype)
        g = g.reshape(b, s, N_BRANCHES, D_MODEL)
        merged = (g[:, :, 0] * (oa @ w_branch_a[layer])
                  + g[:, :, 1] * (ob @ w_branch_b[layer])
                  + g[:, :, 2] * (oc @ w_branch_c[layer]))
        mix = merged @ w_out[layer]
        x = layer_norm(DEEPNORM_ALPHA * x + mix, ln1_g[layer], ln1_b[layer])

        ff = jnp.square(jax.nn.relu(x @ w_up[layer])) @ w_down[layer]
        x = layer_norm(DEEPNORM_ALPHA * x + ff, ln2_g[layer], ln2_b[layer])
    return x


import jax as _jax
import jax.numpy as _jnp

TWIN_FORMAT = 'train_step'
FWD_PARAMS = ['x', 'w_in', 'b_gate', 'q_norm_b', 'k_norm_b', 'rpb_c', 'w_branch_a', 'w_branch_b', 'w_branch_c', 'w_out', 'ln1_g', 'ln1_b', 'w_up', 'w_down', 'ln2_g', 'ln2_b']
TWIN_WEIGHTS = ['w_in', 'b_gate', 'q_norm_b', 'k_norm_b', 'rpb_c', 'w_branch_a', 'w_branch_b', 'w_branch_c', 'w_out', 'ln1_g', 'ln1_b', 'w_up', 'w_down', 'ln2_g', 'ln2_b']
TWIN_DIFF_INPUT = 'x'
TWIN_INPUTS = ['x', 'w_in', 'b_gate', 'q_norm_b', 'k_norm_b', 'rpb_c', 'w_branch_a', 'w_branch_b', 'w_branch_c', 'w_out', 'ln1_g', 'ln1_b', 'w_up', 'w_down', 'ln2_g', 'ln2_b', 'loss_target', 'm_w_in', 'm_b_gate', 'm_q_norm_b', 'm_k_norm_b', 'm_rpb_c', 'm_w_branch_a', 'm_w_branch_b', 'm_w_branch_c', 'm_w_out', 'm_ln1_g', 'm_ln1_b', 'm_w_up', 'm_w_down', 'm_ln2_g', 'm_ln2_b', 'v_w_in', 'v_b_gate', 'v_q_norm_b', 'v_k_norm_b', 'v_rpb_c', 'v_w_branch_a', 'v_w_branch_b', 'v_w_branch_c', 'v_w_out', 'v_ln1_g', 'v_ln1_b', 'v_w_up', 'v_w_down', 'v_ln2_g', 'v_ln2_b']
TWIN_OUTPUTS = ['loss', 'grad_x', 'grad_w_in', 'grad_b_gate', 'grad_q_norm_b', 'grad_k_norm_b', 'grad_rpb_c', 'grad_w_branch_a', 'grad_w_branch_b', 'grad_w_branch_c', 'grad_w_out', 'grad_ln1_g', 'grad_ln1_b', 'grad_w_up', 'grad_w_down', 'grad_ln2_g', 'grad_ln2_b', 'delta_w_in', 'delta_b_gate', 'delta_q_norm_b', 'delta_k_norm_b', 'delta_rpb_c', 'delta_w_branch_a', 'delta_w_branch_b', 'delta_w_branch_c', 'delta_w_out', 'delta_ln1_g', 'delta_ln1_b', 'delta_w_up', 'delta_w_down', 'delta_ln2_g', 'delta_ln2_b', 'new_m_w_in', 'new_m_b_gate', 'new_m_q_norm_b', 'new_m_k_norm_b', 'new_m_rpb_c', 'new_m_w_branch_a', 'new_m_w_branch_b', 'new_m_w_branch_c', 'new_m_w_out', 'new_m_ln1_g', 'new_m_ln1_b', 'new_m_w_up', 'new_m_w_down', 'new_m_ln2_g', 'new_m_ln2_b', 'new_v_w_in', 'new_v_b_gate', 'new_v_q_norm_b', 'new_v_k_norm_b', 'new_v_rpb_c', 'new_v_w_branch_a', 'new_v_w_branch_b', 'new_v_w_branch_c', 'new_v_w_out', 'new_v_ln1_g', 'new_v_ln1_b', 'new_v_w_up', 'new_v_w_down', 'new_v_ln2_g', 'new_v_ln2_b']
TWIN_LEAF_KINDS = {'loss': 'loss', 'grad_x': 'grad_x', 'grad_w_in': 'grad_w', 'grad_b_gate': 'grad_w', 'grad_q_norm_b': 'grad_w', 'grad_k_norm_b': 'grad_w', 'grad_rpb_c': 'grad_w', 'grad_w_branch_a': 'grad_w', 'grad_w_branch_b': 'grad_w', 'grad_w_branch_c': 'grad_w', 'grad_w_out': 'grad_w', 'grad_ln1_g': 'grad_w', 'grad_ln1_b': 'grad_w', 'grad_w_up': 'grad_w', 'grad_w_down': 'grad_w', 'grad_ln2_g': 'grad_w', 'grad_ln2_b': 'grad_w', 'delta_w_in': 'delta_w', 'delta_b_gate': 'delta_w', 'delta_q_norm_b': 'delta_w', 'delta_k_norm_b': 'delta_w', 'delta_rpb_c': 'delta_w', 'delta_w_branch_a': 'delta_w', 'delta_w_branch_b': 'delta_w', 'delta_w_branch_c': 'delta_w', 'delta_w_out': 'delta_w', 'delta_ln1_g': 'delta_w', 'delta_ln1_b': 'delta_w', 'delta_w_up': 'delta_w', 'delta_w_down': 'delta_w', 'delta_ln2_g': 'delta_w', 'delta_ln2_b': 'delta_w', 'new_m_w_in': 'new_m', 'new_m_b_gate': 'new_m', 'new_m_q_norm_b': 'new_m', 'new_m_k_norm_b': 'new_m', 'new_m_rpb_c': 'new_m', 'new_m_w_branch_a': 'new_m', 'new_m_w_branch_b': 'new_m', 'new_m_w_branch_c': 'new_m', 'new_m_w_out': 'new_m', 'new_m_ln1_g': 'new_m', 'new_m_ln1_b': 'new_m', 'new_m_w_up': 'new_m', 'new_m_w_down': 'new_m', 'new_m_ln2_g': 'new_m', 'new_m_ln2_b': 'new_m', 'new_v_w_in': 'new_v', 'new_v_b_gate': 'new_v', 'new_v_q_norm_b': 'new_v', 'new_v_k_norm_b': 'new_v', 'new_v_rpb_c': 'new_v', 'new_v_w_branch_a': 'new_v', 'new_v_w_branch_b': 'new_v', 'new_v_w_branch_c': 'new_v', 'new_v_w_out': 'new_v', 'new_v_ln1_g': 'new_v', 'new_v_ln1_b': 'new_v', 'new_v_w_up': 'new_v', 'new_v_w_down': 'new_v', 'new_v_ln2_g': 'new_v', 'new_v_ln2_b': 'new_v'}


def _forward(args):
    return _fwd_reference(*[args[k] for k in FWD_PARAMS])


def _output_shape():
    def fwd():
        inp = _fwd_setup_inputs(0)
        return _fwd_reference(*[inp[k] for k in FWD_PARAMS])
    out = _jax.eval_shape(fwd)
    return out.shape, out.dtype

N_MICROBATCH = 1
ADAM_LR = 0.001
ADAM_B1 = 0.9
ADAM_B2 = 0.999
ADAM_EPS = 1e-08
ADAM_WD = 0.01
ADAM_STEP = 10
PER_EXAMPLE_BATCH_AXIS = {'x': 0, 'loss_target': 0}
SHARED_INPUTS = []
_WEIGHT_DTYPES = {'w_in': _jnp.float32, 'b_gate': _jnp.float32, 'q_norm_b': _jnp.float32, 'k_norm_b': _jnp.float32, 'rpb_c': _jnp.float32, 'w_branch_a': _jnp.float32, 'w_branch_b': _jnp.float32, 'w_branch_c': _jnp.float32, 'w_out': _jnp.float32, 'ln1_g': _jnp.float32, 'ln1_b': _jnp.float32, 'w_up': _jnp.float32, 'w_down': _jnp.float32, 'ln2_g': _jnp.float32, 'ln2_b': _jnp.float32}
MOMENT_SCALE = {'w_in': 4.313195e-03, 'b_gate': 2.841246e-03, 'q_norm_b': 5.692599e-03, 'k_norm_b': 5.911681e-03, 'rpb_c': 1.951414e-03, 'w_branch_a': 1.634098e-02, 'w_branch_b': 1.849025e-02, 'w_branch_c': 1.469239e-02, 'w_out': 2.826823e-02, 'ln1_g': 1.863386e+00, 'ln1_b': 1.089619e+00, 'w_up': 4.473789e-02, 'w_down': 3.008574e-01, 'ln2_g': 3.225605e+01, 'ln2_b': 7.576857e+00}


def _to_microbatches(a, axis):
    t = _jnp.moveaxis(a, axis, 0)
    t = t.reshape((N_MICROBATCH, t.shape[0] // N_MICROBATCH) + t.shape[1:])
    return _jnp.moveaxis(t, 1, axis + 1)


def setup_inputs(seed: int = 0) -> dict:
    inp = _fwd_setup_inputs(seed)
    key = _jax.random.fold_in(_jax.random.key(seed), 7919)
    shape, _ = _output_shape()
    out = dict(inp)
    out["loss_target"] = _jax.random.normal(_jax.random.fold_in(key, 0), shape, _jnp.float32)
    for i, name in enumerate(TWIN_WEIGHTS):
        w = inp[name].astype(_jnp.float32)
        if MOMENT_SCALE is None:
            s = _jnp.sqrt(_jnp.mean(_jnp.square(w)) + 1e-30)
        else:
            s = MOMENT_SCALE[name]
        km, kv = _jax.random.split(_jax.random.fold_in(key, i + 1))
        out[name] = w
        out["m_" + name] = s * _jax.random.normal(km, w.shape, _jnp.float32)
        out["v_" + name] = (s * s) * _jax.random.uniform(kv, w.shape, _jnp.float32, 0.5, 1.5)
    if N_MICROBATCH > 1:
        for name, axis in PER_EXAMPLE_BATCH_AXIS.items():
            out[name] = _to_microbatches(out[name], axis)
    return {'x': out['x'], 'w_in': out['w_in'], 'b_gate': out['b_gate'], 'q_norm_b': out['q_norm_b'], 'k_norm_b': out['k_norm_b'], 'rpb_c': out['rpb_c'], 'w_branch_a': out['w_branch_a'], 'w_branch_b': out['w_branch_b'], 'w_branch_c': out['w_branch_c'], 'w_out': out['w_out'], 'ln1_g': out['ln1_g'], 'ln1_b': out['ln1_b'], 'w_up': out['w_up'], 'w_down': out['w_down'], 'ln2_g': out['ln2_g'], 'ln2_b': out['ln2_b'], 'loss_target': out['loss_target'], 'm_w_in': out['m_w_in'], 'm_b_gate': out['m_b_gate'], 'm_q_norm_b': out['m_q_norm_b'], 'm_k_norm_b': out['m_k_norm_b'], 'm_rpb_c': out['m_rpb_c'], 'm_w_branch_a': out['m_w_branch_a'], 'm_w_branch_b': out['m_w_branch_b'], 'm_w_branch_c': out['m_w_branch_c'], 'm_w_out': out['m_w_out'], 'm_ln1_g': out['m_ln1_g'], 'm_ln1_b': out['m_ln1_b'], 'm_w_up': out['m_w_up'], 'm_w_down': out['m_w_down'], 'm_ln2_g': out['m_ln2_g'], 'm_ln2_b': out['m_ln2_b'], 'v_w_in': out['v_w_in'], 'v_b_gate': out['v_b_gate'], 'v_q_norm_b': out['v_q_norm_b'], 'v_k_norm_b': out['v_k_norm_b'], 'v_rpb_c': out['v_rpb_c'], 'v_w_branch_a': out['v_w_branch_a'], 'v_w_branch_b': out['v_w_branch_b'], 'v_w_branch_c': out['v_w_branch_c'], 'v_w_out': out['v_w_out'], 'v_ln1_g': out['v_ln1_g'], 'v_ln1_b': out['v_ln1_b'], 'v_w_up': out['v_w_up'], 'v_w_down': out['v_w_down'], 'v_ln2_g': out['v_ln2_g'], 'v_ln2_b': out['v_ln2_b']}


def _loss(weights, diff, rest, loss_target):
    with _jax.named_scope("forward"):
        args = {**rest, TWIN_DIFF_INPUT: diff, **{k: w.astype(_WEIGHT_DTYPES[k]) for k, w in weights.items()}}
        y = _forward(args)
    with _jax.named_scope("loss_head"):
        err = _jnp.square(y.astype(_jnp.float32) - loss_target)
        return 0.5 * _jnp.sum(_jnp.mean(err, axis=-1)) if err.ndim else 0.5 * err


def _adamw(w, g, m, v):
    m = ADAM_B1 * m + (1.0 - ADAM_B1) * g
    v = ADAM_B2 * v + (1.0 - ADAM_B2) * _jnp.square(g)
    m_hat = m / (1.0 - ADAM_B1 ** ADAM_STEP)
    v_hat = v / (1.0 - ADAM_B2 ** ADAM_STEP)
    delta = -ADAM_LR * (m_hat / (_jnp.sqrt(v_hat) + ADAM_EPS) + ADAM_WD * w)
    return delta, m, v


def reference(x, w_in, b_gate, q_norm_b, k_norm_b, rpb_c, w_branch_a, w_branch_b, w_branch_c, w_out, ln1_g, ln1_b, w_up, w_down, ln2_g, ln2_b, loss_target, m_w_in, m_b_gate, m_q_norm_b, m_k_norm_b, m_rpb_c, m_w_branch_a, m_w_branch_b, m_w_branch_c, m_w_out, m_ln1_g, m_ln1_b, m_w_up, m_w_down, m_ln2_g, m_ln2_b, v_w_in, v_b_gate, v_q_norm_b, v_k_norm_b, v_rpb_c, v_w_branch_a, v_w_branch_b, v_w_branch_c, v_w_out, v_ln1_g, v_ln1_b, v_w_up, v_w_down, v_ln2_g, v_ln2_b):
    given = dict(x=x, w_in=w_in, b_gate=b_gate, q_norm_b=q_norm_b, k_norm_b=k_norm_b, rpb_c=rpb_c, w_branch_a=w_branch_a, w_branch_b=w_branch_b, w_branch_c=w_branch_c, w_out=w_out, ln1_g=ln1_g, ln1_b=ln1_b, w_up=w_up, w_down=w_down, ln2_g=ln2_g, ln2_b=ln2_b, loss_target=loss_target, m_w_in=m_w_in, m_b_gate=m_b_gate, m_q_norm_b=m_q_norm_b, m_k_norm_b=m_k_norm_b, m_rpb_c=m_rpb_c, m_w_branch_a=m_w_branch_a, m_w_branch_b=m_w_branch_b, m_w_branch_c=m_w_branch_c, m_w_out=m_w_out, m_ln1_g=m_ln1_g, m_ln1_b=m_ln1_b, m_w_up=m_w_up, m_w_down=m_w_down, m_ln2_g=m_ln2_g, m_ln2_b=m_ln2_b, v_w_in=v_w_in, v_b_gate=v_b_gate, v_q_norm_b=v_q_norm_b, v_k_norm_b=v_k_norm_b, v_rpb_c=v_rpb_c, v_w_branch_a=v_w_branch_a, v_w_branch_b=v_w_branch_b, v_w_branch_c=v_w_branch_c, v_w_out=v_w_out, v_ln1_g=v_ln1_g, v_ln1_b=v_ln1_b, v_w_up=v_w_up, v_w_down=v_w_down, v_ln2_g=v_ln2_g, v_ln2_b=v_ln2_b)
    weights = {n: given[n] for n in TWIN_WEIGHTS}
    shared = {n: given[n] for n in SHARED_INPUTS}
    per_example = {n: given[n] for n in ['x']}
    grad_fn = _jax.value_and_grad(_loss, argnums=(0, 1))

    def one_microbatch(ex, loss_target):
        ex = dict(ex)
        diff = ex.pop(TWIN_DIFF_INPUT)
        return grad_fn(weights, diff, {**shared, **ex}, loss_target)

    if N_MICROBATCH == 1:
        loss, (grad_w, grad_x) = one_microbatch(per_example, given["loss_target"])
    else:
        def body(carry, xs):
            loss_sum, grad_sum = carry
            l_k, (gw_k, gx_k) = one_microbatch(xs[0], xs[1])
            with _jax.named_scope("update"):
                return (loss_sum + l_k, _jax.tree.map(_jnp.add, grad_sum, gw_k)), gx_k

        init = (_jnp.zeros((), _jnp.float32), _jax.tree.map(_jnp.zeros_like, weights))
        (loss, grad_w), grad_x = _jax.lax.scan(body, init, (per_example, given["loss_target"]))
    with _jax.named_scope("update"):
        delta_w, new_m, new_v = {}, {}, {}
        for n in TWIN_WEIGHTS:
            delta_w[n], new_m[n], new_v[n] = _adamw(weights[n], grad_w[n], given["m_" + n], given["v_" + n])
    return (loss, grad_x, *[grad_w[n] for n in TWIN_WEIGHTS], *[delta_w[n] for n in TWIN_WEIGHTS],
            *[new_m[n] for n in TWIN_WEIGHTS], *[new_v[n] for n in TWIN_WEIGHTS])
```

```python
import functools
import math

import numpy as np
import jax
import jax.numpy as jnp
from jax import lax
from jax.experimental import pallas as pl
from jax.experimental.pallas import tpu as pltpu

F32 = jnp.float32
BF = jnp.bfloat16
HI = lax.Precision.HIGHEST
NEG = -1e30
MESH = pl.DeviceIdType.MESH
AXES = ("x", "y", "c")

D = 1024
DEPTH = 4
HD = 64
A_W, BQ_W, BKV_W, C_W = 256, 512, 128, 256
QKV = 2304
GATE = 3072
D_FF = 4096
GRID_W = 64
ALPHA = (2 * DEPTH) ** 0.25
LN_EPS = 1e-5
RMS_EPS = 1e-6
SCALE = HD ** -0.5
ROPE_THETA = 500000.0
AXIAL_THETA = 10000.0
A_CONFIGS = ((128, 1), (512, 4), (2048, 16))
LR, B1, B2, EPS, WD, STEP = 0.001, 0.9, 0.999, 1e-08, 0.01, 10

VMEM_LIMIT = 56 * 1024 * 1024
BAND_T = 256


def _pcall(body, **kw):
    return pl.pallas_call(body, **kw)


def _pcall_comm(body, **kw):
    return pl.pallas_call(body, **kw)


def _cp(*sem):
    return pltpu.CompilerParams(dimension_semantics=sem, vmem_limit_bytes=VMEM_LIMIT)


def _sds(shape, dtype):
    return jax.ShapeDtypeStruct(shape, dtype)


def _mm(a, b, dims, outs, tm, tn, tk, epilogue=None, extras=(), name="mm"):
    if dims == "tn":
        K, M = a.shape
    else:
        M, K = a.shape
    N = b.shape[0] if dims == "nt" else b.shape[1]
    tm, tn, tk = min(tm, M), min(tn, N), min(tk, K)
    assert M % tm == 0 and N % tn == 0 and K % tk == 0, (name, M, N, K, tm, tn, tk)
    nk = K // tk
    ne, no = len(extras), len(outs)

    def body(a_ref, b_ref, *rest):
        extra_refs, out_refs = rest[:ne], rest[ne:ne + no]
        av, bv = a_ref[...].astype(BF), b_ref[...].astype(BF)
        if dims == "nn":
            p = jnp.dot(av, bv, preferred_element_type=F32)
        elif dims == "nt":
            p = lax.dot_general(av, bv, (((1,), (1,)), ((), ())), preferred_element_type=F32)
        else:
            p = lax.dot_general(av, bv, (((0,), (0,)), ((), ())), preferred_element_type=F32)

        def finish(acc):
            res = epilogue(acc, *[r[...] for r in extra_refs]) if epilogue else (acc,)
            for o, r in zip(out_refs, res):
                o[...] = r.astype(o.dtype)

        if nk == 1:
            finish(p)
        else:
            acc_ref = rest[-1]
            k = pl.program_id(2)

            @pl.when(k == 0)
            def _():
                acc_ref[...] = p

            @pl.when(k > 0)
            def _():
                acc_ref[...] += p

            @pl.when(k == nk - 1)
            def _():
                finish(acc_ref[...])

    if dims == "tn":
        a_spec = pl.BlockSpec((tk, tm), lambda i, j, k: (k, i))
    else:
        a_spec = pl.BlockSpec((tm, tk), lambda i, j, k: (i, k))
    if dims == "nt":
        b_spec = pl.BlockSpec((tn, tk), lambda i, j, k: (j, k))
    else:
        b_spec = pl.BlockSpec((tk, tn), lambda i, j, k: (k, j))
    o_spec = pl.BlockSpec((tm, tn), lambda i, j, k: (i, j))
    res = _pcall(
        body, name=name, grid=(M // tm, N // tn, nk),
        in_specs=[a_spec, b_spec] + [o_spec] * ne,
        out_specs=[o_spec] * no,
        out_shape=[_sds((M, N), dt) for dt in outs],
        scratch_shapes=[pltpu.VMEM((tm, tn), F32)] if nk > 1 else [],
        compiler_params=_cp("parallel", "parallel", "arbitrary"),
    )(a, b, *extras)
    return res


def _rope_tables(T):
    pos = jnp.arange(T)

    def cs(p, theta, half):
        inv = theta ** (-jnp.arange(half, dtype=F32) / half)
        ang = p.astype(F32)[:, None] * inv[None, :]
        return jnp.cos(ang), jnp.sin(ang)

    ca, sa = cs(pos, ROPE_THETA, 8)
    one, zero, z8 = jnp.ones((T, 48), F32), jnp.zeros((T, 48), F32), jnp.zeros((T, 8), F32)
    tab_a = [jnp.concatenate(t, 1) for t in ([ca, ca, one], [-sa, z8, zero], [z8, sa, zero])]
    cr, sr = cs(pos // GRID_W, AXIAL_THETA, 16)
    cc, sc = cs(pos % GRID_W, AXIAL_THETA, 16)
    z16 = jnp.zeros((T, 16), F32)
    tab_b = [jnp.concatenate(t, 1) for t in ([cr, cr, cc, cc], [-sr, z16, -sc, z16], [z16, sr, z16, sc])]
    return [jnp.tile(t, (1, 2)) for t in tab_a], [jnp.tile(t, (1, 2)) for t in tab_b]


def _rot(x, C, S1, S2, k):
    return x * C + pltpu.roll(x, 128 - k, 1) * S1 + pltpu.roll(x, k, 1) * S2


def _rot_t(d, C, S1, S2, k):
    return d * C + pltpu.roll(d * S1, k, 1) + pltpu.roll(d * S2, 128 - k, 1)


def _group_mean_matrix():
    m = np.zeros((128, 128), np.float32)
    m[:64, :64] = 1.0 / 64
    m[64:, 64:] = 1.0 / 64
    return jnp.asarray(m)


def _prep_fwd(hq, tab_a, tab_b, qn, kn, gm):
    T = hq.shape[0]
    tt = min(256, T)
    widths = [A_W, A_W, A_W, BQ_W, BKV_W, BKV_W, C_W, C_W, C_W]

    def body(h_ref, ca, s1a, s2a, cb, s1b, s2b, qn_ref, kn_ref, gm_ref,
             qa, ka, va, qb, kb, vb, qc, kc, vc):
        def col(off, j):
            return h_ref[:, off + 128 * j: off + 128 * (j + 1)]

        for j in range(2):
            sl = slice(128 * j, 128 * (j + 1))
            qa[:, sl] = (_rot(col(0, j), ca[...], s1a[...], s2a[...], 8) * SCALE).astype(qa.dtype)
            ka[:, sl] = _rot(col(256, j), ca[...], s1a[...], s2a[...], 8).astype(ka.dtype)
            va[:, sl] = col(512, j).astype(va.dtype)
            qc[:, sl] = (col(1536, j) * SCALE).astype(qc.dtype)
            kc[:, sl] = col(1792, j).astype(kc.dtype)
            vc[:, sl] = col(2048, j).astype(vc.dtype)

        def normed(x, w):
            ms = jnp.dot(x * x, gm_ref[...], precision=HI, preferred_element_type=F32)
            return x * lax.rsqrt(ms + RMS_EPS) * w

        for j in range(4):
            y = normed(col(768, j), qn_ref[...])
            qb[:, 128 * j:128 * (j + 1)] = (_rot(y, cb[...], s1b[...], s2b[...], 16) * SCALE).astype(qb.dtype)
        y = normed(col(1280, 0), kn_ref[...])
        kb[...] = _rot(y, cb[...], s1b[...], s2b[...], 16).astype(kb.dtype)
        vb[...] = col(1408, 0).astype(vb.dtype)

    row = lambda w: pl.BlockSpec((tt, w), lambda i: (i, 0))
    const = lambda s: pl.BlockSpec(s, lambda i: (0, 0))
    return _pcall(
        body, name="prep_fwd", grid=(T // tt,),
        in_specs=[row(QKV)] + [row(128)] * 6 + [const((1, 128))] * 2 + [const((128, 128))],
        out_specs=[row(w) for w in widths],
        out_shape=[_sds((T, w), BF) for w in widths],
        compiler_params=_cp("parallel"),
    )(hq, *tab_a, *tab_b, qn, kn, gm)


def _prep_bwd(hq, grads, tab_a, tab_b, qn, kn, gm):
    T = hq.shape[0]
    tt = min(256, T)
    widths = [A_W, A_W, A_W, BQ_W, BKV_W, BKV_W, C_W, C_W, C_W]

    def body(h_ref, dqa, dka, dva, dqb, dkb, dvb, dqc, dkc, dvc,
             ca, s1a, s2a, cb, s1b, s2b, qn_ref, kn_ref, gm_ref, dh, dqn, dkn):
        i = pl.program_id(0)

        @pl.when(i == 0)
        def _():
            dqn[...] = jnp.zeros_like(dqn)
            dkn[...] = jnp.zeros_like(dkn)

        def put(off, j, val):
            dh[:, off + 128 * j: off + 128 * (j + 1)] = val.astype(dh.dtype)

        for j in range(2):
            sl = slice(128 * j, 128 * (j + 1))
            put(0, j, _rot_t(dqa[:, sl] * SCALE, ca[...], s1a[...], s2a[...], 8))
            put(256, j, _rot_t(dka[:, sl], ca[...], s1a[...], s2a[...], 8))
            put(512, j, dva[:, sl])
            put(1536, j, dqc[:, sl] * SCALE)
            put(1792, j, dkc[:, sl])
            put(2048, j, dvc[:, sl])

        def norm_bwd(x, w, e):
            ms = jnp.dot(x * x, gm_ref[...], precision=HI, preferred_element_type=F32)
            r = lax.rsqrt(ms + RMS_EPS)
            n = x * r
            dn = e * w
            proj = jnp.dot(dn * n, gm_ref[...], precision=HI, preferred_element_type=F32)
            return r * (dn - n * proj), jnp.sum(e * n, axis=0, keepdims=True)

        for j in range(4):
            sl = slice(128 * j, 128 * (j + 1))
            e = _rot_t(dqb[:, sl] * SCALE, cb[...], s1b[...], s2b[...], 16)
            dx, dw = norm_bwd(h_ref[:, 768 + 128 * j: 768 + 128 * (j + 1)], qn_ref[...], e)
            put(768, j, dx)
            dqn[:, sl] += dw
        e = _rot_t(dkb[...], cb[...], s1b[...], s2b[...], 16)
        dx, dw = norm_bwd(h_ref[:, 1280:1408], kn_ref[...], e)
        put(1280, 0, dx)
        dkn[...] += dw
        put(1408, 0, dvb[...])

    row = lambda w: pl.BlockSpec((tt, w), lambda i: (i, 0))
    const = lambda s: pl.BlockSpec(s, lambda i: (0, 0))
    return _pcall(
        body, name="prep_bwd", grid=(T // tt,),
        in_specs=[row(QKV)] + [row(w) for w in widths] + [row(128)] * 6
        + [const((1, 128))] * 2 + [const((128, 128))],
        out_specs=[row(QKV), const((1, BQ_W)), const((1, BKV_W))],
        out_shape=[_sds((T, QKV), BF), _sds((1, BQ_W), F32), _sds((1, BKV_W), F32)],
        compiler_params=_cp("arbitrary"),
    )(hq, *grads, *tab_a, *tab_b, qn, kn, gm)


def _to_heads(x):
    T, W = x.shape
    return x.reshape(T, W // HD, HD).transpose(1, 0, 2)


def _from_heads(x):
    H, T, _ = x.shape
    return x.transpose(1, 0, 2).reshape(T, H * HD)


def _rows(x, t):
    H, T, _ = x.shape
    return x.reshape(H, T // t, 1, t)


def _nt(a, b):
    return lax.dot_general(a, b, (((1,), (1,)), ((), ())), preferred_element_type=F32)


def _attn_full_fwd(q, k, v):
    Hq, T, _ = q.shape
    Hk = k.shape[0]
    G = Hq // Hk
    tq, tk = min(256, T), min(512, T)
    R = G * tq

    def body(q_ref, k_ref, v_ref, o_ref, lse_ref, m_ref, l_ref, acc_ref):
        qv = q_ref[...].reshape(R, HD)
        m_ref[...] = jnp.full((R, 1), NEG, F32)
        l_ref[...] = jnp.zeros((R, 1), F32)
        acc_ref[...] = jnp.zeros((R, HD), F32)

        def step(j, carry):
            st = pl.multiple_of(j * tk, tk)
            ks, vs = k_ref[pl.ds(st, tk), :], v_ref[pl.ds(st, tk), :]
            s = _nt(qv, ks)
            m_old = m_ref[...]
            m_new = jnp.maximum(m_old, jnp.max(s, axis=-1, keepdims=True))
            a = jnp.exp(m_old - m_new)
            p = jnp.exp(s - m_new)
            l_ref[...] = a * l_ref[...] + jnp.sum(p, axis=-1, keepdims=True)
            acc_ref[...] = a * acc_ref[...] + jnp.dot(p.astype(BF), vs, preferred_element_type=F32)
            m_ref[...] = m_new
            return carry

        lax.fori_loop(0, T // tk, step, 0)
        o_ref[...] = (acc_ref[...] / l_ref[...]).reshape(G, tq, HD).astype(o_ref.dtype)
        lse_ref[...] = (m_ref[...] + jnp.log(l_ref[...])).reshape(G, tq, 1)

    qs = pl.BlockSpec((G, tq, HD), lambda g, i: (g, i, 0))
    kvs = pl.BlockSpec((None, T, HD), lambda g, i: (g, 0, 0))
    return _pcall(
        body, name="attn_full_fwd", grid=(Hk, T // tq),
        in_specs=[qs, kvs, kvs],
        out_specs=[qs, pl.BlockSpec((G, tq, 1), lambda g, i: (g, i, 0))],
        out_shape=[_sds((Hq, T, HD), BF), _sds((Hq, T, 1), F32)],
        scratch_shapes=[pltpu.VMEM((R, 1), F32), pltpu.VMEM((R, 1), F32), pltpu.VMEM((R, HD), F32)],
        compiler_params=_cp("parallel", "parallel"),
    )(q, k, v)


def _attn_full_dq(q, k, v, o, do, lse):
    Hq, T, _ = q.shape
    Hk = k.shape[0]
    G = Hq // Hk
    tq, tk = min(256, T), min(512, T)
    R = G * tq

    def body(q_ref, k_ref, v_ref, o_ref, do_ref, lse_ref, dq_ref, dl_ref, acc_ref):
        qv = q_ref[...].reshape(R, HD)
        dov = do_ref[...].reshape(R, HD)
        lse = lse_ref[...].reshape(R, 1)
        delta = jnp.sum(dov.astype(F32) * o_ref[...].reshape(R, HD).astype(F32), axis=-1, keepdims=True)
        acc_ref[...] = jnp.zeros((R, HD), F32)

        def step(j, carry):
            st = pl.multiple_of(j * tk, tk)
            ks, vs = k_ref[pl.ds(st, tk), :], v_ref[pl.ds(st, tk), :]
            p = jnp.exp(_nt(qv, ks) - lse)
            ds = p * (_nt(dov, vs) - delta)
            acc_ref[...] += jnp.dot(ds.astype(BF), ks, preferred_element_type=F32)
            return carry

        lax.fori_loop(0, T // tk, step, 0)
        dq_ref[...] = acc_ref[...].reshape(G, tq, HD)
        dl_ref[...] = delta.reshape(G, tq, 1)

    qs = pl.BlockSpec((G, tq, HD), lambda g, i: (g, i, 0))
    kvs = pl.BlockSpec((None, T, HD), lambda g, i: (g, 0, 0))
    cs = pl.BlockSpec((G, tq, 1), lambda g, i: (g, i, 0))
    return _pcall(
        body, name="attn_full_dq", grid=(Hk, T // tq),
        in_specs=[qs, kvs, kvs, qs, qs, cs],
        out_specs=[qs, cs],
        out_shape=[_sds((Hq, T, HD), F32), _sds((Hq, T, 1), F32)],
        scratch_shapes=[pltpu.VMEM((R, HD), F32)],
        compiler_params=_cp("parallel", "parallel"),
    )(q, k, v, o, do, lse)


def _attn_full_dkv(q, k, v, do, lse, delta):
    Hq, T, _ = q.shape
    Hk = k.shape[0]
    G = Hq // Hk
    tk, tq = min(256, T), min(1024, T)
    nq = T // tq
    lse_r, dl_r = _rows(lse, tq), _rows(delta, tq)

    def body(k_ref, v_ref, q_ref, do_ref, lse_ref, dl_ref, dk_ref, dv_ref, dk_acc, dv_acc):
        h = pl.program_id(2)

        @pl.when(h == 0)
        def _():
            dk_acc[...] = jnp.zeros_like(dk_acc)
            dv_acc[...] = jnp.zeros_like(dv_acc)

        kv, vv = k_ref[...], v_ref[...]

        def step(i, carry):
            st = pl.multiple_of(i * tq, tq)
            qs, dos = q_ref[pl.ds(st, tq), :], do_ref[pl.ds(st, tq), :]
            pT = jnp.exp(_nt(kv, qs) - lse_ref[i])
            dv_acc[...] += jnp.dot(pT.astype(BF), dos, preferred_element_type=F32)
            dsT = pT * (_nt(vv, dos) - dl_ref[i])
            dk_acc[...] += jnp.dot(dsT.astype(BF), qs, preferred_element_type=F32)
            return carry

        lax.fori_loop(0, nq, step, 0)

        @pl.when(h == G - 1)
        def _():
            dk_ref[...] = dk_acc[...]
            dv_ref[...] = dv_acc[...]

    kvs = pl.BlockSpec((None, tk, HD), lambda g, j, h: (g, j, 0))
    qs = pl.BlockSpec((None, T, HD), lambda g, j, h: (g * G + h, 0, 0))
    rs = pl.BlockSpec((None, nq, 1, tq), lambda g, j, h: (g * G + h, 0, 0, 0))
    return _pcall(
        body, name="attn_full_dkv", grid=(Hk, T // tk, G),
        in_specs=[kvs, kvs, qs, qs, rs, rs],
        out_specs=[kvs, kvs],
        out_shape=[_sds((Hk, T, HD), F32)] * 2,
        scratch_shapes=[pltpu.VMEM((tk, HD), F32)] * 2,
        compiler_params=_cp("parallel", "parallel", "arbitrary"),
    )(k, v, q, do, lse_r, dl_r)


def _band_offsets(radius):
    offs = [0]
    for r in range(1, radius + 1):
        offs += [-r, r]
    return offs


def _dilated_bias(t):
    radius = max(w // 2 for w, _ in A_CONFIGS) // t
    tabs = []
    i = np.arange(t)
    for off in _band_offsets(radius):
        d = off * t + i[None, :] - i[:, None]
        mult = np.zeros((t, t), np.float32)
        for w, dil in A_CONFIGS:
            mult += ((d % dil) == 0) & (np.abs(d) <= w // 2)
        tabs.append(np.where(mult > 0, np.log(np.maximum(mult, 1.0)), NEG).astype(np.float32))
    return jnp.asarray(np.stack(tabs)[None]), radius


def _nbr_index(t):
    rpt = t // GRID_W
    i = np.arange(t)
    c0 = np.clip(i % GRID_W - 8, 0, GRID_W - 16)
    col_ok = ((i[None, :] % GRID_W) >= c0[:, None]) & ((i[None, :] % GRID_W) < c0[:, None] + 16)
    oks = []
    for off in _band_offsets(1):
        dr = (i[None, :] // GRID_W) - (i[:, None] // GRID_W) + rpt * off
        oks.append(col_ok & (np.abs(dr) <= 7))
    return np.stack(oks)


def _nbr_bias(rpb, t):
    rpt = t // GRID_W
    e1, e2 = _rpb_fold_matrices(t)
    ok = _nbr_index(t)
    padded = jnp.pad(rpb, ((0, 0), (0, 1), (0, 128 - rpb.shape[2]))).reshape(64, 128)

    def body(e2t_ref, rpb_ref, e1t_ref, out_ref):
        picked = jnp.dot(e2t_ref[...], rpb_ref[...], precision=HI, preferred_element_type=F32)
        out_ref[...] = jnp.dot(picked, e1t_ref[...], precision=HI, preferred_element_type=F32)

    n = e2.shape[1]
    sub = _pcall(body, name="rpb_expand", out_shape=_sds((n, GRID_W * GRID_W), F32),
                 compiler_params=pltpu.CompilerParams(vmem_limit_bytes=VMEM_LIMIT))(e2.T, padded, e1.T)
    tiles = sub.reshape(4, 3, rpt, rpt, GRID_W, GRID_W).transpose(0, 1, 2, 4, 3, 5).reshape(4, 3, t, t)
    return jnp.where(ok[None], tiles, NEG)


def _nbr_mask(qi, kb, t, rows, q_on_lanes):
    rpt = t // GRID_W
    qshape, kshape = ((1, t), (t, 1)) if q_on_lanes else ((t, 1), (1, t))
    rq = rpt * qi + lax.broadcasted_iota(jnp.int32, qshape, 1 if q_on_lanes else 0) // GRID_W
    rk = rpt * kb + lax.broadcasted_iota(jnp.int32, kshape, 0 if q_on_lanes else 1) // GRID_W
    r0 = jnp.clip(rq - 4, 0, rows - 8)
    return (rk >= r0) & (rk < r0 + 8)


def _attn_band_fwd(q, k, v, bias, radius, rowmask):
    H, T, _ = q.shape
    t = BAND_T
    nq = T // t
    Hb = bias.shape[0]
    offs = _band_offsets(radius)
    rows = T // GRID_W

    def body(q_ref, k_ref, v_ref, b_ref, o_ref, lse_ref, m_ref, l_ref, acc_ref):
        i = pl.program_id(1)
        qv = q_ref[...]
        m_ref[...] = jnp.full((t, 1), NEG, F32)
        l_ref[...] = jnp.zeros((t, 1), F32)
        acc_ref[...] = jnp.zeros((t, HD), F32)

        def tile(o, off):
            kb = i + off
            st = pl.multiple_of(kb * t, t)
            ks, vs = k_ref[pl.ds(st, t), :], v_ref[pl.ds(st, t), :]
            s = _nt(qv, ks) + b_ref[o]
            if rowmask:
                s = jnp.where(_nbr_mask(i, kb, t, rows, False), s, NEG)
            m_old = m_ref[...]
            m_new = jnp.maximum(m_old, jnp.max(s, axis=-1, keepdims=True))
            a = jnp.exp(m_old - m_new)
            p = jnp.exp(s - m_new)
            l_ref[...] = a * l_ref[...] + jnp.sum(p, axis=-1, keepdims=True)
            acc_ref[...] = a * acc_ref[...] + jnp.dot(p.astype(BF), vs, preferred_element_type=F32)
            m_ref[...] = m_new

        for o, off in enumerate(offs):
            if off == 0:
                tile(o, off)
            else:
                pl.when((i + off >= 0) & (i + off < nq))(functools.partial(tile, o, off))
        o_ref[...] = (acc_ref[...] / l_ref[...]).astype(o_ref.dtype)
        lse_ref[...] = m_ref[...] + jnp.log(l_ref[...])

    qs = pl.BlockSpec((None, t, HD), lambda h, i: (h, i, 0))
    kvs = pl.BlockSpec((None, T, HD), lambda h, i: (h, 0, 0))
    bs = pl.BlockSpec((None, len(offs), t, t), lambda h, i: (h if Hb > 1 else 0, 0, 0, 0))
    return _pcall(
        body, name="attn_band_fwd_c" if rowmask else "attn_band_fwd_a", grid=(H, nq),
        in_specs=[qs, kvs, kvs, bs],
        out_specs=[qs, pl.BlockSpec((None, t, 1), lambda h, i: (h, i, 0))],
        out_shape=[_sds((H, T, HD), BF), _sds((H, T, 1), F32)],
        scratch_shapes=[pltpu.VMEM((t, 1), F32), pltpu.VMEM((t, 1), F32), pltpu.VMEM((t, HD), F32)],
        compiler_params=_cp("parallel", "parallel"),
    )(q, k, v, bias)


def _attn_band_dq(q, k, v, o, do, lse, bias, radius, rowmask):
    H, T, _ = q.shape
    t = BAND_T
    nq = T // t
    Hb = bias.shape[0]
    offs = _band_offsets(radius)
    rows = T // GRID_W

    def body(q_ref, k_ref, v_ref, o_ref, do_ref, lse_ref, b_ref, dq_ref, dl_ref, *rest):
        db_ref = rest[0] if rowmask else None
        acc_ref = rest[-1]
        i = pl.program_id(1)
        qv, dov, lse = q_ref[...], do_ref[...], lse_ref[...]
        delta = jnp.sum(dov.astype(F32) * o_ref[...].astype(F32), axis=-1, keepdims=True)
        acc_ref[...] = jnp.zeros((t, HD), F32)
        if rowmask:
            @pl.when(i == 0)
            def _():
                db_ref[...] = jnp.zeros_like(db_ref)

        def tile(o, off):
            kb = i + off
            st = pl.multiple_of(kb * t, t)
            ks, vs = k_ref[pl.ds(st, t), :], v_ref[pl.ds(st, t), :]
            s = _nt(qv, ks) + b_ref[o]
            if rowmask:
                s = jnp.where(_nbr_mask(i, kb, t, rows, False), s, NEG)
            p = jnp.exp(s - lse)
            ds = p * (_nt(dov, vs) - delta)
            acc_ref[...] += jnp.dot(ds.astype(BF), ks, preferred_element_type=F32)
            if rowmask:
                db_ref[o] += ds

        for o, off in enumerate(offs):
            if off == 0:
                tile(o, off)
            else:
                pl.when((i + off >= 0) & (i + off < nq))(functools.partial(tile, o, off))
        dq_ref[...] = acc_ref[...]
        dl_ref[...] = delta

    qs = pl.BlockSpec((None, t, HD), lambda h, i: (h, i, 0))
    kvs = pl.BlockSpec((None, T, HD), lambda h, i: (h, 0, 0))
    cs = pl.BlockSpec((None, t, 1), lambda h, i: (h, i, 0))
    bs = pl.BlockSpec((None, len(offs), t, t), lambda h, i: (h if Hb > 1 else 0, 0, 0, 0))
    out_specs = [qs, cs]
    out_shape = [_sds((H, T, HD), F32), _sds((H, T, 1), F32)]
    if rowmask:
        out_specs.append(pl.BlockSpec((None, len(offs), t, t), lambda h, i: (h, 0, 0, 0)))
        out_shape.append(_sds((H, len(offs), t, t), F32))
    return _pcall(
        body, name="attn_band_dq_c" if rowmask else "attn_band_dq_a", grid=(H, nq),
        in_specs=[qs, kvs, kvs, qs, qs, cs, bs],
        out_specs=out_specs, out_shape=out_shape,
        scratch_shapes=[pltpu.VMEM((t, HD), F32)],
        compiler_params=_cp("parallel", "arbitrary"),
    )(q, k, v, o, do, lse, bias)


def _attn_band_dkv(q, k, v, do, lse, delta, bias_t, radius, rowmask):
    H, T, _ = q.shape
    t = BAND_T
    nq = T // t
    Hb = bias_t.shape[0]
    offs = _band_offsets(radius)
    rows = T // GRID_W
    lse_r, dl_r = _rows(lse, t), _rows(delta, t)

    def body(k_ref, v_ref, q_ref, do_ref, lse_ref, dl_ref, b_ref, dk_ref, dv_ref, dk_acc, dv_acc):
        jb = pl.program_id(1)
        kv, vv = k_ref[...], v_ref[...]
        dk_acc[...] = jnp.zeros((t, HD), F32)
        dv_acc[...] = jnp.zeros((t, HD), F32)

        def tile(o, off):
            qi = jb - off
            st = pl.multiple_of(qi * t, t)
            qs, dos = q_ref[pl.ds(st, t), :], do_ref[pl.ds(st, t), :]
            sT = _nt(kv, qs) + b_ref[o]
            if rowmask:
                sT = jnp.where(_nbr_mask(qi, jb, t, rows, True), sT, NEG)
            pT = jnp.exp(sT - lse_ref[qi])
            dv_acc[...] += jnp.dot(pT.astype(BF), dos, preferred_element_type=F32)
            dsT = pT * (_nt(vv, dos) - dl_ref[qi])
            dk_acc[...] += jnp.dot(dsT.astype(BF), qs, preferred_element_type=F32)

        for o, off in enumerate(offs):
            if off == 0:
                tile(o, off)
            else:
                pl.when((jb - off >= 0) & (jb - off < nq))(functools.partial(tile, o, off))
        dk_ref[...] = dk_acc[...]
        dv_ref[...] = dv_acc[...]

    kvs = pl.BlockSpec((None, t, HD), lambda h, j: (h, j, 0))
    qs = pl.BlockSpec((None, T, HD), lambda h, j: (h, 0, 0))
    rs = pl.BlockSpec((None, nq, 1, t), lambda h, j: (h, 0, 0, 0))
    bs = pl.BlockSpec((None, len(offs), t, t), lambda h, j: (h if Hb > 1 else 0, 0, 0, 0))
    return _pcall(
        body, name="attn_band_dkv_c" if rowmask else "attn_band_dkv_a", grid=(H, nq),
        in_specs=[kvs, kvs, qs, qs, rs, rs, bs],
        out_specs=[kvs, kvs],
        out_shape=[_sds((H, T, HD), F32)] * 2,
        scratch_shapes=[pltpu.VMEM((t, HD), F32)] * 2,
        compiler_params=_cp("parallel", "parallel"),
    )(k, v, q, do, lse_r, dl_r, bias_t)


def _rpb_fold_matrices(t):
    rpt = t // GRID_W
    e1 = np.zeros((GRID_W * GRID_W, 128), np.float32)
    ic, jc = np.meshgrid(np.arange(GRID_W), np.arange(GRID_W), indexing="ij")
    dc = (jc - ic + 15).reshape(-1)
    keep = (dc >= 0) & (dc <= 30)
    e1[np.arange(GRID_W * GRID_W)[keep], dc[keep]] = 1.0
    offs = _band_offsets(1)
    n = 4 * len(offs) * rpt * rpt
    e2 = np.zeros((64, n), np.float32)
    col = 0
    for h in range(4):
        for off in offs:
            for ib in range(rpt):
                for jb in range(rpt):
                    dr = jb - ib + rpt * off + 7
                    if 0 <= dr <= 14:
                        e2[h * 16 + dr, col] = 1.0
                    col += 1
    return jnp.asarray(e1), jnp.asarray(e2)


def _rpb_grad(dbias):
    t = dbias.shape[-1]
    rpt = t // GRID_W
    e1, e2 = _rpb_fold_matrices(t)
    sub = dbias.reshape(4, 3, rpt, GRID_W, rpt, GRID_W).transpose(0, 1, 2, 4, 3, 5)
    sub = sub.reshape(4 * 3 * rpt * rpt, GRID_W * GRID_W)

    def body(e2_ref, sub_ref, e1_ref, out_ref):
        diag = jnp.dot(sub_ref[...], e1_ref[...], precision=HI, preferred_element_type=F32)
        out_ref[...] = jnp.dot(e2_ref[...], diag, precision=HI, preferred_element_type=F32)

    out = _pcall(body, name="rpb_fold", out_shape=_sds((64, 128), F32),
                 compiler_params=pltpu.CompilerParams(vmem_limit_bytes=VMEM_LIMIT))(e2, sub, e1)
    return out.reshape(4, 16, 128)[:, :15, :31]


def _sigmoid(z):
    return 1.0 / (1.0 + jnp.exp(-z))


def _merge_fwd(oa, ob, oc, hg, bg, wa, wb, wc):
    T = oa.shape[0]
    tt = min(512, T)

    def body(oa_ref, ob_ref, oc_ref, hg_ref, bg_ref, wa_ref, wb_ref, wc_ref, out_ref):
        acc = None
        for k, (o_ref, w_ref) in enumerate(((oa_ref, wa_ref), (ob_ref, wb_ref), (oc_ref, wc_ref))):
            y = jnp.dot(o_ref[...], w_ref[...], preferred_element_type=F32)
            g = _sigmoid(hg_ref[:, D * k:D * (k + 1)] + bg_ref[:, D * k:D * (k + 1)])
            acc = g * y if acc is None else acc + g * y
        out_ref[...] = acc.astype(out_ref.dtype)

    row = lambda w: pl.BlockSpec((tt, w), lambda i: (i, 0))
    const = lambda a: pl.BlockSpec(a.shape, lambda i: (0, 0))
    return _pcall(
        body, name="merge_fwd", grid=(T // tt,),
        in_specs=[row(A_W), row(BQ_W), row(C_W), row(GATE), const(bg), const(wa), const(wb), const(wc)],
        out_specs=row(D), out_shape=_sds((T, D), BF),
        compiler_params=_cp("parallel"),
    )(oa, ob, oc, hg, bg, wa, wb, wc)


def _merge_bwd(dm, oa, ob, oc, hg, bg, wa, wb, wc):
    T = oa.shape[0]
    tt = min(256, T)

    def body(dm_ref, oa_ref, ob_ref, oc_ref, hg_ref, bg_ref, wa_ref, wb_ref, wc_ref,
             dya, dyb, dyc, doa, dob, doc, dhg, dbg):
        @pl.when(pl.program_id(0) == 0)
        def _():
            dbg[...] = jnp.zeros_like(dbg)

        dmv = dm_ref[...]
        for k, (o_ref, w_ref, dy_ref, do_ref) in enumerate(
                ((oa_ref, wa_ref, dya, doa), (ob_ref, wb_ref, dyb, dob), (oc_ref, wc_ref, dyc, doc))):
            sl = slice(D * k, D * (k + 1))
            y = jnp.dot(o_ref[...], w_ref[...], preferred_element_type=F32)
            g = _sigmoid(hg_ref[:, sl] + bg_ref[:, sl])
            dy = (dmv * g).astype(BF)
            dy_ref[...] = dy
            do_ref[...] = _nt(dy, w_ref[...]).astype(do_ref.dtype)
            dz = dmv * y * (g * (1.0 - g))
            dhg[:, sl] = dz.astype(dhg.dtype)
            dbg[:, sl] += jnp.sum(dz, axis=0, keepdims=True)

    row = lambda w: pl.BlockSpec((tt, w), lambda i: (i, 0))
    const = lambda a: pl.BlockSpec(a.shape, lambda i: (0, 0))
    return _pcall(
        body, name="merge_bwd", grid=(T // tt,),
        in_specs=[row(D), row(A_W), row(BQ_W), row(C_W), row(GATE), const(bg), const(wa), const(wb), const(wc)],
        out_specs=[row(D)] * 3 + [row(A_W), row(BQ_W), row(C_W), row(GATE),
                                  pl.BlockSpec((1, GATE), lambda i: (0, 0))],
        out_shape=[_sds((T, D), BF)] * 3 + [_sds((T, A_W), BF), _sds((T, BQ_W), BF), _sds((T, C_W), BF),
                                            _sds((T, GATE), BF), _sds((1, GATE), F32)],
        compiler_params=_cp("arbitrary"),
    )(dm, oa, ob, oc, hg, bg, wa, wb, wc)


def _lin_ln(a, w, res, g, b):
    T, K = a.shape
    tt = min(256, T)

    def body(a_ref, w_ref, res_ref, g_ref, b_ref, y_ref, yb_ref, xh_ref, rs_ref):
        u = ALPHA * res_ref[...] + jnp.dot(a_ref[...], w_ref[...], preferred_element_type=F32)
        mu = jnp.mean(u, axis=-1, keepdims=True)
        c = u - mu
        r = lax.rsqrt(jnp.mean(c * c, axis=-1, keepdims=True) + LN_EPS)
        xh = c * r
        y = xh * g_ref[...] + b_ref[...]
        y_ref[...] = y
        yb_ref[...] = y.astype(BF)
        xh_ref[...] = xh
        rs_ref[...] = r

    row = lambda w_: pl.BlockSpec((tt, w_), lambda i: (i, 0))
    const = lambda s: pl.BlockSpec(s, lambda i: (0, 0))
    return _pcall(
        body, name="lin_ln", grid=(T // tt,),
        in_specs=[row(K), const((K, D)), row(D), const((1, D)), const((1, D))],
        out_specs=[row(D), row(D), row(D), row(1)],
        out_shape=[_sds((T, D), F32), _sds((T, D), BF), _sds((T, D), F32), _sds((T, 1), F32)],
        compiler_params=_cp("parallel"),
    )(a, w, res, g, b)


def _ln_bwd(dy, xh, rs, g):
    T = dy.shape[0]
    tt = min(512, T)

    def body(dy_ref, xh_ref, rs_ref, g_ref, du_ref, dub_ref, dg_ref, db_ref):
        @pl.when(pl.program_id(0) == 0)
        def _():
            dg_ref[...] = jnp.zeros_like(dg_ref)
            db_ref[...] = jnp.zeros_like(db_ref)

        dyv, xhv = dy_ref[...], xh_ref[...]
        dg_ref[...] += jnp.sum(dyv * xhv, axis=0, keepdims=True)
        db_ref[...] += jnp.sum(dyv, axis=0, keepdims=True)
        dxh = dyv * g_ref[...]
        m1 = jnp.mean(dxh, axis=-1, keepdims=True)
        m2 = jnp.mean(dxh * xhv, axis=-1, keepdims=True)
        du = rs_ref[...] * (dxh - m1 - xhv * m2)
        du_ref[...] = du
        dub_ref[...] = du.astype(BF)

    row = lambda w_: pl.BlockSpec((tt, w_), lambda i: (i, 0))
    const = lambda s: pl.BlockSpec(s, lambda i: (0, 0))
    return _pcall(
        body, name="ln_bwd", grid=(T // tt,),
        in_specs=[row(D), row(D), row(1), const((1, D))],
        out_specs=[row(D), row(D), const((1, D)), const((1, D))],
        out_shape=[_sds((T, D), F32), _sds((T, D), BF), _sds((1, D), F32), _sds((1, D), F32)],
        compiler_params=_cp("arbitrary"),
    )(dy, xh, rs, g)


def _loss_grad(y, tgt):
    T = y.shape[0]
    tt = min(512, T)

    def body(y_ref, t_ref, dy_ref, sq_ref):
        @pl.when(pl.program_id(0) == 0)
        def _():
            sq_ref[...] = jnp.zeros_like(sq_ref)

        e = y_ref[...] - t_ref[...]
        dy_ref[...] = e * (1.0 / D)
        sq_ref[...] += jnp.sum(e * e, axis=0, keepdims=True)

    row = pl.BlockSpec((tt, D), lambda i: (i, 0))
    return _pcall(
        body, name="loss_grad", grid=(T // tt,),
        in_specs=[row, row], out_specs=[row, pl.BlockSpec((1, D), lambda i: (0, 0))],
        out_shape=[_sds((T, D), F32), _sds((1, D), F32)],
        compiler_params=_cp("arbitrary"),
    )(y, tgt)


def _position():
    return lax.axis_index("x"), lax.axis_index("y"), lax.axis_index("c")


def _all_gather(xs, name):
    n = len(xs)
    hbm = pl.BlockSpec(memory_space=pl.ANY)

    def body(*refs):
        x_refs, out_refs = refs[:n], refs[n:2 * n]
        send, recv, loc = refs[2 * n:]
        x, y, c = _position()
        me, sib = (x, y, c), (x, y, 1 - c)
        chips = [(1 - x, y), (x, 1 - y), (1 - x, 1 - y)]

        def copy(a, k, block, to, src=None):
            px, py, pc = block
            dst = out_refs[a].at[4 * px + 2 * py + pc]
            return pltpu.make_async_remote_copy(
                src_ref=dst if src is None else src, dst_ref=dst,
                send_sem=send.at[a, k], recv_sem=recv.at[a, k], device_id=to, device_id_type=MESH)

        mine = [pltpu.make_async_copy(x_refs[a], out_refs[a].at[4 * x + 2 * y + c], loc.at[a]) for a in range(n)]
        for cp in mine:
            cp.start()
        first = []
        for a in range(n):
            first.append(copy(a, 0, me, sib, src=x_refs[a]))
            first += [copy(a, 1 + j, me, (*chip, c), src=x_refs[a]) for j, chip in enumerate(chips)]
        for cp in first:
            cp.start()
        passed = []
        for j, chip in enumerate(chips):
            for a in range(n):
                copy(a, 1 + j, (*chip, c), me).wait_recv()
                fwd = copy(a, 4 + j, (*chip, c), sib)
                fwd.start()
                passed.append(fwd)
        for a in range(n):
            copy(a, 0, sib, me).wait_recv()
            for j, chip in enumerate(chips):
                copy(a, 4 + j, (*chip, 1 - c), me).wait_recv()
        for cp in first + passed:
            cp.wait_send()
        for cp in mine:
            cp.wait()

    return _pcall_comm(
        body, name=name,
        in_specs=[hbm] * n, out_specs=[hbm] * n,
        out_shape=[_sds((8,) + x.shape, x.dtype) for x in xs],
        scratch_shapes=[pltpu.SemaphoreType.DMA((n, 7)), pltpu.SemaphoreType.DMA((n, 7)),
                        pltpu.SemaphoreType.DMA((n,))],
    )(*xs)


def _exchange_pair(gs, name):
    n = len(gs)
    hbm = pl.BlockSpec(memory_space=pl.ANY)

    def body(*refs):
        g_refs, out_refs = refs[:n], refs[n:2 * n]
        send, recv = refs[2 * n:]
        x, y, c = _position()
        copies = []
        for a in range(n):
            for p in range(4):
                copies.append(pltpu.make_async_remote_copy(
                    src_ref=g_refs[a].at[p, 1 - c], dst_ref=out_refs[a].at[p],
                    send_sem=send.at[a, p], recv_sem=recv.at[a, p],
                    device_id=(x, y, 1 - c), device_id_type=MESH))
        for cp in copies:
            cp.start()
        for cp in copies:
            cp.wait()

    return _pcall_comm(
        body, name=name,
        in_specs=[hbm] * n, out_specs=[hbm] * n,
        out_shape=[_sds((4,) + g.shape[2:], g.dtype) for g in gs],
        scratch_shapes=[pltpu.SemaphoreType.DMA((n, 4)), pltpu.SemaphoreType.DMA((n, 4))],
    )(*gs)


def _exchange_chips(ps, name):
    n = len(ps)
    hbm = pl.BlockSpec(memory_space=pl.ANY)

    def body(*refs):
        p_refs, out_refs = refs[:n], refs[n:2 * n]
        send, recv = refs[2 * n:]
        x, y, c = _position()
        chips = [(1 - x, y), (x, 1 - y), (1 - x, 1 - y)]
        copies = []
        for a in range(n):
            for j, (px, py) in enumerate(chips):
                copies.append(pltpu.make_async_remote_copy(
                    src_ref=p_refs[a].at[2 * px + py], dst_ref=out_refs[a].at[j],
                    send_sem=send.at[a, j], recv_sem=recv.at[a, j],
                    device_id=(px, py, c), device_id_type=MESH))
        for cp in copies:
            cp.start()
        for cp in copies:
            cp.wait()

    return _pcall_comm(
        body, name=name,
        in_specs=[hbm] * n, out_specs=[hbm] * n,
        out_shape=[_sds((3,) + p.shape[1:], p.dtype) for p in ps],
        scratch_shapes=[pltpu.SemaphoreType.DMA((n, 3)), pltpu.SemaphoreType.DMA((n, 3))],
    )(*ps)


def _pair_sum(g, got, core):
    _, _, R, C = g.shape
    tr = min(512, R)

    def body(core_ref, g_ref, r_ref, out_ref):
        out_ref[...] = g_ref[...] + r_ref[...]

    return _pcall(
        body, name="pair_sum",
        grid_spec=pltpu.PrefetchScalarGridSpec(
            num_scalar_prefetch=1, grid=(4, R // tr),
            in_specs=[pl.BlockSpec((None, None, tr, C), lambda p, i, cr: (p, cr[0], i, 0)),
                      pl.BlockSpec((None, tr, C), lambda p, i, cr: (p, i, 0))],
            out_specs=pl.BlockSpec((None, tr, C), lambda p, i, cr: (p, i, 0))),
        out_shape=_sds((4, R, C), F32),
        compiler_params=_cp("parallel", "parallel"),
    )(core, g, got)


def _adamw_math(w, g, m, v):
    m = B1 * m + (1.0 - B1) * g
    v = B2 * v + (1.0 - B2) * (g * g)
    m_hat = m / (1.0 - B1 ** STEP)
    v_hat = v / (1.0 - B2 ** STEP)
    delta = -LR * (m_hat / (jnp.sqrt(v_hat) + EPS) + WD * w)
    return delta, m, v


def _chip_sum_adamw(p, got, chip, w, m, v):
    R, C = w.shape
    tr = min(512, R)

    def body(chip_ref, p_ref, r_ref, w_ref, m_ref, v_ref, g_out, d_out, m_out, v_out):
        g = ((p_ref[...] + r_ref[0]) + r_ref[1]) + r_ref[2]
        d, mn, vn = _adamw_math(w_ref[...], g, m_ref[...], v_ref[...])
        g_out[...], d_out[...], m_out[...], v_out[...] = g, d, mn, vn

    blk = pl.BlockSpec((tr, C), lambda i, ch: (i, 0))
    return _pcall(
        body, name="chip_sum_adamw",
        grid_spec=pltpu.PrefetchScalarGridSpec(
            num_scalar_prefetch=1, grid=(R // tr,),
            in_specs=[pl.BlockSpec((None, tr, C), lambda i, ch: (ch[0], i, 0)),
                      pl.BlockSpec((3, tr, C), lambda i, ch: (0, i, 0)), blk, blk, blk],
            out_specs=[blk] * 4),
        out_shape=[_sds((R, C), F32)] * 4,
        compiler_params=_cp("parallel"),
    )(chip, p, got, w, m, v)


def _small_sum_adamw(parts, w, m, v):
    _, R, C = parts.shape

    def body(p_ref, w_ref, m_ref, v_ref, g_out, d_out, m_out, v_out):
        g = p_ref[0]
        for k in range(1, 8):
            g = g + p_ref[k]
        d, mn, vn = _adamw_math(w_ref[...], g, m_ref[...], v_ref[...])
        g_out[...], d_out[...], m_out[...], v_out[...] = g, d, mn, vn

    return _pcall(body, name="small_sum_adamw", out_shape=[_sds((R, C), F32)] * 4,
                  compiler_params=pltpu.CompilerParams(vmem_limit_bytes=VMEM_LIMIT))(parts, w, m, v)


BIG = ("w_in", "w_branch_a", "w_branch_b", "w_branch_c", "w_out", "w_up", "w_down")
ROW_SHARDED = ("w_out", "w_down")
SMALL = ("b_gate", "q_norm_b", "k_norm_b", "rpb_c", "ln1_g", "ln1_b", "ln2_g", "ln2_b")
NAMES = ("w_in", "b_gate", "q_norm_b", "k_norm_b", "rpb_c", "w_branch_a", "w_branch_b", "w_branch_c",
         "w_out", "ln1_g", "ln1_b", "w_up", "w_down", "ln2_g", "ln2_b")


def _full_weight(gathered, name, layer):
    blk = gathered.reshape(8, DEPTH, gathered.shape[1] // DEPTH, gathered.shape[2])[:, layer]
    if name in ROW_SHARDED:
        return blk.reshape(-1, blk.shape[2])
    return blk.transpose(1, 0, 2).reshape(blk.shape[1], -1)


def _chunks(grad, name):
    if name in ROW_SHARDED:
        return grad.reshape(8, grad.shape[0] // 8, grad.shape[1])
    return grad.reshape(grad.shape[0], 8, grad.shape[1] // 8).transpose(1, 0, 2)


def _layer_fwd(x, xb, W, P, tabs, gm, bias_a, radius_a):
    hq, = _mm(xb, W["w_qkv"], "nn", [F32], 1024, 768, 1024, name="in_qkv")
    hg, = _mm(xb, W["w_gate"], "nn", [F32], 1024, 1024, 1024, name="in_gate")
    tab_a, tab_b = tabs
    prepped = _prep_fwd(hq, tab_a, tab_b, P["qn"], P["kn"], gm)
    qa, ka, va, qb, kb, vb, qc, kc, vc = [_to_heads(t) for t in prepped]
    oa, lse_a = _attn_band_fwd(qa, ka, va, bias_a, radius_a, False)
    ob, lse_b = _attn_full_fwd(qb, kb, vb)
    bias_c = _nbr_bias(P["rpb"], BAND_T)
    oc, lse_c = _attn_band_fwd(qc, kc, vc, bias_c, 1, True)
    oa_t, ob_t, oc_t = _from_heads(oa), _from_heads(ob), _from_heads(oc)
    merged = _merge_fwd(oa_t, ob_t, oc_t, hg, P["bg"], W["w_branch_a"], W["w_branch_b"], W["w_branch_c"])
    x1, x1b, xh1, rs1 = _lin_ln(merged, W["w_out"], x, P["ln1_g"], P["ln1_b"])

    def relu2(acc):
        r = jnp.maximum(acc, 0.0)
        return r * r, r

    f, r = _mm(x1b, W["w_up"], "nn", [BF, BF], 1024, 1024, 1024, epilogue=relu2, name="mlp_up")
    x2, x2b, xh2, rs2 = _lin_ln(f, W["w_down"], x1, P["ln2_g"], P["ln2_b"])
    saved = dict(xb=xb, hq=hq, hg=hg, qkv=(qa, ka, va, qb, kb, vb, qc, kc, vc), o=(oa, ob, oc),
                 lse=(lse_a, lse_b, lse_c), o_t=(oa_t, ob_t, oc_t), bias_c=bias_c, merged=merged,
                 xh1=xh1, rs1=rs1, x1b=x1b, f=f, r=r, xh2=xh2, rs2=rs2)
    return x2, x2b, saved


def _layer_bwd(dx2, S, W, P, tabs, gm, bias_a, bias_a_t, radius_a):
    G = {}
    du2, du2b, G["ln2_g"], G["ln2_b"] = _ln_bwd(dx2, S["xh2"], S["rs2"], P["ln2_g"])
    G["w_down"], = _mm(S["f"], du2b, "tn", [F32], 1024, 1024, 512, name="dw_down")
    da, = _mm(du2b, W["w_down"], "nt", [BF], 1024, 1024, 1024,
              epilogue=lambda acc, r: (acc * (2.0 * r.astype(F32)),), extras=(S["r"],), name="d_act")
    G["w_up"], = _mm(S["x1b"], da, "tn", [F32], 1024, 1024, 512, name="dw_up")
    dx1, = _mm(da, W["w_up"], "nt", [F32], 1024, 1024, 1024,
               epilogue=lambda acc, d: (ALPHA * d + acc,), extras=(du2,), name="dx_mlp")
    du1, du1b, G["ln1_g"], G["ln1_b"] = _ln_bwd(dx1, S["xh1"], S["rs1"], P["ln1_g"])
    G["w_out"], = _mm(S["merged"], du1b, "tn", [F32], 1024, 1024, 512, name="dw_out")
    dm, = _mm(du1b, W["w_out"], "nt", [F32], 1024, 1024, 1024, name="d_merged")
    oa_t, ob_t, oc_t = S["o_t"]
    dya, dyb, dyc, doa, dob, doc, dhg, G["b_gate"] = _merge_bwd(
        dm, oa_t, ob_t, oc_t, S["hg"], P["bg"], W["w_branch_a"], W["w_branch_b"], W["w_branch_c"])
    G["w_branch_a"], = _mm(oa_t, dya, "tn", [F32], 256, 1024, 512, name="dw_branch_a")
    G["w_branch_b"], = _mm(ob_t, dyb, "tn", [F32], 512, 1024, 512, name="dw_branch_b")
    G["w_branch_c"], = _mm(oc_t, dyc, "tn", [F32], 256, 1024, 512, name="dw_branch_c")

    qa, ka, va, qb, kb, vb, qc, kc, vc = S["qkv"]
    oa, ob, oc = S["o"]
    lse_a, lse_b, lse_c = S["lse"]
    doa_h, dob_h, doc_h = _to_heads(doa), _to_heads(dob), _to_heads(doc)
    dqa, dl_a = _attn_band_dq(qa, ka, va, oa, doa_h, lse_a, bias_a, radius_a, False)
    dka, dva = _attn_band_dkv(qa, ka, va, doa_h, lse_a, dl_a, bias_a_t, radius_a, False)
    dqb, dl_b = _attn_full_dq(qb, kb, vb, ob, dob_h, lse_b)
    dkb, dvb = _attn_full_dkv(qb, kb, vb, dob_h, lse_b, dl_b)
    bias_c = S["bias_c"]
    dqc, dl_c, dbias_c = _attn_band_dq(qc, kc, vc, oc, doc_h, lse_c, bias_c, 1, True)
    dkc, dvc = _attn_band_dkv(qc, kc, vc, doc_h, lse_c, dl_c, bias_c.transpose(0, 1, 3, 2), 1, True)
    G["rpb_c"] = _rpb_grad(dbias_c)

    tab_a, tab_b = tabs
    grads = [_from_heads(t) for t in (dqa, dka, dva, dqb, dkb, dvb, dqc, dkc, dvc)]
    dhq, dqn, dkn = _prep_bwd(S["hq"], grads, tab_a, tab_b, P["qn"], P["kn"], gm)
    G["q_norm_b"] = dqn.reshape(BQ_W // HD, HD).sum(0)
    G["k_norm_b"] = dkn.reshape(BKV_W // HD, HD).sum(0)
    dw_qkv, = _mm(S["xb"], dhq, "tn", [F32], 1024, 768, 512, name="dw_qkv")
    dw_gate, = _mm(S["xb"], dhg, "tn", [F32], 1024, 1024, 512, name="dw_gate")
    G["w_in"] = jnp.concatenate([dw_qkv, dw_gate], axis=1)
    dx_a, = _mm(dhq, W["w_qkv"], "nt", [F32], 1024, 1024, 768,
                epilogue=lambda acc, d: (ALPHA * d + acc,), extras=(du1,), name="dx_qkv")
    dx, = _mm(dhg, W["w_gate"], "nt", [F32], 1024, 1024, 1024,
              epilogue=lambda acc, d: (d + acc,), extras=(dx_a,), name="dx_gate")
    return dx, G


def _pack_small(vals):
    flat = jnp.concatenate([vals[n].reshape(-1).astype(F32) for n in SMALL])
    pad = (-flat.shape[0]) % (8 * 128)
    return jnp.pad(flat, (0, pad)).reshape(-1, 128)


def _unpack_small(packed, like):
    flat, out, off = packed.reshape(-1), {}, 0
    for n in SMALL:
        size = math.prod(like[n].shape)
        out[n] = flat[off:off + size].reshape(like[n].shape)
        off += size
    return out


def kernel(x, w_in, b_gate, q_norm_b, k_norm_b, rpb_c, w_branch_a, w_branch_b, w_branch_c, w_out, ln1_g, ln1_b, w_up, w_down, ln2_g, ln2_b, loss_target, m_w_in, m_b_gate, m_q_norm_b, m_k_norm_b, m_rpb_c, m_w_branch_a, m_w_branch_b, m_w_branch_c, m_w_out, m_ln1_g, m_ln1_b, m_w_up, m_w_down, m_ln2_g, m_ln2_b, v_w_in, v_b_gate, v_q_norm_b, v_k_norm_b, v_rpb_c, v_w_branch_a, v_w_branch_b, v_w_branch_c, v_w_out, v_ln1_g, v_ln1_b, v_w_up, v_w_down, v_ln2_g, v_ln2_b):
    w = dict(w_in=w_in, b_gate=b_gate, q_norm_b=q_norm_b, k_norm_b=k_norm_b, rpb_c=rpb_c,
             w_branch_a=w_branch_a, w_branch_b=w_branch_b, w_branch_c=w_branch_c, w_out=w_out,
             ln1_g=ln1_g, ln1_b=ln1_b, w_up=w_up, w_down=w_down, ln2_g=ln2_g, ln2_b=ln2_b)
    m = dict(w_in=m_w_in, b_gate=m_b_gate, q_norm_b=m_q_norm_b, k_norm_b=m_k_norm_b, rpb_c=m_rpb_c,
             w_branch_a=m_w_branch_a, w_branch_b=m_w_branch_b, w_branch_c=m_w_branch_c, w_out=m_w_out,
             ln1_g=m_ln1_g, ln1_b=m_ln1_b, w_up=m_w_up, w_down=m_w_down, ln2_g=m_ln2_g, ln2_b=m_ln2_b)
    v = dict(w_in=v_w_in, b_gate=v_b_gate, q_norm_b=v_q_norm_b, k_norm_b=v_k_norm_b, rpb_c=v_rpb_c,
             w_branch_a=v_w_branch_a, w_branch_b=v_w_branch_b, w_branch_c=v_w_branch_c, w_out=v_w_out,
             ln1_g=v_ln1_g, ln1_b=v_ln1_b, w_up=v_w_up, w_down=v_w_down, ln2_g=v_ln2_g, ln2_b=v_ln2_b)
    T = x.shape[1]
    xc, yc, cc = _position()

    flat2 = lambda a: a.reshape(-1, a.shape[-1])
    gathered = _all_gather([flat2(w[n]).astype(BF) for n in BIG], "gather_weights")
    gathered = dict(zip(BIG, gathered))

    tabs = _rope_tables(T)
    gm = _group_mean_matrix()
    bias_a, radius_a = _dilated_bias(BAND_T)
    radius_a = min(radius_a, T // BAND_T)
    bias_a = bias_a[:, :2 * radius_a + 1]
    bias_a_t = bias_a.transpose(0, 1, 3, 2)

    Ws, Ps = [], []
    for l in range(DEPTH):
        W = {n: _full_weight(gathered[n], n, l) for n in BIG}
        W["w_qkv"], W["w_gate"] = W["w_in"][:, :QKV], W["w_in"][:, QKV:]
        Ws.append(W)
        Ps.append(dict(qn=jnp.tile(q_norm_b[l][None], (1, 2)), kn=jnp.tile(k_norm_b[l][None], (1, 2)),
                       rpb=rpb_c[l], bg=b_gate[l][None], ln1_g=ln1_g[l][None], ln1_b=ln1_b[l][None],
                       ln2_g=ln2_g[l][None], ln2_b=ln2_b[l][None]))

    h = x[0]
    hb = h.astype(BF)
    saved = []
    for l in range(DEPTH):
        h, hb, S = _layer_fwd(h, hb, Ws[l], Ps[l], tabs, gm, bias_a, radius_a)
        saved.append(S)
    dy, sq = _loss_grad(h, loss_target[0])
    loss = lax.psum(0.5 / D * jnp.sum(sq), AXES)
    grads = [None] * DEPTH
    for l in reversed(range(DEPTH)):
        dy, grads[l] = _layer_bwd(dy, saved[l], Ws[l], Ps[l], tabs, gm, bias_a, bias_a_t, radius_a)
    grad_x = dy[None]

    core = cc.reshape(1).astype(jnp.int32)
    chip = (2 * xc + yc).reshape(1).astype(jnp.int32)
    chunked = []
    for n in BIG:
        g = jnp.stack([_chunks(grads[l][n], n) for l in range(DEPTH)], axis=1)
        chunked.append(g.reshape(4, 2, DEPTH * g.shape[2], g.shape[3]))
    from_sibling = _exchange_pair(chunked, "grads_to_sibling")
    pair = [_pair_sum(g, r, core) for g, r in zip(chunked, from_sibling)]
    from_chips = _exchange_chips(pair, "grads_to_chips")
    out_g, out_d, out_m, out_v = {}, {}, {}, {}
    for n, p, r in zip(BIG, pair, from_chips):
        res = _chip_sum_adamw(p, r, chip, flat2(w[n]), flat2(m[n]), flat2(v[n]))
        out_g[n], out_d[n], out_m[n], out_v[n] = [t.reshape(w[n].shape) for t in res]

    part = _pack_small({n: jnp.stack([grads[l][n].reshape(w[n].shape[1:]) for l in range(DEPTH)]) for n in SMALL})
    parts, = _all_gather([part], "gather_small_grads")
    res = _small_sum_adamw(parts, _pack_small(w), _pack_small(m), _pack_small(v))
    for dst, packed in zip((out_g, out_d, out_m, out_v), res):
        dst.update(_unpack_small(packed, w))

    return (loss, grad_x, *[out_g[n] for n in NAMES], *[out_d[n] for n in NAMES],
            *[out_m[n] for n in NAMES], *[out_v[n] for n in NAMES])
```

```python
import functools
import math

import numpy as np
import jax
import jax.numpy as jnp
from jax import lax
from jax.experimental import pallas as pl
from jax.experimental.pallas import tpu as pltpu

F32 = jnp.float32
BF = jnp.bfloat16
HI = lax.Precision.HIGHEST
NEG = -1e30
MESH = pl.DeviceIdType.MESH
AXES = ("x", "y", "c")

D = 1024
DEPTH = 4
HD = 64
A_W, BQ_W, BKV_W, C_W = 256, 512, 128, 256
QKV = 2304
GATE = 3072
D_FF = 4096
GRID_W = 64
ALPHA = (2 * DEPTH) ** 0.25
LN_EPS = 1e-5
RMS_EPS = 1e-6
SCALE = HD ** -0.5
ROPE_THETA = 500000.0
AXIAL_THETA = 10000.0
A_CONFIGS = ((128, 1), (512, 4), (2048, 16))
LR, B1, B2, EPS, WD, STEP = 0.001, 0.9, 0.999, 1e-08, 0.01, 10

VMEM_LIMIT = 56 * 1024 * 1024
BAND_T = 256


def _pcall(body, **kw):
    return pl.pallas_call(body, **kw)


def _pcall_comm(body, **kw):
    return pl.pallas_call(body, **kw)


def _cp(*sem):
    return pltpu.CompilerParams(dimension_semantics=sem, vmem_limit_bytes=VMEM_LIMIT)


def _sds(shape, dtype):
    return jax.ShapeDtypeStruct(shape, dtype)


def _mm(a, b, dims, outs, tm, tn, tk, epilogue=None, extras=(), name="mm"):
    if dims == "tn":
        K, M = a.shape
    else:
        M, K = a.shape
    N = b.shape[0] if dims == "nt" else b.shape[1]
    tm, tn, tk = min(tm, M), min(tn, N), min(tk, K)
    assert M % tm == 0 and N % tn == 0 and K % tk == 0, (name, M, N, K, tm, tn, tk)
    nk = K // tk
    ne, no = len(extras), len(outs)

    def body(a_ref, b_ref, *rest):
        extra_refs, out_refs = rest[:ne], rest[ne:ne + no]
        av, bv = a_ref[...].astype(BF), b_ref[...].astype(BF)
        if dims == "nn":
            p = jnp.dot(av, bv, preferred_element_type=F32)
        elif dims == "nt":
            p = lax.dot_general(av, bv, (((1,), (1,)), ((), ())), preferred_element_type=F32)
        else:
            p = lax.dot_general(av, bv, (((0,), (0,)), ((), ())), preferred_element_type=F32)

        def finish(acc):
            res = epilogue(acc, *[r[...] for r in extra_refs]) if epilogue else (acc,)
            for o, r in zip(out_refs, res):
                o[...] = r.astype(o.dtype)

        if nk == 1:
            finish(p)
        else:
            acc_ref = rest[-1]
            k = pl.program_id(2)

            @pl.when(k == 0)
            def _():
                acc_ref[...] = p

            @pl.when(k > 0)
            def _():
                acc_ref[...] += p

            @pl.when(k == nk - 1)
            def _():
                finish(acc_ref[...])

    if dims == "tn":
        a_spec = pl.BlockSpec((tk, tm), lambda i, j, k: (k, i))
    else:
        a_spec = pl.BlockSpec((tm, tk), lambda i, j, k: (i, k))
    if dims == "nt":
        b_spec = pl.BlockSpec((tn, tk), lambda i, j, k: (j, k))
    else:
        b_spec = pl.BlockSpec((tk, tn), lambda i, j, k: (k, j))
    o_spec = pl.BlockSpec((tm, tn), lambda i, j, k: (i, j))
    res = _pcall(
        body, name=name, grid=(M // tm, N // tn, nk),
        in_specs=[a_spec, b_spec] + [o_spec] * ne,
        out_specs=[o_spec] * no,
        out_shape=[_sds((M, N), dt) for dt in outs],
        scratch_shapes=[pltpu.VMEM((tm, tn), F32)] if nk > 1 else [],
        compiler_params=_cp("parallel", "parallel", "arbitrary"),
    )(a, b, *extras)
    return res


def _rope_tables(T):
    pos = jnp.arange(T)

    def cs(p, theta, half):
        inv = theta ** (-jnp.arange(half, dtype=F32) / half)
        ang = p.astype(F32)[:, None] * inv[None, :]
        return jnp.cos(ang), jnp.sin(ang)

    ca, sa = cs(pos, ROPE_THETA, 8)
    one, zero, z8 = jnp.ones((T, 48), F32), jnp.zeros((T, 48), F32), jnp.zeros((T, 8), F32)
    tab_a = [jnp.concatenate(t, 1) for t in ([ca, ca, one], [-sa, z8, zero], [z8, sa, zero])]
    cr, sr = cs(pos // GRID_W, AXIAL_THETA, 16)
    cc, sc = cs(pos % GRID_W, AXIAL_THETA, 16)
    z16 = jnp.zeros((T, 16), F32)
    tab_b = [jnp.concatenate(t, 1) for t in ([cr, cr, cc, cc], [-sr, z16, -sc, z16], [z16, sr, z16, sc])]
    return [jnp.tile(t, (1, 2)) for t in tab_a], [jnp.tile(t, (1, 2)) for t in tab_b]


def _rot(x, C, S1, S2, k):
    return x * C + pltpu.roll(x, 128 - k, 1) * S1 + pltpu.roll(x, k, 1) * S2


def _rot_t(d, C, S1, S2, k):
    return d * C + pltpu.roll(d * S1, k, 1) + pltpu.roll(d * S2, 128 - k, 1)


def _group_mean_matrix():
    m = np.zeros((128, 128), np.float32)
    m[:64, :64] = 1.0 / 64
    m[64:, 64:] = 1.0 / 64
    return jnp.asarray(m)


def _prep_fwd(hq, tab_a, tab_b, qn, kn, gm):
    T = hq.shape[0]
    tt = min(256, T)
    widths = [A_W, A_W, A_W, BQ_W, BKV_W, BKV_W, C_W, C_W, C_W]

    def body(h_ref, ca, s1a, s2a, cb, s1b, s2b, qn_ref, kn_ref, gm_ref,
             qa, ka, va, qb, kb, vb, qc, kc, vc):
        def col(off, j):
            return h_ref[:, off + 128 * j: off + 128 * (j + 1)]

        for j in range(2):
            sl = slice(128 * j, 128 * (j + 1))
            qa[:, sl] = (_rot(col(0, j), ca[...], s1a[...], s2a[...], 8) * SCALE).astype(qa.dtype)
            ka[:, sl] = _rot(col(256, j), ca[...], s1a[...], s2a[...], 8).astype(ka.dtype)
            va[:, sl] = col(512, j).astype(va.dtype)
            qc[:, sl] = (col(1536, j) * SCALE).astype(qc.dtype)
            kc[:, sl] = col(1792, j).astype(kc.dtype)
            vc[:, sl] = col(2048, j).astype(vc.dtype)

        def normed(x, w):
            ms = jnp.dot(x * x, gm_ref[...], precision=HI, preferred_element_type=F32)
            return x * lax.rsqrt(ms + RMS_EPS) * w

        for j in range(4):
            y = normed(col(768, j), qn_ref[...])
            qb[:, 128 * j:128 * (j + 1)] = (_rot(y, cb[...], s1b[...], s2b[...], 16) * SCALE).astype(qb.dtype)
        y = normed(col(1280, 0), kn_ref[...])
        kb[...] = _rot(y, cb[...], s1b[...], s2b[...], 16).astype(kb.dtype)
        vb[...] = col(1408, 0).astype(vb.dtype)

    row = lambda w: pl.BlockSpec((tt, w), lambda i: (i, 0))
    const = lambda s: pl.BlockSpec(s, lambda i: (0, 0))
    return _pcall(
        body, name="prep_fwd", grid=(T // tt,),
        in_specs=[row(QKV)] + [row(128)] * 6 + [const((1, 128))] * 2 + [const((128, 128))],
        out_specs=[row(w) for w in widths],
        out_shape=[_sds((T, w), BF) for w in widths],
        compiler_params=_cp("parallel"),
    )(hq, *tab_a, *tab_b, qn, kn, gm)


def _prep_bwd(hq, grads, tab_a, tab_b, qn, kn, gm):
    T = hq.shape[0]
    tt = min(256, T)
    widths = [A_W, A_W, A_W, BQ_W, BKV_W, BKV_W, C_W, C_W, C_W]

    def body(h_ref, dqa, dka, dva, dqb, dkb, dvb, dqc, dkc, dvc,
             ca, s1a, s2a, cb, s1b, s2b, qn_ref, kn_ref, gm_ref, dh, dqn, dkn):
        i = pl.program_id(0)

        @pl.when(i == 0)
        def _():
            dqn[...] = jnp.zeros_like(dqn)
            dkn[...] = jnp.zeros_like(dkn)

        def put(off, j, val):
            dh[:, off + 128 * j: off + 128 * (j + 1)] = val.astype(dh.dtype)

        for j in range(2):
            sl = slice(128 * j, 128 * (j + 1))
            put(0, j, _rot_t(dqa[:, sl] * SCALE, ca[...], s1a[...], s2a[...], 8))
            put(256, j, _rot_t(dka[:, sl], ca[...], s1a[...], s2a[...], 8))
            put(512, j, dva[:, sl])
            put(1536, j, dqc[:, sl] * SCALE)
            put(1792, j, dkc[:, sl])
            put(2048, j, dvc[:, sl])

        def norm_bwd(x, w, e):
            ms = jnp.dot(x * x, gm_ref[...], precision=HI, preferred_element_type=F32)
            r = lax.rsqrt(ms + RMS_EPS)
            n = x * r
            dn = e * w
            proj = jnp.dot(dn * n, gm_ref[...], precision=HI, preferred_element_type=F32)
            return r * (dn - n * proj), jnp.sum(e * n, axis=0, keepdims=True)

        for j in range(4):
            sl = slice(128 * j, 128 * (j + 1))
            e = _rot_t(dqb[:, sl] * SCALE, cb[...], s1b[...], s2b[...], 16)
            dx, dw = norm_bwd(h_ref[:, 768 + 128 * j: 768 + 128 * (j + 1)], qn_ref[...], e)
            put(768, j, dx)
            dqn[:, sl] += dw
        e = _rot_t(dkb[...], cb[...], s1b[...], s2b[...], 16)
        dx, dw = norm_bwd(h_ref[:, 1280:1408], kn_ref[...], e)
        put(1280, 0, dx)
        dkn[...] += dw
        put(1408, 0, dvb[...])

    row = lambda w: pl.BlockSpec((tt, w), lambda i: (i, 0))
    const = lambda s: pl.BlockSpec(s, lambda i: (0, 0))
    return _pcall(
        body, name="prep_bwd", grid=(T // tt,),
        in_specs=[row(QKV)] + [row(w) for w in widths] + [row(128)] * 6
        + [const((1, 128))] * 2 + [const((128, 128))],
        out_specs=[row(QKV), const((1, BQ_W)), const((1, BKV_W))],
        out_shape=[_sds((T, QKV), BF), _sds((1, BQ_W), F32), _sds((1, BKV_W), F32)],
        compiler_params=_cp("arbitrary"),
    )(hq, *grads, *tab_a, *tab_b, qn, kn, gm)


def _to_heads(x):
    T, W = x.shape
    return x.reshape(T, W // HD, HD).transpose(1, 0, 2)


def _from_heads(x):
    H, T, _ = x.shape
    return x.transpose(1, 0, 2).reshape(T, H * HD)


def _rows(x, t):
    H, T, _ = x.shape
    return x.reshape(H, T // t, 1, t)


def _nt(a, b):
    return lax.dot_general(a, b, (((1,), (1,)), ((), ())), preferred_element_type=F32)


def _chunked_t(x, t):
    T, W = x.shape
    return x.reshape(T // t, t, W // HD, HD).transpose(2, 0, 3, 1)


def _unchunk_t(x):
    H, n, _, t = x.shape
    return x.transpose(1, 3, 0, 2).reshape(n * t, H * HD)


FULL_T = 512
FULL_HEADS = 2
V_ROWS = 72


def _attn_full_fwd(qT, k, vT1):
    Hq, nq, _, t = qT.shape
    Hk, T, _ = k.shape
    G = Hq // Hk
    nk = T // t

    HB = FULL_HEADS
    assert G % HB == 0

    def body(q_ref, k_ref, v_ref, o_ref, lse_ref, m_ref, acc_ref):
        m_ref[...] = jnp.full((HB, 1, t), NEG, F32)
        acc_ref[...] = jnp.zeros((HB, V_ROWS, t), F32)

        def step(j, carry):
            ks = k_ref[pl.ds(pl.multiple_of(j * t, t), t), :]
            vs = v_ref[j]
            for b in range(HB):
                sT = jnp.dot(ks, q_ref[b], preferred_element_type=F32)
                m_old = m_ref[b]
                m_new = jnp.maximum(m_old, jnp.max(sT, axis=0, keepdims=True))
                pT = jnp.exp(sT - m_new).astype(BF)
                acc_ref[b] = jnp.exp(m_old - m_new) * acc_ref[b] + jnp.dot(vs, pT, preferred_element_type=F32)
                m_ref[b] = m_new
            return carry

        lax.fori_loop(0, nk, step, 0)
        for b in range(HB):
            l = acc_ref[b, pl.ds(HD, 1), :]
            o_ref[b] = (acc_ref[b, pl.ds(0, HD), :] / l).astype(o_ref.dtype)
            lse_ref[b] = m_ref[b] + jnp.log(l)

    qs = pl.BlockSpec((HB, None, HD, t), lambda h, i: (h, i, 0, 0))
    return _pcall(
        body, name="attn_full_fwd", grid=(Hq // HB, nq),
        in_specs=[qs, pl.BlockSpec((None, T, HD), lambda h, i: (h * HB // G, 0, 0)),
                  pl.BlockSpec((None, nk, V_ROWS, t), lambda h, i: (h * HB // G, 0, 0, 0))],
        out_specs=[qs, pl.BlockSpec((HB, None, 1, t), lambda h, i: (h, i, 0, 0))],
        out_shape=[_sds((Hq, nq, HD, t), BF), _sds((Hq, nq, 1, t), F32)],
        scratch_shapes=[pltpu.VMEM((HB, 1, t), F32), pltpu.VMEM((HB, V_ROWS, t), F32)],
        compiler_params=_cp("parallel", "parallel"),
    )(qT, k, vT1)


def _attn_delta(doT, oT):
    Hq, nq, _, t = doT.shape

    def body(do_ref, o_ref, dl_ref):
        dl_ref[...] = jnp.sum(do_ref[...].astype(F32) * o_ref[...].astype(F32), axis=0, keepdims=True)

    qs = pl.BlockSpec((None, None, HD, t), lambda h, i: (h, i, 0, 0))
    rs = pl.BlockSpec((None, None, 1, t), lambda h, i: (h, i, 0, 0))
    return _pcall(body, name="attn_delta", grid=(Hq, nq), in_specs=[qs, qs], out_specs=rs,
                  out_shape=_sds((Hq, nq, 1, t), F32), compiler_params=_cp("parallel", "parallel"))(doT, oT)


def _attn_full_bwd(q, qT, k, kT, v, do, doT, lse, delta):
    Hq, nq, _, t = qT.shape
    Hk, T, _ = k.shape
    G = Hq // Hk

    def body(q_ref, qT_ref, do_ref, doT_ref, lse_ref, dl_ref, k_ref, kT_ref, v_ref,
             dq_ref, dk_ref, dv_ref, dk_acc, dv_acc):
        @pl.when(pl.program_id(1) == 0)
        def _():
            dq_ref[...] = jnp.zeros_like(dq_ref)

        kv, kTv, vv = k_ref[...], kT_ref[...], v_ref[...]
        dk_acc[...] = jnp.zeros((t, HD), F32)
        dv_acc[...] = jnp.zeros((t, HD), F32)

        def step(i, carry):
            rows = pl.ds(pl.multiple_of(i * t, t), t)
            pT = jnp.exp(jnp.dot(kv, qT_ref[i], preferred_element_type=F32) - lse_ref[i])
            dv_acc[...] += jnp.dot(pT.astype(BF), do_ref[rows, :], preferred_element_type=F32)
            dpT = jnp.dot(vv, doT_ref[i], preferred_element_type=F32)
            dsT = (pT * (dpT - dl_ref[i])).astype(BF)
            dk_acc[...] += jnp.dot(dsT, q_ref[rows, :], preferred_element_type=F32)
            dq_ref[i] += jnp.dot(kTv, dsT, preferred_element_type=F32)
            return carry

        lax.fori_loop(0, nq, step, 0)
        dk_ref[...] = dk_acc[...]
        dv_ref[...] = dv_acc[...]

    tok = pl.BlockSpec((None, T, HD), lambda h, j: (h, 0, 0))
    chk = pl.BlockSpec((None, nq, HD, t), lambda h, j: (h, 0, 0, 0))
    row = pl.BlockSpec((None, nq, 1, t), lambda h, j: (h, 0, 0, 0))
    kvs = pl.BlockSpec((None, t, HD), lambda h, j: (h // G, j, 0))
    out = pl.BlockSpec((None, t, HD), lambda h, j: (h, j, 0))
    return _pcall(
        body, name="attn_full_bwd", grid=(Hq, T // t),
        in_specs=[tok, chk, tok, chk, row, row, kvs, pl.BlockSpec((None, HD, t), lambda h, j: (h // G, 0, j)), kvs],
        out_specs=[chk, out, out],
        out_shape=[_sds((Hq, nq, HD, t), F32), _sds((Hq, T, HD), F32), _sds((Hq, T, HD), F32)],
        scratch_shapes=[pltpu.VMEM((t, HD), F32)] * 2,
        compiler_params=_cp("parallel", "arbitrary"),
    )(q, qT, do, doT, lse, delta, k, kT, v)


DIL_Q = 128
DIL_K = 256
DIL_R = 64


def _vh(x, d):
    T = x.shape[0]
    return x.reshape(T // d, d, 4, HD).transpose(1, 2, 0, 3).reshape(4 * d, T // d, HD)


def _vh_inv(y, d):
    L = y.shape[1]
    return y.reshape(d, 4, L, HD).transpose(2, 0, 1, 3).reshape(L * d, 4 * HD)


def _vh_chunks(x, d):
    y = _vh(x, d)
    return y.reshape(y.shape[0], y.shape[1] // DIL_Q, DIL_Q, HD).transpose(0, 1, 3, 2)


def _chunks_to_dims(c, d):
    _, nq, R, _ = c.shape
    return c.reshape(d, 4, nq, R, DIL_Q).transpose(1, 3, 2, 4, 0).reshape(4, R, nq * DIL_Q * d)


def _dims_to_chunks(x, d):
    _, R, T = x.shape
    nq = T // (DIL_Q * d)
    return x.reshape(4, R, nq, DIL_Q, d).transpose(4, 0, 2, 1, 3).reshape(4 * d, nq, R, DIL_Q)


def _chunks_to_tokens(c, d):
    _, nq, _, _ = c.shape
    return c.reshape(d, 4, nq, HD, DIL_Q).transpose(2, 4, 0, 1, 3).reshape(nq * DIL_Q * d, 4 * HD)


def _dil_window(i, L):
    start = pl.multiple_of(jnp.clip(i * DIL_Q - DIL_R, 0, L - DIL_K), DIL_R)
    kk = start + lax.broadcasted_iota(jnp.int32, (DIL_K, 1), 0)
    qq = i * DIL_Q + lax.broadcasted_iota(jnp.int32, (1, DIL_Q), 1)
    return start, jnp.abs(kk - qq) <= DIL_R


def _tn(a, b):
    return lax.dot_general(a, b, (((0,), (0,)), ((), ())), preferred_element_type=F32)


def _dil_fwd(qT, k, v1):
    V, nq, _, _ = qT.shape
    L = k.shape[1]
    assert L >= DIL_K

    def body(q_ref, k_ref, v_ref, o_ref, lse_ref):
        def tile(i, carry):
            start, mask = _dil_window(i, L)
            win = pl.ds(start, DIL_K)
            sT = jnp.where(mask, jnp.dot(k_ref[win, :], q_ref[i], preferred_element_type=F32), NEG)
            m = jnp.max(sT, axis=0, keepdims=True)
            pT = jnp.exp(sT - m).astype(BF)
            acc = _tn(v_ref[win, :], pT)
            l = jnp.max(acc[HD:HD + 8], axis=0, keepdims=True)
            o_ref[i] = acc[:HD] / l
            lse_ref[i] = m + jnp.log(l)
            return carry

        lax.fori_loop(0, nq, tile, 0)

    chk = lambda r, dt: (pl.BlockSpec((None, nq, r, DIL_Q), lambda h: (h, 0, 0, 0)), _sds((V, nq, r, DIL_Q), dt))
    tok = lambda w: pl.BlockSpec((None, L, w), lambda h: (h, 0, 0))
    (o_spec, o_shape), (l_spec, l_shape) = chk(HD, F32), chk(1, F32)
    return _pcall(
        body, name=f"dil_fwd_{V // 4}", grid=(V,),
        in_specs=[chk(HD, BF)[0], tok(HD), tok(128)],
        out_specs=[o_spec, l_spec], out_shape=[o_shape, l_shape],
        compiler_params=_cp("parallel"),
    )(qT, k, v1)


def _dil_merge(os_, lses):
    _, _, T = os_[0].shape
    tt = min(1024, T)
    n = len(os_)

    def body(*refs):
        o_refs, l_refs, (o_out, l_out) = refs[:n], refs[n:2 * n], refs[2 * n:]
        m = l_refs[0][...]
        for r in l_refs[1:]:
            m = jnp.maximum(m, r[...])
        ws = [jnp.exp(r[...] - m) for r in l_refs]
        tot = ws[0]
        for w_ in ws[1:]:
            tot = tot + w_
        acc = ws[0] * o_refs[0][...]
        for w_, o in zip(ws[1:], o_refs[1:]):
            acc = acc + w_ * o[...]
        o_out[...] = (acc / tot).astype(o_out.dtype)
        l_out[...] = m + jnp.log(tot)

    os_spec = pl.BlockSpec((None, HD, tt), lambda h, i: (h, 0, i))
    ls_spec = pl.BlockSpec((None, 1, tt), lambda h, i: (h, 0, i))
    return _pcall(
        body, name="dil_merge", grid=(4, T // tt),
        in_specs=[os_spec] * n + [ls_spec] * n, out_specs=[os_spec, ls_spec],
        out_shape=[_sds((4, HD, T), BF), _sds((4, 1, T), F32)],
        compiler_params=_cp("parallel", "parallel"),
    )(*os_, *lses)


def _dims_delta(doT, oT):
    _, _, T = doT.shape
    tt = min(2048, T)

    def body(do_ref, o_ref, dl_ref):
        dl_ref[...] = jnp.sum(do_ref[...].astype(F32) * o_ref[...].astype(F32), axis=0, keepdims=True)

    spec = pl.BlockSpec((None, HD, tt), lambda h, i: (h, 0, i))
    return _pcall(body, name="dims_delta", grid=(4, T // tt), in_specs=[spec, spec],
                  out_specs=pl.BlockSpec((None, 1, tt), lambda h, i: (h, 0, i)),
                  out_shape=_sds((4, 1, T), F32), compiler_params=_cp("parallel", "parallel"))(doT, oT)


def _dil_bwd(q, qT, k, v, do, doT, lse, delta):
    V, nq, _, _ = qT.shape
    L = k.shape[1]

    def body(q_ref, qT_ref, k_ref, v_ref, do_ref, doT_ref, lse_ref, dl_ref, dq_ref, dk_ref, dv_ref):
        dk_ref[...] = jnp.zeros_like(dk_ref)
        dv_ref[...] = jnp.zeros_like(dv_ref)

        def tile(i, carry):
            start, mask = _dil_window(i, L)
            win = pl.ds(start, DIL_K)
            rows = pl.ds(pl.multiple_of(i * DIL_Q, DIL_Q), DIL_Q)
            kw = k_ref[win, :]
            sT = jnp.where(mask, jnp.dot(kw, qT_ref[i], preferred_element_type=F32), NEG)
            pT = jnp.exp(sT - lse_ref[i])
            dv_ref[win, :] += jnp.dot(pT.astype(BF), do_ref[rows, :], preferred_element_type=F32)
            dpT = jnp.dot(v_ref[win, :], doT_ref[i], preferred_element_type=F32)
            dsT = (pT * (dpT - dl_ref[i])).astype(BF)
            dk_ref[win, :] += jnp.dot(dsT, q_ref[rows, :], preferred_element_type=F32)
            dq_ref[i] = _tn(kw, dsT)
            return carry

        lax.fori_loop(0, nq, tile, 0)

    chk = lambda r: pl.BlockSpec((None, nq, r, DIL_Q), lambda h: (h, 0, 0, 0))
    tok = pl.BlockSpec((None, L, HD), lambda h: (h, 0, 0))
    return _pcall(
        body, name=f"dil_bwd_{V // 4}", grid=(V,),
        in_specs=[tok, chk(HD), tok, tok, tok, chk(HD), chk(1), chk(1)],
        out_specs=[chk(HD), tok, tok],
        out_shape=[_sds((V, nq, HD, DIL_Q), F32), _sds((V, L, HD), F32), _sds((V, L, HD), F32)],
        compiler_params=_cp("parallel"),
    )(q, qT, k, v, do, doT, lse, delta)


def _dilated_fwd(qa, ka, va):
    outs, lses, saved = [], [], []
    for _, d in A_CONFIGS:
        q, qT, k, v = _vh(qa, d), _vh_chunks(qa, d), _vh(ka, d), _vh(va, d)
        oT, lse = _dil_fwd(qT, k, jnp.concatenate([v, jnp.ones_like(v)], axis=2))
        outs.append(_chunks_to_dims(oT, d))
        lses.append(_chunks_to_dims(lse, d))
        saved.append((q, qT, k, v))
    o, lse = _dil_merge(outs, lses)
    return o, lse, saved


def _dilated_bwd(do_t, o, lse, saved):
    doT = do_t.reshape(do_t.shape[0], 4, HD).transpose(1, 2, 0)
    delta = _dims_delta(doT, o)
    dq = dk = dv = None
    for (_, d), (q, qT, k, v) in zip(A_CONFIGS, saved):
        dqT, dk_c, dv_c = _dil_bwd(q, qT, k, v, _vh(do_t, d), _dims_to_chunks(doT, d),
                                   _dims_to_chunks(lse, d), _dims_to_chunks(delta, d))
        parts = (_chunks_to_tokens(dqT, d), _vh_inv(dk_c, d), _vh_inv(dv_c, d))
        dq, dk, dv = parts if dq is None else (dq + parts[0], dk + parts[1], dv + parts[2])
    return dq, dk, dv


def _band_offsets(radius):
    offs = [0]
    for r in range(1, radius + 1):
        offs += [-r, r]
    return offs


def _dilated_bias(t):
    radius = max(w // 2 for w, _ in A_CONFIGS) // t
    tabs = []
    i = np.arange(t)
    for off in _band_offsets(radius):
        d = off * t + i[None, :] - i[:, None]
        mult = np.zeros((t, t), np.float32)
        for w, dil in A_CONFIGS:
            mult += ((d % dil) == 0) & (np.abs(d) <= w // 2)
        tabs.append(np.where(mult > 0, np.log(np.maximum(mult, 1.0)), NEG).astype(np.float32))
    return jnp.asarray(np.stack(tabs)[None]), radius


def _nbr_index(t):
    rpt = t // GRID_W
    i = np.arange(t)
    c0 = np.clip(i % GRID_W - 8, 0, GRID_W - 16)
    col_ok = ((i[None, :] % GRID_W) >= c0[:, None]) & ((i[None, :] % GRID_W) < c0[:, None] + 16)
    oks = []
    for off in _band_offsets(1):
        dr = (i[None, :] // GRID_W) - (i[:, None] // GRID_W) + rpt * off
        oks.append(col_ok & (np.abs(dr) <= 7))
    return np.stack(oks)


def _nbr_bias(rpb, t):
    rpt = t // GRID_W
    e1, e2 = _rpb_fold_matrices(t)
    ok = _nbr_index(t)
    padded = jnp.pad(rpb, ((0, 0), (0, 1), (0, 128 - rpb.shape[2]))).reshape(64, 128)

    def body(e2t_ref, rpb_ref, e1t_ref, out_ref):
        picked = jnp.dot(e2t_ref[...], rpb_ref[...], precision=HI, preferred_element_type=F32)
        out_ref[...] = jnp.dot(picked, e1t_ref[...], precision=HI, preferred_element_type=F32)

    n = e2.shape[1]
    sub = _pcall(body, name="rpb_expand", out_shape=_sds((n, GRID_W * GRID_W), F32),
                 compiler_params=pltpu.CompilerParams(vmem_limit_bytes=VMEM_LIMIT))(e2.T, padded, e1.T)
    tiles = sub.reshape(4, 3, rpt, rpt, GRID_W, GRID_W).transpose(0, 1, 2, 4, 3, 5).reshape(4, 3, t, t)
    return jnp.where(ok[None], tiles, NEG)


def _nbr_mask(qi, kb, t, rows, q_on_lanes):
    rpt = t // GRID_W
    qshape, kshape = ((1, t), (t, 1)) if q_on_lanes else ((t, 1), (1, t))
    rq = rpt * qi + lax.broadcasted_iota(jnp.int32, qshape, 1 if q_on_lanes else 0) // GRID_W
    rk = rpt * kb + lax.broadcasted_iota(jnp.int32, kshape, 0 if q_on_lanes else 1) // GRID_W
    r0 = jnp.clip(rq - 4, 0, rows - 8)
    return (rk >= r0) & (rk < r0 + 8)


def _attn_band_fwd(q, k, v, bias, radius, rowmask):
    H, T, _ = q.shape
    t = BAND_T
    nq = T // t
    Hb = bias.shape[0]
    offs = _band_offsets(radius)
    rows = T // GRID_W

    def body(q_ref, k_ref, v_ref, b_ref, o_ref, lse_ref, m_ref, l_ref, acc_ref):
        i = pl.program_id(1)
        qv = q_ref[...]
        m_ref[...] = jnp.full((t, 1), NEG, F32)
        l_ref[...] = jnp.zeros((t, 1), F32)
        acc_ref[...] = jnp.zeros((t, HD), F32)

        def tile(o, off):
            kb = i + off
            st = pl.multiple_of(kb * t, t)
            ks, vs = k_ref[pl.ds(st, t), :], v_ref[pl.ds(st, t), :]
            s = _nt(qv, ks) + b_ref[o]
            if rowmask:
                s = jnp.where(_nbr_mask(i, kb, t, rows, False), s, NEG)
            m_old = m_ref[...]
            m_new = jnp.maximum(m_old, jnp.max(s, axis=-1, keepdims=True))
            a = jnp.exp(m_old - m_new)
            p = jnp.exp(s - m_new)
            l_ref[...] = a * l_ref[...] + jnp.sum(p, axis=-1, keepdims=True)
            acc_ref[...] = a * acc_ref[...] + jnp.dot(p.astype(BF), vs, preferred_element_type=F32)
            m_ref[...] = m_new

        for o, off in enumerate(offs):
            if off == 0:
                tile(o, off)
            else:
                pl.when((i + off >= 0) & (i + off < nq))(functools.partial(tile, o, off))
        o_ref[...] = (acc_ref[...] / l_ref[...]).astype(o_ref.dtype)
        lse_ref[...] = m_ref[...] + jnp.log(l_ref[...])

    qs = pl.BlockSpec((None, t, HD), lambda h, i: (h, i, 0))
    kvs = pl.BlockSpec((None, T, HD), lambda h, i: (h, 0, 0))
    bs = pl.BlockSpec((None, len(offs), t, t), lambda h, i: (h if Hb > 1 else 0, 0, 0, 0))
    return _pcall(
        body, name="attn_band_fwd_c" if rowmask else "attn_band_fwd_a", grid=(H, nq),
        in_specs=[qs, kvs, kvs, bs],
        out_specs=[qs, pl.BlockSpec((None, t, 1), lambda h, i: (h, i, 0))],
        out_shape=[_sds((H, T, HD), BF), _sds((H, T, 1), F32)],
        scratch_shapes=[pltpu.VMEM((t, 1), F32), pltpu.VMEM((t, 1), F32), pltpu.VMEM((t, HD), F32)],
        compiler_params=_cp("parallel", "parallel"),
    )(q, k, v, bias)


def _attn_band_dq(q, k, v, o, do, lse, bias, radius, rowmask):
    H, T, _ = q.shape
    t = BAND_T
    nq = T // t
    Hb = bias.shape[0]
    offs = _band_offsets(radius)
    rows = T // GRID_W

    def body(q_ref, k_ref, v_ref, o_ref, do_ref, lse_ref, b_ref, dq_ref, dl_ref, *rest):
        db_ref = rest[0] if rowmask else None
        acc_ref = rest[-1]
        i = pl.program_id(1)
        qv, dov, lse = q_ref[...], do_ref[...], lse_ref[...]
        delta = jnp.sum(dov.astype(F32) * o_ref[...].astype(F32), axis=-1, keepdims=True)
        acc_ref[...] = jnp.zeros((t, HD), F32)
        if rowmask:
            @pl.when(i == 0)
            def _():
                db_ref[...] = jnp.zeros_like(db_ref)

        def tile(o, off):
            kb = i + off
            st = pl.multiple_of(kb * t, t)
            ks, vs = k_ref[pl.ds(st, t), :], v_ref[pl.ds(st, t), :]
            s = _nt(qv, ks) + b_ref[o]
            if rowmask:
                s = jnp.where(_nbr_mask(i, kb, t, rows, False), s, NEG)
            p = jnp.exp(s - lse)
            ds = p * (_nt(dov, vs) - delta)
            acc_ref[...] += jnp.dot(ds.astype(BF), ks, preferred_element_type=F32)
            if rowmask:
                db_ref[o] += ds

        for o, off in enumerate(offs):
            if off == 0:
                tile(o, off)
            else:
                pl.when((i + off >= 0) & (i + off < nq))(functools.partial(tile, o, off))
        dq_ref[...] = acc_ref[...]
        dl_ref[...] = delta

    qs = pl.BlockSpec((None, t, HD), lambda h, i: (h, i, 0))
    kvs = pl.BlockSpec((None, T, HD), lambda h, i: (h, 0, 0))
    cs = pl.BlockSpec((None, t, 1), lambda h, i: (h, i, 0))
    bs = pl.BlockSpec((None, len(offs), t, t), lambda h, i: (h if Hb > 1 else 0, 0, 0, 0))
    out_specs = [qs, cs]
    out_shape = [_sds((H, T, HD), F32), _sds((H, T, 1), F32)]
    if rowmask:
        out_specs.append(pl.BlockSpec((None, len(offs), t, t), lambda h, i: (h, 0, 0, 0)))
        out_shape.append(_sds((H, len(offs), t, t), F32))
    return _pcall(
        body, name="attn_band_dq_c" if rowmask else "attn_band_dq_a", grid=(H, nq),
        in_specs=[qs, kvs, kvs, qs, qs, cs, bs],
        out_specs=out_specs, out_shape=out_shape,
        scratch_shapes=[pltpu.VMEM((t, HD), F32)],
        compiler_params=_cp("parallel", "arbitrary"),
    )(q, k, v, o, do, lse, bias)


def _attn_band_dkv(q, k, v, do, lse, delta, bias_t, radius, rowmask):
    H, T, _ = q.shape
    t = BAND_T
    nq = T // t
    Hb = bias_t.shape[0]
    offs = _band_offsets(radius)
    rows = T // GRID_W
    lse_r, dl_r = _rows(lse, t), _rows(delta, t)

    def body(k_ref, v_ref, q_ref, do_ref, lse_ref, dl_ref, b_ref, dk_ref, dv_ref, dk_acc, dv_acc):
        jb = pl.program_id(1)
        kv, vv = k_ref[...], v_ref[...]
        dk_acc[...] = jnp.zeros((t, HD), F32)
        dv_acc[...] = jnp.zeros((t, HD), F32)

        def tile(o, off):
            qi = jb - off
            st = pl.multiple_of(qi * t, t)
            qs, dos = q_ref[pl.ds(st, t), :], do_ref[pl.ds(st, t), :]
            sT = _nt(kv, qs) + b_ref[o]
            if rowmask:
                sT = jnp.where(_nbr_mask(qi, jb, t, rows, True), sT, NEG)
            pT = jnp.exp(sT - lse_ref[qi])
            dv_acc[...] += jnp.dot(pT.astype(BF), dos, preferred_element_type=F32)
            dsT = pT * (_nt(vv, dos) - dl_ref[qi])
            dk_acc[...] += jnp.dot(dsT.astype(BF), qs, preferred_element_type=F32)

        for o, off in enumerate(offs):
            if off == 0:
                tile(o, off)
            else:
                pl.when((jb - off >= 0) & (jb - off < nq))(functools.partial(tile, o, off))
        dk_ref[...] = dk_acc[...]
        dv_ref[...] = dv_acc[...]

    kvs = pl.BlockSpec((None, t, HD), lambda h, j: (h, j, 0))
    qs = pl.BlockSpec((None, T, HD), lambda h, j: (h, 0, 0))
    rs = pl.BlockSpec((None, nq, 1, t), lambda h, j: (h, 0, 0, 0))
    bs = pl.BlockSpec((None, len(offs), t, t), lambda h, j: (h if Hb > 1 else 0, 0, 0, 0))
    return _pcall(
        body, name="attn_band_dkv_c" if rowmask else "attn_band_dkv_a", grid=(H, nq),
        in_specs=[kvs, kvs, qs, qs, rs, rs, bs],
        out_specs=[kvs, kvs],
        out_shape=[_sds((H, T, HD), F32)] * 2,
        scratch_shapes=[pltpu.VMEM((t, HD), F32)] * 2,
        compiler_params=_cp("parallel", "parallel"),
    )(k, v, q, do, lse_r, dl_r, bias_t)


def _rpb_fold_matrices(t):
    rpt = t // GRID_W
    e1 = np.zeros((GRID_W * GRID_W, 128), np.float32)
    ic, jc = np.meshgrid(np.arange(GRID_W), np.arange(GRID_W), indexing="ij")
    dc = (jc - ic + 15).reshape(-1)
    keep = (dc >= 0) & (dc <= 30)
    e1[np.arange(GRID_W * GRID_W)[keep], dc[keep]] = 1.0
    offs = _band_offsets(1)
    n = 4 * len(offs) * rpt * rpt
    e2 = np.zeros((64, n), np.float32)
    col = 0
    for h in range(4):
        for off in offs:
            for ib in range(rpt):
                for jb in range(rpt):
                    dr = jb - ib + rpt * off + 7
                    if 0 <= dr <= 14:
                        e2[h * 16 + dr, col] = 1.0
                    col += 1
    return jnp.asarray(e1), jnp.asarray(e2)


def _rpb_grad(dbias):
    t = dbias.shape[-1]
    rpt = t // GRID_W
    e1, e2 = _rpb_fold_matrices(t)
    sub = dbias.reshape(4, 3, rpt, GRID_W, rpt, GRID_W).transpose(0, 1, 2, 4, 3, 5)
    sub = sub.reshape(4 * 3 * rpt * rpt, GRID_W * GRID_W)

    def body(e2_ref, sub_ref, e1_ref, out_ref):
        diag = jnp.dot(sub_ref[...], e1_ref[...], precision=HI, preferred_element_type=F32)
        out_ref[...] = jnp.dot(e2_ref[...], diag, precision=HI, preferred_element_type=F32)

    out = _pcall(body, name="rpb_fold", out_shape=_sds((64, 128), F32),
                 compiler_params=pltpu.CompilerParams(vmem_limit_bytes=VMEM_LIMIT))(e2, sub, e1)
    return out.reshape(4, 16, 128)[:, :15, :31]


def _sigmoid(z):
    return 1.0 / (1.0 + jnp.exp(-z))


def _merge_fwd(oa, ob, oc, hg, bg, wa, wb, wc):
    T = oa.shape[0]
    tt = min(512, T)

    def body(oa_ref, ob_ref, oc_ref, hg_ref, bg_ref, wa_ref, wb_ref, wc_ref, out_ref):
        acc = None
        for k, (o_ref, w_ref) in enumerate(((oa_ref, wa_ref), (ob_ref, wb_ref), (oc_ref, wc_ref))):
            y = jnp.dot(o_ref[...], w_ref[...], preferred_element_type=F32)
            g = _sigmoid(hg_ref[:, D * k:D * (k + 1)] + bg_ref[:, D * k:D * (k + 1)])
            acc = g * y if acc is None else acc + g * y
        out_ref[...] = acc.astype(out_ref.dtype)

    row = lambda w: pl.BlockSpec((tt, w), lambda i: (i, 0))
    const = lambda a: pl.BlockSpec(a.shape, lambda i: (0, 0))
    return _pcall(
        body, name="merge_fwd", grid=(T // tt,),
        in_specs=[row(A_W), row(BQ_W), row(C_W), row(GATE), const(bg), const(wa), const(wb), const(wc)],
        out_specs=row(D), out_shape=_sds((T, D), BF),
        compiler_params=_cp("parallel"),
    )(oa, ob, oc, hg, bg, wa, wb, wc)


def _merge_bwd(dm, oa, ob, oc, hg, bg, wa, wb, wc):
    T = oa.shape[0]
    tt = min(256, T)

    def body(dm_ref, oa_ref, ob_ref, oc_ref, hg_ref, bg_ref, wa_ref, wb_ref, wc_ref,
             dya, dyb, dyc, doa, dob, doc, dhg, dbg):
        @pl.when(pl.program_id(0) == 0)
        def _():
            dbg[...] = jnp.zeros_like(dbg)

        dmv = dm_ref[...]
        for k, (o_ref, w_ref, dy_ref, do_ref) in enumerate(
                ((oa_ref, wa_ref, dya, doa), (ob_ref, wb_ref, dyb, dob), (oc_ref, wc_ref, dyc, doc))):
            sl = slice(D * k, D * (k + 1))
            y = jnp.dot(o_ref[...], w_ref[...], preferred_element_type=F32)
            g = _sigmoid(hg_ref[:, sl] + bg_ref[:, sl])
            dy = (dmv * g).astype(BF)
            dy_ref[...] = dy
            do_ref[...] = _nt(dy, w_ref[...]).astype(do_ref.dtype)
            dz = dmv * y * (g * (1.0 - g))
            dhg[:, sl] = dz.astype(dhg.dtype)
            dbg[:, sl] += jnp.sum(dz, axis=0, keepdims=True)

    row = lambda w: pl.BlockSpec((tt, w), lambda i: (i, 0))
    const = lambda a: pl.BlockSpec(a.shape, lambda i: (0, 0))
    return _pcall(
        body, name="merge_bwd", grid=(T // tt,),
        in_specs=[row(D), row(A_W), row(BQ_W), row(C_W), row(GATE), const(bg), const(wa), const(wb), const(wc)],
        out_specs=[row(D)] * 3 + [row(A_W), row(BQ_W), row(C_W), row(GATE),
                                  pl.BlockSpec((1, GATE), lambda i: (0, 0))],
        out_shape=[_sds((T, D), BF)] * 3 + [_sds((T, A_W), BF), _sds((T, BQ_W), BF), _sds((T, C_W), BF),
                                            _sds((T, GATE), BF), _sds((1, GATE), F32)],
        compiler_params=_cp("arbitrary"),
    )(dm, oa, ob, oc, hg, bg, wa, wb, wc)


def _lin_ln(a, w, res, g, b):
    T, K = a.shape
    tt = min(256, T)

    def body(a_ref, w_ref, res_ref, g_ref, b_ref, y_ref, yb_ref, xh_ref, rs_ref):
        u = ALPHA * res_ref[...] + jnp.dot(a_ref[...], w_ref[...], preferred_element_type=F32)
        mu = jnp.mean(u, axis=-1, keepdims=True)
        c = u - mu
        r = lax.rsqrt(jnp.mean(c * c, axis=-1, keepdims=True) + LN_EPS)
        xh = c * r
        y = xh * g_ref[...] + b_ref[...]
        y_ref[...] = y
        yb_ref[...] = y.astype(BF)
        xh_ref[...] = xh
        rs_ref[...] = r

    row = lambda w_: pl.BlockSpec((tt, w_), lambda i: (i, 0))
    const = lambda s: pl.BlockSpec(s, lambda i: (0, 0))
    return _pcall(
        body, name="lin_ln", grid=(T // tt,),
        in_specs=[row(K), const((K, D)), row(D), const((1, D)), const((1, D))],
        out_specs=[row(D), row(D), row(D), row(1)],
        out_shape=[_sds((T, D), F32), _sds((T, D), BF), _sds((T, D), F32), _sds((T, 1), F32)],
        compiler_params=_cp("parallel"),
    )(a, w, res, g, b)


def _ln_bwd(dy, xh, rs, g):
    T = dy.shape[0]
    tt = min(512, T)

    def body(dy_ref, xh_ref, rs_ref, g_ref, du_ref, dub_ref, dg_ref, db_ref):
        @pl.when(pl.program_id(0) == 0)
        def _():
            dg_ref[...] = jnp.zeros_like(dg_ref)
            db_ref[...] = jnp.zeros_like(db_ref)

        dyv, xhv = dy_ref[...], xh_ref[...]
        dg_ref[...] += jnp.sum(dyv * xhv, axis=0, keepdims=True)
        db_ref[...] += jnp.sum(dyv, axis=0, keepdims=True)
        dxh = dyv * g_ref[...]
        m1 = jnp.mean(dxh, axis=-1, keepdims=True)
        m2 = jnp.mean(dxh * xhv, axis=-1, keepdims=True)
        du = rs_ref[...] * (dxh - m1 - xhv * m2)
        du_ref[...] = du
        dub_ref[...] = du.astype(BF)

    row = lambda w_: pl.BlockSpec((tt, w_), lambda i: (i, 0))
    const = lambda s: pl.BlockSpec(s, lambda i: (0, 0))
    return _pcall(
        body, name="ln_bwd", grid=(T // tt,),
        in_specs=[row(D), row(D), row(1), const((1, D))],
        out_specs=[row(D), row(D), const((1, D)), const((1, D))],
        out_shape=[_sds((T, D), F32), _sds((T, D), BF), _sds((1, D), F32), _sds((1, D), F32)],
        compiler_params=_cp("arbitrary"),
    )(dy, xh, rs, g)


def _loss_grad(y, tgt):
    T = y.shape[0]
    tt = min(512, T)

    def body(y_ref, t_ref, dy_ref, sq_ref):
        @pl.when(pl.program_id(0) == 0)
        def _():
            sq_ref[...] = jnp.zeros_like(sq_ref)

        e = y_ref[...] - t_ref[...]
        dy_ref[...] = e * (1.0 / D)
        sq_ref[...] += jnp.sum(e * e, axis=0, keepdims=True)

    row = pl.BlockSpec((tt, D), lambda i: (i, 0))
    return _pcall(
        body, name="loss_grad", grid=(T // tt,),
        in_specs=[row, row], out_specs=[row, pl.BlockSpec((1, D), lambda i: (0, 0))],
        out_shape=[_sds((T, D), F32), _sds((1, D), F32)],
        compiler_params=_cp("arbitrary"),
    )(y, tgt)


def _position():
    return lax.axis_index("x"), lax.axis_index("y"), lax.axis_index("c")


def _all_gather(xs, name):
    n = len(xs)
    hbm = pl.BlockSpec(memory_space=pl.ANY)

    def body(*refs):
        x_refs, out_refs = refs[:n], refs[n:2 * n]
        send, recv, loc = refs[2 * n:]
        x, y, c = _position()
        me, sib = (x, y, c), (x, y, 1 - c)
        chips = [(1 - x, y), (x, 1 - y), (1 - x, 1 - y)]

        def copy(a, k, block, to, src=None):
            px, py, pc = block
            dst = out_refs[a].at[4 * px + 2 * py + pc]
            return pltpu.make_async_remote_copy(
                src_ref=dst if src is None else src, dst_ref=dst,
                send_sem=send.at[a, k], recv_sem=recv.at[a, k], device_id=to, device_id_type=MESH)

        mine = [pltpu.make_async_copy(x_refs[a], out_refs[a].at[4 * x + 2 * y + c], loc.at[a]) for a in range(n)]
        for cp in mine:
            cp.start()
        first = []
        for a in range(n):
            first.append(copy(a, 0, me, sib, src=x_refs[a]))
            first += [copy(a, 1 + j, me, (*chip, c), src=x_refs[a]) for j, chip in enumerate(chips)]
        for cp in first:
            cp.start()
        passed = []
        for j, chip in enumerate(chips):
            for a in range(n):
                copy(a, 1 + j, (*chip, c), me).wait_recv()
                fwd = copy(a, 4 + j, (*chip, c), sib)
                fwd.start()
                passed.append(fwd)
        for a in range(n):
            copy(a, 0, sib, me).wait_recv()
            for j, chip in enumerate(chips):
                copy(a, 4 + j, (*chip, 1 - c), me).wait_recv()
        for cp in first + passed:
            cp.wait_send()
        for cp in mine:
            cp.wait()

    return _pcall_comm(
        body, name=name,
        in_specs=[hbm] * n, out_specs=[hbm] * n,
        out_shape=[_sds((8,) + x.shape, x.dtype) for x in xs],
        scratch_shapes=[pltpu.SemaphoreType.DMA((n, 7)), pltpu.SemaphoreType.DMA((n, 7)),
                        pltpu.SemaphoreType.DMA((n,))],
    )(*xs)


def _exchange_pair(gs, name):
    n = len(gs)
    hbm = pl.BlockSpec(memory_space=pl.ANY)

    def body(*refs):
        g_refs, out_refs = refs[:n], refs[n:2 * n]
        send, recv = refs[2 * n:]
        x, y, c = _position()
        copies = []
        for a in range(n):
            for p in range(4):
                copies.append(pltpu.make_async_remote_copy(
                    src_ref=g_refs[a].at[p, 1 - c], dst_ref=out_refs[a].at[p],
                    send_sem=send.at[a, p], recv_sem=recv.at[a, p],
                    device_id=(x, y, 1 - c), device_id_type=MESH))
        for cp in copies:
            cp.start()
        for cp in copies:
            cp.wait()

    return _pcall_comm(
        body, name=name,
        in_specs=[hbm] * n, out_specs=[hbm] * n,
        out_shape=[_sds((4,) + g.shape[2:], g.dtype) for g in gs],
        scratch_shapes=[pltpu.SemaphoreType.DMA((n, 4)), pltpu.SemaphoreType.DMA((n, 4))],
    )(*gs)


def _exchange_chips(ps, name):
    n = len(ps)
    hbm = pl.BlockSpec(memory_space=pl.ANY)

    def body(*refs):
        p_refs, out_refs = refs[:n], refs[n:2 * n]
        send, recv = refs[2 * n:]
        x, y, c = _position()
        chips = [(1 - x, y), (x, 1 - y), (1 - x, 1 - y)]
        copies = []
        for a in range(n):
            for j, (px, py) in enumerate(chips):
                copies.append(pltpu.make_async_remote_copy(
                    src_ref=p_refs[a].at[2 * px + py], dst_ref=out_refs[a].at[j],
                    send_sem=send.at[a, j], recv_sem=recv.at[a, j],
                    device_id=(px, py, c), device_id_type=MESH))
        for cp in copies:
            cp.start()
        for cp in copies:
            cp.wait()

    return _pcall_comm(
        body, name=name,
        in_specs=[hbm] * n, out_specs=[hbm] * n,
        out_shape=[_sds((3,) + p.shape[1:], p.dtype) for p in ps],
        scratch_shapes=[pltpu.SemaphoreType.DMA((n, 3)), pltpu.SemaphoreType.DMA((n, 3))],
    )(*ps)


def _pair_sum(g, got, core):
    _, _, R, C = g.shape
    tr = min(512, R)

    def body(core_ref, g_ref, r_ref, out_ref):
        out_ref[...] = g_ref[...] + r_ref[...]

    return _pcall(
        body, name="pair_sum",
        grid_spec=pltpu.PrefetchScalarGridSpec(
            num_scalar_prefetch=1, grid=(4, R // tr),
            in_specs=[pl.BlockSpec((None, None, tr, C), lambda p, i, cr: (p, cr[0], i, 0)),
                      pl.BlockSpec((None, tr, C), lambda p, i, cr: (p, i, 0))],
            out_specs=pl.BlockSpec((None, tr, C), lambda p, i, cr: (p, i, 0))),
        out_shape=_sds((4, R, C), F32),
        compiler_params=_cp("parallel", "parallel"),
    )(core, g, got)


def _adamw_math(w, g, m, v):
    m = B1 * m + (1.0 - B1) * g
    v = B2 * v + (1.0 - B2) * (g * g)
    m_hat = m / (1.0 - B1 ** STEP)
    v_hat = v / (1.0 - B2 ** STEP)
    delta = -LR * (m_hat / (jnp.sqrt(v_hat) + EPS) + WD * w)
    return delta, m, v


def _chip_sum_adamw(p, got, chip, w, m, v):
    R, C = w.shape
    tr = min(512, R)

    def body(chip_ref, p_ref, r_ref, w_ref, m_ref, v_ref, g_out, d_out, m_out, v_out):
        g = ((p_ref[...] + r_ref[0]) + r_ref[1]) + r_ref[2]
        d, mn, vn = _adamw_math(w_ref[...], g, m_ref[...], v_ref[...])
        g_out[...], d_out[...], m_out[...], v_out[...] = g, d, mn, vn

    blk = pl.BlockSpec((tr, C), lambda i, ch: (i, 0))
    return _pcall(
        body, name="chip_sum_adamw",
        grid_spec=pltpu.PrefetchScalarGridSpec(
            num_scalar_prefetch=1, grid=(R // tr,),
            in_specs=[pl.BlockSpec((None, tr, C), lambda i, ch: (ch[0], i, 0)),
                      pl.BlockSpec((3, tr, C), lambda i, ch: (0, i, 0)), blk, blk, blk],
            out_specs=[blk] * 4),
        out_shape=[_sds((R, C), F32)] * 4,
        compiler_params=_cp("parallel"),
    )(chip, p, got, w, m, v)


def _small_sum_adamw(parts, w, m, v):
    _, R, C = parts.shape

    def body(p_ref, w_ref, m_ref, v_ref, g_out, d_out, m_out, v_out):
        g = p_ref[0]
        for k in range(1, 8):
            g = g + p_ref[k]
        d, mn, vn = _adamw_math(w_ref[...], g, m_ref[...], v_ref[...])
        g_out[...], d_out[...], m_out[...], v_out[...] = g, d, mn, vn

    return _pcall(body, name="small_sum_adamw", out_shape=[_sds((R, C), F32)] * 4,
                  compiler_params=pltpu.CompilerParams(vmem_limit_bytes=VMEM_LIMIT))(parts, w, m, v)


BIG = ("w_in", "w_branch_a", "w_branch_b", "w_branch_c", "w_out", "w_up", "w_down")
ROW_SHARDED = ("w_out", "w_down")
SMALL = ("b_gate", "q_norm_b", "k_norm_b", "rpb_c", "ln1_g", "ln1_b", "ln2_g", "ln2_b")
NAMES = ("w_in", "b_gate", "q_norm_b", "k_norm_b", "rpb_c", "w_branch_a", "w_branch_b", "w_branch_c",
         "w_out", "ln1_g", "ln1_b", "w_up", "w_down", "ln2_g", "ln2_b")


def _full_weight(gathered, name, layer):
    blk = gathered.reshape(8, DEPTH, gathered.shape[1] // DEPTH, gathered.shape[2])[:, layer]
    if name in ROW_SHARDED:
        return blk.reshape(-1, blk.shape[2])
    return blk.transpose(1, 0, 2).reshape(blk.shape[1], -1)


def _chunks(grad, name):
    if name in ROW_SHARDED:
        return grad.reshape(8, grad.shape[0] // 8, grad.shape[1])
    return grad.reshape(grad.shape[0], 8, grad.shape[1] // 8).transpose(1, 0, 2)


def _layer_fwd(x, xb, W, P, tabs, gm):
    hq, = _mm(xb, W["w_qkv"], "nn", [F32], 1024, 768, 1024, name="in_qkv")
    hg, = _mm(xb, W["w_gate"], "nn", [F32], 1024, 1024, 1024, name="in_gate")
    tab_a, tab_b = tabs
    prepped = _prep_fwd(hq, tab_a, tab_b, P["qn"], P["kn"], gm)
    T = x.shape[0]
    tb = min(FULL_T, T)
    oa, lse_a, qa = _dilated_fwd(*prepped[0:3])
    qb_t, kb_t, vb_t = prepped[3:6]
    qb, kb, vb = _chunked_t(qb_t, tb), _to_heads(kb_t), _chunked_t(vb_t, tb)
    vb1 = jnp.concatenate([vb, jnp.ones((vb.shape[0], vb.shape[1], V_ROWS - HD, tb), vb.dtype)], axis=2)
    ob, lse_b = _attn_full_fwd(qb, kb, vb1)
    qc, kc, vc = [_to_heads(t) for t in prepped[6:9]]
    bias_c = _nbr_bias(P["rpb"], BAND_T)
    oc, lse_c = _attn_band_fwd(qc, kc, vc, bias_c, 1, True)
    oa_t, ob_t, oc_t = oa.transpose(2, 0, 1).reshape(T, A_W), _unchunk_t(ob), _from_heads(oc)
    qb = (_to_heads(qb_t), qb, kb, kb.transpose(0, 2, 1), _to_heads(vb_t))
    ka = va = None
    merged = _merge_fwd(oa_t, ob_t, oc_t, hg, P["bg"], W["w_branch_a"], W["w_branch_b"], W["w_branch_c"])
    x1, x1b, xh1, rs1 = _lin_ln(merged, W["w_out"], x, P["ln1_g"], P["ln1_b"])

    def relu2(acc):
        r = jnp.maximum(acc, 0.0)
        return r * r, r

    f, r = _mm(x1b, W["w_up"], "nn", [BF, BF], 1024, 1024, 1024, epilogue=relu2, name="mlp_up")
    x2, x2b, xh2, rs2 = _lin_ln(f, W["w_down"], x1, P["ln2_g"], P["ln2_b"])
    saved = dict(xb=xb, hq=hq, hg=hg, qkv=(qa, ka, va, qb, kb, vb, qc, kc, vc), o=(oa, ob, oc),
                 lse=(lse_a, lse_b, lse_c), o_t=(oa_t, ob_t, oc_t), bias_c=bias_c, merged=merged,
                 xh1=xh1, rs1=rs1, x1b=x1b, f=f, r=r, xh2=xh2, rs2=rs2)
    return x2, x2b, saved


def _layer_bwd(dx2, S, W, P, tabs, gm):
    G = {}
    du2, du2b, G["ln2_g"], G["ln2_b"] = _ln_bwd(dx2, S["xh2"], S["rs2"], P["ln2_g"])
    G["w_down"], = _mm(S["f"], du2b, "tn", [F32], 1024, 1024, 512, name="dw_down")
    da, = _mm(du2b, W["w_down"], "nt", [BF], 1024, 1024, 1024,
              epilogue=lambda acc, r: (acc * (2.0 * r.astype(F32)),), extras=(S["r"],), name="d_act")
    G["w_up"], = _mm(S["x1b"], da, "tn", [F32], 1024, 1024, 512, name="dw_up")
    dx1, = _mm(da, W["w_up"], "nt", [F32], 1024, 1024, 1024,
               epilogue=lambda acc, d: (ALPHA * d + acc,), extras=(du2,), name="dx_mlp")
    du1, du1b, G["ln1_g"], G["ln1_b"] = _ln_bwd(dx1, S["xh1"], S["rs1"], P["ln1_g"])
    G["w_out"], = _mm(S["merged"], du1b, "tn", [F32], 1024, 1024, 512, name="dw_out")
    dm, = _mm(du1b, W["w_out"], "nt", [F32], 1024, 1024, 1024, name="d_merged")
    oa_t, ob_t, oc_t = S["o_t"]
    dya, dyb, dyc, doa, dob, doc, dhg, G["b_gate"] = _merge_bwd(
        dm, oa_t, ob_t, oc_t, S["hg"], P["bg"], W["w_branch_a"], W["w_branch_b"], W["w_branch_c"])
    G["w_branch_a"], = _mm(oa_t, dya, "tn", [F32], 256, 1024, 512, name="dw_branch_a")
    G["w_branch_b"], = _mm(ob_t, dyb, "tn", [F32], 512, 1024, 512, name="dw_branch_b")
    G["w_branch_c"], = _mm(oc_t, dyc, "tn", [F32], 256, 1024, 512, name="dw_branch_c")

    qa, ka, va, qb, kb, vb, qc, kc, vc = S["qkv"]
    oa, ob, oc = S["o"]
    lse_a, lse_b, lse_c = S["lse"]
    dob_h, doc_h = _to_heads(dob), _to_heads(doc)
    dqa, dka, dva = _dilated_bwd(doa, oa, lse_a, qa)
    q_b, qT_b, k_b, kT_b, v_b = qb
    dobT = _chunked_t(dob, ob.shape[-1])
    dqbT, dkb8, dvb8 = _attn_full_bwd(q_b, qT_b, k_b, kT_b, v_b, dob_h, dobT, lse_b, _attn_delta(dobT, ob))
    group_sum = lambda t: t.reshape(k_b.shape[0], -1, t.shape[1], HD).sum(1)
    dqb, dkb, dvb = _unchunk_t(dqbT), _from_heads(group_sum(dkb8)), _from_heads(group_sum(dvb8))
    bias_c = S["bias_c"]
    dqc, dl_c, dbias_c = _attn_band_dq(qc, kc, vc, oc, doc_h, lse_c, bias_c, 1, True)
    dkc, dvc = _attn_band_dkv(qc, kc, vc, doc_h, lse_c, dl_c, bias_c.transpose(0, 1, 3, 2), 1, True)
    G["rpb_c"] = _rpb_grad(dbias_c)

    tab_a, tab_b = tabs
    grads = [dqa, dka, dva, dqb, dkb, dvb] + [_from_heads(t) for t in (dqc, dkc, dvc)]
    dhq, dqn, dkn = _prep_bwd(S["hq"], grads, tab_a, tab_b, P["qn"], P["kn"], gm)
    G["q_norm_b"] = dqn.reshape(BQ_W // HD, HD).sum(0)
    G["k_norm_b"] = dkn.reshape(BKV_W // HD, HD).sum(0)
    dw_qkv, = _mm(S["xb"], dhq, "tn", [F32], 1024, 768, 512, name="dw_qkv")
    dw_gate, = _mm(S["xb"], dhg, "tn", [F32], 1024, 1024, 512, name="dw_gate")
    G["w_in"] = jnp.concatenate([dw_qkv, dw_gate], axis=1)
    dx_a, = _mm(dhq, W["w_qkv"], "nt", [F32], 1024, 1024, 768,
                epilogue=lambda acc, d: (ALPHA * d + acc,), extras=(du1,), name="dx_qkv")
    dx, = _mm(dhg, W["w_gate"], "nt", [F32], 1024, 1024, 1024,
              epilogue=lambda acc, d: (d + acc,), extras=(dx_a,), name="dx_gate")
    return dx, G


def _pack_small(vals):
    flat = jnp.concatenate([vals[n].reshape(-1).astype(F32) for n in SMALL])
    pad = (-flat.shape[0]) % (8 * 128)
    return jnp.pad(flat, (0, pad)).reshape(-1, 128)


def _unpack_small(packed, like):
    flat, out, off = packed.reshape(-1), {}, 0
    for n in SMALL:
        size = math.prod(like[n].shape)
        out[n] = flat[off:off + size].reshape(like[n].shape)
        off += size
    return out


def kernel(x, w_in, b_gate, q_norm_b, k_norm_b, rpb_c, w_branch_a, w_branch_b, w_branch_c, w_out, ln1_g, ln1_b, w_up, w_down, ln2_g, ln2_b, loss_target, m_w_in, m_b_gate, m_q_norm_b, m_k_norm_b, m_rpb_c, m_w_branch_a, m_w_branch_b, m_w_branch_c, m_w_out, m_ln1_g, m_ln1_b, m_w_up, m_w_down, m_ln2_g, m_ln2_b, v_w_in, v_b_gate, v_q_norm_b, v_k_norm_b, v_rpb_c, v_w_branch_a, v_w_branch_b, v_w_branch_c, v_w_out, v_ln1_g, v_ln1_b, v_w_up, v_w_down, v_ln2_g, v_ln2_b):
    w = dict(w_in=w_in, b_gate=b_gate, q_norm_b=q_norm_b, k_norm_b=k_norm_b, rpb_c=rpb_c,
             w_branch_a=w_branch_a, w_branch_b=w_branch_b, w_branch_c=w_branch_c, w_out=w_out,
             ln1_g=ln1_g, ln1_b=ln1_b, w_up=w_up, w_down=w_down, ln2_g=ln2_g, ln2_b=ln2_b)
    m = dict(w_in=m_w_in, b_gate=m_b_gate, q_norm_b=m_q_norm_b, k_norm_b=m_k_norm_b, rpb_c=m_rpb_c,
             w_branch_a=m_w_branch_a, w_branch_b=m_w_branch_b, w_branch_c=m_w_branch_c, w_out=m_w_out,
             ln1_g=m_ln1_g, ln1_b=m_ln1_b, w_up=m_w_up, w_down=m_w_down, ln2_g=m_ln2_g, ln2_b=m_ln2_b)
    v = dict(w_in=v_w_in, b_gate=v_b_gate, q_norm_b=v_q_norm_b, k_norm_b=v_k_norm_b, rpb_c=v_rpb_c,
             w_branch_a=v_w_branch_a, w_branch_b=v_w_branch_b, w_branch_c=v_w_branch_c, w_out=v_w_out,
             ln1_g=v_ln1_g, ln1_b=v_ln1_b, w_up=v_w_up, w_down=v_w_down, ln2_g=v_ln2_g, ln2_b=v_ln2_b)
    T = x.shape[1]
    xc, yc, cc = _position()

    flat2 = lambda a: a.reshape(-1, a.shape[-1])
    gathered = _all_gather([flat2(w[n]).astype(BF) for n in BIG], "gather_weights")
    gathered = dict(zip(BIG, gathered))

    tabs = _rope_tables(T)
    gm = _group_mean_matrix()

    Ws, Ps = [], []
    for l in range(DEPTH):
        W = {n: _full_weight(gathered[n], n, l) for n in BIG}
        W["w_qkv"], W["w_gate"] = W["w_in"][:, :QKV], W["w_in"][:, QKV:]
        Ws.append(W)
        Ps.append(dict(qn=jnp.tile(q_norm_b[l][None], (1, 2)), kn=jnp.tile(k_norm_b[l][None], (1, 2)),
                       rpb=rpb_c[l], bg=b_gate[l][None], ln1_g=ln1_g[l][None], ln1_b=ln1_b[l][None],
                       ln2_g=ln2_g[l][None], ln2_b=ln2_b[l][None]))

    h = x[0]
    hb = h.astype(BF)
    saved = []
    for l in range(DEPTH):
        h, hb, S = _layer_fwd(h, hb, Ws[l], Ps[l], tabs, gm)
        saved.append(S)
    dy, sq = _loss_grad(h, loss_target[0])
    loss = lax.psum(0.5 / D * jnp.sum(sq), AXES)
    grads = [None] * DEPTH
    for l in reversed(range(DEPTH)):
        dy, grads[l] = _layer_bwd(dy, saved[l], Ws[l], Ps[l], tabs, gm)
    grad_x = dy[None]

    core = cc.reshape(1).astype(jnp.int32)
    chip = (2 * xc + yc).reshape(1).astype(jnp.int32)
    chunked = []
    for n in BIG:
        g = jnp.stack([_chunks(grads[l][n], n) for l in range(DEPTH)], axis=1)
        chunked.append(g.reshape(4, 2, DEPTH * g.shape[2], g.shape[3]))
    from_sibling = _exchange_pair(chunked, "grads_to_sibling")
    pair = [_pair_sum(g, r, core) for g, r in zip(chunked, from_sibling)]
    from_chips = _exchange_chips(pair, "grads_to_chips")
    out_g, out_d, out_m, out_v = {}, {}, {}, {}
    for n, p, r in zip(BIG, pair, from_chips):
        res = _chip_sum_adamw(p, r, chip, flat2(w[n]), flat2(m[n]), flat2(v[n]))
        out_g[n], out_d[n], out_m[n], out_v[n] = [t.reshape(w[n].shape) for t in res]

    part = _pack_small({n: jnp.stack([grads[l][n].reshape(w[n].shape[1:]) for l in range(DEPTH)]) for n in SMALL})
    parts, = _all_gather([part], "gather_small_grads")
    res = _small_sum_adamw(parts, _pack_small(w), _pack_small(m), _pack_small(v))
    for dst, packed in zip((out_g, out_d, out_m, out_v), res):
        dst.update(_unpack_small(packed, w))

    return (loss, grad_x, *[out_g[n] for n in NAMES], *[out_d[n] for n in NAMES],
            *[out_m[n] for n in NAMES], *[out_v[n] for n in NAMES])
```

```python
import functools
import math

import numpy as np
import jax
import jax.numpy as jnp
from jax import lax
from jax.experimental import pallas as pl
from jax.experimental.pallas import tpu as pltpu

F32 = jnp.float32
BF = jnp.bfloat16
HI = lax.Precision.HIGHEST
NEG = -1e30
MESH = pl.DeviceIdType.MESH
AXES = ("x", "y", "c")

D = 1024
DEPTH = 4
HD = 64
A_W, BQ_W, BKV_W, C_W = 256, 512, 128, 256
QKV = 2304
GATE = 3072
D_FF = 4096
GRID_W = 64
ALPHA = (2 * DEPTH) ** 0.25
LN_EPS = 1e-5
RMS_EPS = 1e-6
SCALE = HD ** -0.5
ROPE_THETA = 500000.0
AXIAL_THETA = 10000.0
A_CONFIGS = ((128, 1), (512, 4), (2048, 16))
LR, B1, B2, EPS, WD, STEP = 0.001, 0.9, 0.999, 1e-08, 0.01, 10

VMEM_LIMIT = 56 * 1024 * 1024
BAND_T = 256


def _pcall(body, **kw):
    return pl.pallas_call(body, **kw)


def _pcall_comm(body, **kw):
    return pl.pallas_call(body, **kw)


def _cp(*sem):
    return pltpu.CompilerParams(dimension_semantics=sem, vmem_limit_bytes=VMEM_LIMIT)


def _sds(shape, dtype):
    return jax.ShapeDtypeStruct(shape, dtype)


def _mm(a, b, dims, outs, tm, tn, tk, epilogue=None, extras=(), name="mm"):
    if dims == "tn":
        K, M = a.shape
    else:
        M, K = a.shape
    N = b.shape[0] if dims == "nt" else b.shape[1]
    tm, tn, tk = min(tm, M), min(tn, N), min(tk, K)
    assert M % tm == 0 and N % tn == 0 and K % tk == 0, (name, M, N, K, tm, tn, tk)
    nk = K // tk
    ne, no = len(extras), len(outs)

    def body(a_ref, b_ref, *rest):
        extra_refs, out_refs = rest[:ne], rest[ne:ne + no]
        av, bv = a_ref[...].astype(BF), b_ref[...].astype(BF)
        if dims == "nn":
            p = jnp.dot(av, bv, preferred_element_type=F32)
        elif dims == "nt":
            p = lax.dot_general(av, bv, (((1,), (1,)), ((), ())), preferred_element_type=F32)
        else:
            p = lax.dot_general(av, bv, (((0,), (0,)), ((), ())), preferred_element_type=F32)

        def finish(acc):
            res = epilogue(acc, *[r[...] for r in extra_refs]) if epilogue else (acc,)
            for o, r in zip(out_refs, res):
                o[...] = r.astype(o.dtype)

        if nk == 1:
            finish(p)
        else:
            acc_ref = rest[-1]
            k = pl.program_id(2)

            @pl.when(k == 0)
            def _():
                acc_ref[...] = p

            @pl.when(k > 0)
            def _():
                acc_ref[...] += p

            @pl.when(k == nk - 1)
            def _():
                finish(acc_ref[...])

    if dims == "tn":
        a_spec = pl.BlockSpec((tk, tm), lambda i, j, k: (k, i))
    else:
        a_spec = pl.BlockSpec((tm, tk), lambda i, j, k: (i, k))
    if dims == "nt":
        b_spec = pl.BlockSpec((tn, tk), lambda i, j, k: (j, k))
    else:
        b_spec = pl.BlockSpec((tk, tn), lambda i, j, k: (k, j))
    o_spec = pl.BlockSpec((tm, tn), lambda i, j, k: (i, j))
    res = _pcall(
        body, name=name, grid=(M // tm, N // tn, nk),
        in_specs=[a_spec, b_spec] + [o_spec] * ne,
        out_specs=[o_spec] * no,
        out_shape=[_sds((M, N), dt) for dt in outs],
        scratch_shapes=[pltpu.VMEM((tm, tn), F32)] if nk > 1 else [],
        compiler_params=_cp("parallel", "parallel", "arbitrary"),
    )(a, b, *extras)
    return res


def _rope_tables(T):
    pos = jnp.arange(T)

    def cs(p, theta, half):
        inv = theta ** (-jnp.arange(half, dtype=F32) / half)
        ang = p.astype(F32)[:, None] * inv[None, :]
        return jnp.cos(ang), jnp.sin(ang)

    ca, sa = cs(pos, ROPE_THETA, 8)
    one, zero, z8 = jnp.ones((T, 48), F32), jnp.zeros((T, 48), F32), jnp.zeros((T, 8), F32)
    tab_a = [jnp.concatenate(t, 1) for t in ([ca, ca, one], [-sa, z8, zero], [z8, sa, zero])]
    cr, sr = cs(pos // GRID_W, AXIAL_THETA, 16)
    cc, sc = cs(pos % GRID_W, AXIAL_THETA, 16)
    z16 = jnp.zeros((T, 16), F32)
    tab_b = [jnp.concatenate(t, 1) for t in ([cr, cr, cc, cc], [-sr, z16, -sc, z16], [z16, sr, z16, sc])]
    return [jnp.tile(t, (1, 2)) for t in tab_a], [jnp.tile(t, (1, 2)) for t in tab_b]


def _rot(x, C, S1, S2, k):
    return x * C + pltpu.roll(x, 128 - k, 1) * S1 + pltpu.roll(x, k, 1) * S2


def _rot_t(d, C, S1, S2, k):
    return d * C + pltpu.roll(d * S1, k, 1) + pltpu.roll(d * S2, 128 - k, 1)


def _group_mean_matrix():
    m = np.zeros((128, 128), np.float32)
    m[:64, :64] = 1.0 / 64
    m[64:, 64:] = 1.0 / 64
    return jnp.asarray(m)


def _prep_fwd(hq, tab_a, tab_b, qn, kn, gm):
    T = hq.shape[0]
    tt = min(256, T)
    widths = [A_W, A_W, A_W, BQ_W, BKV_W, BKV_W, C_W, C_W, C_W]

    def body(h_ref, ca, s1a, s2a, cb, s1b, s2b, qn_ref, kn_ref, gm_ref,
             qa, ka, va, qb, kb, vb, qc, kc, vc):
        def col(off, j):
            return h_ref[:, off + 128 * j: off + 128 * (j + 1)]

        for j in range(2):
            sl = slice(128 * j, 128 * (j + 1))
            qa[:, sl] = (_rot(col(0, j), ca[...], s1a[...], s2a[...], 8) * SCALE).astype(qa.dtype)
            ka[:, sl] = _rot(col(256, j), ca[...], s1a[...], s2a[...], 8).astype(ka.dtype)
            va[:, sl] = col(512, j).astype(va.dtype)
            qc[:, sl] = (col(1536, j) * SCALE).astype(qc.dtype)
            kc[:, sl] = col(1792, j).astype(kc.dtype)
            vc[:, sl] = col(2048, j).astype(vc.dtype)

        def normed(x, w):
            ms = jnp.dot(x * x, gm_ref[...], precision=HI, preferred_element_type=F32)
            return x * lax.rsqrt(ms + RMS_EPS) * w

        for j in range(4):
            y = normed(col(768, j), qn_ref[...])
            qb[:, 128 * j:128 * (j + 1)] = (_rot(y, cb[...], s1b[...], s2b[...], 16) * SCALE).astype(qb.dtype)
        y = normed(col(1280, 0), kn_ref[...])
        kb[...] = _rot(y, cb[...], s1b[...], s2b[...], 16).astype(kb.dtype)
        vb[...] = col(1408, 0).astype(vb.dtype)

    row = lambda w: pl.BlockSpec((tt, w), lambda i: (i, 0))
    const = lambda s: pl.BlockSpec(s, lambda i: (0, 0))
    return _pcall(
        body, name="prep_fwd", grid=(T // tt,),
        in_specs=[row(QKV)] + [row(128)] * 6 + [const((1, 128))] * 2 + [const((128, 128))],
        out_specs=[row(w) for w in widths],
        out_shape=[_sds((T, w), BF) for w in widths],
        compiler_params=_cp("parallel"),
    )(hq, *tab_a, *tab_b, qn, kn, gm)


def _prep_bwd(hq, grads, tab_a, tab_b, qn, kn, gm):
    T = hq.shape[0]
    tt = min(256, T)
    widths = [A_W, A_W, A_W, BQ_W, BKV_W, BKV_W, C_W, C_W, C_W]

    def body(h_ref, dqa, dka, dva, dqb, dkb, dvb, dqc, dkc, dvc,
             ca, s1a, s2a, cb, s1b, s2b, qn_ref, kn_ref, gm_ref, dh, dqn, dkn):
        i = pl.program_id(0)

        @pl.when(i == 0)
        def _():
            dqn[...] = jnp.zeros_like(dqn)
            dkn[...] = jnp.zeros_like(dkn)

        def put(off, j, val):
            dh[:, off + 128 * j: off + 128 * (j + 1)] = val.astype(dh.dtype)

        for j in range(2):
            sl = slice(128 * j, 128 * (j + 1))
            put(0, j, _rot_t(dqa[:, sl] * SCALE, ca[...], s1a[...], s2a[...], 8))
            put(256, j, _rot_t(dka[:, sl], ca[...], s1a[...], s2a[...], 8))
            put(512, j, dva[:, sl])
            put(1536, j, dqc[:, sl] * SCALE)
            put(1792, j, dkc[:, sl])
            put(2048, j, dvc[:, sl])

        def norm_bwd(x, w, e):
            ms = jnp.dot(x * x, gm_ref[...], precision=HI, preferred_element_type=F32)
            r = lax.rsqrt(ms + RMS_EPS)
            n = x * r
            dn = e * w
            proj = jnp.dot(dn * n, gm_ref[...], precision=HI, preferred_element_type=F32)
            return r * (dn - n * proj), jnp.sum(e * n, axis=0, keepdims=True)

        for j in range(4):
            sl = slice(128 * j, 128 * (j + 1))
            e = _rot_t(dqb[:, sl] * SCALE, cb[...], s1b[...], s2b[...], 16)
            dx, dw = norm_bwd(h_ref[:, 768 + 128 * j: 768 + 128 * (j + 1)], qn_ref[...], e)
            put(768, j, dx)
            dqn[:, sl] += dw
        e = _rot_t(dkb[...], cb[...], s1b[...], s2b[...], 16)
        dx, dw = norm_bwd(h_ref[:, 1280:1408], kn_ref[...], e)
        put(1280, 0, dx)
        dkn[...] += dw
        put(1408, 0, dvb[...])

    row = lambda w: pl.BlockSpec((tt, w), lambda i: (i, 0))
    const = lambda s: pl.BlockSpec(s, lambda i: (0, 0))
    return _pcall(
        body, name="prep_bwd", grid=(T // tt,),
        in_specs=[row(QKV)] + [row(w) for w in widths] + [row(128)] * 6
        + [const((1, 128))] * 2 + [const((128, 128))],
        out_specs=[row(QKV), const((1, BQ_W)), const((1, BKV_W))],
        out_shape=[_sds((T, QKV), BF), _sds((1, BQ_W), F32), _sds((1, BKV_W), F32)],
        compiler_params=_cp("arbitrary"),
    )(hq, *grads, *tab_a, *tab_b, qn, kn, gm)


def _to_heads(x):
    T, W = x.shape
    return x.reshape(T, W // HD, HD).transpose(1, 0, 2)


def _from_heads(x):
    H, T, _ = x.shape
    return x.transpose(1, 0, 2).reshape(T, H * HD)


def _rows(x, t):
    H, T, _ = x.shape
    return x.reshape(H, T // t, 1, t)


def _nt(a, b):
    return lax.dot_general(a, b, (((1,), (1,)), ((), ())), preferred_element_type=F32)


def _chunked_t(x, t):
    T, W = x.shape
    return x.reshape(T // t, t, W // HD, HD).transpose(2, 0, 3, 1)


def _unchunk_t(x):
    H, n, _, t = x.shape
    return x.transpose(1, 3, 0, 2).reshape(n * t, H * HD)


FULL_T = 512
FULL_TK = 512
FULL_HEADS = 1
V_ROWS = 72


def _attn_full_fwd(qT, k, vT1):
    Hq, nq, _, t = qT.shape
    Hk, T, _ = k.shape
    G = Hq // Hk
    nk, tk = vT1.shape[1], vT1.shape[3]

    HB = FULL_HEADS
    assert G % HB == 0

    def body(q_ref, k_ref, v_ref, o_ref, lse_ref, *acc_refs):
        for acc_ref in acc_refs:
            acc_ref[...] = jnp.zeros((V_ROWS, t), F32)

        def scores(j, b):
            sT = jnp.dot(k_ref[pl.ds(pl.multiple_of(j * tk, tk), tk), :], q_ref[b], preferred_element_type=F32)
            return sT, jnp.max(sT, axis=0, keepdims=True)

        def update(j, b, scored, m_old):
            sT, m_tile = scored
            m_new = jnp.maximum(m_old, m_tile)
            pT = jnp.exp(sT - m_new).astype(BF)
            acc_refs[b][...] = (jnp.exp(m_old - m_new) * acc_refs[b][...]
                                + jnp.dot(v_ref[j], pT, preferred_element_type=F32))
            return m_new

        def step(j, carry):
            ms, ss = carry
            nxt = jnp.minimum(j + 1, nk - 1)
            new_s = tuple(scores(nxt, b) for b in range(HB))
            new_m = tuple(update(j, b, ss[b], ms[b]) for b in range(HB))
            return new_m, new_s

        init = (tuple(jnp.full((1, t), NEG, F32) for _ in range(HB)), tuple(scores(0, b) for b in range(HB)))
        ms, _ = lax.fori_loop(0, nk, step, init)
        for b in range(HB):
            l = acc_refs[b][pl.ds(HD, 1), :]
            o_ref[b] = (acc_refs[b][pl.ds(0, HD), :] / l).astype(o_ref.dtype)
            lse_ref[b] = ms[b] + jnp.log(l)

    qs = pl.BlockSpec((HB, None, HD, t), lambda h, i: (h, i, 0, 0))
    return _pcall(
        body, name="attn_full_fwd", grid=(Hq // HB, nq),
        in_specs=[qs, pl.BlockSpec((None, T, HD), lambda h, i: (h * HB // G, 0, 0)),
                  pl.BlockSpec((None, nk, V_ROWS, tk), lambda h, i: (h * HB // G, 0, 0, 0))],
        out_specs=[qs, pl.BlockSpec((HB, None, 1, t), lambda h, i: (h, i, 0, 0))],
        out_shape=[_sds((Hq, nq, HD, t), BF), _sds((Hq, nq, 1, t), F32)],
        scratch_shapes=[pltpu.VMEM((V_ROWS, t), F32)] * HB,
        compiler_params=_cp("parallel", "parallel"),
    )(qT, k, vT1)


def _attn_delta(doT, oT):
    Hq, nq, _, t = doT.shape

    def body(do_ref, o_ref, dl_ref):
        dl_ref[...] = jnp.sum(do_ref[...].astype(F32) * o_ref[...].astype(F32), axis=0, keepdims=True)

    qs = pl.BlockSpec((None, None, HD, t), lambda h, i: (h, i, 0, 0))
    rs = pl.BlockSpec((None, None, 1, t), lambda h, i: (h, i, 0, 0))
    return _pcall(body, name="attn_delta", grid=(Hq, nq), in_specs=[qs, qs], out_specs=rs,
                  out_shape=_sds((Hq, nq, 1, t), F32), compiler_params=_cp("parallel", "parallel"))(doT, oT)


def _attn_full_bwd(q, qT, k, kT, v, do, doT, lse, delta):
    Hq, nq, _, t = qT.shape
    Hk, T, _ = k.shape
    G = Hq // Hk

    UNROLL = 1

    def body(q_ref, qT_ref, do_ref, doT_ref, lse_ref, dl_ref, k_ref, kT_ref, v_ref,
             dq_ref, dk_ref, dv_ref, dk_acc, dv_acc):
        @pl.when(pl.program_id(1) == 0)
        def _():
            dq_ref[...] = jnp.zeros_like(dq_ref)

        kv, kTv, vv = k_ref[...], kT_ref[...], v_ref[...]
        dk_acc[...] = jnp.zeros((t, HD), F32)
        dv_acc[...] = jnp.zeros((t, HD), F32)

        def step(ii, carry):
            dks, dvs = [], []
            for u in range(UNROLL):
                i = ii * UNROLL + u
                rows = pl.ds(pl.multiple_of(i * t, t), t)
                pT = jnp.exp(jnp.dot(kv, qT_ref[i], preferred_element_type=F32) - lse_ref[i])
                dvs.append(jnp.dot(pT.astype(BF), do_ref[rows, :], preferred_element_type=F32))
                dpT = jnp.dot(vv, doT_ref[i], preferred_element_type=F32)
                dsT = (pT * (dpT - dl_ref[i])).astype(BF)
                dks.append(jnp.dot(dsT, q_ref[rows, :], preferred_element_type=F32))
                dq_ref[i] += jnp.dot(kTv, dsT, preferred_element_type=F32)
            dk_acc[...] += sum(dks[1:], dks[0])
            dv_acc[...] += sum(dvs[1:], dvs[0])
            return carry

        lax.fori_loop(0, nq // UNROLL, step, 0)
        dk_ref[...] = dk_acc[...]
        dv_ref[...] = dv_acc[...]

    tok = pl.BlockSpec((None, T, HD), lambda h, j: (h, 0, 0))
    chk = pl.BlockSpec((None, nq, HD, t), lambda h, j: (h, 0, 0, 0))
    row = pl.BlockSpec((None, nq, 1, t), lambda h, j: (h, 0, 0, 0))
    kvs = pl.BlockSpec((None, t, HD), lambda h, j: (h // G, j, 0))
    out = pl.BlockSpec((None, t, HD), lambda h, j: (h, j, 0))
    return _pcall(
        body, name="attn_full_bwd", grid=(Hq, T // t),
        in_specs=[tok, chk, tok, chk, row, row, kvs, pl.BlockSpec((None, HD, t), lambda h, j: (h // G, 0, j)), kvs],
        out_specs=[chk, out, out],
        out_shape=[_sds((Hq, nq, HD, t), F32), _sds((Hq, T, HD), F32), _sds((Hq, T, HD), F32)],
        scratch_shapes=[pltpu.VMEM((t, HD), F32)] * 2,
        compiler_params=_cp("parallel", "arbitrary"),
    )(q, qT, do, doT, lse, delta, k, kT, v)


DIL_Q = 128
DIL_K = 256
DIL_R = 64


def _vh(x, d):
    T = x.shape[0]
    return x.reshape(T // d, d, 4, HD).transpose(1, 2, 0, 3).reshape(4 * d, T // d, HD)


def _vh_inv(y, d):
    L = y.shape[1]
    return y.reshape(d, 4, L, HD).transpose(2, 0, 1, 3).reshape(L * d, 4 * HD)


def _vh_chunks(x, d):
    y = _vh(x, d)
    return y.reshape(y.shape[0], y.shape[1] // DIL_Q, DIL_Q, HD).transpose(0, 1, 3, 2)


def _chunks_to_dims(c, d):
    _, nq, R, _ = c.shape
    return c.reshape(d, 4, nq, R, DIL_Q).transpose(1, 3, 2, 4, 0).reshape(4, R, nq * DIL_Q * d)


def _dims_to_chunks(x, d):
    _, R, T = x.shape
    nq = T // (DIL_Q * d)
    return x.reshape(4, R, nq, DIL_Q, d).transpose(4, 0, 2, 1, 3).reshape(4 * d, nq, R, DIL_Q)


def _chunks_to_tokens(c, d):
    _, nq, _, _ = c.shape
    return c.reshape(d, 4, nq, HD, DIL_Q).transpose(2, 4, 0, 1, 3).reshape(nq * DIL_Q * d, 4 * HD)


def _dil_window(i, L):
    start = pl.multiple_of(jnp.clip(i * DIL_Q - DIL_R, 0, L - DIL_K), DIL_R)
    kk = start + lax.broadcasted_iota(jnp.int32, (DIL_K, 1), 0)
    qq = i * DIL_Q + lax.broadcasted_iota(jnp.int32, (1, DIL_Q), 1)
    return start, jnp.abs(kk - qq) <= DIL_R


def _tn(a, b):
    return lax.dot_general(a, b, (((0,), (0,)), ((), ())), preferred_element_type=F32)


def _dil_fwd(qT, k, v1):
    V, nq, _, _ = qT.shape
    L = k.shape[1]
    assert L >= DIL_K

    unroll = 4 if nq % 4 == 0 else 1

    def body(q_ref, k_ref, v_ref, o_ref, lse_ref):
        def tile(i):
            start, mask = _dil_window(i, L)
            win = pl.ds(start, DIL_K)
            sT = jnp.where(mask, jnp.dot(k_ref[win, :], q_ref[i], preferred_element_type=F32), NEG)
            m = jnp.max(sT, axis=0, keepdims=True)
            pT = jnp.exp(sT - m).astype(BF)
            acc = _tn(v_ref[win, :], pT)
            l = jnp.max(acc[HD:HD + 8], axis=0, keepdims=True)
            o_ref[i] = acc[:HD] / l
            lse_ref[i] = m + jnp.log(l)

        def tiles(ii, carry):
            for u in range(unroll):
                tile(ii * unroll + u)
            return carry

        lax.fori_loop(0, nq // unroll, tiles, 0)

    chk = lambda r, dt: (pl.BlockSpec((None, nq, r, DIL_Q), lambda h: (h, 0, 0, 0)), _sds((V, nq, r, DIL_Q), dt))
    tok = lambda w: pl.BlockSpec((None, L, w), lambda h: (h, 0, 0))
    (o_spec, o_shape), (l_spec, l_shape) = chk(HD, F32), chk(1, F32)
    return _pcall(
        body, name=f"dil_fwd_{V // 4}", grid=(V,),
        in_specs=[chk(HD, BF)[0], tok(HD), tok(128)],
        out_specs=[o_spec, l_spec], out_shape=[o_shape, l_shape],
        compiler_params=_cp("parallel"),
    )(qT, k, v1)


def _dil_merge(os_, lses):
    _, _, T = os_[0].shape
    tt = min(1024, T)
    n = len(os_)

    def body(*refs):
        o_refs, l_refs, (o_out, l_out) = refs[:n], refs[n:2 * n], refs[2 * n:]
        m = l_refs[0][...]
        for r in l_refs[1:]:
            m = jnp.maximum(m, r[...])
        ws = [jnp.exp(r[...] - m) for r in l_refs]
        tot = ws[0]
        for w_ in ws[1:]:
            tot = tot + w_
        acc = ws[0] * o_refs[0][...]
        for w_, o in zip(ws[1:], o_refs[1:]):
            acc = acc + w_ * o[...]
        o_out[...] = (acc / tot).astype(o_out.dtype)
        l_out[...] = m + jnp.log(tot)

    os_spec = pl.BlockSpec((None, HD, tt), lambda h, i: (h, 0, i))
    ls_spec = pl.BlockSpec((None, 1, tt), lambda h, i: (h, 0, i))
    return _pcall(
        body, name="dil_merge", grid=(4, T // tt),
        in_specs=[os_spec] * n + [ls_spec] * n, out_specs=[os_spec, ls_spec],
        out_shape=[_sds((4, HD, T), BF), _sds((4, 1, T), F32)],
        compiler_params=_cp("parallel", "parallel"),
    )(*os_, *lses)


def _dims_delta(doT, oT):
    _, _, T = doT.shape
    tt = min(2048, T)

    def body(do_ref, o_ref, dl_ref):
        dl_ref[...] = jnp.sum(do_ref[...].astype(F32) * o_ref[...].astype(F32), axis=0, keepdims=True)

    spec = pl.BlockSpec((None, HD, tt), lambda h, i: (h, 0, i))
    return _pcall(body, name="dims_delta", grid=(4, T // tt), in_specs=[spec, spec],
                  out_specs=pl.BlockSpec((None, 1, tt), lambda h, i: (h, 0, i)),
                  out_shape=_sds((4, 1, T), F32), compiler_params=_cp("parallel", "parallel"))(doT, oT)


def _dil_bwd(q, qT, k, v, do, doT, lse, delta):
    V, nq, _, _ = qT.shape
    L = k.shape[1]
    unroll = 2 if nq % 2 == 0 else 1

    def body(q_ref, qT_ref, k_ref, v_ref, do_ref, doT_ref, lse_ref, dl_ref, dq_ref, dk_ref, dv_ref):
        dk_ref[...] = jnp.zeros_like(dk_ref)
        dv_ref[...] = jnp.zeros_like(dv_ref)

        def tile(i):
            start, mask = _dil_window(i, L)
            win = pl.ds(start, DIL_K)
            rows = pl.ds(pl.multiple_of(i * DIL_Q, DIL_Q), DIL_Q)
            kw = k_ref[win, :]
            sT = jnp.where(mask, jnp.dot(kw, qT_ref[i], preferred_element_type=F32), NEG)
            pT = jnp.exp(sT - lse_ref[i])
            dv = jnp.dot(pT.astype(BF), do_ref[rows, :], preferred_element_type=F32)
            dpT = jnp.dot(v_ref[win, :], doT_ref[i], preferred_element_type=F32)
            dsT = (pT * (dpT - dl_ref[i])).astype(BF)
            dk = jnp.dot(dsT, q_ref[rows, :], preferred_element_type=F32)
            dq_ref[i] = _tn(kw, dsT)
            return win, dk, dv

        def tiles(ii, carry):
            done = [tile(ii * unroll + u) for u in range(unroll)]
            for win, dk, dv in done:
                dk_ref[win, :] += dk
                dv_ref[win, :] += dv
            return carry

        lax.fori_loop(0, nq // unroll, tiles, 0)

    chk = lambda r: pl.BlockSpec((None, nq, r, DIL_Q), lambda h: (h, 0, 0, 0))
    tok = pl.BlockSpec((None, L, HD), lambda h: (h, 0, 0))
    return _pcall(
        body, name=f"dil_bwd_{V // 4}", grid=(V,),
        in_specs=[tok, chk(HD), tok, tok, tok, chk(HD), chk(1), chk(1)],
        out_specs=[chk(HD), tok, tok],
        out_shape=[_sds((V, nq, HD, DIL_Q), F32), _sds((V, L, HD), F32), _sds((V, L, HD), F32)],
        compiler_params=_cp("parallel"),
    )(q, qT, k, v, do, doT, lse, delta)


def _dilated_fwd(qa, ka, va):
    outs, lses, saved = [], [], []
    for _, d in A_CONFIGS:
        q, qT, k, v = _vh(qa, d), _vh_chunks(qa, d), _vh(ka, d), _vh(va, d)
        oT, lse = _dil_fwd(qT, k, jnp.concatenate([v, jnp.ones_like(v)], axis=2))
        outs.append(_chunks_to_dims(oT, d))
        lses.append(_chunks_to_dims(lse, d))
        saved.append((q, qT, k, v))
    o, lse = _dil_merge(outs, lses)
    return o, lse, saved


def _dilated_bwd(do_t, o, lse, saved):
    doT = do_t.reshape(do_t.shape[0], 4, HD).transpose(1, 2, 0)
    delta = _dims_delta(doT, o)
    dq = dk = dv = None
    for (_, d), (q, qT, k, v) in zip(A_CONFIGS, saved):
        dqT, dk_c, dv_c = _dil_bwd(q, qT, k, v, _vh(do_t, d), _dims_to_chunks(doT, d),
                                   _dims_to_chunks(lse, d), _dims_to_chunks(delta, d))
        parts = (_chunks_to_tokens(dqT, d), _vh_inv(dk_c, d), _vh_inv(dv_c, d))
        dq, dk, dv = parts if dq is None else (dq + parts[0], dk + parts[1], dv + parts[2])
    return dq, dk, dv


def _band_offsets(radius):
    offs = [0]
    for r in range(1, radius + 1):
        offs += [-r, r]
    return offs


def _dilated_bias(t):
    radius = max(w // 2 for w, _ in A_CONFIGS) // t
    tabs = []
    i = np.arange(t)
    for off in _band_offsets(radius):
        d = off * t + i[None, :] - i[:, None]
        mult = np.zeros((t, t), np.float32)
        for w, dil in A_CONFIGS:
            mult += ((d % dil) == 0) & (np.abs(d) <= w // 2)
        tabs.append(np.where(mult > 0, np.log(np.maximum(mult, 1.0)), NEG).astype(np.float32))
    return jnp.asarray(np.stack(tabs)[None]), radius


def _nbr_index(t):
    rpt = t // GRID_W
    i = np.arange(t)
    c0 = np.clip(i % GRID_W - 8, 0, GRID_W - 16)
    col_ok = ((i[None, :] % GRID_W) >= c0[:, None]) & ((i[None, :] % GRID_W) < c0[:, None] + 16)
    oks = []
    for off in _band_offsets(1):
        dr = (i[None, :] // GRID_W) - (i[:, None] // GRID_W) + rpt * off
        oks.append(col_ok & (np.abs(dr) <= 7))
    return np.stack(oks)


def _nbr_bias(rpb, t):
    rpt = t // GRID_W
    e1, e2 = _rpb_fold_matrices(t)
    ok = _nbr_index(t)
    padded = jnp.pad(rpb, ((0, 0), (0, 1), (0, 128 - rpb.shape[2]))).reshape(64, 128)

    def body(e2t_ref, rpb_ref, e1t_ref, out_ref):
        picked = jnp.dot(e2t_ref[...], rpb_ref[...], precision=HI, preferred_element_type=F32)
        out_ref[...] = jnp.dot(picked, e1t_ref[...], precision=HI, preferred_element_type=F32)

    n = e2.shape[1]
    sub = _pcall(body, name="rpb_expand", out_shape=_sds((n, GRID_W * GRID_W), F32),
                 compiler_params=pltpu.CompilerParams(vmem_limit_bytes=VMEM_LIMIT))(e2.T, padded, e1.T)
    tiles = sub.reshape(4, 3, rpt, rpt, GRID_W, GRID_W).transpose(0, 1, 2, 4, 3, 5).reshape(4, 3, t, t)
    return jnp.where(ok[None], tiles, NEG)


def _nbr_mask(qi, kb, t, rows, q_on_lanes):
    rpt = t // GRID_W
    qshape, kshape = ((1, t), (t, 1)) if q_on_lanes else ((t, 1), (1, t))
    rq = rpt * qi + lax.broadcasted_iota(jnp.int32, qshape, 1 if q_on_lanes else 0) // GRID_W
    rk = rpt * kb + lax.broadcasted_iota(jnp.int32, kshape, 0 if q_on_lanes else 1) // GRID_W
    r0 = jnp.clip(rq - 4, 0, rows - 8)
    return (rk >= r0) & (rk < r0 + 8)


def _attn_band_fwd(q, k, v, bias, radius, rowmask):
    H, T, _ = q.shape
    t = BAND_T
    nq = T // t
    Hb = bias.shape[0]
    offs = _band_offsets(radius)
    rows = T // GRID_W

    def body(q_ref, k_ref, v_ref, b_ref, o_ref, lse_ref, m_ref, l_ref, acc_ref):
        i = pl.program_id(1)
        qv = q_ref[...]
        m_ref[...] = jnp.full((t, 1), NEG, F32)
        l_ref[...] = jnp.zeros((t, 1), F32)
        acc_ref[...] = jnp.zeros((t, HD), F32)

        def tile(o, off):
            kb = i + off
            st = pl.multiple_of(kb * t, t)
            ks, vs = k_ref[pl.ds(st, t), :], v_ref[pl.ds(st, t), :]
            s = _nt(qv, ks) + b_ref[o]
            if rowmask:
                s = jnp.where(_nbr_mask(i, kb, t, rows, False), s, NEG)
            m_old = m_ref[...]
            m_new = jnp.maximum(m_old, jnp.max(s, axis=-1, keepdims=True))
            a = jnp.exp(m_old - m_new)
            p = jnp.exp(s - m_new)
            l_ref[...] = a * l_ref[...] + jnp.sum(p, axis=-1, keepdims=True)
            acc_ref[...] = a * acc_ref[...] + jnp.dot(p.astype(BF), vs, preferred_element_type=F32)
            m_ref[...] = m_new

        for o, off in enumerate(offs):
            if off == 0:
                tile(o, off)
            else:
                pl.when((i + off >= 0) & (i + off < nq))(functools.partial(tile, o, off))
        o_ref[...] = (acc_ref[...] / l_ref[...]).astype(o_ref.dtype)
        lse_ref[...] = m_ref[...] + jnp.log(l_ref[...])

    qs = pl.BlockSpec((None, t, HD), lambda h, i: (h, i, 0))
    kvs = pl.BlockSpec((None, T, HD), lambda h, i: (h, 0, 0))
    bs = pl.BlockSpec((None, len(offs), t, t), lambda h, i: (h if Hb > 1 else 0, 0, 0, 0))
    return _pcall(
        body, name="attn_band_fwd_c" if rowmask else "attn_band_fwd_a", grid=(H, nq),
        in_specs=[qs, kvs, kvs, bs],
        out_specs=[qs, pl.BlockSpec((None, t, 1), lambda h, i: (h, i, 0))],
        out_shape=[_sds((H, T, HD), BF), _sds((H, T, 1), F32)],
        scratch_shapes=[pltpu.VMEM((t, 1), F32), pltpu.VMEM((t, 1), F32), pltpu.VMEM((t, HD), F32)],
        compiler_params=_cp("parallel", "parallel"),
    )(q, k, v, bias)


def _attn_band_dq(q, k, v, o, do, lse, bias, radius, rowmask):
    H, T, _ = q.shape
    t = BAND_T
    nq = T // t
    Hb = bias.shape[0]
    offs = _band_offsets(radius)
    rows = T // GRID_W

    def body(q_ref, k_ref, v_ref, o_ref, do_ref, lse_ref, b_ref, dq_ref, dl_ref, *rest):
        db_ref = rest[0] if rowmask else None
        acc_ref = rest[-1]
        i = pl.program_id(1)
        qv, dov, lse = q_ref[...], do_ref[...], lse_ref[...]
        delta = jnp.sum(dov.astype(F32) * o_ref[...].astype(F32), axis=-1, keepdims=True)
        acc_ref[...] = jnp.zeros((t, HD), F32)
        if rowmask:
            @pl.when(i == 0)
            def _():
                db_ref[...] = jnp.zeros_like(db_ref)

        def tile(o, off):
            kb = i + off
            st = pl.multiple_of(kb * t, t)
            ks, vs = k_ref[pl.ds(st, t), :], v_ref[pl.ds(st, t), :]
            s = _nt(qv, ks) + b_ref[o]
            if rowmask:
                s = jnp.where(_nbr_mask(i, kb, t, rows, False), s, NEG)
            p = jnp.exp(s - lse)
            ds = p * (_nt(dov, vs) - delta)
            acc_ref[...] += jnp.dot(ds.astype(BF), ks, preferred_element_type=F32)
            if rowmask:
                db_ref[o] += ds

        for o, off in enumerate(offs):
            if off == 0:
                tile(o, off)
            else:
                pl.when((i + off >= 0) & (i + off < nq))(functools.partial(tile, o, off))
        dq_ref[...] = acc_ref[...]
        dl_ref[...] = delta

    qs = pl.BlockSpec((None, t, HD), lambda h, i: (h, i, 0))
    kvs = pl.BlockSpec((None, T, HD), lambda h, i: (h, 0, 0))
    cs = pl.BlockSpec((None, t, 1), lambda h, i: (h, i, 0))
    bs = pl.BlockSpec((None, len(offs), t, t), lambda h, i: (h if Hb > 1 else 0, 0, 0, 0))
    out_specs = [qs, cs]
    out_shape = [_sds((H, T, HD), F32), _sds((H, T, 1), F32)]
    if rowmask:
        out_specs.append(pl.BlockSpec((None, len(offs), t, t), lambda h, i: (h, 0, 0, 0)))
        out_shape.append(_sds((H, len(offs), t, t), F32))
    return _pcall(
        body, name="attn_band_dq_c" if rowmask else "attn_band_dq_a", grid=(H, nq),
        in_specs=[qs, kvs, kvs, qs, qs, cs, bs],
        out_specs=out_specs, out_shape=out_shape,
        scratch_shapes=[pltpu.VMEM((t, HD), F32)],
        compiler_params=_cp("parallel", "arbitrary"),
    )(q, k, v, o, do, lse, bias)


def _attn_band_dkv(q, k, v, do, lse, delta, bias_t, radius, rowmask):
    H, T, _ = q.shape
    t = BAND_T
    nq = T // t
    Hb = bias_t.shape[0]
    offs = _band_offsets(radius)
    rows = T // GRID_W
    lse_r, dl_r = _rows(lse, t), _rows(delta, t)

    def body(k_ref, v_ref, q_ref, do_ref, lse_ref, dl_ref, b_ref, dk_ref, dv_ref, dk_acc, dv_acc):
        jb = pl.program_id(1)
        kv, vv = k_ref[...], v_ref[...]
        dk_acc[...] = jnp.zeros((t, HD), F32)
        dv_acc[...] = jnp.zeros((t, HD), F32)

        def tile(o, off):
            qi = jb - off
            st = pl.multiple_of(qi * t, t)
            qs, dos = q_ref[pl.ds(st, t), :], do_ref[pl.ds(st, t), :]
            sT = _nt(kv, qs) + b_ref[o]
            if rowmask:
                sT = jnp.where(_nbr_mask(qi, jb, t, rows, True), sT, NEG)
            pT = jnp.exp(sT - lse_ref[qi])
            dv_acc[...] += jnp.dot(pT.astype(BF), dos, preferred_element_type=F32)
            dsT = pT * (_nt(vv, dos) - dl_ref[qi])
            dk_acc[...] += jnp.dot(dsT.astype(BF), qs, preferred_element_type=F32)

        for o, off in enumerate(offs):
            if off == 0:
                tile(o, off)
            else:
                pl.when((jb - off >= 0) & (jb - off < nq))(functools.partial(tile, o, off))
        dk_ref[...] = dk_acc[...]
        dv_ref[...] = dv_acc[...]

    kvs = pl.BlockSpec((None, t, HD), lambda h, j: (h, j, 0))
    qs = pl.BlockSpec((None, T, HD), lambda h, j: (h, 0, 0))
    rs = pl.BlockSpec((None, nq, 1, t), lambda h, j: (h, 0, 0, 0))
    bs = pl.BlockSpec((None, len(offs), t, t), lambda h, j: (h if Hb > 1 else 0, 0, 0, 0))
    return _pcall(
        body, name="attn_band_dkv_c" if rowmask else "attn_band_dkv_a", grid=(H, nq),
        in_specs=[kvs, kvs, qs, qs, rs, rs, bs],
        out_specs=[kvs, kvs],
        out_shape=[_sds((H, T, HD), F32)] * 2,
        scratch_shapes=[pltpu.VMEM((t, HD), F32)] * 2,
        compiler_params=_cp("parallel", "parallel"),
    )(k, v, q, do, lse_r, dl_r, bias_t)


def _rpb_fold_matrices(t):
    rpt = t // GRID_W
    e1 = np.zeros((GRID_W * GRID_W, 128), np.float32)
    ic, jc = np.meshgrid(np.arange(GRID_W), np.arange(GRID_W), indexing="ij")
    dc = (jc - ic + 15).reshape(-1)
    keep = (dc >= 0) & (dc <= 30)
    e1[np.arange(GRID_W * GRID_W)[keep], dc[keep]] = 1.0
    offs = _band_offsets(1)
    n = 4 * len(offs) * rpt * rpt
    e2 = np.zeros((64, n), np.float32)
    col = 0
    for h in range(4):
        for off in offs:
            for ib in range(rpt):
                for jb in range(rpt):
                    dr = jb - ib + rpt * off + 7
                    if 0 <= dr <= 14:
                        e2[h * 16 + dr, col] = 1.0
                    col += 1
    return jnp.asarray(e1), jnp.asarray(e2)


def _rpb_grad(dbias):
    t = dbias.shape[-1]
    rpt = t // GRID_W
    e1, e2 = _rpb_fold_matrices(t)
    sub = dbias.reshape(4, 3, rpt, GRID_W, rpt, GRID_W).transpose(0, 1, 2, 4, 3, 5)
    sub = sub.reshape(4 * 3 * rpt * rpt, GRID_W * GRID_W)

    def body(e2_ref, sub_ref, e1_ref, out_ref):
        diag = jnp.dot(sub_ref[...], e1_ref[...], precision=HI, preferred_element_type=F32)
        out_ref[...] = jnp.dot(e2_ref[...], diag, precision=HI, preferred_element_type=F32)

    out = _pcall(body, name="rpb_fold", out_shape=_sds((64, 128), F32),
                 compiler_params=pltpu.CompilerParams(vmem_limit_bytes=VMEM_LIMIT))(e2, sub, e1)
    return out.reshape(4, 16, 128)[:, :15, :31]


def _sigmoid(z):
    return 1.0 / (1.0 + jnp.exp(-z))


def _merge_fwd(oa, ob, oc, hg, bg, wa, wb, wc):
    T = oa.shape[0]
    tt = min(512, T)

    def body(oa_ref, ob_ref, oc_ref, hg_ref, bg_ref, wa_ref, wb_ref, wc_ref, out_ref):
        acc = None
        for k, (o_ref, w_ref) in enumerate(((oa_ref, wa_ref), (ob_ref, wb_ref), (oc_ref, wc_ref))):
            y = jnp.dot(o_ref[...], w_ref[...], preferred_element_type=F32)
            g = _sigmoid(hg_ref[:, D * k:D * (k + 1)] + bg_ref[:, D * k:D * (k + 1)])
            acc = g * y if acc is None else acc + g * y
        out_ref[...] = acc.astype(out_ref.dtype)

    row = lambda w: pl.BlockSpec((tt, w), lambda i: (i, 0))
    const = lambda a: pl.BlockSpec(a.shape, lambda i: (0, 0))
    return _pcall(
        body, name="merge_fwd", grid=(T // tt,),
        in_specs=[row(A_W), row(BQ_W), row(C_W), row(GATE), const(bg), const(wa), const(wb), const(wc)],
        out_specs=row(D), out_shape=_sds((T, D), BF),
        compiler_params=_cp("parallel"),
    )(oa, ob, oc, hg, bg, wa, wb, wc)


def _merge_bwd(dm, oa, ob, oc, hg, bg, wa, wb, wc):
    T = oa.shape[0]
    tt = min(256, T)

    def body(dm_ref, oa_ref, ob_ref, oc_ref, hg_ref, bg_ref, wa_ref, wb_ref, wc_ref,
             dya, dyb, dyc, doa, dob, doc, dhg, dbg):
        @pl.when(pl.program_id(0) == 0)
        def _():
            dbg[...] = jnp.zeros_like(dbg)

        dmv = dm_ref[...]
        for k, (o_ref, w_ref, dy_ref, do_ref) in enumerate(
                ((oa_ref, wa_ref, dya, doa), (ob_ref, wb_ref, dyb, dob), (oc_ref, wc_ref, dyc, doc))):
            sl = slice(D * k, D * (k + 1))
            y = jnp.dot(o_ref[...], w_ref[...], preferred_element_type=F32)
            g = _sigmoid(hg_ref[:, sl] + bg_ref[:, sl])
            dy = (dmv * g).astype(BF)
            dy_ref[...] = dy
            do_ref[...] = _nt(dy, w_ref[...]).astype(do_ref.dtype)
            dz = dmv * y * (g * (1.0 - g))
            dhg[:, sl] = dz.astype(dhg.dtype)
            dbg[:, sl] += jnp.sum(dz, axis=0, keepdims=True)

    row = lambda w: pl.BlockSpec((tt, w), lambda i: (i, 0))
    const = lambda a: pl.BlockSpec(a.shape, lambda i: (0, 0))
    return _pcall(
        body, name="merge_bwd", grid=(T // tt,),
        in_specs=[row(D), row(A_W), row(BQ_W), row(C_W), row(GATE), const(bg), const(wa), const(wb), const(wc)],
        out_specs=[row(D)] * 3 + [row(A_W), row(BQ_W), row(C_W), row(GATE),
                                  pl.BlockSpec((1, GATE), lambda i: (0, 0))],
        out_shape=[_sds((T, D), BF)] * 3 + [_sds((T, A_W), BF), _sds((T, BQ_W), BF), _sds((T, C_W), BF),
                                            _sds((T, GATE), BF), _sds((1, GATE), F32)],
        compiler_params=_cp("arbitrary"),
    )(dm, oa, ob, oc, hg, bg, wa, wb, wc)


def _lin_ln(a, w, res, g, b):
    T, K = a.shape
    tt = min(256, T)

    def body(a_ref, w_ref, res_ref, g_ref, b_ref, y_ref, yb_ref, xh_ref, rs_ref):
        u = ALPHA * res_ref[...] + jnp.dot(a_ref[...], w_ref[...], preferred_element_type=F32)
        mu = jnp.mean(u, axis=-1, keepdims=True)
        c = u - mu
        r = lax.rsqrt(jnp.mean(c * c, axis=-1, keepdims=True) + LN_EPS)
        xh = c * r
        y = xh * g_ref[...] + b_ref[...]
        y_ref[...] = y
        yb_ref[...] = y.astype(BF)
        xh_ref[...] = xh
        rs_ref[...] = r

    row = lambda w_: pl.BlockSpec((tt, w_), lambda i: (i, 0))
    const = lambda s: pl.BlockSpec(s, lambda i: (0, 0))
    return _pcall(
        body, name="lin_ln", grid=(T // tt,),
        in_specs=[row(K), const((K, D)), row(D), const((1, D)), const((1, D))],
        out_specs=[row(D), row(D), row(D), row(1)],
        out_shape=[_sds((T, D), F32), _sds((T, D), BF), _sds((T, D), F32), _sds((T, 1), F32)],
        compiler_params=_cp("parallel"),
    )(a, w, res, g, b)


def _ln_bwd(dy, xh, rs, g):
    T = dy.shape[0]
    tt = min(512, T)

    def body(dy_ref, xh_ref, rs_ref, g_ref, du_ref, dub_ref, dg_ref, db_ref):
        @pl.when(pl.program_id(0) == 0)
        def _():
            dg_ref[...] = jnp.zeros_like(dg_ref)
            db_ref[...] = jnp.zeros_like(db_ref)

        dyv, xhv = dy_ref[...], xh_ref[...]
        dg_ref[...] += jnp.sum(dyv * xhv, axis=0, keepdims=True)
        db_ref[...] += jnp.sum(dyv, axis=0, keepdims=True)
        dxh = dyv * g_ref[...]
        m1 = jnp.mean(dxh, axis=-1, keepdims=True)
        m2 = jnp.mean(dxh * xhv, axis=-1, keepdims=True)
        du = rs_ref[...] * (dxh - m1 - xhv * m2)
        du_ref[...] = du
        dub_ref[...] = du.astype(BF)

    row = lambda w_: pl.BlockSpec((tt, w_), lambda i: (i, 0))
    const = lambda s: pl.BlockSpec(s, lambda i: (0, 0))
    return _pcall(
        body, name="ln_bwd", grid=(T // tt,),
        in_specs=[row(D), row(D), row(1), const((1, D))],
        out_specs=[row(D), row(D), const((1, D)), const((1, D))],
        out_shape=[_sds((T, D), F32), _sds((T, D), BF), _sds((1, D), F32), _sds((1, D), F32)],
        compiler_params=_cp("arbitrary"),
    )(dy, xh, rs, g)


def _loss_grad(y, tgt):
    T = y.shape[0]
    tt = min(512, T)

    def body(y_ref, t_ref, dy_ref, sq_ref):
        @pl.when(pl.program_id(0) == 0)
        def _():
            sq_ref[...] = jnp.zeros_like(sq_ref)

        e = y_ref[...] - t_ref[...]
        dy_ref[...] = e * (1.0 / D)
        sq_ref[...] += jnp.sum(e * e, axis=0, keepdims=True)

    row = pl.BlockSpec((tt, D), lambda i: (i, 0))
    return _pcall(
        body, name="loss_grad", grid=(T // tt,),
        in_specs=[row, row], out_specs=[row, pl.BlockSpec((1, D), lambda i: (0, 0))],
        out_shape=[_sds((T, D), F32), _sds((1, D), F32)],
        compiler_params=_cp("arbitrary"),
    )(y, tgt)


def _position():
    return lax.axis_index("x"), lax.axis_index("y"), lax.axis_index("c")


def _all_gather(xs, name):
    n = len(xs)
    hbm = pl.BlockSpec(memory_space=pl.ANY)

    def body(*refs):
        x_refs, out_refs = refs[:n], refs[n:2 * n]
        send, recv, loc = refs[2 * n:]
        x, y, c = _position()
        me, sib = (x, y, c), (x, y, 1 - c)
        chips = [(1 - x, y), (x, 1 - y), (1 - x, 1 - y)]

        def copy(a, k, block, to, src=None):
            px, py, pc = block
            dst = out_refs[a].at[4 * px + 2 * py + pc]
            return pltpu.make_async_remote_copy(
                src_ref=dst if src is None else src, dst_ref=dst,
                send_sem=send.at[a, k], recv_sem=recv.at[a, k], device_id=to, device_id_type=MESH)

        mine = [pltpu.make_async_copy(x_refs[a], out_refs[a].at[4 * x + 2 * y + c], loc.at[a]) for a in range(n)]
        for cp in mine:
            cp.start()
        first = []
        for a in range(n):
            first.append(copy(a, 0, me, sib, src=x_refs[a]))
            first += [copy(a, 1 + j, me, (*chip, c), src=x_refs[a]) for j, chip in enumerate(chips)]
        for cp in first:
            cp.start()
        passed = []
        for j, chip in enumerate(chips):
            for a in range(n):
                copy(a, 1 + j, (*chip, c), me).wait_recv()
                fwd = copy(a, 4 + j, (*chip, c), sib)
                fwd.start()
                passed.append(fwd)
        for a in range(n):
            copy(a, 0, sib, me).wait_recv()
            for j, chip in enumerate(chips):
                copy(a, 4 + j, (*chip, 1 - c), me).wait_recv()
        for cp in first + passed:
            cp.wait_send()
        for cp in mine:
            cp.wait()

    return _pcall_comm(
        body, name=name,
        in_specs=[hbm] * n, out_specs=[hbm] * n,
        out_shape=[_sds((8,) + x.shape, x.dtype) for x in xs],
        scratch_shapes=[pltpu.SemaphoreType.DMA((n, 7)), pltpu.SemaphoreType.DMA((n, 7)),
                        pltpu.SemaphoreType.DMA((n,))],
    )(*xs)


def _exchange_pair(gs, name):
    n = len(gs)
    hbm = pl.BlockSpec(memory_space=pl.ANY)

    def body(*refs):
        g_refs, out_refs = refs[:n], refs[n:2 * n]
        send, recv = refs[2 * n:]
        x, y, c = _position()
        copies = []
        for a in range(n):
            for p in range(4):
                copies.append(pltpu.make_async_remote_copy(
                    src_ref=g_refs[a].at[p, 1 - c], dst_ref=out_refs[a].at[p],
                    send_sem=send.at[a, p], recv_sem=recv.at[a, p],
                    device_id=(x, y, 1 - c), device_id_type=MESH))
        for cp in copies:
            cp.start()
        for cp in copies:
            cp.wait()

    return _pcall_comm(
        body, name=name,
        in_specs=[hbm] * n, out_specs=[hbm] * n,
        out_shape=[_sds((4,) + g.shape[2:], g.dtype) for g in gs],
        scratch_shapes=[pltpu.SemaphoreType.DMA((n, 4)), pltpu.SemaphoreType.DMA((n, 4))],
    )(*gs)


def _exchange_chips(ps, name):
    n = len(ps)
    hbm = pl.BlockSpec(memory_space=pl.ANY)

    def body(*refs):
        p_refs, out_refs = refs[:n], refs[n:2 * n]
        send, recv = refs[2 * n:]
        x, y, c = _position()
        chips = [(1 - x, y), (x, 1 - y), (1 - x, 1 - y)]
        copies = []
        for a in range(n):
            for j, (px, py) in enumerate(chips):
                copies.append(pltpu.make_async_remote_copy(
                    src_ref=p_refs[a].at[2 * px + py], dst_ref=out_refs[a].at[j],
                    send_sem=send.at[a, j], recv_sem=recv.at[a, j],
                    device_id=(px, py, c), device_id_type=MESH))
        for cp in copies:
            cp.start()
        for cp in copies:
            cp.wait()

    return _pcall_comm(
        body, name=name,
        in_specs=[hbm] * n, out_specs=[hbm] * n,
        out_shape=[_sds((3,) + p.shape[1:], p.dtype) for p in ps],
        scratch_shapes=[pltpu.SemaphoreType.DMA((n, 3)), pltpu.SemaphoreType.DMA((n, 3))],
    )(*ps)


def _pair_sum(g, got, core):
    _, _, R, C = g.shape
    tr = min(512, R)

    def body(core_ref, g_ref, r_ref, out_ref):
        out_ref[...] = (g_ref[...].astype(F32) + r_ref[...].astype(F32)).astype(out_ref.dtype)

    return _pcall(
        body, name="pair_sum",
        grid_spec=pltpu.PrefetchScalarGridSpec(
            num_scalar_prefetch=1, grid=(4, R // tr),
            in_specs=[pl.BlockSpec((None, None, tr, C), lambda p, i, cr: (p, cr[0], i, 0)),
                      pl.BlockSpec((None, tr, C), lambda p, i, cr: (p, i, 0))],
            out_specs=pl.BlockSpec((None, tr, C), lambda p, i, cr: (p, i, 0))),
        out_shape=_sds((4, R, C), g.dtype),
        compiler_params=_cp("parallel", "parallel"),
    )(core, g, got)


def _adamw_math(w, g, m, v):
    m = B1 * m + (1.0 - B1) * g
    v = B2 * v + (1.0 - B2) * (g * g)
    m_hat = m / (1.0 - B1 ** STEP)
    v_hat = v / (1.0 - B2 ** STEP)
    delta = -LR * (m_hat / (jnp.sqrt(v_hat) + EPS) + WD * w)
    return delta, m, v


def _chip_sum_adamw(p, got, chip, w, m, v):
    R, C = w.shape
    tr = min(512, R)

    def body(chip_ref, p_ref, r_ref, w_ref, m_ref, v_ref, g_out, d_out, m_out, v_out):
        g = ((p_ref[...].astype(F32) + r_ref[0].astype(F32)) + r_ref[1].astype(F32)) + r_ref[2].astype(F32)
        d, mn, vn = _adamw_math(w_ref[...], g, m_ref[...], v_ref[...])
        g_out[...], d_out[...], m_out[...], v_out[...] = g, d, mn, vn

    blk = pl.BlockSpec((tr, C), lambda i, ch: (i, 0))
    return _pcall(
        body, name="chip_sum_adamw",
        grid_spec=pltpu.PrefetchScalarGridSpec(
            num_scalar_prefetch=1, grid=(R // tr,),
            in_specs=[pl.BlockSpec((None, tr, C), lambda i, ch: (ch[0], i, 0)),
                      pl.BlockSpec((3, tr, C), lambda i, ch: (0, i, 0)), blk, blk, blk],
            out_specs=[blk] * 4),
        out_shape=[_sds((R, C), F32)] * 4,
        compiler_params=_cp("parallel"),
    )(chip, p, got, w, m, v)


def _small_sum_adamw(parts, w, m, v):
    _, R, C = parts.shape

    def body(p_ref, w_ref, m_ref, v_ref, g_out, d_out, m_out, v_out):
        g = p_ref[0]
        for k in range(1, 8):
            g = g + p_ref[k]
        d, mn, vn = _adamw_math(w_ref[...], g, m_ref[...], v_ref[...])
        g_out[...], d_out[...], m_out[...], v_out[...] = g, d, mn, vn

    return _pcall(body, name="small_sum_adamw", out_shape=[_sds((R, C), F32)] * 4,
                  compiler_params=pltpu.CompilerParams(vmem_limit_bytes=VMEM_LIMIT))(parts, w, m, v)


BIG = ("w_in", "w_branch_a", "w_branch_b", "w_branch_c", "w_out", "w_up", "w_down")
ROW_SHARDED = ("w_out", "w_down")
SMALL = ("b_gate", "q_norm_b", "k_norm_b", "rpb_c", "ln1_g", "ln1_b", "ln2_g", "ln2_b")
NAMES = ("w_in", "b_gate", "q_norm_b", "k_norm_b", "rpb_c", "w_branch_a", "w_branch_b", "w_branch_c",
         "w_out", "ln1_g", "ln1_b", "w_up", "w_down", "ln2_g", "ln2_b")


def _full_weight(gathered, name, layer):
    blk = gathered.reshape(8, DEPTH, gathered.shape[1] // DEPTH, gathered.shape[2])[:, layer]
    if name in ROW_SHARDED:
        return blk.reshape(-1, blk.shape[2])
    return blk.transpose(1, 0, 2).reshape(blk.shape[1], -1)


def _chunks(grad, name):
    if name in ROW_SHARDED:
        return grad.reshape(8, grad.shape[0] // 8, grad.shape[1])
    return grad.reshape(grad.shape[0], 8, grad.shape[1] // 8).transpose(1, 0, 2)


def _layer_fwd(x, xb, W, P, tabs, gm):
    hq, = _mm(xb, W["w_qkv"], "nn", [F32], 1024, 768, 1024, name="in_qkv")
    hg, = _mm(xb, W["w_gate"], "nn", [F32], 1024, 1024, 1024, name="in_gate")
    tab_a, tab_b = tabs
    prepped = _prep_fwd(hq, tab_a, tab_b, P["qn"], P["kn"], gm)
    T = x.shape[0]
    tb = min(FULL_T, T)
    oa, lse_a, qa = _dilated_fwd(*prepped[0:3])
    qb_t, kb_t, vb_t = prepped[3:6]
    qb, kb, vb = _chunked_t(qb_t, tb), _to_heads(kb_t), _chunked_t(vb_t, min(FULL_TK, T))
    vb1 = jnp.concatenate([vb, jnp.ones(vb.shape[:2] + (V_ROWS - HD, vb.shape[3]), vb.dtype)], axis=2)
    ob, lse_b = _attn_full_fwd(qb, kb, vb1)
    qc, kc, vc = [_to_heads(t) for t in prepped[6:9]]
    bias_c = _nbr_bias(P["rpb"], BAND_T)
    oc, lse_c = _attn_band_fwd(qc, kc, vc, bias_c, 1, True)
    oa_t, ob_t, oc_t = oa.transpose(2, 0, 1).reshape(T, A_W), _unchunk_t(ob), _from_heads(oc)
    qb = (_to_heads(qb_t), qb, kb, kb.transpose(0, 2, 1), _to_heads(vb_t))
    ka = va = None
    merged = _merge_fwd(oa_t, ob_t, oc_t, hg, P["bg"], W["w_branch_a"], W["w_branch_b"], W["w_branch_c"])
    x1, x1b, xh1, rs1 = _lin_ln(merged, W["w_out"], x, P["ln1_g"], P["ln1_b"])

    def relu2(acc):
        r = jnp.maximum(acc, 0.0)
        return r * r, r

    f, r = _mm(x1b, W["w_up"], "nn", [BF, BF], 1024, 1024, 1024, epilogue=relu2, name="mlp_up")
    x2, x2b, xh2, rs2 = _lin_ln(f, W["w_down"], x1, P["ln2_g"], P["ln2_b"])
    saved = dict(xb=xb, hq=hq, hg=hg, qkv=(qa, ka, va, qb, kb, vb, qc, kc, vc), o=(oa, ob, oc),
                 lse=(lse_a, lse_b, lse_c), o_t=(oa_t, ob_t, oc_t), bias_c=bias_c, merged=merged,
                 xh1=xh1, rs1=rs1, x1b=x1b, f=f, r=r, xh2=xh2, rs2=rs2)
    return x2, x2b, saved


def _layer_bwd(dx2, S, W, P, tabs, gm):
    G = {}
    du2, du2b, G["ln2_g"], G["ln2_b"] = _ln_bwd(dx2, S["xh2"], S["rs2"], P["ln2_g"])
    G["w_down"], = _mm(S["f"], du2b, "tn", [F32], 1024, 1024, 512, name="dw_down")
    da, = _mm(du2b, W["w_down"], "nt", [BF], 1024, 1024, 1024,
              epilogue=lambda acc, r: (acc * (2.0 * r.astype(F32)),), extras=(S["r"],), name="d_act")
    G["w_up"], = _mm(S["x1b"], da, "tn", [F32], 1024, 1024, 512, name="dw_up")
    dx1, = _mm(da, W["w_up"], "nt", [F32], 1024, 1024, 1024,
               epilogue=lambda acc, d: (ALPHA * d + acc,), extras=(du2,), name="dx_mlp")
    du1, du1b, G["ln1_g"], G["ln1_b"] = _ln_bwd(dx1, S["xh1"], S["rs1"], P["ln1_g"])
    G["w_out"], = _mm(S["merged"], du1b, "tn", [F32], 1024, 1024, 512, name="dw_out")
    dm, = _mm(du1b, W["w_out"], "nt", [F32], 1024, 1024, 1024, name="d_merged")
    oa_t, ob_t, oc_t = S["o_t"]
    dya, dyb, dyc, doa, dob, doc, dhg, G["b_gate"] = _merge_bwd(
        dm, oa_t, ob_t, oc_t, S["hg"], P["bg"], W["w_branch_a"], W["w_branch_b"], W["w_branch_c"])
    G["w_branch_a"], = _mm(oa_t, dya, "tn", [F32], 256, 1024, 512, name="dw_branch_a")
    G["w_branch_b"], = _mm(ob_t, dyb, "tn", [F32], 512, 1024, 512, name="dw_branch_b")
    G["w_branch_c"], = _mm(oc_t, dyc, "tn", [F32], 256, 1024, 512, name="dw_branch_c")

    qa, ka, va, qb, kb, vb, qc, kc, vc = S["qkv"]
    oa, ob, oc = S["o"]
    lse_a, lse_b, lse_c = S["lse"]
    doc_h = _to_heads(doc)
    dqa, dka, dva = _dilated_bwd(doa, oa, lse_a, qa)
    q_b, qT_b, k_b, kT_b, v_b = qb
    dobT = _chunked_t(dob, ob.shape[-1])
    dqbT, dkb8, dvb8 = _attn_full_bwd(q_b, qT_b, k_b, kT_b, v_b, _to_heads(dob), dobT, lse_b, _attn_delta(dobT, ob))
    group_sum = lambda t: t.reshape(k_b.shape[0], -1, t.shape[1], HD).sum(1)
    dqb, dkb, dvb = _unchunk_t(dqbT), _from_heads(group_sum(dkb8)), _from_heads(group_sum(dvb8))
    bias_c = S["bias_c"]
    dqc, dl_c, dbias_c = _attn_band_dq(qc, kc, vc, oc, doc_h, lse_c, bias_c, 1, True)
    dkc, dvc = _attn_band_dkv(qc, kc, vc, doc_h, lse_c, dl_c, bias_c.transpose(0, 1, 3, 2), 1, True)
    G["rpb_c"] = _rpb_grad(dbias_c)

    tab_a, tab_b = tabs
    grads = [dqa, dka, dva, dqb, dkb, dvb] + [_from_heads(t) for t in (dqc, dkc, dvc)]
    dhq, dqn, dkn = _prep_bwd(S["hq"], grads, tab_a, tab_b, P["qn"], P["kn"], gm)
    G["q_norm_b"] = dqn.reshape(BQ_W // HD, HD).sum(0)
    G["k_norm_b"] = dkn.reshape(BKV_W // HD, HD).sum(0)
    dw_qkv, = _mm(S["xb"], dhq, "tn", [F32], 1024, 768, 512, name="dw_qkv")
    dw_gate, = _mm(S["xb"], dhg, "tn", [F32], 1024, 1024, 512, name="dw_gate")
    G["w_in"] = jnp.concatenate([dw_qkv, dw_gate], axis=1)
    dx_a, = _mm(dhq, W["w_qkv"], "nt", [F32], 1024, 1024, 768,
                epilogue=lambda acc, d: (ALPHA * d + acc,), extras=(du1,), name="dx_qkv")
    dx, = _mm(dhg, W["w_gate"], "nt", [F32], 1024, 1024, 1024,
              epilogue=lambda acc, d: (d + acc,), extras=(dx_a,), name="dx_gate")
    return dx, G


def _pack_small(vals):
    flat = jnp.concatenate([vals[n].reshape(-1).astype(F32) for n in SMALL])
    pad = (-flat.shape[0]) % (8 * 128)
    return jnp.pad(flat, (0, pad)).reshape(-1, 128)


def _unpack_small(packed, like):
    flat, out, off = packed.reshape(-1), {}, 0
    for n in SMALL:
        size = math.prod(like[n].shape)
        out[n] = flat[off:off + size].reshape(like[n].shape)
        off += size
    return out


def kernel(x, w_in, b_gate, q_norm_b, k_norm_b, rpb_c, w_branch_a, w_branch_b, w_branch_c, w_out, ln1_g, ln1_b, w_up, w_down, ln2_g, ln2_b, loss_target, m_w_in, m_b_gate, m_q_norm_b, m_k_norm_b, m_rpb_c, m_w_branch_a, m_w_branch_b, m_w_branch_c, m_w_out, m_ln1_g, m_ln1_b, m_w_up, m_w_down, m_ln2_g, m_ln2_b, v_w_in, v_b_gate, v_q_norm_b, v_k_norm_b, v_rpb_c, v_w_branch_a, v_w_branch_b, v_w_branch_c, v_w_out, v_ln1_g, v_ln1_b, v_w_up, v_w_down, v_ln2_g, v_ln2_b):
    w = dict(w_in=w_in, b_gate=b_gate, q_norm_b=q_norm_b, k_norm_b=k_norm_b, rpb_c=rpb_c,
             w_branch_a=w_branch_a, w_branch_b=w_branch_b, w_branch_c=w_branch_c, w_out=w_out,
             ln1_g=ln1_g, ln1_b=ln1_b, w_up=w_up, w_down=w_down, ln2_g=ln2_g, ln2_b=ln2_b)
    m = dict(w_in=m_w_in, b_gate=m_b_gate, q_norm_b=m_q_norm_b, k_norm_b=m_k_norm_b, rpb_c=m_rpb_c,
             w_branch_a=m_w_branch_a, w_branch_b=m_w_branch_b, w_branch_c=m_w_branch_c, w_out=m_w_out,
             ln1_g=m_ln1_g, ln1_b=m_ln1_b, w_up=m_w_up, w_down=m_w_down, ln2_g=m_ln2_g, ln2_b=m_ln2_b)
    v = dict(w_in=v_w_in, b_gate=v_b_gate, q_norm_b=v_q_norm_b, k_norm_b=v_k_norm_b, rpb_c=v_rpb_c,
             w_branch_a=v_w_branch_a, w_branch_b=v_w_branch_b, w_branch_c=v_w_branch_c, w_out=v_w_out,
             ln1_g=v_ln1_g, ln1_b=v_ln1_b, w_up=v_w_up, w_down=v_w_down, ln2_g=v_ln2_g, ln2_b=v_ln2_b)
    T = x.shape[1]
    xc, yc, cc = _position()

    flat2 = lambda a: a.reshape(-1, a.shape[-1])
    gathered = _all_gather([flat2(w[n]).astype(BF) for n in BIG], "gather_weights")
    gathered = dict(zip(BIG, gathered))

    tabs = _rope_tables(T)
    gm = _group_mean_matrix()

    Ws, Ps = [], []
    for l in range(DEPTH):
        W = {n: _full_weight(gathered[n], n, l) for n in BIG}
        W["w_qkv"], W["w_gate"] = W["w_in"][:, :QKV], W["w_in"][:, QKV:]
        Ws.append(W)
        Ps.append(dict(qn=jnp.tile(q_norm_b[l][None], (1, 2)), kn=jnp.tile(k_norm_b[l][None], (1, 2)),
                       rpb=rpb_c[l], bg=b_gate[l][None], ln1_g=ln1_g[l][None], ln1_b=ln1_b[l][None],
                       ln2_g=ln2_g[l][None], ln2_b=ln2_b[l][None]))

    h = x[0]
    hb = h.astype(BF)
    saved = []
    for l in range(DEPTH):
        h, hb, S = _layer_fwd(h, hb, Ws[l], Ps[l], tabs, gm)
        saved.append(S)
    dy, sq = _loss_grad(h, loss_target[0])
    loss = lax.psum(0.5 / D * jnp.sum(sq), AXES)
    grads = [None] * DEPTH
    for l in reversed(range(DEPTH)):
        dy, grads[l] = _layer_bwd(dy, saved[l], Ws[l], Ps[l], tabs, gm)
    grad_x = dy[None]

    core = cc.reshape(1).astype(jnp.int32)
    chip = (2 * xc + yc).reshape(1).astype(jnp.int32)
    chunked = []
    for n in BIG:
        g = jnp.stack([_chunks(grads[l][n], n) for l in range(DEPTH)], axis=1)
        chunked.append(g.reshape(4, 2, DEPTH * g.shape[2], g.shape[3]).astype(BF))
    from_sibling = _exchange_pair(chunked, "grads_to_sibling")
    pair = [_pair_sum(g, r, core) for g, r in zip(chunked, from_sibling)]
    from_chips = _exchange_chips(pair, "grads_to_chips")
    out_g, out_d, out_m, out_v = {}, {}, {}, {}
    for n, p, r in zip(BIG, pair, from_chips):
        res = _chip_sum_adamw(p, r, chip, flat2(w[n]), flat2(m[n]), flat2(v[n]))
        out_g[n], out_d[n], out_m[n], out_v[n] = [t.reshape(w[n].shape) for t in res]

    part = _pack_small({n: jnp.stack([grads[l][n].reshape(w[n].shape[1:]) for l in range(DEPTH)]) for n in SMALL})
    parts, = _all_gather([part], "gather_small_grads")
    res = _small_sum_adamw(parts, _pack_small(w), _pack_small(m), _pack_small(v))
    for dst, packed in zip((out_g, out_d, out_m, out_v), res):
        dst.update(_unpack_small(packed, w))

    return (loss, grad_x, *[out_g[n] for n in NAMES], *[out_d[n] for n in NAMES],
            *[out_m[n] for n in NAMES], *[out_v[n] for n in NAMES])
```

```python
import functools
import math

import numpy as np
import jax
import jax.numpy as jnp
from jax import lax
from jax.experimental import pallas as pl
from jax.experimental.pallas import tpu as pltpu

F32 = jnp.float32
BF = jnp.bfloat16
HI = lax.Precision.HIGHEST
NEG = -1e30
MESH = pl.DeviceIdType.MESH
AXES = ("x", "y", "c")

D = 1024
DEPTH = 4
HD = 64
A_W, BQ_W, BKV_W, C_W = 256, 512, 128, 256
QKV = 2304
GATE = 3072
D_FF = 4096
GRID_W = 64
ALPHA = (2 * DEPTH) ** 0.25
LN_EPS = 1e-5
RMS_EPS = 1e-6
SCALE = HD ** -0.5
ROPE_THETA = 500000.0
AXIAL_THETA = 10000.0
A_CONFIGS = ((128, 1), (512, 4), (2048, 16))
LR, B1, B2, EPS, WD, STEP = 0.001, 0.9, 0.999, 1e-08, 0.01, 10

VMEM_LIMIT = 56 * 1024 * 1024
BAND_T = 256


def _pcall(body, **kw):
    return pl.pallas_call(body, **kw)


def _pcall_comm(body, **kw):
    return pl.pallas_call(body, **kw)


def _cp(*sem):
    return pltpu.CompilerParams(dimension_semantics=sem, vmem_limit_bytes=VMEM_LIMIT)


def _sds(shape, dtype):
    return jax.ShapeDtypeStruct(shape, dtype)


def _mm(a, b, dims, outs, tm, tn, tk, epilogue=None, extras=(), name="mm"):
    if dims == "tn":
        K, M = a.shape
    else:
        M, K = a.shape
    N = b.shape[0] if dims == "nt" else b.shape[1]
    tm, tn, tk = min(tm, M), min(tn, N), min(tk, K)
    assert M % tm == 0 and N % tn == 0 and K % tk == 0, (name, M, N, K, tm, tn, tk)
    nk = K // tk
    ne, no = len(extras), len(outs)

    def body(a_ref, b_ref, *rest):
        extra_refs, out_refs = rest[:ne], rest[ne:ne + no]
        av, bv = a_ref[...].astype(BF), b_ref[...].astype(BF)
        if dims == "nn":
            p = jnp.dot(av, bv, preferred_element_type=F32)
        elif dims == "nt":
            p = lax.dot_general(av, bv, (((1,), (1,)), ((), ())), preferred_element_type=F32)
        else:
            p = lax.dot_general(av, bv, (((0,), (0,)), ((), ())), preferred_element_type=F32)

        def finish(acc):
            res = epilogue(acc, *[r[...] for r in extra_refs]) if epilogue else (acc,)
            for o, r in zip(out_refs, res):
                o[...] = r.astype(o.dtype)

        if nk == 1:
            finish(p)
        else:
            acc_ref = rest[-1]
            k = pl.program_id(2)

            @pl.when(k == 0)
            def _():
                acc_ref[...] = p

            @pl.when(k > 0)
            def _():
                acc_ref[...] += p

            @pl.when(k == nk - 1)
            def _():
                finish(acc_ref[...])

    if dims == "tn":
        a_spec = pl.BlockSpec((tk, tm), lambda i, j, k: (k, i))
    else:
        a_spec = pl.BlockSpec((tm, tk), lambda i, j, k: (i, k))
    if dims == "nt":
        b_spec = pl.BlockSpec((tn, tk), lambda i, j, k: (j, k))
    else:
        b_spec = pl.BlockSpec((tk, tn), lambda i, j, k: (k, j))
    o_spec = pl.BlockSpec((tm, tn), lambda i, j, k: (i, j))
    res = _pcall(
        body, name=name, grid=(M // tm, N // tn, nk),
        in_specs=[a_spec, b_spec] + [o_spec] * ne,
        out_specs=[o_spec] * no,
        out_shape=[_sds((M, N), dt) for dt in outs],
        scratch_shapes=[pltpu.VMEM((tm, tn), F32)] if nk > 1 else [],
        compiler_params=_cp("parallel", "parallel", "arbitrary"),
    )(a, b, *extras)
    return res


def _rope_tables(T):
    pos = jnp.arange(T)

    def cs(p, theta, half):
        inv = theta ** (-jnp.arange(half, dtype=F32) / half)
        ang = p.astype(F32)[:, None] * inv[None, :]
        return jnp.cos(ang), jnp.sin(ang)

    ca, sa = cs(pos, ROPE_THETA, 8)
    one, zero, z8 = jnp.ones((T, 48), F32), jnp.zeros((T, 48), F32), jnp.zeros((T, 8), F32)
    tab_a = [jnp.concatenate(t, 1) for t in ([ca, ca, one], [-sa, z8, zero], [z8, sa, zero])]
    cr, sr = cs(pos // GRID_W, AXIAL_THETA, 16)
    cc, sc = cs(pos % GRID_W, AXIAL_THETA, 16)
    z16 = jnp.zeros((T, 16), F32)
    tab_b = [jnp.concatenate(t, 1) for t in ([cr, cr, cc, cc], [-sr, z16, -sc, z16], [z16, sr, z16, sc])]
    return [jnp.tile(t, (1, 2)) for t in tab_a], [jnp.tile(t, (1, 2)) for t in tab_b]


def _rot(x, C, S1, S2, k):
    return x * C + pltpu.roll(x, 128 - k, 1) * S1 + pltpu.roll(x, k, 1) * S2


def _rot_t(d, C, S1, S2, k):
    return d * C + pltpu.roll(d * S1, k, 1) + pltpu.roll(d * S2, 128 - k, 1)


def _group_mean_matrix():
    m = np.zeros((128, 128), np.float32)
    m[:64, :64] = 1.0 / 64
    m[64:, 64:] = 1.0 / 64
    return jnp.asarray(m)


def _prep_fwd(hq, tab_a, tab_b, qn, kn, gm):
    T = hq.shape[0]
    tt = min(256, T)
    widths = [A_W, A_W, A_W, BQ_W, BKV_W, BKV_W, C_W, C_W, C_W]

    def body(h_ref, ca, s1a, s2a, cb, s1b, s2b, qn_ref, kn_ref, gm_ref,
             qa, ka, va, qb, kb, vb, qc, kc, vc):
        def col(off, j):
            return h_ref[:, off + 128 * j: off + 128 * (j + 1)]

        for j in range(2):
            sl = slice(128 * j, 128 * (j + 1))
            qa[:, sl] = (_rot(col(0, j), ca[...], s1a[...], s2a[...], 8) * SCALE).astype(qa.dtype)
            ka[:, sl] = _rot(col(256, j), ca[...], s1a[...], s2a[...], 8).astype(ka.dtype)
            va[:, sl] = col(512, j).astype(va.dtype)
            qc[:, sl] = (col(1536, j) * SCALE).astype(qc.dtype)
            kc[:, sl] = col(1792, j).astype(kc.dtype)
            vc[:, sl] = col(2048, j).astype(vc.dtype)

        def normed(x, w):
            ms = jnp.dot(x * x, gm_ref[...], precision=HI, preferred_element_type=F32)
            return x * lax.rsqrt(ms + RMS_EPS) * w

        for j in range(4):
            y = normed(col(768, j), qn_ref[...])
            qb[:, 128 * j:128 * (j + 1)] = (_rot(y, cb[...], s1b[...], s2b[...], 16) * SCALE).astype(qb.dtype)
        y = normed(col(1280, 0), kn_ref[...])
        kb[...] = _rot(y, cb[...], s1b[...], s2b[...], 16).astype(kb.dtype)
        vb[...] = col(1408, 0).astype(vb.dtype)

    row = lambda w: pl.BlockSpec((tt, w), lambda i: (i, 0))
    const = lambda s: pl.BlockSpec(s, lambda i: (0, 0))
    return _pcall(
        body, name="prep_fwd", grid=(T // tt,),
        in_specs=[row(QKV)] + [row(128)] * 6 + [const((1, 128))] * 2 + [const((128, 128))],
        out_specs=[row(w) for w in widths],
        out_shape=[_sds((T, w), BF) for w in widths],
        compiler_params=_cp("parallel"),
    )(hq, *tab_a, *tab_b, qn, kn, gm)


def _prep_bwd(hq, grads, tab_a, tab_b, qn, kn, gm):
    T = hq.shape[0]
    tt = min(256, T)
    widths = [A_W, A_W, A_W, BQ_W, BKV_W, BKV_W, C_W, C_W, C_W]

    def body(h_ref, dqa, dka, dva, dqb, dkb, dvb, dqc, dkc, dvc,
             ca, s1a, s2a, cb, s1b, s2b, qn_ref, kn_ref, gm_ref, dh, dqn, dkn):
        i = pl.program_id(0)

        @pl.when(i == 0)
        def _():
            dqn[...] = jnp.zeros_like(dqn)
            dkn[...] = jnp.zeros_like(dkn)

        def put(off, j, val):
            dh[:, off + 128 * j: off + 128 * (j + 1)] = val.astype(dh.dtype)

        for j in range(2):
            sl = slice(128 * j, 128 * (j + 1))
            put(0, j, _rot_t(dqa[:, sl] * SCALE, ca[...], s1a[...], s2a[...], 8))
            put(256, j, _rot_t(dka[:, sl], ca[...], s1a[...], s2a[...], 8))
            put(512, j, dva[:, sl])
            put(1536, j, dqc[:, sl] * SCALE)
            put(1792, j, dkc[:, sl])
            put(2048, j, dvc[:, sl])

        def norm_bwd(x, w, e):
            ms = jnp.dot(x * x, gm_ref[...], precision=HI, preferred_element_type=F32)
            r = lax.rsqrt(ms + RMS_EPS)
            n = x * r
            dn = e * w
            proj = jnp.dot(dn * n, gm_ref[...], precision=HI, preferred_element_type=F32)
            return r * (dn - n * proj), jnp.sum(e * n, axis=0, keepdims=True)

        for j in range(4):
            sl = slice(128 * j, 128 * (j + 1))
            e = _rot_t(dqb[:, sl] * SCALE, cb[...], s1b[...], s2b[...], 16)
            dx, dw = norm_bwd(h_ref[:, 768 + 128 * j: 768 + 128 * (j + 1)], qn_ref[...], e)
            put(768, j, dx)
            dqn[:, sl] += dw
        e = _rot_t(dkb[...], cb[...], s1b[...], s2b[...], 16)
        dx, dw = norm_bwd(h_ref[:, 1280:1408], kn_ref[...], e)
        put(1280, 0, dx)
        dkn[...] += dw
        put(1408, 0, dvb[...])

    row = lambda w: pl.BlockSpec((tt, w), lambda i: (i, 0))
    const = lambda s: pl.BlockSpec(s, lambda i: (0, 0))
    return _pcall(
        body, name="prep_bwd", grid=(T // tt,),
        in_specs=[row(QKV)] + [row(w) for w in widths] + [row(128)] * 6
        + [const((1, 128))] * 2 + [const((128, 128))],
        out_specs=[row(QKV), const((1, BQ_W)), const((1, BKV_W))],
        out_shape=[_sds((T, QKV), BF), _sds((1, BQ_W), F32), _sds((1, BKV_W), F32)],
        compiler_params=_cp("arbitrary"),
    )(hq, *grads, *tab_a, *tab_b, qn, kn, gm)


def _to_heads(x):
    T, W = x.shape
    return x.reshape(T, W // HD, HD).transpose(1, 0, 2)


def _from_heads(x):
    H, T, _ = x.shape
    return x.transpose(1, 0, 2).reshape(T, H * HD)


def _rows(x, t):
    H, T, _ = x.shape
    return x.reshape(H, T // t, 1, t)


def _nt(a, b):
    return lax.dot_general(a, b, (((1,), (1,)), ((), ())), preferred_element_type=F32)


def _chunked_t(x, t):
    T, W = x.shape
    return x.reshape(T // t, t, W // HD, HD).transpose(2, 0, 3, 1)


def _unchunk_t(x):
    H, n, _, t = x.shape
    return x.transpose(1, 3, 0, 2).reshape(n * t, H * HD)


FULL_T = 512
FULL_TK = 512
FULL_HEADS = 1
V_ROWS = 72


def _attn_full_fwd(qT, k, vT1):
    Hq, nq, _, t = qT.shape
    Hk, T, _ = k.shape
    G = Hq // Hk
    nk, tk = vT1.shape[1], vT1.shape[3]

    HB = FULL_HEADS
    assert G % HB == 0

    def body(q_ref, k_ref, v_ref, o_ref, lse_ref, *acc_refs):
        for acc_ref in acc_refs:
            acc_ref[...] = jnp.zeros((V_ROWS, t), F32)

        def scores(j, b):
            sT = jnp.dot(k_ref[pl.ds(pl.multiple_of(j * tk, tk), tk), :], q_ref[b], preferred_element_type=F32)
            return sT, jnp.max(sT, axis=0, keepdims=True)

        def update(j, b, scored, m_old):
            sT, m_tile = scored
            m_new = jnp.maximum(m_old, m_tile)
            pT = jnp.exp(sT - m_new).astype(BF)
            acc_refs[b][...] = (jnp.exp(m_old - m_new) * acc_refs[b][...]
                                + jnp.dot(v_ref[j], pT, preferred_element_type=F32))
            return m_new

        def step(j, carry):
            ms, ss = carry
            nxt = jnp.minimum(j + 1, nk - 1)
            new_s = tuple(scores(nxt, b) for b in range(HB))
            new_m = tuple(update(j, b, ss[b], ms[b]) for b in range(HB))
            return new_m, new_s

        init = (tuple(jnp.full((1, t), NEG, F32) for _ in range(HB)), tuple(scores(0, b) for b in range(HB)))
        ms, _ = lax.fori_loop(0, nk, step, init)
        for b in range(HB):
            l = acc_refs[b][pl.ds(HD, 1), :]
            o_ref[b] = (acc_refs[b][pl.ds(0, HD), :] / l).astype(o_ref.dtype)
            lse_ref[b] = ms[b] + jnp.log(l)

    qs = pl.BlockSpec((HB, None, HD, t), lambda h, i: (h, i, 0, 0))
    return _pcall(
        body, name="attn_full_fwd", grid=(Hq // HB, nq),
        in_specs=[qs, pl.BlockSpec((None, T, HD), lambda h, i: (h * HB // G, 0, 0)),
                  pl.BlockSpec((None, nk, V_ROWS, tk), lambda h, i: (h * HB // G, 0, 0, 0))],
        out_specs=[qs, pl.BlockSpec((HB, None, 1, t), lambda h, i: (h, i, 0, 0))],
        out_shape=[_sds((Hq, nq, HD, t), BF), _sds((Hq, nq, 1, t), F32)],
        scratch_shapes=[pltpu.VMEM((V_ROWS, t), F32)] * HB,
        compiler_params=_cp("parallel", "parallel"),
    )(qT, k, vT1)


def _attn_delta(doT, oT):
    Hq, nq, _, t = doT.shape

    def body(do_ref, o_ref, dl_ref):
        dl_ref[...] = jnp.sum(do_ref[...].astype(F32) * o_ref[...].astype(F32), axis=0, keepdims=True)

    qs = pl.BlockSpec((None, None, HD, t), lambda h, i: (h, i, 0, 0))
    rs = pl.BlockSpec((None, None, 1, t), lambda h, i: (h, i, 0, 0))
    return _pcall(body, name="attn_delta", grid=(Hq, nq), in_specs=[qs, qs], out_specs=rs,
                  out_shape=_sds((Hq, nq, 1, t), F32), compiler_params=_cp("parallel", "parallel"))(doT, oT)


def _attn_full_bwd(q, qT, k, kT, v, do, doT, lse, delta):
    Hq, nq, _, t = qT.shape
    Hk, T, _ = k.shape
    G = Hq // Hk

    UNROLL = 1

    def body(q_ref, qT_ref, do_ref, doT_ref, lse_ref, dl_ref, k_ref, kT_ref, v_ref,
             dq_ref, dk_ref, dv_ref, dk_acc, dv_acc):
        @pl.when(pl.program_id(1) == 0)
        def _():
            dq_ref[...] = jnp.zeros_like(dq_ref)

        kv, kTv, vv = k_ref[...], kT_ref[...], v_ref[...]
        dk_acc[...] = jnp.zeros((t, HD), F32)
        dv_acc[...] = jnp.zeros((t, HD), F32)

        def step(ii, carry):
            dks, dvs = [], []
            for u in range(UNROLL):
                i = ii * UNROLL + u
                rows = pl.ds(pl.multiple_of(i * t, t), t)
                pT = jnp.exp(jnp.dot(kv, qT_ref[i], preferred_element_type=F32) - lse_ref[i])
                dvs.append(jnp.dot(pT.astype(BF), do_ref[rows, :], preferred_element_type=F32))
                dpT = jnp.dot(vv, doT_ref[i], preferred_element_type=F32)
                dsT = (pT * (dpT - dl_ref[i])).astype(BF)
                dks.append(jnp.dot(dsT, q_ref[rows, :], preferred_element_type=F32))
                dq_ref[i] += jnp.dot(kTv, dsT, preferred_element_type=F32)
            dk_acc[...] += sum(dks[1:], dks[0])
            dv_acc[...] += sum(dvs[1:], dvs[0])
            return carry

        lax.fori_loop(0, nq // UNROLL, step, 0)
        dk_ref[...] = dk_acc[...]
        dv_ref[...] = dv_acc[...]

    tok = pl.BlockSpec((None, T, HD), lambda h, j: (h, 0, 0))
    chk = pl.BlockSpec((None, nq, HD, t), lambda h, j: (h, 0, 0, 0))
    row = pl.BlockSpec((None, nq, 1, t), lambda h, j: (h, 0, 0, 0))
    kvs = pl.BlockSpec((None, t, HD), lambda h, j: (h // G, j, 0))
    out = pl.BlockSpec((None, t, HD), lambda h, j: (h, j, 0))
    return _pcall(
        body, name="attn_full_bwd", grid=(Hq, T // t),
        in_specs=[tok, chk, tok, chk, row, row, kvs, pl.BlockSpec((None, HD, t), lambda h, j: (h // G, 0, j)), kvs],
        out_specs=[chk, out, out],
        out_shape=[_sds((Hq, nq, HD, t), F32), _sds((Hq, T, HD), F32), _sds((Hq, T, HD), F32)],
        scratch_shapes=[pltpu.VMEM((t, HD), F32)] * 2,
        compiler_params=_cp("parallel", "arbitrary"),
    )(q, qT, do, doT, lse, delta, k, kT, v)


DIL_Q = 128
DIL_K = 256
DIL_R = 64


def _vh(x, d):
    T = x.shape[0]
    return x.reshape(T // d, d, 4, HD).transpose(1, 2, 0, 3).reshape(4 * d, T // d, HD)


def _vh_inv(y, d):
    L = y.shape[1]
    return y.reshape(d, 4, L, HD).transpose(2, 0, 1, 3).reshape(L * d, 4 * HD)


def _vh_chunks(x, d):
    y = _vh(x, d)
    return y.reshape(y.shape[0], y.shape[1] // DIL_Q, DIL_Q, HD).transpose(0, 1, 3, 2)


def _chunks_to_dims(c, d):
    _, nq, R, _ = c.shape
    return c.reshape(d, 4, nq, R, DIL_Q).transpose(1, 3, 2, 4, 0).reshape(4, R, nq * DIL_Q * d)


def _dims_to_chunks(x, d):
    _, R, T = x.shape
    nq = T // (DIL_Q * d)
    return x.reshape(4, R, nq, DIL_Q, d).transpose(4, 0, 2, 1, 3).reshape(4 * d, nq, R, DIL_Q)


def _chunks_to_tokens(c, d):
    _, nq, _, _ = c.shape
    return c.reshape(d, 4, nq, HD, DIL_Q).transpose(2, 4, 0, 1, 3).reshape(nq * DIL_Q * d, 4 * HD)


def _dil_window(i, L):
    start = pl.multiple_of(jnp.clip(i * DIL_Q - DIL_R, 0, L - DIL_K), DIL_R)
    kk = start + lax.broadcasted_iota(jnp.int32, (DIL_K, 1), 0)
    qq = i * DIL_Q + lax.broadcasted_iota(jnp.int32, (1, DIL_Q), 1)
    return start, jnp.abs(kk - qq) <= DIL_R


def _tn(a, b):
    return lax.dot_general(a, b, (((0,), (0,)), ((), ())), preferred_element_type=F32)


def _dil_fwd(qT, k, v1):
    V, nq, _, _ = qT.shape
    L = k.shape[1]
    assert L >= DIL_K

    unroll = 4 if nq % 4 == 0 else 1

    def body(q_ref, k_ref, v_ref, o_ref, lse_ref):
        def tile(i):
            start, mask = _dil_window(i, L)
            win = pl.ds(start, DIL_K)
            sT = jnp.where(mask, jnp.dot(k_ref[win, :], q_ref[i], preferred_element_type=F32), NEG)
            m = jnp.max(sT, axis=0, keepdims=True)
            pT = jnp.exp(sT - m).astype(BF)
            acc = _tn(v_ref[win, :], pT)
            l = jnp.max(acc[HD:HD + 8], axis=0, keepdims=True)
            o_ref[i] = acc[:HD] / l
            lse_ref[i] = m + jnp.log(l)

        def tiles(ii, carry):
            for u in range(unroll):
                tile(ii * unroll + u)
            return carry

        lax.fori_loop(0, nq // unroll, tiles, 0)

    chk = lambda r, dt: (pl.BlockSpec((None, nq, r, DIL_Q), lambda h: (h, 0, 0, 0)), _sds((V, nq, r, DIL_Q), dt))
    tok = lambda w: pl.BlockSpec((None, L, w), lambda h: (h, 0, 0))
    (o_spec, o_shape), (l_spec, l_shape) = chk(HD, F32), chk(1, F32)
    return _pcall(
        body, name=f"dil_fwd_{V // 4}", grid=(V,),
        in_specs=[chk(HD, BF)[0], tok(HD), tok(128)],
        out_specs=[o_spec, l_spec], out_shape=[o_shape, l_shape],
        compiler_params=_cp("parallel"),
    )(qT, k, v1)


def _dil_merge(os_, lses):
    _, _, T = os_[0].shape
    tt = min(1024, T)
    n = len(os_)

    def body(*refs):
        o_refs, l_refs, (o_out, l_out) = refs[:n], refs[n:2 * n], refs[2 * n:]
        m = l_refs[0][...]
        for r in l_refs[1:]:
            m = jnp.maximum(m, r[...])
        ws = [jnp.exp(r[...] - m) for r in l_refs]
        tot = ws[0]
        for w_ in ws[1:]:
            tot = tot + w_
        acc = ws[0] * o_refs[0][...]
        for w_, o in zip(ws[1:], o_refs[1:]):
            acc = acc + w_ * o[...]
        o_out[...] = (acc / tot).astype(o_out.dtype)
        l_out[...] = m + jnp.log(tot)

    os_spec = pl.BlockSpec((None, HD, tt), lambda h, i: (h, 0, i))
    ls_spec = pl.BlockSpec((None, 1, tt), lambda h, i: (h, 0, i))
    return _pcall(
        body, name="dil_merge", grid=(4, T // tt),
        in_specs=[os_spec] * n + [ls_spec] * n, out_specs=[os_spec, ls_spec],
        out_shape=[_sds((4, HD, T), BF), _sds((4, 1, T), F32)],
        compiler_params=_cp("parallel", "parallel"),
    )(*os_, *lses)


def _dims_delta(doT, oT):
    _, _, T = doT.shape
    tt = min(2048, T)

    def body(do_ref, o_ref, dl_ref):
        dl_ref[...] = jnp.sum(do_ref[...].astype(F32) * o_ref[...].astype(F32), axis=0, keepdims=True)

    spec = pl.BlockSpec((None, HD, tt), lambda h, i: (h, 0, i))
    return _pcall(body, name="dims_delta", grid=(4, T // tt), in_specs=[spec, spec],
                  out_specs=pl.BlockSpec((None, 1, tt), lambda h, i: (h, 0, i)),
                  out_shape=_sds((4, 1, T), F32), compiler_params=_cp("parallel", "parallel"))(doT, oT)


def _dil_bwd(q, qT, k, v, do, doT, lse, delta):
    V, nq, _, _ = qT.shape
    L = k.shape[1]
    unroll = 2 if nq % 2 == 0 else 1

    def body(q_ref, qT_ref, k_ref, v_ref, do_ref, doT_ref, lse_ref, dl_ref, dq_ref, dk_ref, dv_ref):
        dk_ref[...] = jnp.zeros_like(dk_ref)
        dv_ref[...] = jnp.zeros_like(dv_ref)

        def tile(i):
            start, mask = _dil_window(i, L)
            win = pl.ds(start, DIL_K)
            rows = pl.ds(pl.multiple_of(i * DIL_Q, DIL_Q), DIL_Q)
            kw = k_ref[win, :]
            sT = jnp.where(mask, jnp.dot(kw, qT_ref[i], preferred_element_type=F32), NEG)
            pT = jnp.exp(sT - lse_ref[i])
            dv = jnp.dot(pT.astype(BF), do_ref[rows, :], preferred_element_type=F32)
            dpT = jnp.dot(v_ref[win, :], doT_ref[i], preferred_element_type=F32)
            dsT = (pT * (dpT - dl_ref[i])).astype(BF)
            dk = jnp.dot(dsT, q_ref[rows, :], preferred_element_type=F32)
            dq_ref[i] = _tn(kw, dsT)
            return win, dk, dv

        def tiles(ii, carry):
            done = [tile(ii * unroll + u) for u in range(unroll)]
            for win, dk, dv in done:
                dk_ref[win, :] += dk
                dv_ref[win, :] += dv
            return carry

        lax.fori_loop(0, nq // unroll, tiles, 0)

    chk = lambda r: pl.BlockSpec((None, nq, r, DIL_Q), lambda h: (h, 0, 0, 0))
    tok = pl.BlockSpec((None, L, HD), lambda h: (h, 0, 0))
    return _pcall(
        body, name=f"dil_bwd_{V // 4}", grid=(V,),
        in_specs=[tok, chk(HD), tok, tok, tok, chk(HD), chk(1), chk(1)],
        out_specs=[chk(HD), tok, tok],
        out_shape=[_sds((V, nq, HD, DIL_Q), F32), _sds((V, L, HD), F32), _sds((V, L, HD), F32)],
        compiler_params=_cp("parallel"),
    )(q, qT, k, v, do, doT, lse, delta)


def _dilated_fwd(qa, ka, va):
    outs, lses, saved = [], [], []
    for _, d in A_CONFIGS:
        q, qT, k, v = _vh(qa, d), _vh_chunks(qa, d), _vh(ka, d), _vh(va, d)
        oT, lse = _dil_fwd(qT, k, jnp.concatenate([v, jnp.ones_like(v)], axis=2))
        outs.append(_chunks_to_dims(oT, d))
        lses.append(_chunks_to_dims(lse, d))
        saved.append((q, qT, k, v))
    o, lse = _dil_merge(outs, lses)
    return o, lse, saved


def _dilated_bwd(do_t, o, lse, saved):
    doT = do_t.reshape(do_t.shape[0], 4, HD).transpose(1, 2, 0)
    delta = _dims_delta(doT, o)
    dq = dk = dv = None
    for (_, d), (q, qT, k, v) in zip(A_CONFIGS, saved):
        dqT, dk_c, dv_c = _dil_bwd(q, qT, k, v, _vh(do_t, d), _dims_to_chunks(doT, d),
                                   _dims_to_chunks(lse, d), _dims_to_chunks(delta, d))
        parts = (_chunks_to_tokens(dqT, d), _vh_inv(dk_c, d), _vh_inv(dv_c, d))
        dq, dk, dv = parts if dq is None else (dq + parts[0], dk + parts[1], dv + parts[2])
    return dq, dk, dv


NBR_Q = 128
NBR_ROWS = 10
NBR_K = NBR_ROWS * GRID_W


def _nbr_class_tiles(nq):
    return [2, 0, 1, nq - 2, nq - 1]


def _nbr_geometry(T):
    rows, nq = T // GRID_W, T // NBR_Q
    assert rows >= NBR_ROWS + 4 and nq >= 5
    kr, kc = np.divmod(np.arange(NBR_K), GRID_W)
    qr, qc = np.divmod(np.arange(NBR_Q), GRID_W)
    c0 = np.clip(qc - 8, 0, GRID_W - 16)
    col_ok = (kc[:, None] >= c0[None, :]) & (kc[:, None] < c0[None, :] + 16)
    drs, valids = [], []
    for i in _nbr_class_tiles(nq):
        start_row = int(np.clip(2 * i - 4, 0, rows - NBR_ROWS))
        r = 2 * i + qr
        r0 = np.clip(r - 4, 0, rows - 8)
        rk = start_row + kr
        row_ok = (rk[:, None] >= r0[None, :]) & (rk[:, None] < r0[None, :] + 8)
        valids.append(row_ok & col_ok)
        dr = start_row + np.arange(NBR_ROWS)[:, None] - (2 * i + np.arange(2)[None, :]) + 7
        drs.append(np.where((dr >= 0) & (dr <= 14), dr, -1))
    return np.stack(drs), np.stack(valids)


def _nbr_fold_matrices(T):
    dr, _ = _nbr_geometry(T)
    e1 = np.zeros((GRID_W * GRID_W, 128), np.float32)
    kc, qc = np.meshgrid(np.arange(GRID_W), np.arange(GRID_W), indexing="ij")
    dc = (kc - qc + 15).reshape(-1)
    keep = (dc >= 0) & (dc <= 30)
    e1[np.arange(GRID_W * GRID_W)[keep], dc[keep]] = 1.0
    n = 5 * NBR_ROWS * 2
    e2 = np.zeros((64, 4 * n), np.float32)
    for h in range(4):
        for j, d in enumerate(dr.reshape(-1)):
            if d >= 0:
                e2[h * 16 + d, h * n + j] = 1.0
    return jnp.asarray(e1), jnp.asarray(e2)


def _nbr_bias_blocks(rpb, T):
    e1, e2 = _nbr_fold_matrices(T)
    _, valid = _nbr_geometry(T)
    padded = jnp.pad(rpb, ((0, 0), (0, 1), (0, 128 - rpb.shape[2]))).reshape(64, 128)

    def body(e2t_ref, rpb_ref, e1t_ref, out_ref):
        picked = jnp.dot(e2t_ref[...], rpb_ref[...], precision=HI, preferred_element_type=F32)
        out_ref[...] = jnp.dot(picked, e1t_ref[...], precision=HI, preferred_element_type=F32)

    sub = _pcall(body, name="rpb_expand", out_shape=_sds((e2.shape[1], GRID_W * GRID_W), F32),
                 compiler_params=pltpu.CompilerParams(vmem_limit_bytes=VMEM_LIMIT))(e2.T, padded, e1.T)
    blocks = sub.reshape(4, 5, NBR_ROWS, 2, GRID_W, GRID_W).transpose(0, 1, 2, 4, 3, 5).reshape(4, 5, NBR_K, NBR_Q)
    return jnp.where(valid[None], blocks, NEG)


def _nbr_rpb_grad(dbias, T):
    e1, e2 = _nbr_fold_matrices(T)
    sub = dbias.reshape(4, 5, NBR_ROWS, GRID_W, 2, GRID_W).transpose(0, 1, 2, 4, 3, 5).reshape(-1, GRID_W * GRID_W)

    def body(e2_ref, sub_ref, e1_ref, out_ref):
        diag = jnp.dot(sub_ref[...], e1_ref[...], precision=HI, preferred_element_type=F32)
        out_ref[...] = jnp.dot(e2_ref[...], diag, precision=HI, preferred_element_type=F32)

    out = _pcall(body, name="rpb_fold", out_shape=_sds((64, 128), F32),
                 compiler_params=pltpu.CompilerParams(vmem_limit_bytes=VMEM_LIMIT))(e2, sub, e1)
    return out.reshape(4, 16, 128)[:, :15, :31]


def _nbr_tile(i, nq, T):
    start = pl.multiple_of(jnp.clip(i * NBR_Q - 4 * GRID_W, 0, T - NBR_K), NBR_Q)
    cls = jnp.where(i == 0, 1, jnp.where(i == 1, 2, jnp.where(i == nq - 2, 3, jnp.where(i == nq - 1, 4, 0))))
    return start, cls


def _nbr_fwd(qT, k, v1, bias):
    H, nq, _, _ = qT.shape
    T = k.shape[1]
    unroll = 2 if nq % 2 == 0 else 1

    def body(q_ref, k_ref, v_ref, b_ref, o_ref, lse_ref):
        def tile(i):
            start, cls = _nbr_tile(i, nq, T)
            win = pl.ds(start, NBR_K)
            sT = jnp.dot(k_ref[win, :], q_ref[i], preferred_element_type=F32) + b_ref[cls]
            m = jnp.max(sT, axis=0, keepdims=True)
            pT = jnp.exp(sT - m).astype(BF)
            acc = _tn(v_ref[win, :], pT)
            l = jnp.max(acc[HD:HD + 8], axis=0, keepdims=True)
            o_ref[i] = (acc[:HD] / l).astype(o_ref.dtype)
            lse_ref[i] = m + jnp.log(l)

        def tiles(ii, carry):
            for u in range(unroll):
                tile(ii * unroll + u)
            return carry

        lax.fori_loop(0, nq // unroll, tiles, 0)

    chk = lambda r: pl.BlockSpec((None, nq, r, NBR_Q), lambda h: (h, 0, 0, 0))
    tok = lambda w: pl.BlockSpec((None, T, w), lambda h: (h, 0, 0))
    return _pcall(
        body, name="nbr_fwd", grid=(H,),
        in_specs=[chk(HD), tok(HD), tok(128), pl.BlockSpec((None, 5, NBR_K, NBR_Q), lambda h: (h, 0, 0, 0))],
        out_specs=[chk(HD), chk(1)],
        out_shape=[_sds((H, nq, HD, NBR_Q), BF), _sds((H, nq, 1, NBR_Q), F32)],
        compiler_params=_cp("parallel"),
    )(qT, k, v1, bias)


def _nbr_bwd(q, qT, k, v, do, doT, lse, delta, bias):
    H, nq, _, _ = qT.shape
    T = k.shape[1]

    def body(q_ref, qT_ref, k_ref, v_ref, do_ref, doT_ref, lse_ref, dl_ref, b_ref, dq_ref, dk_ref, dv_ref, db_ref):
        dk_ref[...] = jnp.zeros_like(dk_ref)
        dv_ref[...] = jnp.zeros_like(dv_ref)
        db_ref[...] = jnp.zeros_like(db_ref)

        def tile(i, carry):
            start, cls = _nbr_tile(i, nq, T)
            win = pl.ds(start, NBR_K)
            rows = pl.ds(pl.multiple_of(i * NBR_Q, NBR_Q), NBR_Q)
            kw = k_ref[win, :]
            sT = jnp.dot(kw, qT_ref[i], preferred_element_type=F32) + b_ref[cls]
            pT = jnp.exp(sT - lse_ref[i])
            dv_ref[win, :] += jnp.dot(pT.astype(BF), do_ref[rows, :], preferred_element_type=F32)
            dpT = jnp.dot(v_ref[win, :], doT_ref[i], preferred_element_type=F32)
            ds = pT * (dpT - dl_ref[i])
            db_ref[cls] += ds
            dsT = ds.astype(BF)
            dk_ref[win, :] += jnp.dot(dsT, q_ref[rows, :], preferred_element_type=F32)
            dq_ref[i] = _tn(kw, dsT)
            return carry

        lax.fori_loop(0, nq, tile, 0)

    chk = lambda r: pl.BlockSpec((None, nq, r, NBR_Q), lambda h: (h, 0, 0, 0))
    tok = pl.BlockSpec((None, T, HD), lambda h: (h, 0, 0))
    bsp = pl.BlockSpec((None, 5, NBR_K, NBR_Q), lambda h: (h, 0, 0, 0))
    return _pcall(
        body, name="nbr_bwd", grid=(H,),
        in_specs=[tok, chk(HD), tok, tok, tok, chk(HD), chk(1), chk(1), bsp],
        out_specs=[chk(HD), tok, tok, bsp],
        out_shape=[_sds((H, nq, HD, NBR_Q), F32), _sds((H, T, HD), F32), _sds((H, T, HD), F32),
                   _sds((H, 5, NBR_K, NBR_Q), F32)],
        compiler_params=_cp("parallel"),
    )(q, qT, k, v, do, doT, lse, delta, bias)


def _band_offsets(radius):
    offs = [0]
    for r in range(1, radius + 1):
        offs += [-r, r]
    return offs


def _dilated_bias(t):
    radius = max(w // 2 for w, _ in A_CONFIGS) // t
    tabs = []
    i = np.arange(t)
    for off in _band_offsets(radius):
        d = off * t + i[None, :] - i[:, None]
        mult = np.zeros((t, t), np.float32)
        for w, dil in A_CONFIGS:
            mult += ((d % dil) == 0) & (np.abs(d) <= w // 2)
        tabs.append(np.where(mult > 0, np.log(np.maximum(mult, 1.0)), NEG).astype(np.float32))
    return jnp.asarray(np.stack(tabs)[None]), radius


def _nbr_index(t):
    rpt = t // GRID_W
    i = np.arange(t)
    c0 = np.clip(i % GRID_W - 8, 0, GRID_W - 16)
    col_ok = ((i[None, :] % GRID_W) >= c0[:, None]) & ((i[None, :] % GRID_W) < c0[:, None] + 16)
    oks = []
    for off in _band_offsets(1):
        dr = (i[None, :] // GRID_W) - (i[:, None] // GRID_W) + rpt * off
        oks.append(col_ok & (np.abs(dr) <= 7))
    return np.stack(oks)


def _nbr_bias(rpb, t):
    rpt = t // GRID_W
    e1, e2 = _rpb_fold_matrices(t)
    ok = _nbr_index(t)
    padded = jnp.pad(rpb, ((0, 0), (0, 1), (0, 128 - rpb.shape[2]))).reshape(64, 128)

    def body(e2t_ref, rpb_ref, e1t_ref, out_ref):
        picked = jnp.dot(e2t_ref[...], rpb_ref[...], precision=HI, preferred_element_type=F32)
        out_ref[...] = jnp.dot(picked, e1t_ref[...], precision=HI, preferred_element_type=F32)

    n = e2.shape[1]
    sub = _pcall(body, name="rpb_expand", out_shape=_sds((n, GRID_W * GRID_W), F32),
                 compiler_params=pltpu.CompilerParams(vmem_limit_bytes=VMEM_LIMIT))(e2.T, padded, e1.T)
    tiles = sub.reshape(4, 3, rpt, rpt, GRID_W, GRID_W).transpose(0, 1, 2, 4, 3, 5).reshape(4, 3, t, t)
    return jnp.where(ok[None], tiles, NEG)


def _nbr_mask(qi, kb, t, rows, q_on_lanes):
    rpt = t // GRID_W
    qshape, kshape = ((1, t), (t, 1)) if q_on_lanes else ((t, 1), (1, t))
    rq = rpt * qi + lax.broadcasted_iota(jnp.int32, qshape, 1 if q_on_lanes else 0) // GRID_W
    rk = rpt * kb + lax.broadcasted_iota(jnp.int32, kshape, 0 if q_on_lanes else 1) // GRID_W
    r0 = jnp.clip(rq - 4, 0, rows - 8)
    return (rk >= r0) & (rk < r0 + 8)


def _attn_band_fwd(q, k, v, bias, radius, rowmask):
    H, T, _ = q.shape
    t = BAND_T
    nq = T // t
    Hb = bias.shape[0]
    offs = _band_offsets(radius)
    rows = T // GRID_W

    def body(q_ref, k_ref, v_ref, b_ref, o_ref, lse_ref, m_ref, l_ref, acc_ref):
        i = pl.program_id(1)
        qv = q_ref[...]
        m_ref[...] = jnp.full((t, 1), NEG, F32)
        l_ref[...] = jnp.zeros((t, 1), F32)
        acc_ref[...] = jnp.zeros((t, HD), F32)

        def tile(o, off):
            kb = i + off
            st = pl.multiple_of(kb * t, t)
            ks, vs = k_ref[pl.ds(st, t), :], v_ref[pl.ds(st, t), :]
            s = _nt(qv, ks) + b_ref[o]
            if rowmask:
                s = jnp.where(_nbr_mask(i, kb, t, rows, False), s, NEG)
            m_old = m_ref[...]
            m_new = jnp.maximum(m_old, jnp.max(s, axis=-1, keepdims=True))
            a = jnp.exp(m_old - m_new)
            p = jnp.exp(s - m_new)
            l_ref[...] = a * l_ref[...] + jnp.sum(p, axis=-1, keepdims=True)
            acc_ref[...] = a * acc_ref[...] + jnp.dot(p.astype(BF), vs, preferred_element_type=F32)
            m_ref[...] = m_new

        for o, off in enumerate(offs):
            if off == 0:
                tile(o, off)
            else:
                pl.when((i + off >= 0) & (i + off < nq))(functools.partial(tile, o, off))
        o_ref[...] = (acc_ref[...] / l_ref[...]).astype(o_ref.dtype)
        lse_ref[...] = m_ref[...] + jnp.log(l_ref[...])

    qs = pl.BlockSpec((None, t, HD), lambda h, i: (h, i, 0))
    kvs = pl.BlockSpec((None, T, HD), lambda h, i: (h, 0, 0))
    bs = pl.BlockSpec((None, len(offs), t, t), lambda h, i: (h if Hb > 1 else 0, 0, 0, 0))
    return _pcall(
        body, name="attn_band_fwd_c" if rowmask else "attn_band_fwd_a", grid=(H, nq),
        in_specs=[qs, kvs, kvs, bs],
        out_specs=[qs, pl.BlockSpec((None, t, 1), lambda h, i: (h, i, 0))],
        out_shape=[_sds((H, T, HD), BF), _sds((H, T, 1), F32)],
        scratch_shapes=[pltpu.VMEM((t, 1), F32), pltpu.VMEM((t, 1), F32), pltpu.VMEM((t, HD), F32)],
        compiler_params=_cp("parallel", "parallel"),
    )(q, k, v, bias)


def _attn_band_dq(q, k, v, o, do, lse, bias, radius, rowmask):
    H, T, _ = q.shape
    t = BAND_T
    nq = T // t
    Hb = bias.shape[0]
    offs = _band_offsets(radius)
    rows = T // GRID_W

    def body(q_ref, k_ref, v_ref, o_ref, do_ref, lse_ref, b_ref, dq_ref, dl_ref, *rest):
        db_ref = rest[0] if rowmask else None
        acc_ref = rest[-1]
        i = pl.program_id(1)
        qv, dov, lse = q_ref[...], do_ref[...], lse_ref[...]
        delta = jnp.sum(dov.astype(F32) * o_ref[...].astype(F32), axis=-1, keepdims=True)
        acc_ref[...] = jnp.zeros((t, HD), F32)
        if rowmask:
            @pl.when(i == 0)
            def _():
                db_ref[...] = jnp.zeros_like(db_ref)

        def tile(o, off):
            kb = i + off
            st = pl.multiple_of(kb * t, t)
            ks, vs = k_ref[pl.ds(st, t), :], v_ref[pl.ds(st, t), :]
            s = _nt(qv, ks) + b_ref[o]
            if rowmask:
                s = jnp.where(_nbr_mask(i, kb, t, rows, False), s, NEG)
            p = jnp.exp(s - lse)
            ds = p * (_nt(dov, vs) - delta)
            acc_ref[...] += jnp.dot(ds.astype(BF), ks, preferred_element_type=F32)
            if rowmask:
                db_ref[o] += ds

        for o, off in enumerate(offs):
            if off == 0:
                tile(o, off)
            else:
                pl.when((i + off >= 0) & (i + off < nq))(functools.partial(tile, o, off))
        dq_ref[...] = acc_ref[...]
        dl_ref[...] = delta

    qs = pl.BlockSpec((None, t, HD), lambda h, i: (h, i, 0))
    kvs = pl.BlockSpec((None, T, HD), lambda h, i: (h, 0, 0))
    cs = pl.BlockSpec((None, t, 1), lambda h, i: (h, i, 0))
    bs = pl.BlockSpec((None, len(offs), t, t), lambda h, i: (h if Hb > 1 else 0, 0, 0, 0))
    out_specs = [qs, cs]
    out_shape = [_sds((H, T, HD), F32), _sds((H, T, 1), F32)]
    if rowmask:
        out_specs.append(pl.BlockSpec((None, len(offs), t, t), lambda h, i: (h, 0, 0, 0)))
        out_shape.append(_sds((H, len(offs), t, t), F32))
    return _pcall(
        body, name="attn_band_dq_c" if rowmask else "attn_band_dq_a", grid=(H, nq),
        in_specs=[qs, kvs, kvs, qs, qs, cs, bs],
        out_specs=out_specs, out_shape=out_shape,
        scratch_shapes=[pltpu.VMEM((t, HD), F32)],
        compiler_params=_cp("parallel", "arbitrary"),
    )(q, k, v, o, do, lse, bias)


def _attn_band_dkv(q, k, v, do, lse, delta, bias_t, radius, rowmask):
    H, T, _ = q.shape
    t = BAND_T
    nq = T // t
    Hb = bias_t.shape[0]
    offs = _band_offsets(radius)
    rows = T // GRID_W
    lse_r, dl_r = _rows(lse, t), _rows(delta, t)

    def body(k_ref, v_ref, q_ref, do_ref, lse_ref, dl_ref, b_ref, dk_ref, dv_ref, dk_acc, dv_acc):
        jb = pl.program_id(1)
        kv, vv = k_ref[...], v_ref[...]
        dk_acc[...] = jnp.zeros((t, HD), F32)
        dv_acc[...] = jnp.zeros((t, HD), F32)

        def tile(o, off):
            qi = jb - off
            st = pl.multiple_of(qi * t, t)
            qs, dos = q_ref[pl.ds(st, t), :], do_ref[pl.ds(st, t), :]
            sT = _nt(kv, qs) + b_ref[o]
            if rowmask:
                sT = jnp.where(_nbr_mask(qi, jb, t, rows, True), sT, NEG)
            pT = jnp.exp(sT - lse_ref[qi])
            dv_acc[...] += jnp.dot(pT.astype(BF), dos, preferred_element_type=F32)
            dsT = pT * (_nt(vv, dos) - dl_ref[qi])
            dk_acc[...] += jnp.dot(dsT.astype(BF), qs, preferred_element_type=F32)

        for o, off in enumerate(offs):
            if off == 0:
                tile(o, off)
            else:
                pl.when((jb - off >= 0) & (jb - off < nq))(functools.partial(tile, o, off))
        dk_ref[...] = dk_acc[...]
        dv_ref[...] = dv_acc[...]

    kvs = pl.BlockSpec((None, t, HD), lambda h, j: (h, j, 0))
    qs = pl.BlockSpec((None, T, HD), lambda h, j: (h, 0, 0))
    rs = pl.BlockSpec((None, nq, 1, t), lambda h, j: (h, 0, 0, 0))
    bs = pl.BlockSpec((None, len(offs), t, t), lambda h, j: (h if Hb > 1 else 0, 0, 0, 0))
    return _pcall(
        body, name="attn_band_dkv_c" if rowmask else "attn_band_dkv_a", grid=(H, nq),
        in_specs=[kvs, kvs, qs, qs, rs, rs, bs],
        out_specs=[kvs, kvs],
        out_shape=[_sds((H, T, HD), F32)] * 2,
        scratch_shapes=[pltpu.VMEM((t, HD), F32)] * 2,
        compiler_params=_cp("parallel", "parallel"),
    )(k, v, q, do, lse_r, dl_r, bias_t)


def _rpb_fold_matrices(t):
    rpt = t // GRID_W
    e1 = np.zeros((GRID_W * GRID_W, 128), np.float32)
    ic, jc = np.meshgrid(np.arange(GRID_W), np.arange(GRID_W), indexing="ij")
    dc = (jc - ic + 15).reshape(-1)
    keep = (dc >= 0) & (dc <= 30)
    e1[np.arange(GRID_W * GRID_W)[keep], dc[keep]] = 1.0
    offs = _band_offsets(1)
    n = 4 * len(offs) * rpt * rpt
    e2 = np.zeros((64, n), np.float32)
    col = 0
    for h in range(4):
        for off in offs:
            for ib in range(rpt):
                for jb in range(rpt):
                    dr = jb - ib + rpt * off + 7
                    if 0 <= dr <= 14:
                        e2[h * 16 + dr, col] = 1.0
                    col += 1
    return jnp.asarray(e1), jnp.asarray(e2)


def _rpb_grad(dbias):
    t = dbias.shape[-1]
    rpt = t // GRID_W
    e1, e2 = _rpb_fold_matrices(t)
    sub = dbias.reshape(4, 3, rpt, GRID_W, rpt, GRID_W).transpose(0, 1, 2, 4, 3, 5)
    sub = sub.reshape(4 * 3 * rpt * rpt, GRID_W * GRID_W)

    def body(e2_ref, sub_ref, e1_ref, out_ref):
        diag = jnp.dot(sub_ref[...], e1_ref[...], precision=HI, preferred_element_type=F32)
        out_ref[...] = jnp.dot(e2_ref[...], diag, precision=HI, preferred_element_type=F32)

    out = _pcall(body, name="rpb_fold", out_shape=_sds((64, 128), F32),
                 compiler_params=pltpu.CompilerParams(vmem_limit_bytes=VMEM_LIMIT))(e2, sub, e1)
    return out.reshape(4, 16, 128)[:, :15, :31]


def _sigmoid(z):
    return 1.0 / (1.0 + jnp.exp(-z))


def _merge_fwd(oa, ob, oc, hg, bg, wa, wb, wc):
    T = oa.shape[0]
    tt = min(512, T)

    def body(oa_ref, ob_ref, oc_ref, hg_ref, bg_ref, wa_ref, wb_ref, wc_ref, out_ref):
        acc = None
        for k, (o_ref, w_ref) in enumerate(((oa_ref, wa_ref), (ob_ref, wb_ref), (oc_ref, wc_ref))):
            y = jnp.dot(o_ref[...], w_ref[...], preferred_element_type=F32)
            g = _sigmoid(hg_ref[:, D * k:D * (k + 1)] + bg_ref[:, D * k:D * (k + 1)])
            acc = g * y if acc is None else acc + g * y
        out_ref[...] = acc.astype(out_ref.dtype)

    row = lambda w: pl.BlockSpec((tt, w), lambda i: (i, 0))
    const = lambda a: pl.BlockSpec(a.shape, lambda i: (0, 0))
    return _pcall(
        body, name="merge_fwd", grid=(T // tt,),
        in_specs=[row(A_W), row(BQ_W), row(C_W), row(GATE), const(bg), const(wa), const(wb), const(wc)],
        out_specs=row(D), out_shape=_sds((T, D), BF),
        compiler_params=_cp("parallel"),
    )(oa, ob, oc, hg, bg, wa, wb, wc)


def _merge_bwd(dm, oa, ob, oc, hg, bg, wa, wb, wc):
    T = oa.shape[0]
    tt = min(256, T)

    def body(dm_ref, oa_ref, ob_ref, oc_ref, hg_ref, bg_ref, wa_ref, wb_ref, wc_ref,
             dya, dyb, dyc, doa, dob, doc, dhg, dbg):
        @pl.when(pl.program_id(0) == 0)
        def _():
            dbg[...] = jnp.zeros_like(dbg)

        dmv = dm_ref[...]
        for k, (o_ref, w_ref, dy_ref, do_ref) in enumerate(
                ((oa_ref, wa_ref, dya, doa), (ob_ref, wb_ref, dyb, dob), (oc_ref, wc_ref, dyc, doc))):
            sl = slice(D * k, D * (k + 1))
            y = jnp.dot(o_ref[...], w_ref[...], preferred_element_type=F32)
            g = _sigmoid(hg_ref[:, sl] + bg_ref[:, sl])
            dy = (dmv * g).astype(BF)
            dy_ref[...] = dy
            do_ref[...] = _nt(dy, w_ref[...]).astype(do_ref.dtype)
            dz = dmv * y * (g * (1.0 - g))
            dhg[:, sl] = dz.astype(dhg.dtype)
            dbg[:, sl] += jnp.sum(dz, axis=0, keepdims=True)

    row = lambda w: pl.BlockSpec((tt, w), lambda i: (i, 0))
    const = lambda a: pl.BlockSpec(a.shape, lambda i: (0, 0))
    return _pcall(
        body, name="merge_bwd", grid=(T // tt,),
        in_specs=[row(D), row(A_W), row(BQ_W), row(C_W), row(GATE), const(bg), const(wa), const(wb), const(wc)],
        out_specs=[row(D)] * 3 + [row(A_W), row(BQ_W), row(C_W), row(GATE),
                                  pl.BlockSpec((1, GATE), lambda i: (0, 0))],
        out_shape=[_sds((T, D), BF)] * 3 + [_sds((T, A_W), BF), _sds((T, BQ_W), BF), _sds((T, C_W), BF),
                                            _sds((T, GATE), BF), _sds((1, GATE), F32)],
        compiler_params=_cp("arbitrary"),
    )(dm, oa, ob, oc, hg, bg, wa, wb, wc)


def _lin_ln(a, w, res, g, b):
    T, K = a.shape
    tt = min(256, T)

    def body(a_ref, w_ref, res_ref, g_ref, b_ref, y_ref, yb_ref, xh_ref, rs_ref):
        u = ALPHA * res_ref[...] + jnp.dot(a_ref[...], w_ref[...], preferred_element_type=F32)
        mu = jnp.mean(u, axis=-1, keepdims=True)
        c = u - mu
        r = lax.rsqrt(jnp.mean(c * c, axis=-1, keepdims=True) + LN_EPS)
        xh = c * r
        y = xh * g_ref[...] + b_ref[...]
        y_ref[...] = y
        yb_ref[...] = y.astype(BF)
        xh_ref[...] = xh
        rs_ref[...] = r

    row = lambda w_: pl.BlockSpec((tt, w_), lambda i: (i, 0))
    const = lambda s: pl.BlockSpec(s, lambda i: (0, 0))
    return _pcall(
        body, name="lin_ln", grid=(T // tt,),
        in_specs=[row(K), const((K, D)), row(D), const((1, D)), const((1, D))],
        out_specs=[row(D), row(D), row(D), row(1)],
        out_shape=[_sds((T, D), F32), _sds((T, D), BF), _sds((T, D), F32), _sds((T, 1), F32)],
        compiler_params=_cp("parallel"),
    )(a, w, res, g, b)


def _ln_bwd(dy, xh, rs, g):
    T = dy.shape[0]
    tt = min(512, T)

    def body(dy_ref, xh_ref, rs_ref, g_ref, du_ref, dub_ref, dg_ref, db_ref):
        @pl.when(pl.program_id(0) == 0)
        def _():
            dg_ref[...] = jnp.zeros_like(dg_ref)
            db_ref[...] = jnp.zeros_like(db_ref)

        dyv, xhv = dy_ref[...], xh_ref[...]
        dg_ref[...] += jnp.sum(dyv * xhv, axis=0, keepdims=True)
        db_ref[...] += jnp.sum(dyv, axis=0, keepdims=True)
        dxh = dyv * g_ref[...]
        m1 = jnp.mean(dxh, axis=-1, keepdims=True)
        m2 = jnp.mean(dxh * xhv, axis=-1, keepdims=True)
        du = rs_ref[...] * (dxh - m1 - xhv * m2)
        du_ref[...] = du
        dub_ref[...] = du.astype(BF)

    row = lambda w_: pl.BlockSpec((tt, w_), lambda i: (i, 0))
    const = lambda s: pl.BlockSpec(s, lambda i: (0, 0))
    return _pcall(
        body, name="ln_bwd", grid=(T // tt,),
        in_specs=[row(D), row(D), row(1), const((1, D))],
        out_specs=[row(D), row(D), const((1, D)), const((1, D))],
        out_shape=[_sds((T, D), F32), _sds((T, D), BF), _sds((1, D), F32), _sds((1, D), F32)],
        compiler_params=_cp("arbitrary"),
    )(dy, xh, rs, g)


def _loss_grad(y, tgt):
    T = y.shape[0]
    tt = min(512, T)

    def body(y_ref, t_ref, dy_ref, sq_ref):
        @pl.when(pl.program_id(0) == 0)
        def _():
            sq_ref[...] = jnp.zeros_like(sq_ref)

        e = y_ref[...] - t_ref[...]
        dy_ref[...] = e * (1.0 / D)
        sq_ref[...] += jnp.sum(e * e, axis=0, keepdims=True)

    row = pl.BlockSpec((tt, D), lambda i: (i, 0))
    return _pcall(
        body, name="loss_grad", grid=(T // tt,),
        in_specs=[row, row], out_specs=[row, pl.BlockSpec((1, D), lambda i: (0, 0))],
        out_shape=[_sds((T, D), F32), _sds((1, D), F32)],
        compiler_params=_cp("arbitrary"),
    )(y, tgt)


def _position():
    return lax.axis_index("x"), lax.axis_index("y"), lax.axis_index("c")


def _all_gather(xs, name):
    n = len(xs)
    hbm = pl.BlockSpec(memory_space=pl.ANY)

    def body(*refs):
        x_refs, out_refs = refs[:n], refs[n:2 * n]
        send, recv, loc = refs[2 * n:]
        x, y, c = _position()
        me, sib = (x, y, c), (x, y, 1 - c)
        chips = [(1 - x, y), (x, 1 - y), (1 - x, 1 - y)]

        def copy(a, k, block, to, src=None):
            px, py, pc = block
            dst = out_refs[a].at[4 * px + 2 * py + pc]
            return pltpu.make_async_remote_copy(
                src_ref=dst if src is None else src, dst_ref=dst,
                send_sem=send.at[a, k], recv_sem=recv.at[a, k], device_id=to, device_id_type=MESH)

        mine = [pltpu.make_async_copy(x_refs[a], out_refs[a].at[4 * x + 2 * y + c], loc.at[a]) for a in range(n)]
        for cp in mine:
            cp.start()
        first = []
        for a in range(n):
            first.append(copy(a, 0, me, sib, src=x_refs[a]))
            first += [copy(a, 1 + j, me, (*chip, c), src=x_refs[a]) for j, chip in enumerate(chips)]
        for cp in first:
            cp.start()
        passed = []
        for j, chip in enumerate(chips):
            for a in range(n):
                copy(a, 1 + j, (*chip, c), me).wait_recv()
                fwd = copy(a, 4 + j, (*chip, c), sib)
                fwd.start()
                passed.append(fwd)
        for a in range(n):
            copy(a, 0, sib, me).wait_recv()
            for j, chip in enumerate(chips):
                copy(a, 4 + j, (*chip, 1 - c), me).wait_recv()
        for cp in first + passed:
            cp.wait_send()
        for cp in mine:
            cp.wait()

    return _pcall_comm(
        body, name=name,
        in_specs=[hbm] * n, out_specs=[hbm] * n,
        out_shape=[_sds((8,) + x.shape, x.dtype) for x in xs],
        scratch_shapes=[pltpu.SemaphoreType.DMA((n, 7)), pltpu.SemaphoreType.DMA((n, 7)),
                        pltpu.SemaphoreType.DMA((n,))],
    )(*xs)


def _exchange_pair(gs, name):
    n = len(gs)
    hbm = pl.BlockSpec(memory_space=pl.ANY)

    def body(*refs):
        g_refs, out_refs = refs[:n], refs[n:2 * n]
        send, recv = refs[2 * n:]
        x, y, c = _position()
        copies = []
        for a in range(n):
            for p in range(4):
                copies.append(pltpu.make_async_remote_copy(
                    src_ref=g_refs[a].at[p, 1 - c], dst_ref=out_refs[a].at[p],
                    send_sem=send.at[a, p], recv_sem=recv.at[a, p],
                    device_id=(x, y, 1 - c), device_id_type=MESH))
        for cp in copies:
            cp.start()
        for cp in copies:
            cp.wait()

    return _pcall_comm(
        body, name=name,
        in_specs=[hbm] * n, out_specs=[hbm] * n,
        out_shape=[_sds((4,) + g.shape[2:], g.dtype) for g in gs],
        scratch_shapes=[pltpu.SemaphoreType.DMA((n, 4)), pltpu.SemaphoreType.DMA((n, 4))],
    )(*gs)


def _exchange_chips(ps, name):
    n = len(ps)
    hbm = pl.BlockSpec(memory_space=pl.ANY)

    def body(*refs):
        p_refs, out_refs = refs[:n], refs[n:2 * n]
        send, recv = refs[2 * n:]
        x, y, c = _position()
        chips = [(1 - x, y), (x, 1 - y), (1 - x, 1 - y)]
        copies = []
        for a in range(n):
            for j, (px, py) in enumerate(chips):
                copies.append(pltpu.make_async_remote_copy(
                    src_ref=p_refs[a].at[2 * px + py], dst_ref=out_refs[a].at[j],
                    send_sem=send.at[a, j], recv_sem=recv.at[a, j],
                    device_id=(px, py, c), device_id_type=MESH))
        for cp in copies:
            cp.start()
        for cp in copies:
            cp.wait()

    return _pcall_comm(
        body, name=name,
        in_specs=[hbm] * n, out_specs=[hbm] * n,
        out_shape=[_sds((3,) + p.shape[1:], p.dtype) for p in ps],
        scratch_shapes=[pltpu.SemaphoreType.DMA((n, 3)), pltpu.SemaphoreType.DMA((n, 3))],
    )(*ps)


def _pair_sum(g, got, core):
    _, _, R, C = g.shape
    tr = min(512, R)

    def body(core_ref, g_ref, r_ref, out_ref):
        out_ref[...] = (g_ref[...].astype(F32) + r_ref[...].astype(F32)).astype(out_ref.dtype)

    return _pcall(
        body, name="pair_sum",
        grid_spec=pltpu.PrefetchScalarGridSpec(
            num_scalar_prefetch=1, grid=(4, R // tr),
            in_specs=[pl.BlockSpec((None, None, tr, C), lambda p, i, cr: (p, cr[0], i, 0)),
                      pl.BlockSpec((None, tr, C), lambda p, i, cr: (p, i, 0))],
            out_specs=pl.BlockSpec((None, tr, C), lambda p, i, cr: (p, i, 0))),
        out_shape=_sds((4, R, C), g.dtype),
        compiler_params=_cp("parallel", "parallel"),
    )(core, g, got)


def _adamw_math(w, g, m, v):
    m = B1 * m + (1.0 - B1) * g
    v = B2 * v + (1.0 - B2) * (g * g)
    m_hat = m / (1.0 - B1 ** STEP)
    v_hat = v / (1.0 - B2 ** STEP)
    delta = -LR * (m_hat / (jnp.sqrt(v_hat) + EPS) + WD * w)
    return delta, m, v


def _chip_sum_adamw(p, got, chip, w, m, v):
    R, C = w.shape
    tr = min(512, R)

    def body(chip_ref, p_ref, r_ref, w_ref, m_ref, v_ref, g_out, d_out, m_out, v_out):
        g = ((p_ref[...].astype(F32) + r_ref[0].astype(F32)) + r_ref[1].astype(F32)) + r_ref[2].astype(F32)
        d, mn, vn = _adamw_math(w_ref[...], g, m_ref[...], v_ref[...])
        g_out[...], d_out[...], m_out[...], v_out[...] = g, d, mn, vn

    blk = pl.BlockSpec((tr, C), lambda i, ch: (i, 0))
    return _pcall(
        body, name="chip_sum_adamw",
        grid_spec=pltpu.PrefetchScalarGridSpec(
            num_scalar_prefetch=1, grid=(R // tr,),
            in_specs=[pl.BlockSpec((None, tr, C), lambda i, ch: (ch[0], i, 0)),
                      pl.BlockSpec((3, tr, C), lambda i, ch: (0, i, 0)), blk, blk, blk],
            out_specs=[blk] * 4),
        out_shape=[_sds((R, C), F32)] * 4,
        compiler_params=_cp("parallel"),
    )(chip, p, got, w, m, v)


def _small_sum_adamw(parts, w, m, v):
    _, R, C = parts.shape

    def body(p_ref, w_ref, m_ref, v_ref, g_out, d_out, m_out, v_out):
        g = p_ref[0]
        for k in range(1, 8):
            g = g + p_ref[k]
        d, mn, vn = _adamw_math(w_ref[...], g, m_ref[...], v_ref[...])
        g_out[...], d_out[...], m_out[...], v_out[...] = g, d, mn, vn

    return _pcall(body, name="small_sum_adamw", out_shape=[_sds((R, C), F32)] * 4,
                  compiler_params=pltpu.CompilerParams(vmem_limit_bytes=VMEM_LIMIT))(parts, w, m, v)


BIG = ("w_in", "w_branch_a", "w_branch_b", "w_branch_c", "w_out", "w_up", "w_down")
ROW_SHARDED = ("w_out", "w_down")
SMALL = ("b_gate", "q_norm_b", "k_norm_b", "rpb_c", "ln1_g", "ln1_b", "ln2_g", "ln2_b")
NAMES = ("w_in", "b_gate", "q_norm_b", "k_norm_b", "rpb_c", "w_branch_a", "w_branch_b", "w_branch_c",
         "w_out", "ln1_g", "ln1_b", "w_up", "w_down", "ln2_g", "ln2_b")


def _full_weight(gathered, name, layer):
    blk = gathered.reshape(8, DEPTH, gathered.shape[1] // DEPTH, gathered.shape[2])[:, layer]
    if name in ROW_SHARDED:
        return blk.reshape(-1, blk.shape[2])
    return blk.transpose(1, 0, 2).reshape(blk.shape[1], -1)


def _chunks(grad, name):
    if name in ROW_SHARDED:
        return grad.reshape(8, grad.shape[0] // 8, grad.shape[1])
    return grad.reshape(grad.shape[0], 8, grad.shape[1] // 8).transpose(1, 0, 2)


def _layer_fwd(x, xb, W, P, tabs, gm):
    hq, = _mm(xb, W["w_qkv"], "nn", [F32], 1024, 768, 1024, name="in_qkv")
    hg, = _mm(xb, W["w_gate"], "nn", [F32], 1024, 1024, 1024, name="in_gate")
    tab_a, tab_b = tabs
    prepped = _prep_fwd(hq, tab_a, tab_b, P["qn"], P["kn"], gm)
    T = x.shape[0]
    tb = min(FULL_T, T)
    oa, lse_a, qa = _dilated_fwd(*prepped[0:3])
    qb_t, kb_t, vb_t = prepped[3:6]
    qb, kb, vb = _chunked_t(qb_t, tb), _to_heads(kb_t), _chunked_t(vb_t, min(FULL_TK, T))
    vb1 = jnp.concatenate([vb, jnp.ones(vb.shape[:2] + (V_ROWS - HD, vb.shape[3]), vb.dtype)], axis=2)
    ob, lse_b = _attn_full_fwd(qb, kb, vb1)
    qc_t, kc_t, vc_t = prepped[6:9]
    qc, kc, vc = _chunked_t(qc_t, NBR_Q), _to_heads(kc_t), _to_heads(vc_t)
    bias_c = _nbr_bias_blocks(P["rpb"], T)
    oc, lse_c = _nbr_fwd(qc, kc, jnp.concatenate([vc, jnp.ones_like(vc)], axis=2), bias_c)
    oa_t, ob_t, oc_t = oa.transpose(2, 0, 1).reshape(T, A_W), _unchunk_t(ob), _unchunk_t(oc)
    qc = (_to_heads(qc_t), qc)
    qb = (_to_heads(qb_t), qb, kb, kb.transpose(0, 2, 1), _to_heads(vb_t))
    ka = va = None
    merged = _merge_fwd(oa_t, ob_t, oc_t, hg, P["bg"], W["w_branch_a"], W["w_branch_b"], W["w_branch_c"])
    x1, x1b, xh1, rs1 = _lin_ln(merged, W["w_out"], x, P["ln1_g"], P["ln1_b"])

    def relu2(acc):
        r = jnp.maximum(acc, 0.0)
        return r * r, r

    f, r = _mm(x1b, W["w_up"], "nn", [BF, BF], 1024, 1024, 1024, epilogue=relu2, name="mlp_up")
    x2, x2b, xh2, rs2 = _lin_ln(f, W["w_down"], x1, P["ln2_g"], P["ln2_b"])
    saved = dict(xb=xb, hq=hq, hg=hg, qkv=(qa, ka, va, qb, kb, vb, qc, kc, vc), o=(oa, ob, oc),
                 lse=(lse_a, lse_b, lse_c), o_t=(oa_t, ob_t, oc_t), bias_c=bias_c, merged=merged,
                 xh1=xh1, rs1=rs1, x1b=x1b, f=f, r=r, xh2=xh2, rs2=rs2)
    return x2, x2b, saved


def _layer_bwd(dx2, S, W, P, tabs, gm):
    G = {}
    du2, du2b, G["ln2_g"], G["ln2_b"] = _ln_bwd(dx2, S["xh2"], S["rs2"], P["ln2_g"])
    G["w_down"], = _mm(S["f"], du2b, "tn", [F32], 1024, 1024, 512, name="dw_down")
    da, = _mm(du2b, W["w_down"], "nt", [BF], 1024, 1024, 1024,
              epilogue=lambda acc, r: (acc * (2.0 * r.astype(F32)),), extras=(S["r"],), name="d_act")
    G["w_up"], = _mm(S["x1b"], da, "tn", [F32], 1024, 1024, 512, name="dw_up")
    dx1, = _mm(da, W["w_up"], "nt", [F32], 1024, 1024, 1024,
               epilogue=lambda acc, d: (ALPHA * d + acc,), extras=(du2,), name="dx_mlp")
    du1, du1b, G["ln1_g"], G["ln1_b"] = _ln_bwd(dx1, S["xh1"], S["rs1"], P["ln1_g"])
    G["w_out"], = _mm(S["merged"], du1b, "tn", [F32], 1024, 1024, 512, name="dw_out")
    dm, = _mm(du1b, W["w_out"], "nt", [F32], 1024, 1024, 1024, name="d_merged")
    oa_t, ob_t, oc_t = S["o_t"]
    dya, dyb, dyc, doa, dob, doc, dhg, G["b_gate"] = _merge_bwd(
        dm, oa_t, ob_t, oc_t, S["hg"], P["bg"], W["w_branch_a"], W["w_branch_b"], W["w_branch_c"])
    G["w_branch_a"], = _mm(oa_t, dya, "tn", [F32], 256, 1024, 512, name="dw_branch_a")
    G["w_branch_b"], = _mm(ob_t, dyb, "tn", [F32], 512, 1024, 512, name="dw_branch_b")
    G["w_branch_c"], = _mm(oc_t, dyc, "tn", [F32], 256, 1024, 512, name="dw_branch_c")

    qa, ka, va, qb, kb, vb, qc, kc, vc = S["qkv"]
    oa, ob, oc = S["o"]
    lse_a, lse_b, lse_c = S["lse"]
    dqa, dka, dva = _dilated_bwd(doa, oa, lse_a, qa)
    q_b, qT_b, k_b, kT_b, v_b = qb
    dobT = _chunked_t(dob, ob.shape[-1])
    dqbT, dkb8, dvb8 = _attn_full_bwd(q_b, qT_b, k_b, kT_b, v_b, _to_heads(dob), dobT, lse_b, _attn_delta(dobT, ob))
    group_sum = lambda t: t.reshape(k_b.shape[0], -1, t.shape[1], HD).sum(1)
    dqb, dkb, dvb = _unchunk_t(dqbT), _from_heads(group_sum(dkb8)), _from_heads(group_sum(dvb8))
    bias_c = S["bias_c"]
    q_c, qT_c = qc
    docT = _chunked_t(doc, NBR_Q)
    dqcT, dkc, dvc, dbias_c = _nbr_bwd(q_c, qT_c, kc, vc, _to_heads(doc), docT, lse_c, _attn_delta(docT, oc), bias_c)
    G["rpb_c"] = _nbr_rpb_grad(dbias_c, dx2.shape[0])

    tab_a, tab_b = tabs
    grads = [dqa, dka, dva, dqb, dkb, dvb, _unchunk_t(dqcT), _from_heads(dkc), _from_heads(dvc)]
    dhq, dqn, dkn = _prep_bwd(S["hq"], grads, tab_a, tab_b, P["qn"], P["kn"], gm)
    G["q_norm_b"] = dqn.reshape(BQ_W // HD, HD).sum(0)
    G["k_norm_b"] = dkn.reshape(BKV_W // HD, HD).sum(0)
    dw_qkv, = _mm(S["xb"], dhq, "tn", [F32], 1024, 768, 512, name="dw_qkv")
    dw_gate, = _mm(S["xb"], dhg, "tn", [F32], 1024, 1024, 512, name="dw_gate")
    G["w_in"] = jnp.concatenate([dw_qkv, dw_gate], axis=1)
    dx_a, = _mm(dhq, W["w_qkv"], "nt", [F32], 1024, 1024, 768,
                epilogue=lambda acc, d: (ALPHA * d + acc,), extras=(du1,), name="dx_qkv")
    dx, = _mm(dhg, W["w_gate"], "nt", [F32], 1024, 1024, 1024,
              epilogue=lambda acc, d: (d + acc,), extras=(dx_a,), name="dx_gate")
    return dx, G


def _pack_small(vals):
    flat = jnp.concatenate([vals[n].reshape(-1).astype(F32) for n in SMALL])
    pad = (-flat.shape[0]) % (8 * 128)
    return jnp.pad(flat, (0, pad)).reshape(-1, 128)


def _unpack_small(packed, like):
    flat, out, off = packed.reshape(-1), {}, 0
    for n in SMALL:
        size = math.prod(like[n].shape)
        out[n] = flat[off:off + size].reshape(like[n].shape)
        off += size
    return out


def kernel(x, w_in, b_gate, q_norm_b, k_norm_b, rpb_c, w_branch_a, w_branch_b, w_branch_c, w_out, ln1_g, ln1_b, w_up, w_down, ln2_g, ln2_b, loss_target, m_w_in, m_b_gate, m_q_norm_b, m_k_norm_b, m_rpb_c, m_w_branch_a, m_w_branch_b, m_w_branch_c, m_w_out, m_ln1_g, m_ln1_b, m_w_up, m_w_down, m_ln2_g, m_ln2_b, v_w_in, v_b_gate, v_q_norm_b, v_k_norm_b, v_rpb_c, v_w_branch_a, v_w_branch_b, v_w_branch_c, v_w_out, v_ln1_g, v_ln1_b, v_w_up, v_w_down, v_ln2_g, v_ln2_b):
    w = dict(w_in=w_in, b_gate=b_gate, q_norm_b=q_norm_b, k_norm_b=k_norm_b, rpb_c=rpb_c,
             w_branch_a=w_branch_a, w_branch_b=w_branch_b, w_branch_c=w_branch_c, w_out=w_out,
             ln1_g=ln1_g, ln1_b=ln1_b, w_up=w_up, w_down=w_down, ln2_g=ln2_g, ln2_b=ln2_b)
    m = dict(w_in=m_w_in, b_gate=m_b_gate, q_norm_b=m_q_norm_b, k_norm_b=m_k_norm_b, rpb_c=m_rpb_c,
             w_branch_a=m_w_branch_a, w_branch_b=m_w_branch_b, w_branch_c=m_w_branch_c, w_out=m_w_out,
             ln1_g=m_ln1_g, ln1_b=m_ln1_b, w_up=m_w_up, w_down=m_w_down, ln2_g=m_ln2_g, ln2_b=m_ln2_b)
    v = dict(w_in=v_w_in, b_gate=v_b_gate, q_norm_b=v_q_norm_b, k_norm_b=v_k_norm_b, rpb_c=v_rpb_c,
             w_branch_a=v_w_branch_a, w_branch_b=v_w_branch_b, w_branch_c=v_w_branch_c, w_out=v_w_out,
             ln1_g=v_ln1_g, ln1_b=v_ln1_b, w_up=v_w_up, w_down=v_w_down, ln2_g=v_ln2_g, ln2_b=v_ln2_b)
    T = x.shape[1]
    xc, yc, cc = _position()

    flat2 = lambda a: a.reshape(-1, a.shape[-1])
    gathered = _all_gather([flat2(w[n]).astype(BF) for n in BIG], "gather_weights")
    gathered = dict(zip(BIG, gathered))

    tabs = _rope_tables(T)
    gm = _group_mean_matrix()

    Ws, Ps = [], []
    for l in range(DEPTH):
        W = {n: _full_weight(gathered[n], n, l) for n in BIG}
        W["w_qkv"], W["w_gate"] = W["w_in"][:, :QKV], W["w_in"][:, QKV:]
        Ws.append(W)
        Ps.append(dict(qn=jnp.tile(q_norm_b[l][None], (1, 2)), kn=jnp.tile(k_norm_b[l][None], (1, 2)),
                       rpb=rpb_c[l], bg=b_gate[l][None], ln1_g=ln1_g[l][None], ln1_b=ln1_b[l][None],
                       ln2_g=ln2_g[l][None], ln2_b=ln2_b[l][None]))

    h = x[0]
    hb = h.astype(BF)
    saved = []
    for l in range(DEPTH):
        h, hb, S = _layer_fwd(h, hb, Ws[l], Ps[l], tabs, gm)
        saved.append(S)
    dy, sq = _loss_grad(h, loss_target[0])
    loss = lax.psum(0.5 / D * jnp.sum(sq), AXES)
    grads = [None] * DEPTH
    for l in reversed(range(DEPTH)):
        dy, grads[l] = _layer_bwd(dy, saved[l], Ws[l], Ps[l], tabs, gm)
    grad_x = dy[None]

    core = cc.reshape(1).astype(jnp.int32)
    chip = (2 * xc + yc).reshape(1).astype(jnp.int32)
    chunked = []
    for n in BIG:
        g = jnp.stack([_chunks(grads[l][n], n) for l in range(DEPTH)], axis=1)
        chunked.append(g.reshape(4, 2, DEPTH * g.shape[2], g.shape[3]).astype(BF))
    from_sibling = _exchange_pair(chunked, "grads_to_sibling")
    pair = [_pair_sum(g, r, core) for g, r in zip(chunked, from_sibling)]
    from_chips = _exchange_chips(pair, "grads_to_chips")
    out_g, out_d, out_m, out_v = {}, {}, {}, {}
    for n, p, r in zip(BIG, pair, from_chips):
        res = _chip_sum_adamw(p, r, chip, flat2(w[n]), flat2(m[n]), flat2(v[n]))
        out_g[n], out_d[n], out_m[n], out_v[n] = [t.reshape(w[n].shape) for t in res]

    part = _pack_small({n: jnp.stack([grads[l][n].reshape(w[n].shape[1:]) for l in range(DEPTH)]) for n in SMALL})
    parts, = _all_gather([part], "gather_small_grads")
    res = _small_sum_adamw(parts, _pack_small(w), _pack_small(m), _pack_small(v))
    for dst, packed in zip((out_g, out_d, out_m, out_v), res):
        dst.update(_unpack_small(packed, w))

    return (loss, grad_x, *[out_g[n] for n in NAMES], *[out_d[n] for n in NAMES],
            *[out_m[n] for n in NAMES], *[out_v[n] for n in NAMES])
```

```python
import functools
import math

import numpy as np
import jax
import jax.numpy as jnp
from jax import lax
from jax.experimental import pallas as pl
from jax.experimental.pallas import tpu as pltpu

F32 = jnp.float32
BF = jnp.bfloat16
HI = lax.Precision.HIGHEST
NEG = -1e30
MESH = pl.DeviceIdType.MESH
AXES = ("x", "y", "c")

D = 1024
DEPTH = 4
HD = 64
A_W, BQ_W, BKV_W, C_W = 256, 512, 128, 256
QKV = 2304
GATE = 3072
D_FF = 4096
GRID_W = 64
ALPHA = (2 * DEPTH) ** 0.25
LN_EPS = 1e-5
RMS_EPS = 1e-6
SCALE = HD ** -0.5
ROPE_THETA = 500000.0
AXIAL_THETA = 10000.0
A_CONFIGS = ((128, 1), (512, 4), (2048, 16))
LR, B1, B2, EPS, WD, STEP = 0.001, 0.9, 0.999, 1e-08, 0.01, 10

VMEM_LIMIT = 56 * 1024 * 1024
BAND_T = 256


def _pcall(body, **kw):
    return pl.pallas_call(body, **kw)


def _pcall_comm(body, **kw):
    return pl.pallas_call(body, **kw)


def _cp(*sem):
    return pltpu.CompilerParams(dimension_semantics=sem, vmem_limit_bytes=VMEM_LIMIT)


def _sds(shape, dtype):
    return jax.ShapeDtypeStruct(shape, dtype)


def _mm(a, b, dims, outs, tm, tn, tk, epilogue=None, extras=(), name="mm"):
    if dims == "tn":
        K, M = a.shape
    else:
        M, K = a.shape
    N = b.shape[0] if dims == "nt" else b.shape[1]
    tm, tn, tk = min(tm, M), min(tn, N), min(tk, K)
    assert M % tm == 0 and N % tn == 0 and K % tk == 0, (name, M, N, K, tm, tn, tk)
    nk = K // tk
    ne, no = len(extras), len(outs)

    def body(a_ref, b_ref, *rest):
        extra_refs, out_refs = rest[:ne], rest[ne:ne + no]
        av, bv = a_ref[...].astype(BF), b_ref[...].astype(BF)
        if dims == "nn":
            p = jnp.dot(av, bv, preferred_element_type=F32)
        elif dims == "nt":
            p = lax.dot_general(av, bv, (((1,), (1,)), ((), ())), preferred_element_type=F32)
        else:
            p = lax.dot_general(av, bv, (((0,), (0,)), ((), ())), preferred_element_type=F32)

        def finish(acc):
            res = epilogue(acc, *[r[...] for r in extra_refs]) if epilogue else (acc,)
            for o, r in zip(out_refs, res):
                o[...] = r.astype(o.dtype)

        if nk == 1:
            finish(p)
        else:
            acc_ref = rest[-1]
            k = pl.program_id(2)

            @pl.when(k == 0)
            def _():
                acc_ref[...] = p

            @pl.when(k > 0)
            def _():
                acc_ref[...] += p

            @pl.when(k == nk - 1)
            def _():
                finish(acc_ref[...])

    if dims == "tn":
        a_spec = pl.BlockSpec((tk, tm), lambda i, j, k: (k, i))
    else:
        a_spec = pl.BlockSpec((tm, tk), lambda i, j, k: (i, k))
    if dims == "nt":
        b_spec = pl.BlockSpec((tn, tk), lambda i, j, k: (j, k))
    else:
        b_spec = pl.BlockSpec((tk, tn), lambda i, j, k: (k, j))
    o_spec = pl.BlockSpec((tm, tn), lambda i, j, k: (i, j))
    res = _pcall(
        body, name=name, grid=(M // tm, N // tn, nk),
        in_specs=[a_spec, b_spec] + [o_spec] * ne,
        out_specs=[o_spec] * no,
        out_shape=[_sds((M, N), dt) for dt in outs],
        scratch_shapes=[pltpu.VMEM((tm, tn), F32)] if nk > 1 else [],
        compiler_params=_cp("parallel", "parallel", "arbitrary"),
    )(a, b, *extras)
    return res


def _rope_tables(T):
    pos = jnp.arange(T)

    def cs(p, theta, half):
        inv = theta ** (-jnp.arange(half, dtype=F32) / half)
        ang = p.astype(F32)[:, None] * inv[None, :]
        return jnp.cos(ang), jnp.sin(ang)

    ca, sa = cs(pos, ROPE_THETA, 8)
    one, zero, z8 = jnp.ones((T, 48), F32), jnp.zeros((T, 48), F32), jnp.zeros((T, 8), F32)
    tab_a = [jnp.concatenate(t, 1) for t in ([ca, ca, one], [-sa, z8, zero], [z8, sa, zero])]
    cr, sr = cs(pos // GRID_W, AXIAL_THETA, 16)
    cc, sc = cs(pos % GRID_W, AXIAL_THETA, 16)
    z16 = jnp.zeros((T, 16), F32)
    tab_b = [jnp.concatenate(t, 1) for t in ([cr, cr, cc, cc], [-sr, z16, -sc, z16], [z16, sr, z16, sc])]
    return [jnp.tile(t, (1, 2)) for t in tab_a], [jnp.tile(t, (1, 2)) for t in tab_b]


def _rot(x, C, S1, S2, k):
    return x * C + pltpu.roll(x, 128 - k, 1) * S1 + pltpu.roll(x, k, 1) * S2


def _rot_t(d, C, S1, S2, k):
    return d * C + pltpu.roll(d * S1, k, 1) + pltpu.roll(d * S2, 128 - k, 1)


def _group_mean_matrix():
    m = np.zeros((128, 128), np.float32)
    m[:64, :64] = 1.0 / 64
    m[64:, 64:] = 1.0 / 64
    return jnp.asarray(m)


def _prep_fwd(hq, tab_a, tab_b, qn, kn, gm):
    T = hq.shape[0]
    tt = min(256, T)
    widths = [A_W, A_W, A_W, BQ_W, BKV_W, BKV_W, C_W, C_W, C_W]

    def body(h_ref, ca, s1a, s2a, cb, s1b, s2b, qn_ref, kn_ref, gm_ref,
             qa, ka, va, qb, kb, vb, qc, kc, vc):
        def col(off, j):
            return h_ref[:, off + 128 * j: off + 128 * (j + 1)]

        for j in range(2):
            sl = slice(128 * j, 128 * (j + 1))
            qa[:, sl] = (_rot(col(0, j), ca[...], s1a[...], s2a[...], 8) * SCALE).astype(qa.dtype)
            ka[:, sl] = _rot(col(256, j), ca[...], s1a[...], s2a[...], 8).astype(ka.dtype)
            va[:, sl] = col(512, j).astype(va.dtype)
            qc[:, sl] = (col(1536, j) * SCALE).astype(qc.dtype)
            kc[:, sl] = col(1792, j).astype(kc.dtype)
            vc[:, sl] = col(2048, j).astype(vc.dtype)

        def normed(x, w):
            ms = jnp.dot(x * x, gm_ref[...], precision=HI, preferred_element_type=F32)
            return x * lax.rsqrt(ms + RMS_EPS) * w

        for j in range(4):
            y = normed(col(768, j), qn_ref[...])
            qb[:, 128 * j:128 * (j + 1)] = (_rot(y, cb[...], s1b[...], s2b[...], 16) * SCALE).astype(qb.dtype)
        y = normed(col(1280, 0), kn_ref[...])
        kb[...] = _rot(y, cb[...], s1b[...], s2b[...], 16).astype(kb.dtype)
        vb[...] = col(1408, 0).astype(vb.dtype)

    row = lambda w: pl.BlockSpec((tt, w), lambda i: (i, 0))
    const = lambda s: pl.BlockSpec(s, lambda i: (0, 0))
    return _pcall(
        body, name="prep_fwd", grid=(T // tt,),
        in_specs=[row(QKV)] + [row(128)] * 6 + [const((1, 128))] * 2 + [const((128, 128))],
        out_specs=[row(w) for w in widths],
        out_shape=[_sds((T, w), BF) for w in widths],
        compiler_params=_cp("parallel"),
    )(hq, *tab_a, *tab_b, qn, kn, gm)


def _prep_bwd(hq, grads, tab_a, tab_b, qn, kn, gm):
    T = hq.shape[0]
    tt = min(256, T)
    widths = [A_W, A_W, A_W, BQ_W, BKV_W, BKV_W, C_W, C_W, C_W]

    def body(h_ref, dqa, dka, dva, dqb, dkb, dvb, dqc, dkc, dvc,
             ca, s1a, s2a, cb, s1b, s2b, qn_ref, kn_ref, gm_ref, dh, dqn, dkn):
        i = pl.program_id(0)

        @pl.when(i == 0)
        def _():
            dqn[...] = jnp.zeros_like(dqn)
            dkn[...] = jnp.zeros_like(dkn)

        def put(off, j, val):
            dh[:, off + 128 * j: off + 128 * (j + 1)] = val.astype(dh.dtype)

        for j in range(2):
            sl = slice(128 * j, 128 * (j + 1))
            put(0, j, _rot_t(dqa[:, sl] * SCALE, ca[...], s1a[...], s2a[...], 8))
            put(256, j, _rot_t(dka[:, sl], ca[...], s1a[...], s2a[...], 8))
            put(512, j, dva[:, sl])
            put(1536, j, dqc[:, sl] * SCALE)
            put(1792, j, dkc[:, sl])
            put(2048, j, dvc[:, sl])

        def norm_bwd(x, w, e):
            ms = jnp.dot(x * x, gm_ref[...], precision=HI, preferred_element_type=F32)
            r = lax.rsqrt(ms + RMS_EPS)
            n = x * r
            dn = e * w
            proj = jnp.dot(dn * n, gm_ref[...], precision=HI, preferred_element_type=F32)
            return r * (dn - n * proj), jnp.sum(e * n, axis=0, keepdims=True)

        for j in range(4):
            sl = slice(128 * j, 128 * (j + 1))
            e = _rot_t(dqb[:, sl] * SCALE, cb[...], s1b[...], s2b[...], 16)
            dx, dw = norm_bwd(h_ref[:, 768 + 128 * j: 768 + 128 * (j + 1)], qn_ref[...], e)
            put(768, j, dx)
            dqn[:, sl] += dw
        e = _rot_t(dkb[...], cb[...], s1b[...], s2b[...], 16)
        dx, dw = norm_bwd(h_ref[:, 1280:1408], kn_ref[...], e)
        put(1280, 0, dx)
        dkn[...] += dw
        put(1408, 0, dvb[...])

    row = lambda w: pl.BlockSpec((tt, w), lambda i: (i, 0))
    const = lambda s: pl.BlockSpec(s, lambda i: (0, 0))
    return _pcall(
        body, name="prep_bwd", grid=(T // tt,),
        in_specs=[row(QKV)] + [row(w) for w in widths] + [row(128)] * 6
        + [const((1, 128))] * 2 + [const((128, 128))],
        out_specs=[row(QKV), const((1, BQ_W)), const((1, BKV_W))],
        out_shape=[_sds((T, QKV), BF), _sds((1, BQ_W), F32), _sds((1, BKV_W), F32)],
        compiler_params=_cp("arbitrary"),
    )(hq, *grads, *tab_a, *tab_b, qn, kn, gm)


def _to_heads(x):
    T, W = x.shape
    return x.reshape(T, W // HD, HD).transpose(1, 0, 2)


def _from_heads(x):
    H, T, _ = x.shape
    return x.transpose(1, 0, 2).reshape(T, H * HD)


def _rows(x, t):
    H, T, _ = x.shape
    return x.reshape(H, T // t, 1, t)


def _nt(a, b):
    return lax.dot_general(a, b, (((1,), (1,)), ((), ())), preferred_element_type=F32)


def _chunked_t(x, t):
    T, W = x.shape
    return x.reshape(T // t, t, W // HD, HD).transpose(2, 0, 3, 1)


def _unchunk_t(x):
    H, n, _, t = x.shape
    return x.transpose(1, 3, 0, 2).reshape(n * t, H * HD)


FULL_T = 512
FULL_TK = 512
FULL_HEADS = 1
V_ROWS = 72


def _attn_full_fwd(qT, k, vT1):
    Hq, nq, _, t = qT.shape
    Hk, T, _ = k.shape
    G = Hq // Hk
    nk, tk = vT1.shape[1], vT1.shape[3]

    HB = FULL_HEADS
    assert G % HB == 0

    def body(q_ref, k_ref, v_ref, o_ref, lse_ref, *acc_refs):
        for acc_ref in acc_refs:
            acc_ref[...] = jnp.zeros((V_ROWS, t), F32)

        def scores(j, b):
            sT = jnp.dot(k_ref[pl.ds(pl.multiple_of(j * tk, tk), tk), :], q_ref[b], preferred_element_type=F32)
            return sT, jnp.max(sT, axis=0, keepdims=True)

        def update(j, b, scored, m_old):
            sT, m_tile = scored
            m_new = jnp.maximum(m_old, m_tile)
            pT = jnp.exp(sT - m_new).astype(BF)
            acc_refs[b][...] = (jnp.exp(m_old - m_new) * acc_refs[b][...]
                                + jnp.dot(v_ref[j], pT, preferred_element_type=F32))
            return m_new

        def step(j, carry):
            ms, ss = carry
            nxt = jnp.minimum(j + 1, nk - 1)
            new_s = tuple(scores(nxt, b) for b in range(HB))
            new_m = tuple(update(j, b, ss[b], ms[b]) for b in range(HB))
            return new_m, new_s

        init = (tuple(jnp.full((1, t), NEG, F32) for _ in range(HB)), tuple(scores(0, b) for b in range(HB)))
        ms, _ = lax.fori_loop(0, nk, step, init)
        for b in range(HB):
            l = acc_refs[b][pl.ds(HD, 1), :]
            o_ref[b] = (acc_refs[b][pl.ds(0, HD), :] / l).astype(o_ref.dtype)
            lse_ref[b] = ms[b] + jnp.log(l)

    qs = pl.BlockSpec((HB, None, HD, t), lambda h, i: (h, i, 0, 0))
    return _pcall(
        body, name="attn_full_fwd", grid=(Hq // HB, nq),
        in_specs=[qs, pl.BlockSpec((None, T, HD), lambda h, i: (h * HB // G, 0, 0)),
                  pl.BlockSpec((None, nk, V_ROWS, tk), lambda h, i: (h * HB // G, 0, 0, 0))],
        out_specs=[qs, pl.BlockSpec((HB, None, 1, t), lambda h, i: (h, i, 0, 0))],
        out_shape=[_sds((Hq, nq, HD, t), BF), _sds((Hq, nq, 1, t), F32)],
        scratch_shapes=[pltpu.VMEM((V_ROWS, t), F32)] * HB,
        compiler_params=_cp("parallel", "parallel"),
    )(qT, k, vT1)


def _attn_delta(doT, oT):
    Hq, nq, _, t = doT.shape

    def body(do_ref, o_ref, dl_ref):
        dl_ref[...] = jnp.sum(do_ref[...].astype(F32) * o_ref[...].astype(F32), axis=0, keepdims=True)

    qs = pl.BlockSpec((None, None, HD, t), lambda h, i: (h, i, 0, 0))
    rs = pl.BlockSpec((None, None, 1, t), lambda h, i: (h, i, 0, 0))
    return _pcall(body, name="attn_delta", grid=(Hq, nq), in_specs=[qs, qs], out_specs=rs,
                  out_shape=_sds((Hq, nq, 1, t), F32), compiler_params=_cp("parallel", "parallel"))(doT, oT)


def _attn_full_bwd(qT, k, kT, v, doT, lse, delta):
    Hq, nq, _, t = qT.shape
    Hk, T, _ = k.shape
    G = Hq // Hk
    nkv = T // t

    def body(qT_ref, doT_ref, lse_ref, dl_ref, k_ref, kT_ref, v_ref, dq_ref, dk_ref, dv_ref, dq_acc, dk_acc, dv_acc):
        j = pl.program_id(1)

        @pl.when(j == 0)
        def _():
            dq_acc[...] = jnp.zeros_like(dq_acc)

        kv, kTv, vv = k_ref[...], kT_ref[...], v_ref[...]
        dk_acc[...] = jnp.zeros((t, HD), F32)
        dv_acc[...] = jnp.zeros((t, HD), F32)

        def step(i, carry):
            qT, doT = qT_ref[i], doT_ref[i]
            pT = jnp.exp(jnp.dot(kv, qT, preferred_element_type=F32) - lse_ref[i])
            dv_acc[...] += _nt(pT.astype(BF), doT)
            dsT = (pT * (jnp.dot(vv, doT, preferred_element_type=F32) - dl_ref[i])).astype(BF)
            dk_acc[...] += _nt(dsT, qT)
            dq_acc[i] += jnp.dot(kTv, dsT, preferred_element_type=F32)
            return carry

        lax.fori_loop(0, nq, step, 0)
        dk_ref[...] = dk_acc[...].astype(dk_ref.dtype)
        dv_ref[...] = dv_acc[...].astype(dv_ref.dtype)

        @pl.when(j == nkv - 1)
        def _():
            dq_ref[...] = dq_acc[...].astype(dq_ref.dtype)

    chk = pl.BlockSpec((None, nq, HD, t), lambda h, j: (h, 0, 0, 0))
    row = pl.BlockSpec((None, nq, 1, t), lambda h, j: (h, 0, 0, 0))
    kvs = pl.BlockSpec((None, t, HD), lambda h, j: (h // G, j, 0))
    out = pl.BlockSpec((None, t, HD), lambda h, j: (h, j, 0))
    return _pcall(
        body, name="attn_full_bwd", grid=(Hq, nkv),
        in_specs=[chk, chk, row, row, kvs, pl.BlockSpec((None, HD, t), lambda h, j: (h // G, 0, j)), kvs],
        out_specs=[chk, out, out],
        out_shape=[_sds((Hq, nq, HD, t), BF), _sds((Hq, T, HD), BF), _sds((Hq, T, HD), BF)],
        scratch_shapes=[pltpu.VMEM((nq, HD, t), F32), pltpu.VMEM((t, HD), F32), pltpu.VMEM((t, HD), F32)],
        compiler_params=_cp("parallel", "arbitrary"),
    )(qT, doT, lse, delta, k, kT, v)


DIL_Q = 128
DIL_K = 256
DIL_R = 64


def _vh(x, d):
    T = x.shape[0]
    return x.reshape(T // d, d, 4, HD).transpose(1, 2, 0, 3).reshape(4 * d, T // d, HD)


def _vh_inv(y, d):
    L = y.shape[1]
    return y.reshape(d, 4, L, HD).transpose(2, 0, 1, 3).reshape(L * d, 4 * HD)


def _vh_chunks(x, d):
    y = _vh(x, d)
    return y.reshape(y.shape[0], y.shape[1] // DIL_Q, DIL_Q, HD).transpose(0, 1, 3, 2)


def _chunks_to_dims(c, d):
    _, nq, R, _ = c.shape
    return c.reshape(d, 4, nq, R, DIL_Q).transpose(1, 3, 2, 4, 0).reshape(4, R, nq * DIL_Q * d)


def _dims_to_chunks(x, d):
    _, R, T = x.shape
    nq = T // (DIL_Q * d)
    return x.reshape(4, R, nq, DIL_Q, d).transpose(4, 0, 2, 1, 3).reshape(4 * d, nq, R, DIL_Q)


def _chunks_to_tokens(c, d):
    _, nq, _, _ = c.shape
    return c.reshape(d, 4, nq, HD, DIL_Q).transpose(2, 4, 0, 1, 3).reshape(nq * DIL_Q * d, 4 * HD)


def _dil_window(i, L):
    start = pl.multiple_of(jnp.clip(i * DIL_Q - DIL_R, 0, L - DIL_K), DIL_R)
    kk = start + lax.broadcasted_iota(jnp.int32, (DIL_K, 1), 0)
    qq = i * DIL_Q + lax.broadcasted_iota(jnp.int32, (1, DIL_Q), 1)
    return start, jnp.abs(kk - qq) <= DIL_R


def _tn(a, b):
    return lax.dot_general(a, b, (((0,), (0,)), ((), ())), preferred_element_type=F32)


def _dil_fwd(qT, k, v):
    V, nq, _, _ = qT.shape
    L = k.shape[1]
    assert L >= DIL_K

    unroll = 4 if nq % 4 == 0 else 1

    def body(q_ref, k_ref, v_ref, o_ref, lse_ref):
        def tile(i):
            start, mask = _dil_window(i, L)
            win = pl.ds(start, DIL_K)
            sT = jnp.where(mask, jnp.dot(k_ref[win, :], q_ref[i], preferred_element_type=F32), NEG)
            m = jnp.max(sT, axis=0, keepdims=True)
            pT = jnp.exp(sT - m)
            l = jnp.sum(pT, axis=0, keepdims=True)
            o_ref[i] = _tn(v_ref[win, :], pT.astype(BF)) / l
            lse_ref[i] = m + jnp.log(l)

        def tiles(ii, carry):
            for u in range(unroll):
                tile(ii * unroll + u)
            return carry

        lax.fori_loop(0, nq // unroll, tiles, 0)

    chk = lambda r, dt: (pl.BlockSpec((None, nq, r, DIL_Q), lambda h: (h, 0, 0, 0)), _sds((V, nq, r, DIL_Q), dt))
    tok = pl.BlockSpec((None, L, HD), lambda h: (h, 0, 0))
    (o_spec, o_shape), (l_spec, l_shape) = chk(HD, F32), chk(1, F32)
    return _pcall(
        body, name=f"dil_fwd_{V // 4}", grid=(V,),
        in_specs=[chk(HD, BF)[0], tok, tok],
        out_specs=[o_spec, l_spec], out_shape=[o_shape, l_shape],
        compiler_params=_cp("parallel"),
    )(qT, k, v)


def _dil_merge(os_, lses):
    _, _, T = os_[0].shape
    tt = min(1024, T)
    n = len(os_)

    def body(*refs):
        o_refs, l_refs, (o_out, l_out) = refs[:n], refs[n:2 * n], refs[2 * n:]
        m = l_refs[0][...]
        for r in l_refs[1:]:
            m = jnp.maximum(m, r[...])
        ws = [jnp.exp(r[...] - m) for r in l_refs]
        tot = ws[0]
        for w_ in ws[1:]:
            tot = tot + w_
        acc = ws[0] * o_refs[0][...]
        for w_, o in zip(ws[1:], o_refs[1:]):
            acc = acc + w_ * o[...]
        o_out[...] = (acc / tot).astype(o_out.dtype)
        l_out[...] = m + jnp.log(tot)

    os_spec = pl.BlockSpec((None, HD, tt), lambda h, i: (h, 0, i))
    ls_spec = pl.BlockSpec((None, 1, tt), lambda h, i: (h, 0, i))
    return _pcall(
        body, name="dil_merge", grid=(4, T // tt),
        in_specs=[os_spec] * n + [ls_spec] * n, out_specs=[os_spec, ls_spec],
        out_shape=[_sds((4, HD, T), BF), _sds((4, 1, T), F32)],
        compiler_params=_cp("parallel", "parallel"),
    )(*os_, *lses)


def _dims_delta(doT, oT):
    _, _, T = doT.shape
    tt = min(2048, T)

    def body(do_ref, o_ref, dl_ref):
        dl_ref[...] = jnp.sum(do_ref[...].astype(F32) * o_ref[...].astype(F32), axis=0, keepdims=True)

    spec = pl.BlockSpec((None, HD, tt), lambda h, i: (h, 0, i))
    return _pcall(body, name="dims_delta", grid=(4, T // tt), in_specs=[spec, spec],
                  out_specs=pl.BlockSpec((None, 1, tt), lambda h, i: (h, 0, i)),
                  out_shape=_sds((4, 1, T), F32), compiler_params=_cp("parallel", "parallel"))(doT, oT)


def _dil_bwd(qT, k, v, doT, lse, delta):
    V, nq, _, _ = qT.shape
    L = k.shape[1]
    unroll = 2 if nq % 2 == 0 else 1

    def body(qT_ref, k_ref, v_ref, doT_ref, lse_ref, dl_ref, dq_ref, dk_ref, dv_ref, dk_acc, dv_acc):
        dk_acc[...] = jnp.zeros_like(dk_acc)
        dv_acc[...] = jnp.zeros_like(dv_acc)

        def tile(i):
            start, mask = _dil_window(i, L)
            win = pl.ds(start, DIL_K)
            kw, qT, doT = k_ref[win, :], qT_ref[i], doT_ref[i]
            sT = jnp.where(mask, jnp.dot(kw, qT, preferred_element_type=F32), NEG)
            pT = jnp.exp(sT - lse_ref[i])
            dv = _nt(pT.astype(BF), doT)
            dsT = (pT * (jnp.dot(v_ref[win, :], doT, preferred_element_type=F32) - dl_ref[i])).astype(BF)
            dk = _nt(dsT, qT)
            dq_ref[i] = _tn(kw, dsT).astype(dq_ref.dtype)
            return win, dk, dv

        def tiles(ii, carry):
            done = [tile(ii * unroll + u) for u in range(unroll)]
            for win, dk, dv in done:
                dk_acc[win, :] += dk
                dv_acc[win, :] += dv
            return carry

        lax.fori_loop(0, nq // unroll, tiles, 0)
        dk_ref[...] = dk_acc[...].astype(dk_ref.dtype)
        dv_ref[...] = dv_acc[...].astype(dv_ref.dtype)

    chk = lambda r: pl.BlockSpec((None, nq, r, DIL_Q), lambda h: (h, 0, 0, 0))
    tok = pl.BlockSpec((None, L, HD), lambda h: (h, 0, 0))
    return _pcall(
        body, name=f"dil_bwd_{V // 4}", grid=(V,),
        in_specs=[chk(HD), tok, tok, chk(HD), chk(1), chk(1)],
        out_specs=[chk(HD), tok, tok],
        out_shape=[_sds((V, nq, HD, DIL_Q), BF), _sds((V, L, HD), BF), _sds((V, L, HD), BF)],
        scratch_shapes=[pltpu.VMEM((L, HD), F32)] * 2,
        compiler_params=_cp("parallel"),
    )(qT, k, v, doT, lse, delta)


def _dilated_fwd(qa, ka, va):
    outs, lses, saved = [], [], []
    for _, d in A_CONFIGS:
        qT, k, v = _vh_chunks(qa, d), _vh(ka, d), _vh(va, d)
        oT, lse = _dil_fwd(qT, k, v)
        outs.append(_chunks_to_dims(oT, d))
        lses.append(_chunks_to_dims(lse, d))
        saved.append((qT, k, v))
    o, lse = _dil_merge(outs, lses)
    return o, lse, saved


def _dilated_bwd(do_t, o, lse, saved):
    doT = do_t.reshape(do_t.shape[0], 4, HD).transpose(1, 2, 0)
    delta = _dims_delta(doT, o)
    parts = []
    for (_, d), (qT, k, v) in zip(A_CONFIGS, saved):
        dqT, dk_c, dv_c = _dil_bwd(qT, k, v, _dims_to_chunks(doT, d), _dims_to_chunks(lse, d), _dims_to_chunks(delta, d))
        parts.append((_chunks_to_tokens(dqT, d), _vh_inv(dk_c, d), _vh_inv(dv_c, d)))
    total = lambda ts: sum(t.astype(F32) for t in ts).astype(do_t.dtype)
    return tuple(total(ts) for ts in zip(*parts))


NBR_Q = 128
NBR_ROWS = 10
NBR_K = NBR_ROWS * GRID_W


def _nbr_class_tiles(nq):
    return [2, 0, 1, nq - 2, nq - 1]


def _nbr_geometry(T):
    rows, nq = T // GRID_W, T // NBR_Q
    assert rows >= NBR_ROWS + 4 and nq >= 5
    kr, kc = np.divmod(np.arange(NBR_K), GRID_W)
    qr, qc = np.divmod(np.arange(NBR_Q), GRID_W)
    c0 = np.clip(qc - 8, 0, GRID_W - 16)
    col_ok = (kc[:, None] >= c0[None, :]) & (kc[:, None] < c0[None, :] + 16)
    drs, valids = [], []
    for i in _nbr_class_tiles(nq):
        start_row = int(np.clip(2 * i - 4, 0, rows - NBR_ROWS))
        r = 2 * i + qr
        r0 = np.clip(r - 4, 0, rows - 8)
        rk = start_row + kr
        row_ok = (rk[:, None] >= r0[None, :]) & (rk[:, None] < r0[None, :] + 8)
        valids.append(row_ok & col_ok)
        dr = start_row + np.arange(NBR_ROWS)[:, None] - (2 * i + np.arange(2)[None, :]) + 7
        drs.append(np.where((dr >= 0) & (dr <= 14), dr, -1))
    return np.stack(drs), np.stack(valids)


def _nbr_fold_matrices(T):
    dr, _ = _nbr_geometry(T)
    e1 = np.zeros((GRID_W * GRID_W, 128), np.float32)
    kc, qc = np.meshgrid(np.arange(GRID_W), np.arange(GRID_W), indexing="ij")
    dc = (kc - qc + 15).reshape(-1)
    keep = (dc >= 0) & (dc <= 30)
    e1[np.arange(GRID_W * GRID_W)[keep], dc[keep]] = 1.0
    n = 5 * NBR_ROWS * 2
    e2 = np.zeros((64, 4 * n), np.float32)
    for h in range(4):
        for j, d in enumerate(dr.reshape(-1)):
            if d >= 0:
                e2[h * 16 + d, h * n + j] = 1.0
    return jnp.asarray(e1), jnp.asarray(e2)


def _nbr_bias_blocks(rpb, T):
    e1, e2 = _nbr_fold_matrices(T)
    _, valid = _nbr_geometry(T)
    padded = jnp.pad(rpb, ((0, 0), (0, 1), (0, 128 - rpb.shape[2]))).reshape(64, 128)

    def body(e2t_ref, rpb_ref, e1t_ref, out_ref):
        picked = jnp.dot(e2t_ref[...], rpb_ref[...], precision=HI, preferred_element_type=F32)
        out_ref[...] = jnp.dot(picked, e1t_ref[...], precision=HI, preferred_element_type=F32)

    sub = _pcall(body, name="rpb_expand", out_shape=_sds((e2.shape[1], GRID_W * GRID_W), F32),
                 compiler_params=pltpu.CompilerParams(vmem_limit_bytes=VMEM_LIMIT))(e2.T, padded, e1.T)
    blocks = sub.reshape(4, 5, NBR_ROWS, 2, GRID_W, GRID_W).transpose(0, 1, 2, 4, 3, 5).reshape(4, 5, NBR_K, NBR_Q)
    return jnp.where(valid[None], blocks, NEG)


def _nbr_rpb_grad(dbias, T):
    e1, e2 = _nbr_fold_matrices(T)
    sub = dbias.reshape(4, 5, NBR_ROWS, GRID_W, 2, GRID_W).transpose(0, 1, 2, 4, 3, 5).reshape(-1, GRID_W * GRID_W)

    def body(e2_ref, sub_ref, e1_ref, out_ref):
        diag = jnp.dot(sub_ref[...], e1_ref[...], precision=HI, preferred_element_type=F32)
        out_ref[...] = jnp.dot(e2_ref[...], diag, precision=HI, preferred_element_type=F32)

    out = _pcall(body, name="rpb_fold", out_shape=_sds((64, 128), F32),
                 compiler_params=pltpu.CompilerParams(vmem_limit_bytes=VMEM_LIMIT))(e2, sub, e1)
    return out.reshape(4, 16, 128)[:, :15, :31]


def _nbr_tile(i, nq, T):
    start = pl.multiple_of(jnp.clip(i * NBR_Q - 4 * GRID_W, 0, T - NBR_K), NBR_Q)
    cls = jnp.where(i == 0, 1, jnp.where(i == 1, 2, jnp.where(i == nq - 2, 3, jnp.where(i == nq - 1, 4, 0))))
    return start, cls


def _nbr_fwd(qT, k, v, bias):
    H, nq, _, _ = qT.shape
    T = k.shape[1]
    unroll = 2 if nq % 2 == 0 else 1

    def body(q_ref, k_ref, v_ref, b_ref, o_ref, lse_ref):
        def tile(i):
            start, cls = _nbr_tile(i, nq, T)
            win = pl.ds(start, NBR_K)
            sT = jnp.dot(k_ref[win, :], q_ref[i], preferred_element_type=F32) + b_ref[cls]
            m = jnp.max(sT, axis=0, keepdims=True)
            pT = jnp.exp(sT - m)
            l = jnp.sum(pT, axis=0, keepdims=True)
            o_ref[i] = (_tn(v_ref[win, :], pT.astype(BF)) / l).astype(o_ref.dtype)
            lse_ref[i] = m + jnp.log(l)

        def tiles(ii, carry):
            for u in range(unroll):
                tile(ii * unroll + u)
            return carry

        lax.fori_loop(0, nq // unroll, tiles, 0)

    chk = lambda r: pl.BlockSpec((None, nq, r, NBR_Q), lambda h: (h, 0, 0, 0))
    tok = pl.BlockSpec((None, T, HD), lambda h: (h, 0, 0))
    return _pcall(
        body, name="nbr_fwd", grid=(H,),
        in_specs=[chk(HD), tok, tok, pl.BlockSpec((None, 5, NBR_K, NBR_Q), lambda h: (h, 0, 0, 0))],
        out_specs=[chk(HD), chk(1)],
        out_shape=[_sds((H, nq, HD, NBR_Q), BF), _sds((H, nq, 1, NBR_Q), F32)],
        compiler_params=_cp("parallel"),
    )(qT, k, v, bias)


def _nbr_bwd(qT, k, v, doT, lse, delta, bias):
    H, nq, _, _ = qT.shape
    T = k.shape[1]

    def body(qT_ref, k_ref, v_ref, doT_ref, lse_ref, dl_ref, b_ref, dq_ref, dk_ref, dv_ref, db_ref, dk_acc, dv_acc):
        dk_acc[...] = jnp.zeros_like(dk_acc)
        dv_acc[...] = jnp.zeros_like(dv_acc)
        db_ref[...] = jnp.zeros_like(db_ref)

        def tile(i, carry):
            start, cls = _nbr_tile(i, nq, T)
            win = pl.ds(start, NBR_K)
            kw, qT, doT = k_ref[win, :], qT_ref[i], doT_ref[i]
            sT = jnp.dot(kw, qT, preferred_element_type=F32) + b_ref[cls]
            pT = jnp.exp(sT - lse_ref[i])
            dv_acc[win, :] += _nt(pT.astype(BF), doT)
            ds = pT * (jnp.dot(v_ref[win, :], doT, preferred_element_type=F32) - dl_ref[i])
            db_ref[cls] += ds
            dsT = ds.astype(BF)
            dk_acc[win, :] += _nt(dsT, qT)
            dq_ref[i] = _tn(kw, dsT).astype(dq_ref.dtype)
            return carry

        lax.fori_loop(0, nq, tile, 0)
        dk_ref[...] = dk_acc[...].astype(dk_ref.dtype)
        dv_ref[...] = dv_acc[...].astype(dv_ref.dtype)

    chk = lambda r: pl.BlockSpec((None, nq, r, NBR_Q), lambda h: (h, 0, 0, 0))
    tok = pl.BlockSpec((None, T, HD), lambda h: (h, 0, 0))
    bsp = pl.BlockSpec((None, 5, NBR_K, NBR_Q), lambda h: (h, 0, 0, 0))
    return _pcall(
        body, name="nbr_bwd", grid=(H,),
        in_specs=[chk(HD), tok, tok, chk(HD), chk(1), chk(1), bsp],
        out_specs=[chk(HD), tok, tok, bsp],
        out_shape=[_sds((H, nq, HD, NBR_Q), BF), _sds((H, T, HD), BF), _sds((H, T, HD), BF),
                   _sds((H, 5, NBR_K, NBR_Q), F32)],
        scratch_shapes=[pltpu.VMEM((T, HD), F32)] * 2,
        compiler_params=_cp("parallel"),
    )(qT, k, v, doT, lse, delta, bias)


def _band_offsets(radius):
    offs = [0]
    for r in range(1, radius + 1):
        offs += [-r, r]
    return offs


def _dilated_bias(t):
    radius = max(w // 2 for w, _ in A_CONFIGS) // t
    tabs = []
    i = np.arange(t)
    for off in _band_offsets(radius):
        d = off * t + i[None, :] - i[:, None]
        mult = np.zeros((t, t), np.float32)
        for w, dil in A_CONFIGS:
            mult += ((d % dil) == 0) & (np.abs(d) <= w // 2)
        tabs.append(np.where(mult > 0, np.log(np.maximum(mult, 1.0)), NEG).astype(np.float32))
    return jnp.asarray(np.stack(tabs)[None]), radius


def _nbr_index(t):
    rpt = t // GRID_W
    i = np.arange(t)
    c0 = np.clip(i % GRID_W - 8, 0, GRID_W - 16)
    col_ok = ((i[None, :] % GRID_W) >= c0[:, None]) & ((i[None, :] % GRID_W) < c0[:, None] + 16)
    oks = []
    for off in _band_offsets(1):
        dr = (i[None, :] // GRID_W) - (i[:, None] // GRID_W) + rpt * off
        oks.append(col_ok & (np.abs(dr) <= 7))
    return np.stack(oks)


def _nbr_bias(rpb, t):
    rpt = t // GRID_W
    e1, e2 = _rpb_fold_matrices(t)
    ok = _nbr_index(t)
    padded = jnp.pad(rpb, ((0, 0), (0, 1), (0, 128 - rpb.shape[2]))).reshape(64, 128)

    def body(e2t_ref, rpb_ref, e1t_ref, out_ref):
        picked = jnp.dot(e2t_ref[...], rpb_ref[...], precision=HI, preferred_element_type=F32)
        out_ref[...] = jnp.dot(picked, e1t_ref[...], precision=HI, preferred_element_type=F32)

    n = e2.shape[1]
    sub = _pcall(body, name="rpb_expand", out_shape=_sds((n, GRID_W * GRID_W), F32),
                 compiler_params=pltpu.CompilerParams(vmem_limit_bytes=VMEM_LIMIT))(e2.T, padded, e1.T)
    tiles = sub.reshape(4, 3, rpt, rpt, GRID_W, GRID_W).transpose(0, 1, 2, 4, 3, 5).reshape(4, 3, t, t)
    return jnp.where(ok[None], tiles, NEG)


def _nbr_mask(qi, kb, t, rows, q_on_lanes):
    rpt = t // GRID_W
    qshape, kshape = ((1, t), (t, 1)) if q_on_lanes else ((t, 1), (1, t))
    rq = rpt * qi + lax.broadcasted_iota(jnp.int32, qshape, 1 if q_on_lanes else 0) // GRID_W
    rk = rpt * kb + lax.broadcasted_iota(jnp.int32, kshape, 0 if q_on_lanes else 1) // GRID_W
    r0 = jnp.clip(rq - 4, 0, rows - 8)
    return (rk >= r0) & (rk < r0 + 8)


def _attn_band_fwd(q, k, v, bias, radius, rowmask):
    H, T, _ = q.shape
    t = BAND_T
    nq = T // t
    Hb = bias.shape[0]
    offs = _band_offsets(radius)
    rows = T // GRID_W

    def body(q_ref, k_ref, v_ref, b_ref, o_ref, lse_ref, m_ref, l_ref, acc_ref):
        i = pl.program_id(1)
        qv = q_ref[...]
        m_ref[...] = jnp.full((t, 1), NEG, F32)
        l_ref[...] = jnp.zeros((t, 1), F32)
        acc_ref[...] = jnp.zeros((t, HD), F32)

        def tile(o, off):
            kb = i + off
            st = pl.multiple_of(kb * t, t)
            ks, vs = k_ref[pl.ds(st, t), :], v_ref[pl.ds(st, t), :]
            s = _nt(qv, ks) + b_ref[o]
            if rowmask:
                s = jnp.where(_nbr_mask(i, kb, t, rows, False), s, NEG)
            m_old = m_ref[...]
            m_new = jnp.maximum(m_old, jnp.max(s, axis=-1, keepdims=True))
            a = jnp.exp(m_old - m_new)
            p = jnp.exp(s - m_new)
            l_ref[...] = a * l_ref[...] + jnp.sum(p, axis=-1, keepdims=True)
            acc_ref[...] = a * acc_ref[...] + jnp.dot(p.astype(BF), vs, preferred_element_type=F32)
            m_ref[...] = m_new

        for o, off in enumerate(offs):
            if off == 0:
                tile(o, off)
            else:
                pl.when((i + off >= 0) & (i + off < nq))(functools.partial(tile, o, off))
        o_ref[...] = (acc_ref[...] / l_ref[...]).astype(o_ref.dtype)
        lse_ref[...] = m_ref[...] + jnp.log(l_ref[...])

    qs = pl.BlockSpec((None, t, HD), lambda h, i: (h, i, 0))
    kvs = pl.BlockSpec((None, T, HD), lambda h, i: (h, 0, 0))
    bs = pl.BlockSpec((None, len(offs), t, t), lambda h, i: (h if Hb > 1 else 0, 0, 0, 0))
    return _pcall(
        body, name="attn_band_fwd_c" if rowmask else "attn_band_fwd_a", grid=(H, nq),
        in_specs=[qs, kvs, kvs, bs],
        out_specs=[qs, pl.BlockSpec((None, t, 1), lambda h, i: (h, i, 0))],
        out_shape=[_sds((H, T, HD), BF), _sds((H, T, 1), F32)],
        scratch_shapes=[pltpu.VMEM((t, 1), F32), pltpu.VMEM((t, 1), F32), pltpu.VMEM((t, HD), F32)],
        compiler_params=_cp("parallel", "parallel"),
    )(q, k, v, bias)


def _attn_band_dq(q, k, v, o, do, lse, bias, radius, rowmask):
    H, T, _ = q.shape
    t = BAND_T
    nq = T // t
    Hb = bias.shape[0]
    offs = _band_offsets(radius)
    rows = T // GRID_W

    def body(q_ref, k_ref, v_ref, o_ref, do_ref, lse_ref, b_ref, dq_ref, dl_ref, *rest):
        db_ref = rest[0] if rowmask else None
        acc_ref = rest[-1]
        i = pl.program_id(1)
        qv, dov, lse = q_ref[...], do_ref[...], lse_ref[...]
        delta = jnp.sum(dov.astype(F32) * o_ref[...].astype(F32), axis=-1, keepdims=True)
        acc_ref[...] = jnp.zeros((t, HD), F32)
        if rowmask:
            @pl.when(i == 0)
            def _():
                db_ref[...] = jnp.zeros_like(db_ref)

        def tile(o, off):
            kb = i + off
            st = pl.multiple_of(kb * t, t)
            ks, vs = k_ref[pl.ds(st, t), :], v_ref[pl.ds(st, t), :]
            s = _nt(qv, ks) + b_ref[o]
            if rowmask:
                s = jnp.where(_nbr_mask(i, kb, t, rows, False), s, NEG)
            p = jnp.exp(s - lse)
            ds = p * (_nt(dov, vs) - delta)
            acc_ref[...] += jnp.dot(ds.astype(BF), ks, preferred_element_type=F32)
            if rowmask:
                db_ref[o] += ds

        for o, off in enumerate(offs):
            if off == 0:
                tile(o, off)
            else:
                pl.when((i + off >= 0) & (i + off < nq))(functools.partial(tile, o, off))
        dq_ref[...] = acc_ref[...]
        dl_ref[...] = delta

    qs = pl.BlockSpec((None, t, HD), lambda h, i: (h, i, 0))
    kvs = pl.BlockSpec((None, T, HD), lambda h, i: (h, 0, 0))
    cs = pl.BlockSpec((None, t, 1), lambda h, i: (h, i, 0))
    bs = pl.BlockSpec((None, len(offs), t, t), lambda h, i: (h if Hb > 1 else 0, 0, 0, 0))
    out_specs = [qs, cs]
    out_shape = [_sds((H, T, HD), F32), _sds((H, T, 1), F32)]
    if rowmask:
        out_specs.append(pl.BlockSpec((None, len(offs), t, t), lambda h, i: (h, 0, 0, 0)))
        out_shape.append(_sds((H, len(offs), t, t), F32))
    return _pcall(
        body, name="attn_band_dq_c" if rowmask else "attn_band_dq_a", grid=(H, nq),
        in_specs=[qs, kvs, kvs, qs, qs, cs, bs],
        out_specs=out_specs, out_shape=out_shape,
        scratch_shapes=[pltpu.VMEM((t, HD), F32)],
        compiler_params=_cp("parallel", "arbitrary"),
    )(q, k, v, o, do, lse, bias)


def _attn_band_dkv(q, k, v, do, lse, delta, bias_t, radius, rowmask):
    H, T, _ = q.shape
    t = BAND_T
    nq = T // t
    Hb = bias_t.shape[0]
    offs = _band_offsets(radius)
    rows = T // GRID_W
    lse_r, dl_r = _rows(lse, t), _rows(delta, t)

    def body(k_ref, v_ref, q_ref, do_ref, lse_ref, dl_ref, b_ref, dk_ref, dv_ref, dk_acc, dv_acc):
        jb = pl.program_id(1)
        kv, vv = k_ref[...], v_ref[...]
        dk_acc[...] = jnp.zeros((t, HD), F32)
        dv_acc[...] = jnp.zeros((t, HD), F32)

        def tile(o, off):
            qi = jb - off
            st = pl.multiple_of(qi * t, t)
            qs, dos = q_ref[pl.ds(st, t), :], do_ref[pl.ds(st, t), :]
            sT = _nt(kv, qs) + b_ref[o]
            if rowmask:
                sT = jnp.where(_nbr_mask(qi, jb, t, rows, True), sT, NEG)
            pT = jnp.exp(sT - lse_ref[qi])
            dv_acc[...] += jnp.dot(pT.astype(BF), dos, preferred_element_type=F32)
            dsT = pT * (_nt(vv, dos) - dl_ref[qi])
            dk_acc[...] += jnp.dot(dsT.astype(BF), qs, preferred_element_type=F32)

        for o, off in enumerate(offs):
            if off == 0:
                tile(o, off)
            else:
                pl.when((jb - off >= 0) & (jb - off < nq))(functools.partial(tile, o, off))
        dk_ref[...] = dk_acc[...]
        dv_ref[...] = dv_acc[...]

    kvs = pl.BlockSpec((None, t, HD), lambda h, j: (h, j, 0))
    qs = pl.BlockSpec((None, T, HD), lambda h, j: (h, 0, 0))
    rs = pl.BlockSpec((None, nq, 1, t), lambda h, j: (h, 0, 0, 0))
    bs = pl.BlockSpec((None, len(offs), t, t), lambda h, j: (h if Hb > 1 else 0, 0, 0, 0))
    return _pcall(
        body, name="attn_band_dkv_c" if rowmask else "attn_band_dkv_a", grid=(H, nq),
        in_specs=[kvs, kvs, qs, qs, rs, rs, bs],
        out_specs=[kvs, kvs],
        out_shape=[_sds((H, T, HD), F32)] * 2,
        scratch_shapes=[pltpu.VMEM((t, HD), F32)] * 2,
        compiler_params=_cp("parallel", "parallel"),
    )(k, v, q, do, lse_r, dl_r, bias_t)


def _rpb_fold_matrices(t):
    rpt = t // GRID_W
    e1 = np.zeros((GRID_W * GRID_W, 128), np.float32)
    ic, jc = np.meshgrid(np.arange(GRID_W), np.arange(GRID_W), indexing="ij")
    dc = (jc - ic + 15).reshape(-1)
    keep = (dc >= 0) & (dc <= 30)
    e1[np.arange(GRID_W * GRID_W)[keep], dc[keep]] = 1.0
    offs = _band_offsets(1)
    n = 4 * len(offs) * rpt * rpt
    e2 = np.zeros((64, n), np.float32)
    col = 0
    for h in range(4):
        for off in offs:
            for ib in range(rpt):
                for jb in range(rpt):
                    dr = jb - ib + rpt * off + 7
                    if 0 <= dr <= 14:
                        e2[h * 16 + dr, col] = 1.0
                    col += 1
    return jnp.asarray(e1), jnp.asarray(e2)


def _rpb_grad(dbias):
    t = dbias.shape[-1]
    rpt = t // GRID_W
    e1, e2 = _rpb_fold_matrices(t)
    sub = dbias.reshape(4, 3, rpt, GRID_W, rpt, GRID_W).transpose(0, 1, 2, 4, 3, 5)
    sub = sub.reshape(4 * 3 * rpt * rpt, GRID_W * GRID_W)

    def body(e2_ref, sub_ref, e1_ref, out_ref):
        diag = jnp.dot(sub_ref[...], e1_ref[...], precision=HI, preferred_element_type=F32)
        out_ref[...] = jnp.dot(e2_ref[...], diag, precision=HI, preferred_element_type=F32)

    out = _pcall(body, name="rpb_fold", out_shape=_sds((64, 128), F32),
                 compiler_params=pltpu.CompilerParams(vmem_limit_bytes=VMEM_LIMIT))(e2, sub, e1)
    return out.reshape(4, 16, 128)[:, :15, :31]


def _sigmoid(z):
    return 1.0 / (1.0 + jnp.exp(-z))


def _merge_fwd(oa, ob, oc, hg, bg, wa, wb, wc):
    T = oa.shape[0]
    tt = min(512, T)

    def body(oa_ref, ob_ref, oc_ref, hg_ref, bg_ref, wa_ref, wb_ref, wc_ref, out_ref):
        acc = None
        for k, (o_ref, w_ref) in enumerate(((oa_ref, wa_ref), (ob_ref, wb_ref), (oc_ref, wc_ref))):
            y = jnp.dot(o_ref[...], w_ref[...], preferred_element_type=F32)
            g = _sigmoid(hg_ref[:, D * k:D * (k + 1)] + bg_ref[:, D * k:D * (k + 1)])
            acc = g * y if acc is None else acc + g * y
        out_ref[...] = acc.astype(out_ref.dtype)

    row = lambda w: pl.BlockSpec((tt, w), lambda i: (i, 0))
    const = lambda a: pl.BlockSpec(a.shape, lambda i: (0, 0))
    return _pcall(
        body, name="merge_fwd", grid=(T // tt,),
        in_specs=[row(A_W), row(BQ_W), row(C_W), row(GATE), const(bg), const(wa), const(wb), const(wc)],
        out_specs=row(D), out_shape=_sds((T, D), BF),
        compiler_params=_cp("parallel"),
    )(oa, ob, oc, hg, bg, wa, wb, wc)


def _merge_bwd(dm, oa, ob, oc, hg, bg, wa, wb, wc):
    T = oa.shape[0]
    tt = min(256, T)

    def body(dm_ref, oa_ref, ob_ref, oc_ref, hg_ref, bg_ref, wa_ref, wb_ref, wc_ref,
             dya, dyb, dyc, doa, dob, doc, dhg, dbg):
        @pl.when(pl.program_id(0) == 0)
        def _():
            dbg[...] = jnp.zeros_like(dbg)

        dmv = dm_ref[...]
        for k, (o_ref, w_ref, dy_ref, do_ref) in enumerate(
                ((oa_ref, wa_ref, dya, doa), (ob_ref, wb_ref, dyb, dob), (oc_ref, wc_ref, dyc, doc))):
            sl = slice(D * k, D * (k + 1))
            y = jnp.dot(o_ref[...], w_ref[...], preferred_element_type=F32)
            g = _sigmoid(hg_ref[:, sl] + bg_ref[:, sl])
            dy = (dmv * g).astype(BF)
            dy_ref[...] = dy
            do_ref[...] = _nt(dy, w_ref[...]).astype(do_ref.dtype)
            dz = dmv * y * (g * (1.0 - g))
            dhg[:, sl] = dz.astype(dhg.dtype)
            dbg[:, sl] += jnp.sum(dz, axis=0, keepdims=True)

    row = lambda w: pl.BlockSpec((tt, w), lambda i: (i, 0))
    const = lambda a: pl.BlockSpec(a.shape, lambda i: (0, 0))
    return _pcall(
        body, name="merge_bwd", grid=(T // tt,),
        in_specs=[row(D), row(A_W), row(BQ_W), row(C_W), row(GATE), const(bg), const(wa), const(wb), const(wc)],
        out_specs=[row(D)] * 3 + [row(A_W), row(BQ_W), row(C_W), row(GATE),
                                  pl.BlockSpec((1, GATE), lambda i: (0, 0))],
        out_shape=[_sds((T, D), BF)] * 3 + [_sds((T, A_W), BF), _sds((T, BQ_W), BF), _sds((T, C_W), BF),
                                            _sds((T, GATE), BF), _sds((1, GATE), F32)],
        compiler_params=_cp("arbitrary"),
    )(dm, oa, ob, oc, hg, bg, wa, wb, wc)


def _lin_ln(a, w, res, g, b):
    T, K = a.shape
    tt = min(256, T)

    def body(a_ref, w_ref, res_ref, g_ref, b_ref, y_ref, yb_ref, xh_ref, rs_ref):
        u = ALPHA * res_ref[...] + jnp.dot(a_ref[...], w_ref[...], preferred_element_type=F32)
        mu = jnp.mean(u, axis=-1, keepdims=True)
        c = u - mu
        r = lax.rsqrt(jnp.mean(c * c, axis=-1, keepdims=True) + LN_EPS)
        xh = c * r
        y = xh * g_ref[...] + b_ref[...]
        y_ref[...] = y
        yb_ref[...] = y.astype(BF)
        xh_ref[...] = xh
        rs_ref[...] = r

    row = lambda w_: pl.BlockSpec((tt, w_), lambda i: (i, 0))
    const = lambda s: pl.BlockSpec(s, lambda i: (0, 0))
    return _pcall(
        body, name="lin_ln", grid=(T // tt,),
        in_specs=[row(K), const((K, D)), row(D), const((1, D)), const((1, D))],
        out_specs=[row(D), row(D), row(D), row(1)],
        out_shape=[_sds((T, D), F32), _sds((T, D), BF), _sds((T, D), F32), _sds((T, 1), F32)],
        compiler_params=_cp("parallel"),
    )(a, w, res, g, b)


def _ln_bwd(dy, xh, rs, g):
    T = dy.shape[0]
    tt = min(512, T)

    def body(dy_ref, xh_ref, rs_ref, g_ref, du_ref, dub_ref, dg_ref, db_ref):
        @pl.when(pl.program_id(0) == 0)
        def _():
            dg_ref[...] = jnp.zeros_like(dg_ref)
            db_ref[...] = jnp.zeros_like(db_ref)

        dyv, xhv = dy_ref[...], xh_ref[...]
        dg_ref[...] += jnp.sum(dyv * xhv, axis=0, keepdims=True)
        db_ref[...] += jnp.sum(dyv, axis=0, keepdims=True)
        dxh = dyv * g_ref[...]
        m1 = jnp.mean(dxh, axis=-1, keepdims=True)
        m2 = jnp.mean(dxh * xhv, axis=-1, keepdims=True)
        du = rs_ref[...] * (dxh - m1 - xhv * m2)
        du_ref[...] = du
        dub_ref[...] = du.astype(BF)

    row = lambda w_: pl.BlockSpec((tt, w_), lambda i: (i, 0))
    const = lambda s: pl.BlockSpec(s, lambda i: (0, 0))
    return _pcall(
        body, name="ln_bwd", grid=(T // tt,),
        in_specs=[row(D), row(D), row(1), const((1, D))],
        out_specs=[row(D), row(D), const((1, D)), const((1, D))],
        out_shape=[_sds((T, D), F32), _sds((T, D), BF), _sds((1, D), F32), _sds((1, D), F32)],
        compiler_params=_cp("arbitrary"),
    )(dy, xh, rs, g)


def _loss_grad(y, tgt):
    T = y.shape[0]
    tt = min(512, T)

    def body(y_ref, t_ref, dy_ref, sq_ref):
        @pl.when(pl.program_id(0) == 0)
        def _():
            sq_ref[...] = jnp.zeros_like(sq_ref)

        e = y_ref[...] - t_ref[...]
        dy_ref[...] = e * (1.0 / D)
        sq_ref[...] += jnp.sum(e * e, axis=0, keepdims=True)

    row = pl.BlockSpec((tt, D), lambda i: (i, 0))
    return _pcall(
        body, name="loss_grad", grid=(T // tt,),
        in_specs=[row, row], out_specs=[row, pl.BlockSpec((1, D), lambda i: (0, 0))],
        out_shape=[_sds((T, D), F32), _sds((1, D), F32)],
        compiler_params=_cp("arbitrary"),
    )(y, tgt)


def _position():
    return lax.axis_index("x"), lax.axis_index("y"), lax.axis_index("c")


def _all_gather(xs, name):
    n = len(xs)
    hbm = pl.BlockSpec(memory_space=pl.ANY)

    def body(*refs):
        x_refs, out_refs = refs[:n], refs[n:2 * n]
        send, recv, loc = refs[2 * n:]
        x, y, c = _position()
        me, sib = (x, y, c), (x, y, 1 - c)
        chips = [(1 - x, y), (x, 1 - y), (1 - x, 1 - y)]

        def copy(a, k, block, to, src=None):
            px, py, pc = block
            dst = out_refs[a].at[4 * px + 2 * py + pc]
            return pltpu.make_async_remote_copy(
                src_ref=dst if src is None else src, dst_ref=dst,
                send_sem=send.at[a, k], recv_sem=recv.at[a, k], device_id=to, device_id_type=MESH)

        mine = [pltpu.make_async_copy(x_refs[a], out_refs[a].at[4 * x + 2 * y + c], loc.at[a]) for a in range(n)]
        for cp in mine:
            cp.start()
        first = []
        for a in range(n):
            first.append(copy(a, 0, me, sib, src=x_refs[a]))
            first += [copy(a, 1 + j, me, (*chip, c), src=x_refs[a]) for j, chip in enumerate(chips)]
        for cp in first:
            cp.start()
        passed = []
        for j, chip in enumerate(chips):
            for a in range(n):
                copy(a, 1 + j, (*chip, c), me).wait_recv()
                fwd = copy(a, 4 + j, (*chip, c), sib)
                fwd.start()
                passed.append(fwd)
        for a in range(n):
            copy(a, 0, sib, me).wait_recv()
            for j, chip in enumerate(chips):
                copy(a, 4 + j, (*chip, 1 - c), me).wait_recv()
        for cp in first + passed:
            cp.wait_send()
        for cp in mine:
            cp.wait()

    return _pcall_comm(
        body, name=name,
        in_specs=[hbm] * n, out_specs=[hbm] * n,
        out_shape=[_sds((8,) + x.shape, x.dtype) for x in xs],
        scratch_shapes=[pltpu.SemaphoreType.DMA((n, 7)), pltpu.SemaphoreType.DMA((n, 7)),
                        pltpu.SemaphoreType.DMA((n,))],
    )(*xs)


def _exchange_pair(gs, name):
    n = len(gs)
    hbm = pl.BlockSpec(memory_space=pl.ANY)

    def body(*refs):
        g_refs, out_refs = refs[:n], refs[n:2 * n]
        send, recv = refs[2 * n:]
        x, y, c = _position()
        copies = []
        for a in range(n):
            for p in range(4):
                copies.append(pltpu.make_async_remote_copy(
                    src_ref=g_refs[a].at[p, 1 - c], dst_ref=out_refs[a].at[p],
                    send_sem=send.at[a, p], recv_sem=recv.at[a, p],
                    device_id=(x, y, 1 - c), device_id_type=MESH))
        for cp in copies:
            cp.start()
        for cp in copies:
            cp.wait()

    return _pcall_comm(
        body, name=name,
        in_specs=[hbm] * n, out_specs=[hbm] * n,
        out_shape=[_sds((4,) + g.shape[2:], g.dtype) for g in gs],
        scratch_shapes=[pltpu.SemaphoreType.DMA((n, 4)), pltpu.SemaphoreType.DMA((n, 4))],
    )(*gs)


def _exchange_chips(ps, name):
    n = len(ps)
    hbm = pl.BlockSpec(memory_space=pl.ANY)

    def body(*refs):
        p_refs, out_refs = refs[:n], refs[n:2 * n]
        send, recv = refs[2 * n:]
        x, y, c = _position()
        chips = [(1 - x, y), (x, 1 - y), (1 - x, 1 - y)]
        copies = []
        for a in range(n):
            for j, (px, py) in enumerate(chips):
                copies.append(pltpu.make_async_remote_copy(
                    src_ref=p_refs[a].at[2 * px + py], dst_ref=out_refs[a].at[j],
                    send_sem=send.at[a, j], recv_sem=recv.at[a, j],
                    device_id=(px, py, c), device_id_type=MESH))
        for cp in copies:
            cp.start()
        for cp in copies:
            cp.wait()

    return _pcall_comm(
        body, name=name,
        in_specs=[hbm] * n, out_specs=[hbm] * n,
        out_shape=[_sds((3,) + p.shape[1:], p.dtype) for p in ps],
        scratch_shapes=[pltpu.SemaphoreType.DMA((n, 3)), pltpu.SemaphoreType.DMA((n, 3))],
    )(*ps)


def _pair_sum(g, got, core):
    _, _, R, C = g.shape
    tr = min(512, R)

    def body(core_ref, g_ref, r_ref, out_ref):
        out_ref[...] = (g_ref[...].astype(F32) + r_ref[...].astype(F32)).astype(out_ref.dtype)

    return _pcall(
        body, name="pair_sum",
        grid_spec=pltpu.PrefetchScalarGridSpec(
            num_scalar_prefetch=1, grid=(4, R // tr),
            in_specs=[pl.BlockSpec((None, None, tr, C), lambda p, i, cr: (p, cr[0], i, 0)),
                      pl.BlockSpec((None, tr, C), lambda p, i, cr: (p, i, 0))],
            out_specs=pl.BlockSpec((None, tr, C), lambda p, i, cr: (p, i, 0))),
        out_shape=_sds((4, R, C), g.dtype),
        compiler_params=_cp("parallel", "parallel"),
    )(core, g, got)


def _adamw_math(w, g, m, v):
    m = B1 * m + (1.0 - B1) * g
    v = B2 * v + (1.0 - B2) * (g * g)
    m_hat = m / (1.0 - B1 ** STEP)
    v_hat = v / (1.0 - B2 ** STEP)
    delta = -LR * (m_hat / (jnp.sqrt(v_hat) + EPS) + WD * w)
    return delta, m, v


def _chip_sum_adamw(p, got, chip, w, m, v):
    R, C = w.shape
    tr = min(512, R)

    def body(chip_ref, p_ref, r_ref, w_ref, m_ref, v_ref, g_out, d_out, m_out, v_out):
        g = ((p_ref[...].astype(F32) + r_ref[0].astype(F32)) + r_ref[1].astype(F32)) + r_ref[2].astype(F32)
        d, mn, vn = _adamw_math(w_ref[...], g, m_ref[...], v_ref[...])
        g_out[...], d_out[...], m_out[...], v_out[...] = g, d, mn, vn

    blk = pl.BlockSpec((tr, C), lambda i, ch: (i, 0))
    return _pcall(
        body, name="chip_sum_adamw",
        grid_spec=pltpu.PrefetchScalarGridSpec(
            num_scalar_prefetch=1, grid=(R // tr,),
            in_specs=[pl.BlockSpec((None, tr, C), lambda i, ch: (ch[0], i, 0)),
                      pl.BlockSpec((3, tr, C), lambda i, ch: (0, i, 0)), blk, blk, blk],
            out_specs=[blk] * 4),
        out_shape=[_sds((R, C), F32)] * 4,
        compiler_params=_cp("parallel"),
    )(chip, p, got, w, m, v)


def _small_sum_adamw(parts, w, m, v):
    _, R, C = parts.shape

    def body(p_ref, w_ref, m_ref, v_ref, g_out, d_out, m_out, v_out):
        g = p_ref[0]
        for k in range(1, 8):
            g = g + p_ref[k]
        d, mn, vn = _adamw_math(w_ref[...], g, m_ref[...], v_ref[...])
        g_out[...], d_out[...], m_out[...], v_out[...] = g, d, mn, vn

    return _pcall(body, name="small_sum_adamw", out_shape=[_sds((R, C), F32)] * 4,
                  compiler_params=pltpu.CompilerParams(vmem_limit_bytes=VMEM_LIMIT))(parts, w, m, v)


BIG = ("w_in", "w_branch_a", "w_branch_b", "w_branch_c", "w_out", "w_up", "w_down")
ROW_SHARDED = ("w_out", "w_down")
SMALL = ("b_gate", "q_norm_b", "k_norm_b", "rpb_c", "ln1_g", "ln1_b", "ln2_g", "ln2_b")
NAMES = ("w_in", "b_gate", "q_norm_b", "k_norm_b", "rpb_c", "w_branch_a", "w_branch_b", "w_branch_c",
         "w_out", "ln1_g", "ln1_b", "w_up", "w_down", "ln2_g", "ln2_b")


def _full_weight(gathered, name, layer):
    blk = gathered.reshape(8, DEPTH, gathered.shape[1] // DEPTH, gathered.shape[2])[:, layer]
    if name in ROW_SHARDED:
        return blk.reshape(-1, blk.shape[2])
    return blk.transpose(1, 0, 2).reshape(blk.shape[1], -1)


def _chunks(grad, name):
    if name in ROW_SHARDED:
        return grad.reshape(8, grad.shape[0] // 8, grad.shape[1])
    return grad.reshape(grad.shape[0], 8, grad.shape[1] // 8).transpose(1, 0, 2)


def _layer_fwd(x, xb, W, P, tabs, gm):
    hq, = _mm(xb, W["w_qkv"], "nn", [F32], 1024, 768, 1024, name="in_qkv")
    hg, = _mm(xb, W["w_gate"], "nn", [F32], 1024, 1024, 1024, name="in_gate")
    tab_a, tab_b = tabs
    prepped = _prep_fwd(hq, tab_a, tab_b, P["qn"], P["kn"], gm)
    T = x.shape[0]
    tb = min(FULL_T, T)
    oa, lse_a, qa = _dilated_fwd(*prepped[0:3])
    qb_t, kb_t, vb_t = prepped[3:6]
    qb, kb, vb = _chunked_t(qb_t, tb), _to_heads(kb_t), _chunked_t(vb_t, min(FULL_TK, T))
    vb1 = jnp.concatenate([vb, jnp.ones(vb.shape[:2] + (V_ROWS - HD, vb.shape[3]), vb.dtype)], axis=2)
    ob, lse_b = _attn_full_fwd(qb, kb, vb1)
    qc_t, kc_t, vc_t = prepped[6:9]
    qc, kc, vc = _chunked_t(qc_t, NBR_Q), _to_heads(kc_t), _to_heads(vc_t)
    bias_c = _nbr_bias_blocks(P["rpb"], T)
    oc, lse_c = _nbr_fwd(qc, kc, vc, bias_c)
    oa_t, ob_t, oc_t = oa.transpose(2, 0, 1).reshape(T, A_W), _unchunk_t(ob), _unchunk_t(oc)
    qb = (qb, kb, kb.transpose(0, 2, 1), _to_heads(vb_t))
    ka = va = None
    merged = _merge_fwd(oa_t, ob_t, oc_t, hg, P["bg"], W["w_branch_a"], W["w_branch_b"], W["w_branch_c"])
    x1, x1b, xh1, rs1 = _lin_ln(merged, W["w_out"], x, P["ln1_g"], P["ln1_b"])

    def relu2(acc):
        r = jnp.maximum(acc, 0.0)
        return r * r, r

    f, r = _mm(x1b, W["w_up"], "nn", [BF, BF], 1024, 1024, 1024, epilogue=relu2, name="mlp_up")
    x2, x2b, xh2, rs2 = _lin_ln(f, W["w_down"], x1, P["ln2_g"], P["ln2_b"])
    saved = dict(xb=xb, hq=hq, hg=hg, qkv=(qa, ka, va, qb, kb, vb, qc, kc, vc), o=(oa, ob, oc),
                 lse=(lse_a, lse_b, lse_c), o_t=(oa_t, ob_t, oc_t), bias_c=bias_c, merged=merged,
                 xh1=xh1, rs1=rs1, x1b=x1b, f=f, r=r, xh2=xh2, rs2=rs2)
    return x2, x2b, saved


def _layer_bwd(dx2, S, W, P, tabs, gm):
    G = {}
    du2, du2b, G["ln2_g"], G["ln2_b"] = _ln_bwd(dx2, S["xh2"], S["rs2"], P["ln2_g"])
    G["w_down"], = _mm(S["f"], du2b, "tn", [F32], 1024, 1024, 512, name="dw_down")
    da, = _mm(du2b, W["w_down"], "nt", [BF], 1024, 1024, 1024,
              epilogue=lambda acc, r: (acc * (2.0 * r.astype(F32)),), extras=(S["r"],), name="d_act")
    G["w_up"], = _mm(S["x1b"], da, "tn", [F32], 1024, 1024, 512, name="dw_up")
    dx1, = _mm(da, W["w_up"], "nt", [F32], 1024, 1024, 1024,
               epilogue=lambda acc, d: (ALPHA * d + acc,), extras=(du2,), name="dx_mlp")
    du1, du1b, G["ln1_g"], G["ln1_b"] = _ln_bwd(dx1, S["xh1"], S["rs1"], P["ln1_g"])
    G["w_out"], = _mm(S["merged"], du1b, "tn", [F32], 1024, 1024, 512, name="dw_out")
    dm, = _mm(du1b, W["w_out"], "nt", [F32], 1024, 1024, 1024, name="d_merged")
    oa_t, ob_t, oc_t = S["o_t"]
    dya, dyb, dyc, doa, dob, doc, dhg, G["b_gate"] = _merge_bwd(
        dm, oa_t, ob_t, oc_t, S["hg"], P["bg"], W["w_branch_a"], W["w_branch_b"], W["w_branch_c"])
    G["w_branch_a"], = _mm(oa_t, dya, "tn", [F32], 256, 1024, 512, name="dw_branch_a")
    G["w_branch_b"], = _mm(ob_t, dyb, "tn", [F32], 512, 1024, 512, name="dw_branch_b")
    G["w_branch_c"], = _mm(oc_t, dyc, "tn", [F32], 256, 1024, 512, name="dw_branch_c")

    qa, ka, va, qb, kb, vb, qc, kc, vc = S["qkv"]
    oa, ob, oc = S["o"]
    lse_a, lse_b, lse_c = S["lse"]
    dqa, dka, dva = _dilated_bwd(doa, oa, lse_a, qa)
    qT_b, k_b, kT_b, v_b = qb
    dobT = _chunked_t(dob, ob.shape[-1])
    dqbT, dkb8, dvb8 = _attn_full_bwd(qT_b, k_b, kT_b, v_b, dobT, lse_b, _attn_delta(dobT, ob))
    group_sum = lambda t: t.reshape(k_b.shape[0], -1, t.shape[1], HD).sum(1)
    dqb, dkb, dvb = _unchunk_t(dqbT), _from_heads(group_sum(dkb8)), _from_heads(group_sum(dvb8))
    bias_c = S["bias_c"]
    docT = _chunked_t(doc, NBR_Q)
    dqcT, dkc, dvc, dbias_c = _nbr_bwd(qc, kc, vc, docT, lse_c, _attn_delta(docT, oc), bias_c)
    G["rpb_c"] = _nbr_rpb_grad(dbias_c, dx2.shape[0])

    tab_a, tab_b = tabs
    grads = [dqa, dka, dva, dqb, dkb, dvb, _unchunk_t(dqcT), _from_heads(dkc), _from_heads(dvc)]
    dhq, dqn, dkn = _prep_bwd(S["hq"], grads, tab_a, tab_b, P["qn"], P["kn"], gm)
    G["q_norm_b"] = dqn.reshape(BQ_W // HD, HD).sum(0)
    G["k_norm_b"] = dkn.reshape(BKV_W // HD, HD).sum(0)
    dw_qkv, = _mm(S["xb"], dhq, "tn", [F32], 1024, 768, 512, name="dw_qkv")
    dw_gate, = _mm(S["xb"], dhg, "tn", [F32], 1024, 1024, 512, name="dw_gate")
    G["w_in"] = jnp.concatenate([dw_qkv, dw_gate], axis=1)
    dx_a, = _mm(dhq, W["w_qkv"], "nt", [F32], 1024, 1024, 768,
                epilogue=lambda acc, d: (ALPHA * d + acc,), extras=(du1,), name="dx_qkv")
    dx, = _mm(dhg, W["w_gate"], "nt", [F32], 1024, 1024, 1024,
              epilogue=lambda acc, d: (d + acc,), extras=(dx_a,), name="dx_gate")
    return dx, G


def _pack_small(vals):
    flat = jnp.concatenate([vals[n].reshape(-1).astype(F32) for n in SMALL])
    pad = (-flat.shape[0]) % (8 * 128)
    return jnp.pad(flat, (0, pad)).reshape(-1, 128)


def _unpack_small(packed, like):
    flat, out, off = packed.reshape(-1), {}, 0
    for n in SMALL:
        size = math.prod(like[n].shape)
        out[n] = flat[off:off + size].reshape(like[n].shape)
        off += size
    return out


def kernel(x, w_in, b_gate, q_norm_b, k_norm_b, rpb_c, w_branch_a, w_branch_b, w_branch_c, w_out, ln1_g, ln1_b, w_up, w_down, ln2_g, ln2_b, loss_target, m_w_in, m_b_gate, m_q_norm_b, m_k_norm_b, m_rpb_c, m_w_branch_a, m_w_branch_b, m_w_branch_c, m_w_out, m_ln1_g, m_ln1_b, m_w_up, m_w_down, m_ln2_g, m_ln2_b, v_w_in, v_b_gate, v_q_norm_b, v_k_norm_b, v_rpb_c, v_w_branch_a, v_w_branch_b, v_w_branch_c, v_w_out, v_ln1_g, v_ln1_b, v_w_up, v_w_down, v_ln2_g, v_ln2_b):
    w = dict(w_in=w_in, b_gate=b_gate, q_norm_b=q_norm_b, k_norm_b=k_norm_b, rpb_c=rpb_c,
             w_branch_a=w_branch_a, w_branch_b=w_branch_b, w_branch_c=w_branch_c, w_out=w_out,
             ln1_g=ln1_g, ln1_b=ln1_b, w_up=w_up, w_down=w_down, ln2_g=ln2_g, ln2_b=ln2_b)
    m = dict(w_in=m_w_in, b_gate=m_b_gate, q_norm_b=m_q_norm_b, k_norm_b=m_k_norm_b, rpb_c=m_rpb_c,
             w_branch_a=m_w_branch_a, w_branch_b=m_w_branch_b, w_branch_c=m_w_branch_c, w_out=m_w_out,
             ln1_g=m_ln1_g, ln1_b=m_ln1_b, w_up=m_w_up, w_down=m_w_down, ln2_g=m_ln2_g, ln2_b=m_ln2_b)
    v = dict(w_in=v_w_in, b_gate=v_b_gate, q_norm_b=v_q_norm_b, k_norm_b=v_k_norm_b, rpb_c=v_rpb_c,
             w_branch_a=v_w_branch_a, w_branch_b=v_w_branch_b, w_branch_c=v_w_branch_c, w_out=v_w_out,
             ln1_g=v_ln1_g, ln1_b=v_ln1_b, w_up=v_w_up, w_down=v_w_down, ln2_g=v_ln2_g, ln2_b=v_ln2_b)
    T = x.shape[1]
    xc, yc, cc = _position()

    flat2 = lambda a: a.reshape(-1, a.shape[-1])
    gathered = _all_gather([flat2(w[n]).astype(BF) for n in BIG], "gather_weights")
    gathered = dict(zip(BIG, gathered))

    tabs = _rope_tables(T)
    gm = _group_mean_matrix()

    Ws, Ps = [], []
    for l in range(DEPTH):
        W = {n: _full_weight(gathered[n], n, l) for n in BIG}
        W["w_qkv"], W["w_gate"] = W["w_in"][:, :QKV], W["w_in"][:, QKV:]
        Ws.append(W)
        Ps.append(dict(qn=jnp.tile(q_norm_b[l][None], (1, 2)), kn=jnp.tile(k_norm_b[l][None], (1, 2)),
                       rpb=rpb_c[l], bg=b_gate[l][None], ln1_g=ln1_g[l][None], ln1_b=ln1_b[l][None],
                       ln2_g=ln2_g[l][None], ln2_b=ln2_b[l][None]))

    h = x[0]
    hb = h.astype(BF)
    saved = []
    for l in range(DEPTH):
        h, hb, S = _layer_fwd(h, hb, Ws[l], Ps[l], tabs, gm)
        saved.append(S)
    dy, sq = _loss_grad(h, loss_target[0])
    loss = lax.psum(0.5 / D * jnp.sum(sq), AXES)
    grads = [None] * DEPTH
    for l in reversed(range(DEPTH)):
        dy, grads[l] = _layer_bwd(dy, saved[l], Ws[l], Ps[l], tabs, gm)
    grad_x = dy[None]

    core = cc.reshape(1).astype(jnp.int32)
    chip = (2 * xc + yc).reshape(1).astype(jnp.int32)
    chunked = []
    for n in BIG:
        g = jnp.stack([_chunks(grads[l][n], n) for l in range(DEPTH)], axis=1)
        chunked.append(g.reshape(4, 2, DEPTH * g.shape[2], g.shape[3]).astype(BF))
    from_sibling = _exchange_pair(chunked, "grads_to_sibling")
    pair = [_pair_sum(g, r, core) for g, r in zip(chunked, from_sibling)]
    from_chips = _exchange_chips(pair, "grads_to_chips")
    out_g, out_d, out_m, out_v = {}, {}, {}, {}
    for n, p, r in zip(BIG, pair, from_chips):
        res = _chip_sum_adamw(p, r, chip, flat2(w[n]), flat2(m[n]), flat2(v[n]))
        out_g[n], out_d[n], out_m[n], out_v[n] = [t.reshape(w[n].shape) for t in res]

    part = _pack_small({n: jnp.stack([grads[l][n].reshape(w[n].shape[1:]) for l in range(DEPTH)]) for n in SMALL})
    parts, = _all_gather([part], "gather_small_grads")
    res = _small_sum_adamw(parts, _pack_small(w), _pack_small(m), _pack_small(v))
    for dst, packed in zip((out_g, out_d, out_m, out_v), res):
        dst.update(_unpack_small(packed, w))

    return (loss, grad_x, *[out_g[n] for n in NAMES], *[out_d[n] for n in NAMES],
            *[out_m[n] for n in NAMES], *[out_v[n] for n in NAMES])
```

```python
import functools
import math

import numpy as np
import jax
import jax.numpy as jnp
from jax import lax
from jax.experimental import pallas as pl
from jax.experimental.pallas import tpu as pltpu

F32 = jnp.float32
BF = jnp.bfloat16
HI = lax.Precision.HIGHEST
NEG = -1e30
MESH = pl.DeviceIdType.MESH
AXES = ("x", "y", "c")

D = 1024
DEPTH = 4
HD = 64
A_W, BQ_W, BKV_W, C_W = 256, 512, 128, 256
QKV = 2304
GATE = 3072
D_FF = 4096
GRID_W = 64
ALPHA = (2 * DEPTH) ** 0.25
LN_EPS = 1e-5
RMS_EPS = 1e-6
SCALE = HD ** -0.5
ROPE_THETA = 500000.0
AXIAL_THETA = 10000.0
A_CONFIGS = ((128, 1), (512, 4), (2048, 16))
LR, B1, B2, EPS, WD, STEP = 0.001, 0.9, 0.999, 1e-08, 0.01, 10

VMEM_LIMIT = 56 * 1024 * 1024
BAND_T = 256


def _pcall(body, **kw):
    return pl.pallas_call(body, **kw)


def _pcall_comm(body, **kw):
    return pl.pallas_call(body, **kw)


def _cp(*sem):
    return pltpu.CompilerParams(dimension_semantics=sem, vmem_limit_bytes=VMEM_LIMIT)


def _sds(shape, dtype):
    return jax.ShapeDtypeStruct(shape, dtype)


def _mm(a, b, dims, outs, tm, tn, tk, epilogue=None, extras=(), name="mm"):
    if dims == "tn":
        K, M = a.shape
    else:
        M, K = a.shape
    N = b.shape[0] if dims == "nt" else b.shape[1]
    tm, tn, tk = min(tm, M), min(tn, N), min(tk, K)
    assert M % tm == 0 and N % tn == 0 and K % tk == 0, (name, M, N, K, tm, tn, tk)
    nk = K // tk
    ne, no = len(extras), len(outs)

    def body(a_ref, b_ref, *rest):
        extra_refs, out_refs = rest[:ne], rest[ne:ne + no]
        av, bv = a_ref[...].astype(BF), b_ref[...].astype(BF)
        if dims == "nn":
            p = jnp.dot(av, bv, preferred_element_type=F32)
        elif dims == "nt":
            p = lax.dot_general(av, bv, (((1,), (1,)), ((), ())), preferred_element_type=F32)
        else:
            p = lax.dot_general(av, bv, (((0,), (0,)), ((), ())), preferred_element_type=F32)

        def finish(acc):
            res = epilogue(acc, *[r[...] for r in extra_refs]) if epilogue else (acc,)
            for o, r in zip(out_refs, res):
                o[...] = r.astype(o.dtype)

        if nk == 1:
            finish(p)
        else:
            acc_ref = rest[-1]
            k = pl.program_id(2)

            @pl.when(k == 0)
            def _():
                acc_ref[...] = p

            @pl.when(k > 0)
            def _():
                acc_ref[...] += p

            @pl.when(k == nk - 1)
            def _():
                finish(acc_ref[...])

    if dims == "tn":
        a_spec = pl.BlockSpec((tk, tm), lambda i, j, k: (k, i))
    else:
        a_spec = pl.BlockSpec((tm, tk), lambda i, j, k: (i, k))
    if dims == "nt":
        b_spec = pl.BlockSpec((tn, tk), lambda i, j, k: (j, k))
    else:
        b_spec = pl.BlockSpec((tk, tn), lambda i, j, k: (k, j))
    o_spec = pl.BlockSpec((tm, tn), lambda i, j, k: (i, j))
    res = _pcall(
        body, name=name, grid=(M // tm, N // tn, nk),
        in_specs=[a_spec, b_spec] + [o_spec] * ne,
        out_specs=[o_spec] * no,
        out_shape=[_sds((M, N), dt) for dt in outs],
        scratch_shapes=[pltpu.VMEM((tm, tn), F32)] if nk > 1 else [],
        compiler_params=_cp("parallel", "parallel", "arbitrary"),
    )(a, b, *extras)
    return res


def _rope_tables(T):
    pos = jnp.arange(T)

    def cs(p, theta, half):
        inv = theta ** (-jnp.arange(half, dtype=F32) / half)
        ang = p.astype(F32)[:, None] * inv[None, :]
        return jnp.cos(ang), jnp.sin(ang)

    ca, sa = cs(pos, ROPE_THETA, 8)
    one, zero, z8 = jnp.ones((T, 48), F32), jnp.zeros((T, 48), F32), jnp.zeros((T, 8), F32)
    tab_a = [jnp.concatenate(t, 1) for t in ([ca, ca, one], [-sa, z8, zero], [z8, sa, zero])]
    cr, sr = cs(pos // GRID_W, AXIAL_THETA, 16)
    cc, sc = cs(pos % GRID_W, AXIAL_THETA, 16)
    z16 = jnp.zeros((T, 16), F32)
    tab_b = [jnp.concatenate(t, 1) for t in ([cr, cr, cc, cc], [-sr, z16, -sc, z16], [z16, sr, z16, sc])]
    return [jnp.tile(t, (1, 2)) for t in tab_a], [jnp.tile(t, (1, 2)) for t in tab_b]


def _rot(x, C, S1, S2, k):
    return x * C + pltpu.roll(x, 128 - k, 1) * S1 + pltpu.roll(x, k, 1) * S2


def _rot_t(d, C, S1, S2, k):
    return d * C + pltpu.roll(d * S1, k, 1) + pltpu.roll(d * S2, 128 - k, 1)


def _group_mean_matrix():
    m = np.zeros((128, 128), np.float32)
    m[:64, :64] = 1.0 / 64
    m[64:, 64:] = 1.0 / 64
    return jnp.asarray(m)


def _prep_fwd(hq, tab_a, tab_b, qn, kn, gm):
    T = hq.shape[0]
    tt = min(256, T)
    widths = [A_W, A_W, A_W, BQ_W, BKV_W, BKV_W, C_W, C_W, C_W]

    def body(h_ref, ca, s1a, s2a, cb, s1b, s2b, qn_ref, kn_ref, gm_ref,
             qa, ka, va, qb, kb, vb, qc, kc, vc):
        def col(off, j):
            return h_ref[:, off + 128 * j: off + 128 * (j + 1)]

        for j in range(2):
            sl = slice(128 * j, 128 * (j + 1))
            qa[:, sl] = (_rot(col(0, j), ca[...], s1a[...], s2a[...], 8) * SCALE).astype(qa.dtype)
            ka[:, sl] = _rot(col(256, j), ca[...], s1a[...], s2a[...], 8).astype(ka.dtype)
            va[:, sl] = col(512, j).astype(va.dtype)
            qc[:, sl] = (col(1536, j) * SCALE).astype(qc.dtype)
            kc[:, sl] = col(1792, j).astype(kc.dtype)
            vc[:, sl] = col(2048, j).astype(vc.dtype)

        def normed(x, w):
            ms = jnp.dot(x * x, gm_ref[...], precision=HI, preferred_element_type=F32)
            return x * lax.rsqrt(ms + RMS_EPS) * w

        for j in range(4):
            y = normed(col(768, j), qn_ref[...])
            qb[:, 128 * j:128 * (j + 1)] = (_rot(y, cb[...], s1b[...], s2b[...], 16) * SCALE).astype(qb.dtype)
        y = normed(col(1280, 0), kn_ref[...])
        kb[...] = _rot(y, cb[...], s1b[...], s2b[...], 16).astype(kb.dtype)
        vb[...] = col(1408, 0).astype(vb.dtype)

    row = lambda w: pl.BlockSpec((tt, w), lambda i: (i, 0))
    const = lambda s: pl.BlockSpec(s, lambda i: (0, 0))
    return _pcall(
        body, name="prep_fwd", grid=(T // tt,),
        in_specs=[row(QKV)] + [row(128)] * 6 + [const((1, 128))] * 2 + [const((128, 128))],
        out_specs=[row(w) for w in widths],
        out_shape=[_sds((T, w), BF) for w in widths],
        compiler_params=_cp("parallel"),
    )(hq, *tab_a, *tab_b, qn, kn, gm)


def _prep_bwd(hq, grads, tab_a, tab_b, qn, kn, gm):
    T = hq.shape[0]
    tt = min(256, T)
    widths = [A_W, A_W, A_W, BQ_W, BKV_W, BKV_W, C_W, C_W, C_W]

    def body(h_ref, dqa, dka, dva, dqb, dkb, dvb, dqc, dkc, dvc,
             ca, s1a, s2a, cb, s1b, s2b, qn_ref, kn_ref, gm_ref, dh, dqn, dkn):
        i = pl.program_id(0)

        @pl.when(i == 0)
        def _():
            dqn[...] = jnp.zeros_like(dqn)
            dkn[...] = jnp.zeros_like(dkn)

        def put(off, j, val):
            dh[:, off + 128 * j: off + 128 * (j + 1)] = val.astype(dh.dtype)

        for j in range(2):
            sl = slice(128 * j, 128 * (j + 1))
            put(0, j, _rot_t(dqa[:, sl] * SCALE, ca[...], s1a[...], s2a[...], 8))
            put(256, j, _rot_t(dka[:, sl], ca[...], s1a[...], s2a[...], 8))
            put(512, j, dva[:, sl])
            put(1536, j, dqc[:, sl] * SCALE)
            put(1792, j, dkc[:, sl])
            put(2048, j, dvc[:, sl])

        def norm_bwd(x, w, e):
            ms = jnp.dot(x * x, gm_ref[...], precision=HI, preferred_element_type=F32)
            r = lax.rsqrt(ms + RMS_EPS)
            n = x * r
            dn = e * w
            proj = jnp.dot(dn * n, gm_ref[...], precision=HI, preferred_element_type=F32)
            return r * (dn - n * proj), jnp.sum(e * n, axis=0, keepdims=True)

        for j in range(4):
            sl = slice(128 * j, 128 * (j + 1))
            e = _rot_t(dqb[:, sl] * SCALE, cb[...], s1b[...], s2b[...], 16)
            dx, dw = norm_bwd(h_ref[:, 768 + 128 * j: 768 + 128 * (j + 1)], qn_ref[...], e)
            put(768, j, dx)
            dqn[:, sl] += dw
        e = _rot_t(dkb[...], cb[...], s1b[...], s2b[...], 16)
        dx, dw = norm_bwd(h_ref[:, 1280:1408], kn_ref[...], e)
        put(1280, 0, dx)
        dkn[...] += dw
        put(1408, 0, dvb[...])

    row = lambda w: pl.BlockSpec((tt, w), lambda i: (i, 0))
    const = lambda s: pl.BlockSpec(s, lambda i: (0, 0))
    return _pcall(
        body, name="prep_bwd", grid=(T // tt,),
        in_specs=[row(QKV)] + [row(w) for w in widths] + [row(128)] * 6
        + [const((1, 128))] * 2 + [const((128, 128))],
        out_specs=[row(QKV), const((1, BQ_W)), const((1, BKV_W))],
        out_shape=[_sds((T, QKV), BF), _sds((1, BQ_W), F32), _sds((1, BKV_W), F32)],
        compiler_params=_cp("arbitrary"),
    )(hq, *grads, *tab_a, *tab_b, qn, kn, gm)


def _to_heads(x):
    T, W = x.shape
    return x.reshape(T, W // HD, HD).transpose(1, 0, 2)


def _from_heads(x):
    H, T, _ = x.shape
    return x.transpose(1, 0, 2).reshape(T, H * HD)


def _rows(x, t):
    H, T, _ = x.shape
    return x.reshape(H, T // t, 1, t)


def _nt(a, b):
    return lax.dot_general(a, b, (((1,), (1,)), ((), ())), preferred_element_type=F32)


def _chunked_t(x, t):
    T, W = x.shape
    return x.reshape(T // t, t, W // HD, HD).transpose(2, 0, 3, 1)


def _unchunk_t(x):
    H, n, _, t = x.shape
    return x.transpose(1, 3, 0, 2).reshape(n * t, H * HD)


FULL_T = 512
FULL_TK = 512
FULL_HEADS = 1
V_ROWS = 72


def _attn_full_fwd(qT, k, vT1):
    Hq, nq, _, t = qT.shape
    Hk, T, _ = k.shape
    G = Hq // Hk
    nk, tk = vT1.shape[1], vT1.shape[3]

    HB = FULL_HEADS
    assert G % HB == 0

    def body(q_ref, k_ref, v_ref, o_ref, lse_ref, *acc_refs):
        for acc_ref in acc_refs:
            acc_ref[...] = jnp.zeros((V_ROWS, t), F32)

        def scores(j, b):
            sT = jnp.dot(k_ref[pl.ds(pl.multiple_of(j * tk, tk), tk), :], q_ref[b], preferred_element_type=F32)
            return sT, jnp.max(sT, axis=0, keepdims=True)

        def update(j, b, scored, m_old):
            sT, m_tile = scored
            m_new = jnp.maximum(m_old, m_tile)
            pT = jnp.exp(sT - m_new).astype(BF)
            acc_refs[b][...] = (jnp.exp(m_old - m_new) * acc_refs[b][...]
                                + jnp.dot(v_ref[j], pT, preferred_element_type=F32))
            return m_new

        def step(j, carry):
            ms, ss = carry
            nxt = jnp.minimum(j + 1, nk - 1)
            new_s = tuple(scores(nxt, b) for b in range(HB))
            new_m = tuple(update(j, b, ss[b], ms[b]) for b in range(HB))
            return new_m, new_s

        init = (tuple(jnp.full((1, t), NEG, F32) for _ in range(HB)), tuple(scores(0, b) for b in range(HB)))
        ms, _ = lax.fori_loop(0, nk, step, init)
        for b in range(HB):
            l = acc_refs[b][pl.ds(HD, 1), :]
            o_ref[b] = (acc_refs[b][pl.ds(0, HD), :] / l).astype(o_ref.dtype)
            lse_ref[b] = ms[b] + jnp.log(l)

    qs = pl.BlockSpec((HB, None, HD, t), lambda h, i: (h, i, 0, 0))
    return _pcall(
        body, name="attn_full_fwd", grid=(Hq // HB, nq),
        in_specs=[qs, pl.BlockSpec((None, T, HD), lambda h, i: (h * HB // G, 0, 0)),
                  pl.BlockSpec((None, nk, V_ROWS, tk), lambda h, i: (h * HB // G, 0, 0, 0))],
        out_specs=[qs, pl.BlockSpec((HB, None, 1, t), lambda h, i: (h, i, 0, 0))],
        out_shape=[_sds((Hq, nq, HD, t), BF), _sds((Hq, nq, 1, t), F32)],
        scratch_shapes=[pltpu.VMEM((V_ROWS, t), F32)] * HB,
        compiler_params=_cp("parallel", "parallel"),
    )(qT, k, vT1)


def _attn_delta(doT, oT):
    Hq, nq, _, t = doT.shape

    def body(do_ref, o_ref, dl_ref):
        dl_ref[...] = jnp.sum(do_ref[...].astype(F32) * o_ref[...].astype(F32), axis=1, keepdims=True)

    qs = pl.BlockSpec((None, nq, HD, t), lambda h: (h, 0, 0, 0))
    rs = pl.BlockSpec((None, nq, 1, t), lambda h: (h, 0, 0, 0))
    return _pcall(body, name="attn_delta", grid=(Hq,), in_specs=[qs, qs], out_specs=rs,
                  out_shape=_sds((Hq, nq, 1, t), F32), compiler_params=_cp("parallel"))(doT, oT)


def _attn_full_bwd(qT, k, kT, v, doT, lse, delta):
    Hq, nq, _, t = qT.shape
    Hk, T, _ = k.shape
    G = Hq // Hk
    nkv = T // t

    def body(qT_ref, doT_ref, lse_ref, dl_ref, k_ref, kT_ref, v_ref, dq_ref, dk_ref, dv_ref, dq_acc, dk_acc, dv_acc):
        j = pl.program_id(1)

        @pl.when(j == 0)
        def _():
            dq_acc[...] = jnp.zeros_like(dq_acc)

        kv, kTv, vv = k_ref[...], kT_ref[...], v_ref[...]
        dk_acc[...] = jnp.zeros((t, HD), F32)
        dv_acc[...] = jnp.zeros((t, HD), F32)

        def step(i, carry):
            qT, doT = qT_ref[i], doT_ref[i]
            pT = jnp.exp(jnp.dot(kv, qT, preferred_element_type=F32) - lse_ref[i])
            dv_acc[...] += _nt(pT.astype(BF), doT)
            dsT = (pT * (jnp.dot(vv, doT, preferred_element_type=F32) - dl_ref[i])).astype(BF)
            dk_acc[...] += _nt(dsT, qT)
            dq_acc[i] += jnp.dot(kTv, dsT, preferred_element_type=F32)
            return carry

        lax.fori_loop(0, nq, step, 0)
        dk_ref[...] = dk_acc[...].astype(dk_ref.dtype)
        dv_ref[...] = dv_acc[...].astype(dv_ref.dtype)

        @pl.when(j == nkv - 1)
        def _():
            dq_ref[...] = dq_acc[...].astype(dq_ref.dtype)

    chk = pl.BlockSpec((None, nq, HD, t), lambda h, j: (h, 0, 0, 0))
    row = pl.BlockSpec((None, nq, 1, t), lambda h, j: (h, 0, 0, 0))
    kvs = pl.BlockSpec((None, t, HD), lambda h, j: (h // G, j, 0))
    out = pl.BlockSpec((None, t, HD), lambda h, j: (h, j, 0))
    return _pcall(
        body, name="attn_full_bwd", grid=(Hq, nkv),
        in_specs=[chk, chk, row, row, kvs, pl.BlockSpec((None, HD, t), lambda h, j: (h // G, 0, j)), kvs],
        out_specs=[chk, out, out],
        out_shape=[_sds((Hq, nq, HD, t), BF), _sds((Hq, T, HD), BF), _sds((Hq, T, HD), BF)],
        scratch_shapes=[pltpu.VMEM((nq, HD, t), F32), pltpu.VMEM((t, HD), F32), pltpu.VMEM((t, HD), F32)],
        compiler_params=_cp("parallel", "arbitrary"),
    )(qT, doT, lse, delta, k, kT, v)


DIL_Q = 128
DIL_K = 256
DIL_R = 64


def _vh(x, d):
    T = x.shape[0]
    return x.reshape(T // d, d, 4, HD).transpose(1, 2, 0, 3).reshape(4 * d, T // d, HD)


def _vh_inv(y, d):
    L = y.shape[1]
    return y.reshape(d, 4, L, HD).transpose(2, 0, 1, 3).reshape(L * d, 4 * HD)


def _vh_chunks(x, d):
    y = _vh(x, d)
    return y.reshape(y.shape[0], y.shape[1] // DIL_Q, DIL_Q, HD).transpose(0, 1, 3, 2)


def _chunks_to_dims(c, d):
    _, nq, R, _ = c.shape
    return c.reshape(d, 4, nq, R, DIL_Q).transpose(1, 3, 2, 4, 0).reshape(4, R, nq * DIL_Q * d)


def _dims_to_chunks(x, d):
    _, R, T = x.shape
    nq = T // (DIL_Q * d)
    return x.reshape(4, R, nq, DIL_Q, d).transpose(4, 0, 2, 1, 3).reshape(4 * d, nq, R, DIL_Q)


def _chunks_to_tokens(c, d):
    _, nq, _, _ = c.shape
    return c.reshape(d, 4, nq, HD, DIL_Q).transpose(2, 4, 0, 1, 3).reshape(nq * DIL_Q * d, 4 * HD)


def _dil_window(i, L):
    start = pl.multiple_of(jnp.clip(i * DIL_Q - DIL_R, 0, L - DIL_K), DIL_R)
    kk = start + lax.broadcasted_iota(jnp.int32, (DIL_K, 1), 0)
    qq = i * DIL_Q + lax.broadcasted_iota(jnp.int32, (1, DIL_Q), 1)
    return start, jnp.abs(kk - qq) <= DIL_R


def _tn(a, b):
    return lax.dot_general(a, b, (((0,), (0,)), ((), ())), preferred_element_type=F32)


def _dil_fwd(qT, k, v):
    V, nq, _, _ = qT.shape
    L = k.shape[1]
    assert L >= DIL_K

    unroll = 4 if nq % 4 == 0 else 1

    def body(q_ref, k_ref, v_ref, o_ref, lse_ref):
        def tile(i):
            start, mask = _dil_window(i, L)
            win = pl.ds(start, DIL_K)
            sT = jnp.where(mask, jnp.dot(k_ref[win, :], q_ref[i], preferred_element_type=F32), NEG)
            m = jnp.max(sT, axis=0, keepdims=True)
            pT = jnp.exp(sT - m).astype(BF)
            acc = _tn(v_ref[win, :], pT)
            l = jnp.max(acc[HD:HD + 8], axis=0, keepdims=True)
            o_ref[i] = acc[:HD] / l
            lse_ref[i] = m + jnp.log(l)

        def tiles(ii, carry):
            for u in range(unroll):
                tile(ii * unroll + u)
            return carry

        lax.fori_loop(0, nq // unroll, tiles, 0)

    chk = lambda r, dt: (pl.BlockSpec((None, nq, r, DIL_Q), lambda h: (h, 0, 0, 0)), _sds((V, nq, r, DIL_Q), dt))
    tok = lambda w: pl.BlockSpec((None, L, w), lambda h: (h, 0, 0))
    (o_spec, o_shape), (l_spec, l_shape) = chk(HD, F32), chk(1, F32)
    return _pcall(
        body, name=f"dil_fwd_{V // 4}", grid=(V,),
        in_specs=[chk(HD, BF)[0], tok(HD), tok(2 * HD)],
        out_specs=[o_spec, l_spec], out_shape=[o_shape, l_shape],
        compiler_params=_cp("parallel"),
    )(qT, k, v)


def _dil_merge(os_, lses):
    _, _, T = os_[0].shape
    tt = min(1024, T)
    n = len(os_)

    def body(*refs):
        o_refs, l_refs, (o_out, l_out) = refs[:n], refs[n:2 * n], refs[2 * n:]
        m = l_refs[0][...]
        for r in l_refs[1:]:
            m = jnp.maximum(m, r[...])
        ws = [jnp.exp(r[...] - m) for r in l_refs]
        tot = ws[0]
        for w_ in ws[1:]:
            tot = tot + w_
        acc = ws[0] * o_refs[0][...]
        for w_, o in zip(ws[1:], o_refs[1:]):
            acc = acc + w_ * o[...]
        o_out[...] = (acc / tot).astype(o_out.dtype)
        l_out[...] = m + jnp.log(tot)

    os_spec = pl.BlockSpec((None, HD, tt), lambda h, i: (h, 0, i))
    ls_spec = pl.BlockSpec((None, 1, tt), lambda h, i: (h, 0, i))
    return _pcall(
        body, name="dil_merge", grid=(4, T // tt),
        in_specs=[os_spec] * n + [ls_spec] * n, out_specs=[os_spec, ls_spec],
        out_shape=[_sds((4, HD, T), BF), _sds((4, 1, T), F32)],
        compiler_params=_cp("parallel", "parallel"),
    )(*os_, *lses)


def _dims_delta(doT, oT):
    _, _, T = doT.shape
    tt = min(2048, T)

    def body(do_ref, o_ref, dl_ref):
        dl_ref[...] = jnp.sum(do_ref[...].astype(F32) * o_ref[...].astype(F32), axis=0, keepdims=True)

    spec = pl.BlockSpec((None, HD, tt), lambda h, i: (h, 0, i))
    return _pcall(body, name="dims_delta", grid=(4, T // tt), in_specs=[spec, spec],
                  out_specs=pl.BlockSpec((None, 1, tt), lambda h, i: (h, 0, i)),
                  out_shape=_sds((4, 1, T), F32), compiler_params=_cp("parallel", "parallel"))(doT, oT)


def _dil_bwd(qT, k, v, doT, lse, delta):
    V, nq, _, _ = qT.shape
    L = k.shape[1]
    unroll = 2 if nq % 2 == 0 else 1

    def body(qT_ref, k_ref, v_ref, doT_ref, lse_ref, dl_ref, dq_ref, dk_ref, dv_ref, dk_acc, dv_acc):
        dk_acc[...] = jnp.zeros_like(dk_acc)
        dv_acc[...] = jnp.zeros_like(dv_acc)

        def tile(i):
            start, mask = _dil_window(i, L)
            win = pl.ds(start, DIL_K)
            kw, qT, doT = k_ref[win, :], qT_ref[i], doT_ref[i]
            sT = jnp.where(mask, jnp.dot(kw, qT, preferred_element_type=F32), NEG)
            pT = jnp.exp(sT - lse_ref[i])
            dv = _nt(pT.astype(BF), doT)
            dsT = (pT * (jnp.dot(v_ref[win, :], doT, preferred_element_type=F32) - dl_ref[i])).astype(BF)
            dk = _nt(dsT, qT)
            dq_ref[i] = _tn(kw, dsT).astype(dq_ref.dtype)
            return win, dk, dv

        def tiles(ii, carry):
            done = [tile(ii * unroll + u) for u in range(unroll)]
            for win, dk, dv in done:
                dk_acc[win, :] += dk
                dv_acc[win, :] += dv
            return carry

        lax.fori_loop(0, nq // unroll, tiles, 0)
        dk_ref[...] = dk_acc[...].astype(dk_ref.dtype)
        dv_ref[...] = dv_acc[...].astype(dv_ref.dtype)

    chk = lambda r: pl.BlockSpec((None, nq, r, DIL_Q), lambda h: (h, 0, 0, 0))
    tok = pl.BlockSpec((None, L, HD), lambda h: (h, 0, 0))
    return _pcall(
        body, name=f"dil_bwd_{V // 4}", grid=(V,),
        in_specs=[chk(HD), tok, tok, chk(HD), chk(1), chk(1)],
        out_specs=[chk(HD), tok, tok],
        out_shape=[_sds((V, nq, HD, DIL_Q), BF), _sds((V, L, HD), BF), _sds((V, L, HD), BF)],
        scratch_shapes=[pltpu.VMEM((L, HD), F32)] * 2,
        compiler_params=_cp("parallel"),
    )(qT, k, v, doT, lse, delta)


def _dilated_fwd(qa, ka, va):
    outs, lses, saved = [], [], []
    for _, d in A_CONFIGS:
        qT, k, v = _vh_chunks(qa, d), _vh(ka, d), _vh(va, d)
        oT, lse = _dil_fwd(qT, k, jnp.concatenate([v, jnp.ones_like(v)], axis=2))
        outs.append(_chunks_to_dims(oT, d))
        lses.append(_chunks_to_dims(lse, d))
        saved.append((qT, k, v))
    o, lse = _dil_merge(outs, lses)
    return o, lse, saved


def _dilated_bwd(do_t, o, lse, saved):
    doT = do_t.reshape(do_t.shape[0], 4, HD).transpose(1, 2, 0)
    delta = _dims_delta(doT, o)
    parts = []
    for (_, d), (qT, k, v) in zip(A_CONFIGS, saved):
        dqT, dk_c, dv_c = _dil_bwd(qT, k, v, _dims_to_chunks(doT, d), _dims_to_chunks(lse, d), _dims_to_chunks(delta, d))
        parts.append((_chunks_to_tokens(dqT, d), _vh_inv(dk_c, d), _vh_inv(dv_c, d)))
    total = lambda ts: sum(t.astype(F32) for t in ts).astype(do_t.dtype)
    return tuple(total(ts) for ts in zip(*parts))


NBR_Q = 128
NBR_ROWS = 10
NBR_K = NBR_ROWS * GRID_W


def _nbr_class_tiles(nq):
    return [2, 0, 1, nq - 2, nq - 1]


def _nbr_geometry(T):
    rows, nq = T // GRID_W, T // NBR_Q
    assert rows >= NBR_ROWS + 4 and nq >= 5
    kr, kc = np.divmod(np.arange(NBR_K), GRID_W)
    qr, qc = np.divmod(np.arange(NBR_Q), GRID_W)
    c0 = np.clip(qc - 8, 0, GRID_W - 16)
    col_ok = (kc[:, None] >= c0[None, :]) & (kc[:, None] < c0[None, :] + 16)
    drs, valids = [], []
    for i in _nbr_class_tiles(nq):
        start_row = int(np.clip(2 * i - 4, 0, rows - NBR_ROWS))
        r = 2 * i + qr
        r0 = np.clip(r - 4, 0, rows - 8)
        rk = start_row + kr
        row_ok = (rk[:, None] >= r0[None, :]) & (rk[:, None] < r0[None, :] + 8)
        valids.append(row_ok & col_ok)
        dr = start_row + np.arange(NBR_ROWS)[:, None] - (2 * i + np.arange(2)[None, :]) + 7
        drs.append(np.where((dr >= 0) & (dr <= 14), dr, -1))
    return np.stack(drs), np.stack(valids)


def _nbr_fold_matrices(T):
    dr, _ = _nbr_geometry(T)
    e1 = np.zeros((GRID_W * GRID_W, 128), np.float32)
    kc, qc = np.meshgrid(np.arange(GRID_W), np.arange(GRID_W), indexing="ij")
    dc = (kc - qc + 15).reshape(-1)
    keep = (dc >= 0) & (dc <= 30)
    e1[np.arange(GRID_W * GRID_W)[keep], dc[keep]] = 1.0
    n = 5 * NBR_ROWS * 2
    e2 = np.zeros((64, 4 * n), np.float32)
    for h in range(4):
        for j, d in enumerate(dr.reshape(-1)):
            if d >= 0:
                e2[h * 16 + d, h * n + j] = 1.0
    return jnp.asarray(e1), jnp.asarray(e2)


def _nbr_bias_blocks(rpb, T):
    e1, e2 = _nbr_fold_matrices(T)
    _, valid = _nbr_geometry(T)
    padded = jnp.pad(rpb, ((0, 0), (0, 1), (0, 128 - rpb.shape[2]))).reshape(64, 128)

    def body(e2t_ref, rpb_ref, e1t_ref, out_ref):
        picked = jnp.dot(e2t_ref[...], rpb_ref[...], precision=HI, preferred_element_type=F32)
        out_ref[...] = jnp.dot(picked, e1t_ref[...], precision=HI, preferred_element_type=F32)

    sub = _pcall(body, name="rpb_expand", out_shape=_sds((e2.shape[1], GRID_W * GRID_W), F32),
                 compiler_params=pltpu.CompilerParams(vmem_limit_bytes=VMEM_LIMIT))(e2.T, padded, e1.T)
    blocks = sub.reshape(4, 5, NBR_ROWS, 2, GRID_W, GRID_W).transpose(0, 1, 2, 4, 3, 5).reshape(4, 5, NBR_K, NBR_Q)
    return jnp.where(valid[None], blocks, NEG)


def _nbr_rpb_grad(dbias, T):
    e1, e2 = _nbr_fold_matrices(T)
    sub = dbias.reshape(4, 5, NBR_ROWS, GRID_W, 2, GRID_W).transpose(0, 1, 2, 4, 3, 5).reshape(-1, GRID_W * GRID_W)

    def body(e2_ref, sub_ref, e1_ref, out_ref):
        diag = jnp.dot(sub_ref[...], e1_ref[...], precision=HI, preferred_element_type=F32)
        out_ref[...] = jnp.dot(e2_ref[...], diag, precision=HI, preferred_element_type=F32)

    out = _pcall(body, name="rpb_fold", out_shape=_sds((64, 128), F32),
                 compiler_params=pltpu.CompilerParams(vmem_limit_bytes=VMEM_LIMIT))(e2, sub, e1)
    return out.reshape(4, 16, 128)[:, :15, :31]


def _nbr_tile(i, nq, T):
    start = pl.multiple_of(jnp.clip(i * NBR_Q - 4 * GRID_W, 0, T - NBR_K), NBR_Q)
    cls = jnp.where(i == 0, 1, jnp.where(i == 1, 2, jnp.where(i == nq - 2, 3, jnp.where(i == nq - 1, 4, 0))))
    return start, cls


def _nbr_fwd(qT, k, v, bias):
    H, nq, _, _ = qT.shape
    T = k.shape[1]
    unroll = 2 if nq % 2 == 0 else 1

    def body(q_ref, k_ref, v_ref, b_ref, o_ref, lse_ref):
        def tile(i):
            start, cls = _nbr_tile(i, nq, T)
            win = pl.ds(start, NBR_K)
            sT = jnp.dot(k_ref[win, :], q_ref[i], preferred_element_type=F32) + b_ref[cls]
            m = jnp.max(sT, axis=0, keepdims=True)
            pT = jnp.exp(sT - m).astype(BF)
            acc = _tn(v_ref[win, :], pT)
            l = jnp.max(acc[HD:HD + 8], axis=0, keepdims=True)
            o_ref[i] = (acc[:HD] / l).astype(o_ref.dtype)
            lse_ref[i] = m + jnp.log(l)

        def tiles(ii, carry):
            for u in range(unroll):
                tile(ii * unroll + u)
            return carry

        lax.fori_loop(0, nq // unroll, tiles, 0)

    chk = lambda r: pl.BlockSpec((None, nq, r, NBR_Q), lambda h: (h, 0, 0, 0))
    tok = lambda w: pl.BlockSpec((None, T, w), lambda h: (h, 0, 0))
    return _pcall(
        body, name="nbr_fwd", grid=(H,),
        in_specs=[chk(HD), tok(HD), tok(2 * HD), pl.BlockSpec((None, 5, NBR_K, NBR_Q), lambda h: (h, 0, 0, 0))],
        out_specs=[chk(HD), chk(1)],
        out_shape=[_sds((H, nq, HD, NBR_Q), BF), _sds((H, nq, 1, NBR_Q), F32)],
        compiler_params=_cp("parallel"),
    )(qT, k, v, bias)


def _nbr_bwd(qT, k, v, doT, lse, delta, bias):
    H, nq, _, _ = qT.shape
    T = k.shape[1]

    def body(qT_ref, k_ref, v_ref, doT_ref, lse_ref, dl_ref, b_ref, dq_ref, dk_ref, dv_ref, db_ref, dk_acc, dv_acc):
        dk_acc[...] = jnp.zeros_like(dk_acc)
        dv_acc[...] = jnp.zeros_like(dv_acc)
        db_ref[...] = jnp.zeros_like(db_ref)

        def tile(i, carry):
            start, cls = _nbr_tile(i, nq, T)
            win = pl.ds(start, NBR_K)
            kw, qT, doT = k_ref[win, :], qT_ref[i], doT_ref[i]
            sT = jnp.dot(kw, qT, preferred_element_type=F32) + b_ref[cls]
            pT = jnp.exp(sT - lse_ref[i])
            dv_acc[win, :] += _nt(pT.astype(BF), doT)
            ds = pT * (jnp.dot(v_ref[win, :], doT, preferred_element_type=F32) - dl_ref[i])
            db_ref[cls] += ds
            dsT = ds.astype(BF)
            dk_acc[win, :] += _nt(dsT, qT)
            dq_ref[i] = _tn(kw, dsT).astype(dq_ref.dtype)
            return carry

        lax.fori_loop(0, nq, tile, 0)
        dk_ref[...] = dk_acc[...].astype(dk_ref.dtype)
        dv_ref[...] = dv_acc[...].astype(dv_ref.dtype)

    chk = lambda r: pl.BlockSpec((None, nq, r, NBR_Q), lambda h: (h, 0, 0, 0))
    tok = pl.BlockSpec((None, T, HD), lambda h: (h, 0, 0))
    bsp = pl.BlockSpec((None, 5, NBR_K, NBR_Q), lambda h: (h, 0, 0, 0))
    return _pcall(
        body, name="nbr_bwd", grid=(H,),
        in_specs=[chk(HD), tok, tok, chk(HD), chk(1), chk(1), bsp],
        out_specs=[chk(HD), tok, tok, bsp],
        out_shape=[_sds((H, nq, HD, NBR_Q), BF), _sds((H, T, HD), BF), _sds((H, T, HD), BF),
                   _sds((H, 5, NBR_K, NBR_Q), F32)],
        scratch_shapes=[pltpu.VMEM((T, HD), F32)] * 2,
        compiler_params=_cp("parallel"),
    )(qT, k, v, doT, lse, delta, bias)


def _band_offsets(radius):
    offs = [0]
    for r in range(1, radius + 1):
        offs += [-r, r]
    return offs


def _dilated_bias(t):
    radius = max(w // 2 for w, _ in A_CONFIGS) // t
    tabs = []
    i = np.arange(t)
    for off in _band_offsets(radius):
        d = off * t + i[None, :] - i[:, None]
        mult = np.zeros((t, t), np.float32)
        for w, dil in A_CONFIGS:
            mult += ((d % dil) == 0) & (np.abs(d) <= w // 2)
        tabs.append(np.where(mult > 0, np.log(np.maximum(mult, 1.0)), NEG).astype(np.float32))
    return jnp.asarray(np.stack(tabs)[None]), radius


def _nbr_index(t):
    rpt = t // GRID_W
    i = np.arange(t)
    c0 = np.clip(i % GRID_W - 8, 0, GRID_W - 16)
    col_ok = ((i[None, :] % GRID_W) >= c0[:, None]) & ((i[None, :] % GRID_W) < c0[:, None] + 16)
    oks = []
    for off in _band_offsets(1):
        dr = (i[None, :] // GRID_W) - (i[:, None] // GRID_W) + rpt * off
        oks.append(col_ok & (np.abs(dr) <= 7))
    return np.stack(oks)


def _nbr_bias(rpb, t):
    rpt = t // GRID_W
    e1, e2 = _rpb_fold_matrices(t)
    ok = _nbr_index(t)
    padded = jnp.pad(rpb, ((0, 0), (0, 1), (0, 128 - rpb.shape[2]))).reshape(64, 128)

    def body(e2t_ref, rpb_ref, e1t_ref, out_ref):
        picked = jnp.dot(e2t_ref[...], rpb_ref[...], precision=HI, preferred_element_type=F32)
        out_ref[...] = jnp.dot(picked, e1t_ref[...], precision=HI, preferred_element_type=F32)

    n = e2.shape[1]
    sub = _pcall(body, name="rpb_expand", out_shape=_sds((n, GRID_W * GRID_W), F32),
                 compiler_params=pltpu.CompilerParams(vmem_limit_bytes=VMEM_LIMIT))(e2.T, padded, e1.T)
    tiles = sub.reshape(4, 3, rpt, rpt, GRID_W, GRID_W).transpose(0, 1, 2, 4, 3, 5).reshape(4, 3, t, t)
    return jnp.where(ok[None], tiles, NEG)


def _nbr_mask(qi, kb, t, rows, q_on_lanes):
    rpt = t // GRID_W
    qshape, kshape = ((1, t), (t, 1)) if q_on_lanes else ((t, 1), (1, t))
    rq = rpt * qi + lax.broadcasted_iota(jnp.int32, qshape, 1 if q_on_lanes else 0) // GRID_W
    rk = rpt * kb + lax.broadcasted_iota(jnp.int32, kshape, 0 if q_on_lanes else 1) // GRID_W
    r0 = jnp.clip(rq - 4, 0, rows - 8)
    return (rk >= r0) & (rk < r0 + 8)


def _attn_band_fwd(q, k, v, bias, radius, rowmask):
    H, T, _ = q.shape
    t = BAND_T
    nq = T // t
    Hb = bias.shape[0]
    offs = _band_offsets(radius)
    rows = T // GRID_W

    def body(q_ref, k_ref, v_ref, b_ref, o_ref, lse_ref, m_ref, l_ref, acc_ref):
        i = pl.program_id(1)
        qv = q_ref[...]
        m_ref[...] = jnp.full((t, 1), NEG, F32)
        l_ref[...] = jnp.zeros((t, 1), F32)
        acc_ref[...] = jnp.zeros((t, HD), F32)

        def tile(o, off):
            kb = i + off
            st = pl.multiple_of(kb * t, t)
            ks, vs = k_ref[pl.ds(st, t), :], v_ref[pl.ds(st, t), :]
            s = _nt(qv, ks) + b_ref[o]
            if rowmask:
                s = jnp.where(_nbr_mask(i, kb, t, rows, False), s, NEG)
            m_old = m_ref[...]
            m_new = jnp.maximum(m_old, jnp.max(s, axis=-1, keepdims=True))
            a = jnp.exp(m_old - m_new)
            p = jnp.exp(s - m_new)
            l_ref[...] = a * l_ref[...] + jnp.sum(p, axis=-1, keepdims=True)
            acc_ref[...] = a * acc_ref[...] + jnp.dot(p.astype(BF), vs, preferred_element_type=F32)
            m_ref[...] = m_new

        for o, off in enumerate(offs):
            if off == 0:
                tile(o, off)
            else:
                pl.when((i + off >= 0) & (i + off < nq))(functools.partial(tile, o, off))
        o_ref[...] = (acc_ref[...] / l_ref[...]).astype(o_ref.dtype)
        lse_ref[...] = m_ref[...] + jnp.log(l_ref[...])

    qs = pl.BlockSpec((None, t, HD), lambda h, i: (h, i, 0))
    kvs = pl.BlockSpec((None, T, HD), lambda h, i: (h, 0, 0))
    bs = pl.BlockSpec((None, len(offs), t, t), lambda h, i: (h if Hb > 1 else 0, 0, 0, 0))
    return _pcall(
        body, name="attn_band_fwd_c" if rowmask else "attn_band_fwd_a", grid=(H, nq),
        in_specs=[qs, kvs, kvs, bs],
        out_specs=[qs, pl.BlockSpec((None, t, 1), lambda h, i: (h, i, 0))],
        out_shape=[_sds((H, T, HD), BF), _sds((H, T, 1), F32)],
        scratch_shapes=[pltpu.VMEM((t, 1), F32), pltpu.VMEM((t, 1), F32), pltpu.VMEM((t, HD), F32)],
        compiler_params=_cp("parallel", "parallel"),
    )(q, k, v, bias)


def _attn_band_dq(q, k, v, o, do, lse, bias, radius, rowmask):
    H, T, _ = q.shape
    t = BAND_T
    nq = T // t
    Hb = bias.shape[0]
    offs = _band_offsets(radius)
    rows = T // GRID_W

    def body(q_ref, k_ref, v_ref, o_ref, do_ref, lse_ref, b_ref, dq_ref, dl_ref, *rest):
        db_ref = rest[0] if rowmask else None
        acc_ref = rest[-1]
        i = pl.program_id(1)
        qv, dov, lse = q_ref[...], do_ref[...], lse_ref[...]
        delta = jnp.sum(dov.astype(F32) * o_ref[...].astype(F32), axis=-1, keepdims=True)
        acc_ref[...] = jnp.zeros((t, HD), F32)
        if rowmask:
            @pl.when(i == 0)
            def _():
                db_ref[...] = jnp.zeros_like(db_ref)

        def tile(o, off):
            kb = i + off
            st = pl.multiple_of(kb * t, t)
            ks, vs = k_ref[pl.ds(st, t), :], v_ref[pl.ds(st, t), :]
            s = _nt(qv, ks) + b_ref[o]
            if rowmask:
                s = jnp.where(_nbr_mask(i, kb, t, rows, False), s, NEG)
            p = jnp.exp(s - lse)
            ds = p * (_nt(dov, vs) - delta)
            acc_ref[...] += jnp.dot(ds.astype(BF), ks, preferred_element_type=F32)
            if rowmask:
                db_ref[o] += ds

        for o, off in enumerate(offs):
            if off == 0:
                tile(o, off)
            else:
                pl.when((i + off >= 0) & (i + off < nq))(functools.partial(tile, o, off))
        dq_ref[...] = acc_ref[...]
        dl_ref[...] = delta

    qs = pl.BlockSpec((None, t, HD), lambda h, i: (h, i, 0))
    kvs = pl.BlockSpec((None, T, HD), lambda h, i: (h, 0, 0))
    cs = pl.BlockSpec((None, t, 1), lambda h, i: (h, i, 0))
    bs = pl.BlockSpec((None, len(offs), t, t), lambda h, i: (h if Hb > 1 else 0, 0, 0, 0))
    out_specs = [qs, cs]
    out_shape = [_sds((H, T, HD), F32), _sds((H, T, 1), F32)]
    if rowmask:
        out_specs.append(pl.BlockSpec((None, len(offs), t, t), lambda h, i: (h, 0, 0, 0)))
        out_shape.append(_sds((H, len(offs), t, t), F32))
    return _pcall(
        body, name="attn_band_dq_c" if rowmask else "attn_band_dq_a", grid=(H, nq),
        in_specs=[qs, kvs, kvs, qs, qs, cs, bs],
        out_specs=out_specs, out_shape=out_shape,
        scratch_shapes=[pltpu.VMEM((t, HD), F32)],
        compiler_params=_cp("parallel", "arbitrary"),
    )(q, k, v, o, do, lse, bias)


def _attn_band_dkv(q, k, v, do, lse, delta, bias_t, radius, rowmask):
    H, T, _ = q.shape
    t = BAND_T
    nq = T // t
    Hb = bias_t.shape[0]
    offs = _band_offsets(radius)
    rows = T // GRID_W
    lse_r, dl_r = _rows(lse, t), _rows(delta, t)

    def body(k_ref, v_ref, q_ref, do_ref, lse_ref, dl_ref, b_ref, dk_ref, dv_ref, dk_acc, dv_acc):
        jb = pl.program_id(1)
        kv, vv = k_ref[...], v_ref[...]
        dk_acc[...] = jnp.zeros((t, HD), F32)
        dv_acc[...] = jnp.zeros((t, HD), F32)

        def tile(o, off):
            qi = jb - off
            st = pl.multiple_of(qi * t, t)
            qs, dos = q_ref[pl.ds(st, t), :], do_ref[pl.ds(st, t), :]
            sT = _nt(kv, qs) + b_ref[o]
            if rowmask:
                sT = jnp.where(_nbr_mask(qi, jb, t, rows, True), sT, NEG)
            pT = jnp.exp(sT - lse_ref[qi])
            dv_acc[...] += jnp.dot(pT.astype(BF), dos, preferred_element_type=F32)
            dsT = pT * (_nt(vv, dos) - dl_ref[qi])
            dk_acc[...] += jnp.dot(dsT.astype(BF), qs, preferred_element_type=F32)

        for o, off in enumerate(offs):
            if off == 0:
                tile(o, off)
            else:
                pl.when((jb - off >= 0) & (jb - off < nq))(functools.partial(tile, o, off))
        dk_ref[...] = dk_acc[...]
        dv_ref[...] = dv_acc[...]

    kvs = pl.BlockSpec((None, t, HD), lambda h, j: (h, j, 0))
    qs = pl.BlockSpec((None, T, HD), lambda h, j: (h, 0, 0))
    rs = pl.BlockSpec((None, nq, 1, t), lambda h, j: (h, 0, 0, 0))
    bs = pl.BlockSpec((None, len(offs), t, t), lambda h, j: (h if Hb > 1 else 0, 0, 0, 0))
    return _pcall(
        body, name="attn_band_dkv_c" if rowmask else "attn_band_dkv_a", grid=(H, nq),
        in_specs=[kvs, kvs, qs, qs, rs, rs, bs],
        out_specs=[kvs, kvs],
        out_shape=[_sds((H, T, HD), F32)] * 2,
        scratch_shapes=[pltpu.VMEM((t, HD), F32)] * 2,
        compiler_params=_cp("parallel", "parallel"),
    )(k, v, q, do, lse_r, dl_r, bias_t)


def _rpb_fold_matrices(t):
    rpt = t // GRID_W
    e1 = np.zeros((GRID_W * GRID_W, 128), np.float32)
    ic, jc = np.meshgrid(np.arange(GRID_W), np.arange(GRID_W), indexing="ij")
    dc = (jc - ic + 15).reshape(-1)
    keep = (dc >= 0) & (dc <= 30)
    e1[np.arange(GRID_W * GRID_W)[keep], dc[keep]] = 1.0
    offs = _band_offsets(1)
    n = 4 * len(offs) * rpt * rpt
    e2 = np.zeros((64, n), np.float32)
    col = 0
    for h in range(4):
        for off in offs:
            for ib in range(rpt):
                for jb in range(rpt):
                    dr = jb - ib + rpt * off + 7
                    if 0 <= dr <= 14:
                        e2[h * 16 + dr, col] = 1.0
                    col += 1
    return jnp.asarray(e1), jnp.asarray(e2)


def _rpb_grad(dbias):
    t = dbias.shape[-1]
    rpt = t // GRID_W
    e1, e2 = _rpb_fold_matrices(t)
    sub = dbias.reshape(4, 3, rpt, GRID_W, rpt, GRID_W).transpose(0, 1, 2, 4, 3, 5)
    sub = sub.reshape(4 * 3 * rpt * rpt, GRID_W * GRID_W)

    def body(e2_ref, sub_ref, e1_ref, out_ref):
        diag = jnp.dot(sub_ref[...], e1_ref[...], precision=HI, preferred_element_type=F32)
        out_ref[...] = jnp.dot(e2_ref[...], diag, precision=HI, preferred_element_type=F32)

    out = _pcall(body, name="rpb_fold", out_shape=_sds((64, 128), F32),
                 compiler_params=pltpu.CompilerParams(vmem_limit_bytes=VMEM_LIMIT))(e2, sub, e1)
    return out.reshape(4, 16, 128)[:, :15, :31]


def _sigmoid(z):
    return 1.0 / (1.0 + jnp.exp(-z))


def _merge_fwd(oa, ob, oc, hg, bg, wa, wb, wc):
    T = oa.shape[0]
    tt = min(512, T)

    def body(oa_ref, ob_ref, oc_ref, hg_ref, bg_ref, wa_ref, wb_ref, wc_ref, out_ref):
        acc = None
        for k, (o_ref, w_ref) in enumerate(((oa_ref, wa_ref), (ob_ref, wb_ref), (oc_ref, wc_ref))):
            y = jnp.dot(o_ref[...], w_ref[...], preferred_element_type=F32)
            g = _sigmoid(hg_ref[:, D * k:D * (k + 1)] + bg_ref[:, D * k:D * (k + 1)])
            acc = g * y if acc is None else acc + g * y
        out_ref[...] = acc.astype(out_ref.dtype)

    row = lambda w: pl.BlockSpec((tt, w), lambda i: (i, 0))
    const = lambda a: pl.BlockSpec(a.shape, lambda i: (0, 0))
    return _pcall(
        body, name="merge_fwd", grid=(T // tt,),
        in_specs=[row(A_W), row(BQ_W), row(C_W), row(GATE), const(bg), const(wa), const(wb), const(wc)],
        out_specs=row(D), out_shape=_sds((T, D), BF),
        compiler_params=_cp("parallel"),
    )(oa, ob, oc, hg, bg, wa, wb, wc)


def _merge_bwd(dm, oa, ob, oc, hg, bg, wa, wb, wc):
    T = oa.shape[0]
    tt = min(256, T)

    def body(dm_ref, oa_ref, ob_ref, oc_ref, hg_ref, bg_ref, wa_ref, wb_ref, wc_ref,
             dya, dyb, dyc, doa, dob, doc, dhg, dbg):
        @pl.when(pl.program_id(0) == 0)
        def _():
            dbg[...] = jnp.zeros_like(dbg)

        dmv = dm_ref[...]
        for k, (o_ref, w_ref, dy_ref, do_ref) in enumerate(
                ((oa_ref, wa_ref, dya, doa), (ob_ref, wb_ref, dyb, dob), (oc_ref, wc_ref, dyc, doc))):
            sl = slice(D * k, D * (k + 1))
            y = jnp.dot(o_ref[...], w_ref[...], preferred_element_type=F32)
            g = _sigmoid(hg_ref[:, sl] + bg_ref[:, sl])
            dy = (dmv * g).astype(BF)
            dy_ref[...] = dy
            do_ref[...] = _nt(dy, w_ref[...]).astype(do_ref.dtype)
            dz = dmv * y * (g * (1.0 - g))
            dhg[:, sl] = dz.astype(dhg.dtype)
            dbg[:, sl] += jnp.sum(dz, axis=0, keepdims=True)

    row = lambda w: pl.BlockSpec((tt, w), lambda i: (i, 0))
    const = lambda a: pl.BlockSpec(a.shape, lambda i: (0, 0))
    return _pcall(
        body, name="merge_bwd", grid=(T // tt,),
        in_specs=[row(D), row(A_W), row(BQ_W), row(C_W), row(GATE), const(bg), const(wa), const(wb), const(wc)],
        out_specs=[row(D)] * 3 + [row(A_W), row(BQ_W), row(C_W), row(GATE),
                                  pl.BlockSpec((1, GATE), lambda i: (0, 0))],
        out_shape=[_sds((T, D), BF)] * 3 + [_sds((T, A_W), BF), _sds((T, BQ_W), BF), _sds((T, C_W), BF),
                                            _sds((T, GATE), BF), _sds((1, GATE), F32)],
        compiler_params=_cp("arbitrary"),
    )(dm, oa, ob, oc, hg, bg, wa, wb, wc)


def _lin_ln(a, w, res, g, b):
    T, K = a.shape
    tt = min(256, T)

    def body(a_ref, w_ref, res_ref, g_ref, b_ref, y_ref, yb_ref, xh_ref, rs_ref):
        u = ALPHA * res_ref[...] + jnp.dot(a_ref[...], w_ref[...], preferred_element_type=F32)
        mu = jnp.mean(u, axis=-1, keepdims=True)
        c = u - mu
        r = lax.rsqrt(jnp.mean(c * c, axis=-1, keepdims=True) + LN_EPS)
        xh = c * r
        y = xh * g_ref[...] + b_ref[...]
        y_ref[...] = y
        yb_ref[...] = y.astype(BF)
        xh_ref[...] = xh
        rs_ref[...] = r

    row = lambda w_: pl.BlockSpec((tt, w_), lambda i: (i, 0))
    const = lambda s: pl.BlockSpec(s, lambda i: (0, 0))
    return _pcall(
        body, name="lin_ln", grid=(T // tt,),
        in_specs=[row(K), const((K, D)), row(D), const((1, D)), const((1, D))],
        out_specs=[row(D), row(D), row(D), row(1)],
        out_shape=[_sds((T, D), F32), _sds((T, D), BF), _sds((T, D), F32), _sds((T, 1), F32)],
        compiler_params=_cp("parallel"),
    )(a, w, res, g, b)


def _ln_bwd(dy, xh, rs, g):
    T = dy.shape[0]
    tt = min(512, T)

    def body(dy_ref, xh_ref, rs_ref, g_ref, du_ref, dub_ref, dg_ref, db_ref):
        @pl.when(pl.program_id(0) == 0)
        def _():
            dg_ref[...] = jnp.zeros_like(dg_ref)
            db_ref[...] = jnp.zeros_like(db_ref)

        dyv, xhv = dy_ref[...], xh_ref[...]
        dg_ref[...] += jnp.sum(dyv * xhv, axis=0, keepdims=True)
        db_ref[...] += jnp.sum(dyv, axis=0, keepdims=True)
        dxh = dyv * g_ref[...]
        m1 = jnp.mean(dxh, axis=-1, keepdims=True)
        m2 = jnp.mean(dxh * xhv, axis=-1, keepdims=True)
        du = rs_ref[...] * (dxh - m1 - xhv * m2)
        du_ref[...] = du
        dub_ref[...] = du.astype(BF)

    row = lambda w_: pl.BlockSpec((tt, w_), lambda i: (i, 0))
    const = lambda s: pl.BlockSpec(s, lambda i: (0, 0))
    return _pcall(
        body, name="ln_bwd", grid=(T // tt,),
        in_specs=[row(D), row(D), row(1), const((1, D))],
        out_specs=[row(D), row(D), const((1, D)), const((1, D))],
        out_shape=[_sds((T, D), F32), _sds((T, D), BF), _sds((1, D), F32), _sds((1, D), F32)],
        compiler_params=_cp("arbitrary"),
    )(dy, xh, rs, g)


def _loss_grad(y, tgt):
    T = y.shape[0]
    tt = min(512, T)

    def body(y_ref, t_ref, dy_ref, sq_ref):
        @pl.when(pl.program_id(0) == 0)
        def _():
            sq_ref[...] = jnp.zeros_like(sq_ref)

        e = y_ref[...] - t_ref[...]
        dy_ref[...] = e * (1.0 / D)
        sq_ref[...] += jnp.sum(e * e, axis=0, keepdims=True)

    row = pl.BlockSpec((tt, D), lambda i: (i, 0))
    return _pcall(
        body, name="loss_grad", grid=(T // tt,),
        in_specs=[row, row], out_specs=[row, pl.BlockSpec((1, D), lambda i: (0, 0))],
        out_shape=[_sds((T, D), F32), _sds((1, D), F32)],
        compiler_params=_cp("arbitrary"),
    )(y, tgt)


def _position():
    return lax.axis_index("x"), lax.axis_index("y"), lax.axis_index("c")


def _all_gather(xs, name):
    n = len(xs)
    hbm = pl.BlockSpec(memory_space=pl.ANY)

    def body(*refs):
        x_refs, out_refs = refs[:n], refs[n:2 * n]
        send, recv, loc = refs[2 * n:]
        x, y, c = _position()
        me, sib = (x, y, c), (x, y, 1 - c)
        chips = [(1 - x, y), (x, 1 - y), (1 - x, 1 - y)]

        def copy(a, k, block, to, src=None):
            px, py, pc = block
            dst = out_refs[a].at[4 * px + 2 * py + pc]
            return pltpu.make_async_remote_copy(
                src_ref=dst if src is None else src, dst_ref=dst,
                send_sem=send.at[a, k], recv_sem=recv.at[a, k], device_id=to, device_id_type=MESH)

        mine = [pltpu.make_async_copy(x_refs[a], out_refs[a].at[4 * x + 2 * y + c], loc.at[a]) for a in range(n)]
        for cp in mine:
            cp.start()
        first = []
        for a in range(n):
            first.append(copy(a, 0, me, sib, src=x_refs[a]))
            first += [copy(a, 1 + j, me, (*chip, c), src=x_refs[a]) for j, chip in enumerate(chips)]
        for cp in first:
            cp.start()
        passed = []
        for j, chip in enumerate(chips):
            for a in range(n):
                copy(a, 1 + j, (*chip, c), me).wait_recv()
                fwd = copy(a, 4 + j, (*chip, c), sib)
                fwd.start()
                passed.append(fwd)
        for a in range(n):
            copy(a, 0, sib, me).wait_recv()
            for j, chip in enumerate(chips):
                copy(a, 4 + j, (*chip, 1 - c), me).wait_recv()
        for cp in first + passed:
            cp.wait_send()
        for cp in mine:
            cp.wait()

    return _pcall_comm(
        body, name=name,
        in_specs=[hbm] * n, out_specs=[hbm] * n,
        out_shape=[_sds((8,) + x.shape, x.dtype) for x in xs],
        scratch_shapes=[pltpu.SemaphoreType.DMA((n, 7)), pltpu.SemaphoreType.DMA((n, 7)),
                        pltpu.SemaphoreType.DMA((n,))],
    )(*xs)


def _exchange_pair(gs, name):
    n = len(gs)
    hbm = pl.BlockSpec(memory_space=pl.ANY)

    def body(*refs):
        g_refs, out_refs = refs[:n], refs[n:2 * n]
        send, recv = refs[2 * n:]
        x, y, c = _position()
        copies = []
        for a in range(n):
            for p in range(4):
                copies.append(pltpu.make_async_remote_copy(
                    src_ref=g_refs[a].at[p, 1 - c], dst_ref=out_refs[a].at[p],
                    send_sem=send.at[a, p], recv_sem=recv.at[a, p],
                    device_id=(x, y, 1 - c), device_id_type=MESH))
        for cp in copies:
            cp.start()
        for cp in copies:
            cp.wait()

    return _pcall_comm(
        body, name=name,
        in_specs=[hbm] * n, out_specs=[hbm] * n,
        out_shape=[_sds((4,) + g.shape[2:], g.dtype) for g in gs],
        scratch_shapes=[pltpu.SemaphoreType.DMA((n, 4)), pltpu.SemaphoreType.DMA((n, 4))],
    )(*gs)


def _exchange_chips(ps, name):
    n = len(ps)
    hbm = pl.BlockSpec(memory_space=pl.ANY)

    def body(*refs):
        p_refs, out_refs = refs[:n], refs[n:2 * n]
        send, recv = refs[2 * n:]
        x, y, c = _position()
        chips = [(1 - x, y), (x, 1 - y), (1 - x, 1 - y)]
        copies = []
        for a in range(n):
            for j, (px, py) in enumerate(chips):
                copies.append(pltpu.make_async_remote_copy(
                    src_ref=p_refs[a].at[2 * px + py], dst_ref=out_refs[a].at[j],
                    send_sem=send.at[a, j], recv_sem=recv.at[a, j],
                    device_id=(px, py, c), device_id_type=MESH))
        for cp in copies:
            cp.start()
        for cp in copies:
            cp.wait()

    return _pcall_comm(
        body, name=name,
        in_specs=[hbm] * n, out_specs=[hbm] * n,
        out_shape=[_sds((3,) + p.shape[1:], p.dtype) for p in ps],
        scratch_shapes=[pltpu.SemaphoreType.DMA((n, 3)), pltpu.SemaphoreType.DMA((n, 3))],
    )(*ps)


def _pair_sum(g, got, core):
    _, _, R, C = g.shape
    tr = min(512, R)

    def body(core_ref, g_ref, r_ref, out_ref):
        out_ref[...] = (g_ref[...].astype(F32) + r_ref[...].astype(F32)).astype(out_ref.dtype)

    return _pcall(
        body, name="pair_sum",
        grid_spec=pltpu.PrefetchScalarGridSpec(
            num_scalar_prefetch=1, grid=(4, R // tr),
            in_specs=[pl.BlockSpec((None, None, tr, C), lambda p, i, cr: (p, cr[0], i, 0)),
                      pl.BlockSpec((None, tr, C), lambda p, i, cr: (p, i, 0))],
            out_specs=pl.BlockSpec((None, tr, C), lambda p, i, cr: (p, i, 0))),
        out_shape=_sds((4, R, C), g.dtype),
        compiler_params=_cp("parallel", "parallel"),
    )(core, g, got)


def _adamw_math(w, g, m, v):
    m = B1 * m + (1.0 - B1) * g
    v = B2 * v + (1.0 - B2) * (g * g)
    m_hat = m / (1.0 - B1 ** STEP)
    v_hat = v / (1.0 - B2 ** STEP)
    delta = -LR * (m_hat / (jnp.sqrt(v_hat) + EPS) + WD * w)
    return delta, m, v


def _chip_sum_adamw(p, got, chip, w, m, v):
    R, C = w.shape
    tr = min(512, R)

    def body(chip_ref, p_ref, r_ref, w_ref, m_ref, v_ref, g_out, d_out, m_out, v_out):
        g = ((p_ref[...].astype(F32) + r_ref[0].astype(F32)) + r_ref[1].astype(F32)) + r_ref[2].astype(F32)
        d, mn, vn = _adamw_math(w_ref[...], g, m_ref[...], v_ref[...])
        g_out[...], d_out[...], m_out[...], v_out[...] = g, d, mn, vn

    blk = pl.BlockSpec((tr, C), lambda i, ch: (i, 0))
    return _pcall(
        body, name="chip_sum_adamw",
        grid_spec=pltpu.PrefetchScalarGridSpec(
            num_scalar_prefetch=1, grid=(R // tr,),
            in_specs=[pl.BlockSpec((None, tr, C), lambda i, ch: (ch[0], i, 0)),
                      pl.BlockSpec((3, tr, C), lambda i, ch: (0, i, 0)), blk, blk, blk],
            out_specs=[blk] * 4),
        out_shape=[_sds((R, C), F32)] * 4,
        compiler_params=_cp("parallel"),
    )(chip, p, got, w, m, v)


def _small_sum_adamw(parts, w, m, v):
    _, R, C = parts.shape

    def body(p_ref, w_ref, m_ref, v_ref, g_out, d_out, m_out, v_out):
        g = p_ref[0]
        for k in range(1, 8):
            g = g + p_ref[k]
        d, mn, vn = _adamw_math(w_ref[...], g, m_ref[...], v_ref[...])
        g_out[...], d_out[...], m_out[...], v_out[...] = g, d, mn, vn

    return _pcall(body, name="small_sum_adamw", out_shape=[_sds((R, C), F32)] * 4,
                  compiler_params=pltpu.CompilerParams(vmem_limit_bytes=VMEM_LIMIT))(parts, w, m, v)


BIG = ("w_in", "w_branch_a", "w_branch_b", "w_branch_c", "w_out", "w_up", "w_down")
ROW_SHARDED = ("w_out", "w_down")
SMALL = ("b_gate", "q_norm_b", "k_norm_b", "rpb_c", "ln1_g", "ln1_b", "ln2_g", "ln2_b")
NAMES = ("w_in", "b_gate", "q_norm_b", "k_norm_b", "rpb_c", "w_branch_a", "w_branch_b", "w_branch_c",
         "w_out", "ln1_g", "ln1_b", "w_up", "w_down", "ln2_g", "ln2_b")


def _full_weight(gathered, name, layer):
    blk = gathered.reshape(8, DEPTH, gathered.shape[1] // DEPTH, gathered.shape[2])[:, layer]
    if name in ROW_SHARDED:
        return blk.reshape(-1, blk.shape[2])
    return blk.transpose(1, 0, 2).reshape(blk.shape[1], -1)


def _chunks(grad, name):
    if name in ROW_SHARDED:
        return grad.reshape(8, grad.shape[0] // 8, grad.shape[1])
    return grad.reshape(grad.shape[0], 8, grad.shape[1] // 8).transpose(1, 0, 2)


def _layer_fwd(x, xb, W, P, tabs, gm):
    hq, = _mm(xb, W["w_qkv"], "nn", [F32], 1024, 768, 1024, name="in_qkv")
    hg, = _mm(xb, W["w_gate"], "nn", [F32], 1024, 1024, 1024, name="in_gate")
    tab_a, tab_b = tabs
    prepped = _prep_fwd(hq, tab_a, tab_b, P["qn"], P["kn"], gm)
    T = x.shape[0]
    tb = min(FULL_T, T)
    oa, lse_a, qa = _dilated_fwd(*prepped[0:3])
    qb_t, kb_t, vb_t = prepped[3:6]
    qb, kb, vb = _chunked_t(qb_t, tb), _to_heads(kb_t), _chunked_t(vb_t, min(FULL_TK, T))
    vb1 = jnp.concatenate([vb, jnp.ones(vb.shape[:2] + (V_ROWS - HD, vb.shape[3]), vb.dtype)], axis=2)
    ob, lse_b = _attn_full_fwd(qb, kb, vb1)
    qc_t, kc_t, vc_t = prepped[6:9]
    qc, kc, vc = _chunked_t(qc_t, NBR_Q), _to_heads(kc_t), _to_heads(vc_t)
    bias_c = _nbr_bias_blocks(P["rpb"], T)
    oc, lse_c = _nbr_fwd(qc, kc, jnp.concatenate([vc, jnp.ones_like(vc)], axis=2), bias_c)
    oa_t, ob_t, oc_t = oa.transpose(2, 0, 1).reshape(T, A_W), _unchunk_t(ob), _unchunk_t(oc)
    qb = (qb, kb, kb.transpose(0, 2, 1), _to_heads(vb_t))
    ka = va = None
    merged = _merge_fwd(oa_t, ob_t, oc_t, hg, P["bg"], W["w_branch_a"], W["w_branch_b"], W["w_branch_c"])
    x1, x1b, xh1, rs1 = _lin_ln(merged, W["w_out"], x, P["ln1_g"], P["ln1_b"])

    def relu2(acc):
        r = jnp.maximum(acc, 0.0)
        return r * r, r

    f, r = _mm(x1b, W["w_up"], "nn", [BF, BF], 1024, 1024, 1024, epilogue=relu2, name="mlp_up")
    x2, x2b, xh2, rs2 = _lin_ln(f, W["w_down"], x1, P["ln2_g"], P["ln2_b"])
    saved = dict(xb=xb, hq=hq, hg=hg, qkv=(qa, ka, va, qb, kb, vb, qc, kc, vc), o=(oa, ob, oc),
                 lse=(lse_a, lse_b, lse_c), o_t=(oa_t, ob_t, oc_t), bias_c=bias_c, merged=merged,
                 xh1=xh1, rs1=rs1, x1b=x1b, f=f, r=r, xh2=xh2, rs2=rs2)
    return x2, x2b, saved


def _layer_bwd(dx2, S, W, P, tabs, gm):
    G = {}
    du2, du2b, G["ln2_g"], G["ln2_b"] = _ln_bwd(dx2, S["xh2"], S["rs2"], P["ln2_g"])
    G["w_down"], = _mm(S["f"], du2b, "tn", [F32], 1024, 1024, 512, name="dw_down")
    da, = _mm(du2b, W["w_down"], "nt", [BF], 1024, 1024, 1024,
              epilogue=lambda acc, r: (acc * (2.0 * r.astype(F32)),), extras=(S["r"],), name="d_act")
    G["w_up"], = _mm(S["x1b"], da, "tn", [F32], 1024, 1024, 512, name="dw_up")
    dx1, = _mm(da, W["w_up"], "nt", [F32], 1024, 1024, 1024,
               epilogue=lambda acc, d: (ALPHA * d + acc,), extras=(du2,), name="dx_mlp")
    du1, du1b, G["ln1_g"], G["ln1_b"] = _ln_bwd(dx1, S["xh1"], S["rs1"], P["ln1_g"])
    G["w_out"], = _mm(S["merged"], du1b, "tn", [F32], 1024, 1024, 512, name="dw_out")
    dm, = _mm(du1b, W["w_out"], "nt", [F32], 1024, 1024, 1024, name="d_merged")
    oa_t, ob_t, oc_t = S["o_t"]
    dya, dyb, dyc, doa, dob, doc, dhg, G["b_gate"] = _merge_bwd(
        dm, oa_t, ob_t, oc_t, S["hg"], P["bg"], W["w_branch_a"], W["w_branch_b"], W["w_branch_c"])
    G["w_branch_a"], = _mm(oa_t, dya, "tn", [F32], 256, 1024, 512, name="dw_branch_a")
    G["w_branch_b"], = _mm(ob_t, dyb, "tn", [F32], 512, 1024, 512, name="dw_branch_b")
    G["w_branch_c"], = _mm(oc_t, dyc, "tn", [F32], 256, 1024, 512, name="dw_branch_c")

    qa, ka, va, qb, kb, vb, qc, kc, vc = S["qkv"]
    oa, ob, oc = S["o"]
    lse_a, lse_b, lse_c = S["lse"]
    dqa, dka, dva = _dilated_bwd(doa, oa, lse_a, qa)
    qT_b, k_b, kT_b, v_b = qb
    dobT = _chunked_t(dob, ob.shape[-1])
    dqbT, dkb8, dvb8 = _attn_full_bwd(qT_b, k_b, kT_b, v_b, dobT, lse_b, _attn_delta(dobT, ob))
    group_sum = lambda t: t.reshape(k_b.shape[0], -1, t.shape[1], HD).sum(1)
    dqb, dkb, dvb = _unchunk_t(dqbT), _from_heads(group_sum(dkb8)), _from_heads(group_sum(dvb8))
    bias_c = S["bias_c"]
    docT = _chunked_t(doc, NBR_Q)
    dqcT, dkc, dvc, dbias_c = _nbr_bwd(qc, kc, vc, docT, lse_c, _attn_delta(docT, oc), bias_c)
    G["rpb_c"] = _nbr_rpb_grad(dbias_c, dx2.shape[0])

    tab_a, tab_b = tabs
    grads = [dqa, dka, dva, dqb, dkb, dvb, _unchunk_t(dqcT), _from_heads(dkc), _from_heads(dvc)]
    dhq, dqn, dkn = _prep_bwd(S["hq"], grads, tab_a, tab_b, P["qn"], P["kn"], gm)
    G["q_norm_b"] = dqn.reshape(BQ_W // HD, HD).sum(0)
    G["k_norm_b"] = dkn.reshape(BKV_W // HD, HD).sum(0)
    dw_qkv, = _mm(S["xb"], dhq, "tn", [F32], 1024, 768, 512, name="dw_qkv")
    dw_gate, = _mm(S["xb"], dhg, "tn", [F32], 1024, 1024, 512, name="dw_gate")
    G["w_in"] = jnp.concatenate([dw_qkv, dw_gate], axis=1)
    dx_a, = _mm(dhq, W["w_qkv"], "nt", [F32], 1024, 1024, 768,
                epilogue=lambda acc, d: (ALPHA * d + acc,), extras=(du1,), name="dx_qkv")
    dx, = _mm(dhg, W["w_gate"], "nt", [F32], 1024, 1024, 1024,
              epilogue=lambda acc, d: (d + acc,), extras=(dx_a,), name="dx_gate")
    return dx, G


def _pack_small(vals):
    flat = jnp.concatenate([vals[n].reshape(-1).astype(F32) for n in SMALL])
    pad = (-flat.shape[0]) % (8 * 128)
    return jnp.pad(flat, (0, pad)).reshape(-1, 128)


def _unpack_small(packed, like):
    flat, out, off = packed.reshape(-1), {}, 0
    for n in SMALL:
        size = math.prod(like[n].shape)
        out[n] = flat[off:off + size].reshape(like[n].shape)
        off += size
    return out


def kernel(x, w_in, b_gate, q_norm_b, k_norm_b, rpb_c, w_branch_a, w_branch_b, w_branch_c, w_out, ln1_g, ln1_b, w_up, w_down, ln2_g, ln2_b, loss_target, m_w_in, m_b_gate, m_q_norm_b, m_k_norm_b, m_rpb_c, m_w_branch_a, m_w_branch_b, m_w_branch_c, m_w_out, m_ln1_g, m_ln1_b, m_w_up, m_w_down, m_ln2_g, m_ln2_b, v_w_in, v_b_gate, v_q_norm_b, v_k_norm_b, v_rpb_c, v_w_branch_a, v_w_branch_b, v_w_branch_c, v_w_out, v_ln1_g, v_ln1_b, v_w_up, v_w_down, v_ln2_g, v_ln2_b):
    w = dict(w_in=w_in, b_gate=b_gate, q_norm_b=q_norm_b, k_norm_b=k_norm_b, rpb_c=rpb_c,
             w_branch_a=w_branch_a, w_branch_b=w_branch_b, w_branch_c=w_branch_c, w_out=w_out,
             ln1_g=ln1_g, ln1_b=ln1_b, w_up=w_up, w_down=w_down, ln2_g=ln2_g, ln2_b=ln2_b)
    m = dict(w_in=m_w_in, b_gate=m_b_gate, q_norm_b=m_q_norm_b, k_norm_b=m_k_norm_b, rpb_c=m_rpb_c,
             w_branch_a=m_w_branch_a, w_branch_b=m_w_branch_b, w_branch_c=m_w_branch_c, w_out=m_w_out,
             ln1_g=m_ln1_g, ln1_b=m_ln1_b, w_up=m_w_up, w_down=m_w_down, ln2_g=m_ln2_g, ln2_b=m_ln2_b)
    v = dict(w_in=v_w_in, b_gate=v_b_gate, q_norm_b=v_q_norm_b, k_norm_b=v_k_norm_b, rpb_c=v_rpb_c,
             w_branch_a=v_w_branch_a, w_branch_b=v_w_branch_b, w_branch_c=v_w_branch_c, w_out=v_w_out,
             ln1_g=v_ln1_g, ln1_b=v_ln1_b, w_up=v_w_up, w_down=v_w_down, ln2_g=v_ln2_g, ln2_b=v_ln2_b)
    T = x.shape[1]
    xc, yc, cc = _position()

    flat2 = lambda a: a.reshape(-1, a.shape[-1])
    gathered = _all_gather([flat2(w[n]).astype(BF) for n in BIG], "gather_weights")
    gathered = dict(zip(BIG, gathered))

    tabs = _rope_tables(T)
    gm = _group_mean_matrix()

    Ws, Ps = [], []
    for l in range(DEPTH):
        W = {n: _full_weight(gathered[n], n, l) for n in BIG}
        W["w_qkv"], W["w_gate"] = W["w_in"][:, :QKV], W["w_in"][:, QKV:]
        Ws.append(W)
        Ps.append(dict(qn=jnp.tile(q_norm_b[l][None], (1, 2)), kn=jnp.tile(k_norm_b[l][None], (1, 2)),
                       rpb=rpb_c[l], bg=b_gate[l][None], ln1_g=ln1_g[l][None], ln1_b=ln1_b[l][None],
                       ln2_g=ln2_g[l][None], ln2_b=ln2_b[l][None]))

    h = x[0]
    hb = h.astype(BF)
    saved = []
    for l in range(DEPTH):
        h, hb, S = _layer_fwd(h, hb, Ws[l], Ps[l], tabs, gm)
        saved.append(S)
    dy, sq = _loss_grad(h, loss_target[0])
    loss = lax.psum(0.5 / D * jnp.sum(sq), AXES)
    grads = [None] * DEPTH
    for l in reversed(range(DEPTH)):
        dy, grads[l] = _layer_bwd(dy, saved[l], Ws[l], Ps[l], tabs, gm)
    grad_x = dy[None]

    core = cc.reshape(1).astype(jnp.int32)
    chip = (2 * xc + yc).reshape(1).astype(jnp.int32)
    chunked = []
    for n in BIG:
        g = jnp.stack([_chunks(grads[l][n], n) for l in range(DEPTH)], axis=1)
        chunked.append(g.reshape(4, 2, DEPTH * g.shape[2], g.shape[3]).astype(BF))
    from_sibling = _exchange_pair(chunked, "grads_to_sibling")
    pair = [_pair_sum(g, r, core) for g, r in zip(chunked, from_sibling)]
    from_chips = _exchange_chips(pair, "grads_to_chips")
    out_g, out_d, out_m, out_v = {}, {}, {}, {}
    for n, p, r in zip(BIG, pair, from_chips):
        res = _chip_sum_adamw(p, r, chip, flat2(w[n]), flat2(m[n]), flat2(v[n]))
        out_g[n], out_d[n], out_m[n], out_v[n] = [t.reshape(w[n].shape) for t in res]

    part = _pack_small({n: jnp.stack([grads[l][n].reshape(w[n].shape[1:]) for l in range(DEPTH)]) for n in SMALL})
    parts, = _all_gather([part], "gather_small_grads")
    res = _small_sum_adamw(parts, _pack_small(w), _pack_small(m), _pack_small(v))
    for dst, packed in zip((out_g, out_d, out_m, out_v), res):
        dst.update(_unpack_small(packed, w))

    return (loss, grad_x, *[out_g[n] for n in NAMES], *[out_d[n] for n in NAMES],
            *[out_m[n] for n in NAMES], *[out_v[n] for n in NAMES])
```

```python
import functools
import math

import numpy as np
import jax
import jax.numpy as jnp
from jax import lax
from jax.experimental import pallas as pl
from jax.experimental.pallas import tpu as pltpu

F32 = jnp.float32
BF = jnp.bfloat16
HI = lax.Precision.HIGHEST
NEG = -1e30
MESH = pl.DeviceIdType.MESH
AXES = ("x", "y", "c")

D = 1024
DEPTH = 4
HD = 64
A_W, BQ_W, BKV_W, C_W = 256, 512, 128, 256
QKV = 2304
GATE = 3072
D_FF = 4096
GRID_W = 64
ALPHA = (2 * DEPTH) ** 0.25
LN_EPS = 1e-5
RMS_EPS = 1e-6
SCALE = HD ** -0.5
ROPE_THETA = 500000.0
AXIAL_THETA = 10000.0
A_CONFIGS = ((128, 1), (512, 4), (2048, 16))
LR, B1, B2, EPS, WD, STEP = 0.001, 0.9, 0.999, 1e-08, 0.01, 10

VMEM_LIMIT = 56 * 1024 * 1024
BAND_T = 256


def _pcall(body, **kw):
    return pl.pallas_call(body, **kw)


def _pcall_comm(body, **kw):
    return pl.pallas_call(body, **kw)


def _cp(*sem):
    return pltpu.CompilerParams(dimension_semantics=sem, vmem_limit_bytes=VMEM_LIMIT)


def _sds(shape, dtype):
    return jax.ShapeDtypeStruct(shape, dtype)


def _mm(a, b, dims, outs, tm, tn, tk, epilogue=None, extras=(), name="mm"):
    if dims == "tn":
        K, M = a.shape
    else:
        M, K = a.shape
    N = b.shape[0] if dims == "nt" else b.shape[1]
    tm, tn, tk = min(tm, M), min(tn, N), min(tk, K)
    assert M % tm == 0 and N % tn == 0 and K % tk == 0, (name, M, N, K, tm, tn, tk)
    nk = K // tk
    ne, no = len(extras), len(outs)

    def body(a_ref, b_ref, *rest):
        extra_refs, out_refs = rest[:ne], rest[ne:ne + no]
        av, bv = a_ref[...].astype(BF), b_ref[...].astype(BF)
        if dims == "nn":
            p = jnp.dot(av, bv, preferred_element_type=F32)
        elif dims == "nt":
            p = lax.dot_general(av, bv, (((1,), (1,)), ((), ())), preferred_element_type=F32)
        else:
            p = lax.dot_general(av, bv, (((0,), (0,)), ((), ())), preferred_element_type=F32)

        def finish(acc):
            res = epilogue(acc, *[r[...] for r in extra_refs]) if epilogue else (acc,)
            for o, r in zip(out_refs, res):
                o[...] = r.astype(o.dtype)

        if nk == 1:
            finish(p)
        else:
            acc_ref = rest[-1]
            k = pl.program_id(2)

            @pl.when(k == 0)
            def _():
                acc_ref[...] = p

            @pl.when(k > 0)
            def _():
                acc_ref[...] += p

            @pl.when(k == nk - 1)
            def _():
                finish(acc_ref[...])

    if dims == "tn":
        a_spec = pl.BlockSpec((tk, tm), lambda i, j, k: (k, i))
    else:
        a_spec = pl.BlockSpec((tm, tk), lambda i, j, k: (i, k))
    if dims == "nt":
        b_spec = pl.BlockSpec((tn, tk), lambda i, j, k: (j, k))
    else:
        b_spec = pl.BlockSpec((tk, tn), lambda i, j, k: (k, j))
    o_spec = pl.BlockSpec((tm, tn), lambda i, j, k: (i, j))
    res = _pcall(
        body, name=name, grid=(M // tm, N // tn, nk),
        in_specs=[a_spec, b_spec] + [o_spec] * ne,
        out_specs=[o_spec] * no,
        out_shape=[_sds((M, N), dt) for dt in outs],
        scratch_shapes=[pltpu.VMEM((tm, tn), F32)] if nk > 1 else [],
        compiler_params=_cp("parallel", "parallel", "arbitrary"),
    )(a, b, *extras)
    return res


def _rope_tables(T):
    pos = jnp.arange(T)

    def cs(p, theta, half):
        inv = theta ** (-jnp.arange(half, dtype=F32) / half)
        ang = p.astype(F32)[:, None] * inv[None, :]
        return jnp.cos(ang), jnp.sin(ang)

    ca, sa = cs(pos, ROPE_THETA, 8)
    one, zero, z8 = jnp.ones((T, 48), F32), jnp.zeros((T, 48), F32), jnp.zeros((T, 8), F32)
    tab_a = [jnp.concatenate(t, 1) for t in ([ca, ca, one], [-sa, z8, zero], [z8, sa, zero])]
    cr, sr = cs(pos // GRID_W, AXIAL_THETA, 16)
    cc, sc = cs(pos % GRID_W, AXIAL_THETA, 16)
    z16 = jnp.zeros((T, 16), F32)
    tab_b = [jnp.concatenate(t, 1) for t in ([cr, cr, cc, cc], [-sr, z16, -sc, z16], [z16, sr, z16, sc])]
    return [jnp.tile(t, (1, 2)) for t in tab_a], [jnp.tile(t, (1, 2)) for t in tab_b]


def _rot(x, C, S1, S2, k):
    return x * C + pltpu.roll(x, 128 - k, 1) * S1 + pltpu.roll(x, k, 1) * S2


def _rot_t(d, C, S1, S2, k):
    return d * C + pltpu.roll(d * S1, k, 1) + pltpu.roll(d * S2, 128 - k, 1)


def _group_mean_matrix():
    m = np.zeros((128, 128), np.float32)
    m[:64, :64] = 1.0 / 64
    m[64:, 64:] = 1.0 / 64
    return jnp.asarray(m)


def _prep_fwd(hq, tab_a, tab_b, qn, kn, gm):
    T = hq.shape[0]
    tt = min(256, T)
    widths = [A_W, A_W, A_W, BQ_W, BKV_W, BKV_W, C_W, C_W, C_W]

    def body(h_ref, ca, s1a, s2a, cb, s1b, s2b, qn_ref, kn_ref, gm_ref,
             qa, ka, va, qb, kb, vb, qc, kc, vc):
        def col(off, j):
            return h_ref[:, off + 128 * j: off + 128 * (j + 1)]

        for j in range(2):
            sl = slice(128 * j, 128 * (j + 1))
            qa[:, sl] = (_rot(col(0, j), ca[...], s1a[...], s2a[...], 8) * SCALE).astype(qa.dtype)
            ka[:, sl] = _rot(col(256, j), ca[...], s1a[...], s2a[...], 8).astype(ka.dtype)
            va[:, sl] = col(512, j).astype(va.dtype)
            qc[:, sl] = (col(1536, j) * SCALE).astype(qc.dtype)
            kc[:, sl] = col(1792, j).astype(kc.dtype)
            vc[:, sl] = col(2048, j).astype(vc.dtype)

        def normed(x, w):
            ms = jnp.dot(x * x, gm_ref[...], precision=HI, preferred_element_type=F32)
            return x * lax.rsqrt(ms + RMS_EPS) * w

        for j in range(4):
            y = normed(col(768, j), qn_ref[...])
            qb[:, 128 * j:128 * (j + 1)] = (_rot(y, cb[...], s1b[...], s2b[...], 16) * SCALE).astype(qb.dtype)
        y = normed(col(1280, 0), kn_ref[...])
        kb[...] = _rot(y, cb[...], s1b[...], s2b[...], 16).astype(kb.dtype)
        vb[...] = col(1408, 0).astype(vb.dtype)

    row = lambda w: pl.BlockSpec((tt, w), lambda i: (i, 0))
    const = lambda s: pl.BlockSpec(s, lambda i: (0, 0))
    return _pcall(
        body, name="prep_fwd", grid=(T // tt,),
        in_specs=[row(QKV)] + [row(128)] * 6 + [const((1, 128))] * 2 + [const((128, 128))],
        out_specs=[row(w) for w in widths],
        out_shape=[_sds((T, w), BF) for w in widths],
        compiler_params=_cp("parallel"),
    )(hq, *tab_a, *tab_b, qn, kn, gm)


def _prep_bwd(hq, grads, tab_a, tab_b, qn, kn, gm):
    T = hq.shape[0]
    tt = min(256, T)
    widths = [A_W, A_W, A_W, BQ_W, BKV_W, BKV_W, C_W, C_W, C_W]

    def body(h_ref, dqa, dka, dva, dqb, dkb, dvb, dqc, dkc, dvc,
             ca, s1a, s2a, cb, s1b, s2b, qn_ref, kn_ref, gm_ref, dh, dqn, dkn):
        i = pl.program_id(0)

        @pl.when(i == 0)
        def _():
            dqn[...] = jnp.zeros_like(dqn)
            dkn[...] = jnp.zeros_like(dkn)

        def put(off, j, val):
            dh[:, off + 128 * j: off + 128 * (j + 1)] = val.astype(dh.dtype)

        for j in range(2):
            sl = slice(128 * j, 128 * (j + 1))
            put(0, j, _rot_t(dqa[:, sl] * SCALE, ca[...], s1a[...], s2a[...], 8))
            put(256, j, _rot_t(dka[:, sl], ca[...], s1a[...], s2a[...], 8))
            put(512, j, dva[:, sl])
            put(1536, j, dqc[:, sl] * SCALE)
            put(1792, j, dkc[:, sl])
            put(2048, j, dvc[:, sl])

        def norm_bwd(x, w, e):
            ms = jnp.dot(x * x, gm_ref[...], precision=HI, preferred_element_type=F32)
            r = lax.rsqrt(ms + RMS_EPS)
            n = x * r
            dn = e * w
            proj = jnp.dot(dn * n, gm_ref[...], precision=HI, preferred_element_type=F32)
            return r * (dn - n * proj), jnp.sum(e * n, axis=0, keepdims=True)

        for j in range(4):
            sl = slice(128 * j, 128 * (j + 1))
            e = _rot_t(dqb[:, sl] * SCALE, cb[...], s1b[...], s2b[...], 16)
            dx, dw = norm_bwd(h_ref[:, 768 + 128 * j: 768 + 128 * (j + 1)], qn_ref[...], e)
            put(768, j, dx)
            dqn[:, sl] += dw
        e = _rot_t(dkb[...], cb[...], s1b[...], s2b[...], 16)
        dx, dw = norm_bwd(h_ref[:, 1280:1408], kn_ref[...], e)
        put(1280, 0, dx)
        dkn[...] += dw
        put(1408, 0, dvb[...])

    row = lambda w: pl.BlockSpec((tt, w), lambda i: (i, 0))
    const = lambda s: pl.BlockSpec(s, lambda i: (0, 0))
    return _pcall(
        body, name="prep_bwd", grid=(T // tt,),
        in_specs=[row(QKV)] + [row(w) for w in widths] + [row(128)] * 6
        + [const((1, 128))] * 2 + [const((128, 128))],
        out_specs=[row(QKV), const((1, BQ_W)), const((1, BKV_W))],
        out_shape=[_sds((T, QKV), BF), _sds((1, BQ_W), F32), _sds((1, BKV_W), F32)],
        compiler_params=_cp("arbitrary"),
    )(hq, *grads, *tab_a, *tab_b, qn, kn, gm)


def _to_heads(x):
    T, W = x.shape
    return x.reshape(T, W // HD, HD).transpose(1, 0, 2)


def _from_heads(x):
    H, T, _ = x.shape
    return x.transpose(1, 0, 2).reshape(T, H * HD)


def _rows(x, t):
    H, T, _ = x.shape
    return x.reshape(H, T // t, 1, t)


def _nt(a, b):
    return lax.dot_general(a, b, (((1,), (1,)), ((), ())), preferred_element_type=F32)


def _chunked_t(x, t):
    T, W = x.shape
    return x.reshape(T // t, t, W // HD, HD).transpose(2, 0, 3, 1)


def _unchunk_t(x):
    H, n, _, t = x.shape
    return x.transpose(1, 3, 0, 2).reshape(n * t, H * HD)


FULL_T = 512
FULL_TK = 512
FULL_HEADS = 1
V_ROWS = 72


def _phased(phases, grid):
    step = pl.program_id(0) * grid[1] + pl.program_id(1)
    last = grid[0] * grid[1] - 1
    for p, phase in enumerate(phases):
        pl.when(step == p * last // (len(phases) - 1))(phase)


def _attn_full_fwd(qT, k, vT1, gather=()):
    Hq, nq, _, t = qT.shape
    Hk, T, _ = k.shape
    G = Hq // Hk
    nk, tk = vT1.shape[1], vT1.shape[3]

    HB = FULL_HEADS
    assert G % HB == 0
    ng = len(gather)
    grid = (Hq // HB, nq)

    def body(q_ref, k_ref, v_ref, *rest):
        x_refs, (o_ref, lse_ref), rest = rest[:ng], rest[ng:ng + 2], rest[ng + 2:]
        out_refs, acc_refs, sems = rest[:ng], rest[ng:ng + HB], rest[ng + HB:]
        if ng:
            _phased(_gather_phases(x_refs, out_refs, *sems), grid)
        for acc_ref in acc_refs:
            acc_ref[...] = jnp.zeros((V_ROWS, t), F32)

        def scores(j, b):
            sT = jnp.dot(k_ref[pl.ds(pl.multiple_of(j * tk, tk), tk), :], q_ref[b], preferred_element_type=F32)
            return sT, jnp.max(sT, axis=0, keepdims=True)

        def update(j, b, scored, m_old):
            sT, m_tile = scored
            m_new = jnp.maximum(m_old, m_tile)
            pT = jnp.exp(sT - m_new).astype(BF)
            acc_refs[b][...] = (jnp.exp(m_old - m_new) * acc_refs[b][...]
                                + jnp.dot(v_ref[j], pT, preferred_element_type=F32))
            return m_new

        def step(j, carry):
            ms, ss = carry
            nxt = jnp.minimum(j + 1, nk - 1)
            new_s = tuple(scores(nxt, b) for b in range(HB))
            new_m = tuple(update(j, b, ss[b], ms[b]) for b in range(HB))
            return new_m, new_s

        init = (tuple(jnp.full((1, t), NEG, F32) for _ in range(HB)), tuple(scores(0, b) for b in range(HB)))
        ms, _ = lax.fori_loop(0, nk, step, init)
        for b in range(HB):
            l = acc_refs[b][pl.ds(HD, 1), :]
            o_ref[b] = (acc_refs[b][pl.ds(0, HD), :] / l).astype(o_ref.dtype)
            lse_ref[b] = ms[b] + jnp.log(l)

    qs = pl.BlockSpec((HB, None, HD, t), lambda h, i: (h, i, 0, 0))
    res = (_pcall_comm if ng else _pcall)(
        body, name="attn_full_fwd_gather" if ng else "attn_full_fwd", grid=grid,
        in_specs=[qs, pl.BlockSpec((None, T, HD), lambda h, i: (h * HB // G, 0, 0)),
                  pl.BlockSpec((None, nk, V_ROWS, tk), lambda h, i: (h * HB // G, 0, 0, 0))] + [HBM] * ng,
        out_specs=[qs, pl.BlockSpec((HB, None, 1, t), lambda h, i: (h, i, 0, 0))] + [HBM] * ng,
        out_shape=[_sds((Hq, nq, HD, t), BF), _sds((Hq, nq, 1, t), F32)]
        + [_sds((8,) + x.shape, x.dtype) for x in gather],
        scratch_shapes=[pltpu.VMEM((V_ROWS, t), F32)] * HB + (_gather_scratch(ng) if ng else []),
        compiler_params=_cp("arbitrary", "arbitrary") if ng else _cp("parallel", "parallel"),
    )(qT, k, vT1, *gather)
    return res[0], res[1], list(res[2:])


def _attn_delta(doT, oT):
    Hq, nq, _, t = doT.shape

    def body(do_ref, o_ref, dl_ref):
        dl_ref[...] = jnp.sum(do_ref[...].astype(F32) * o_ref[...].astype(F32), axis=1, keepdims=True)

    qs = pl.BlockSpec((None, nq, HD, t), lambda h: (h, 0, 0, 0))
    rs = pl.BlockSpec((None, nq, 1, t), lambda h: (h, 0, 0, 0))
    return _pcall(body, name="attn_delta", grid=(Hq,), in_specs=[qs, qs], out_specs=rs,
                  out_shape=_sds((Hq, nq, 1, t), F32), compiler_params=_cp("parallel"))(doT, oT)


def _attn_full_bwd(qT, k, kT, v, doT, lse, delta, scatter=()):
    Hq, nq, _, t = qT.shape
    Hk, T, _ = k.shape
    G = Hq // Hk
    nkv = T // t
    ns = len(scatter)

    def body(qT_ref, doT_ref, lse_ref, dl_ref, k_ref, kT_ref, v_ref, *rest):
        g_refs, (dq_ref, dk_ref, dv_ref), rest = rest[:ns], rest[ns:ns + 3], rest[ns + 3:]
        land_refs, (dq_acc, dk_acc, dv_acc), sems = rest[:ns], rest[ns:ns + 3], rest[ns + 3:]
        if ns:
            _phased(_scatter_phases(g_refs, land_refs, *sems), (Hq, nkv))
        j = pl.program_id(1)

        @pl.when(j == 0)
        def _():
            dq_acc[...] = jnp.zeros_like(dq_acc)

        kv, kTv, vv = k_ref[...], kT_ref[...], v_ref[...]
        dk_acc[...] = jnp.zeros((t, HD), F32)
        dv_acc[...] = jnp.zeros((t, HD), F32)

        def step(i, carry):
            qT, doT = qT_ref[i], doT_ref[i]
            pT = jnp.exp(jnp.dot(kv, qT, preferred_element_type=F32) - lse_ref[i])
            dv_acc[...] += _nt(pT.astype(BF), doT)
            dsT = (pT * (jnp.dot(vv, doT, preferred_element_type=F32) - dl_ref[i])).astype(BF)
            dk_acc[...] += _nt(dsT, qT)
            dq_acc[i] += jnp.dot(kTv, dsT, preferred_element_type=F32)
            return carry

        lax.fori_loop(0, nq, step, 0)
        dk_ref[...] = dk_acc[...].astype(dk_ref.dtype)
        dv_ref[...] = dv_acc[...].astype(dv_ref.dtype)

        @pl.when(j == nkv - 1)
        def _():
            dq_ref[...] = dq_acc[...].astype(dq_ref.dtype)

    chk = pl.BlockSpec((None, nq, HD, t), lambda h, j: (h, 0, 0, 0))
    row = pl.BlockSpec((None, nq, 1, t), lambda h, j: (h, 0, 0, 0))
    kvs = pl.BlockSpec((None, t, HD), lambda h, j: (h // G, j, 0))
    out = pl.BlockSpec((None, t, HD), lambda h, j: (h, j, 0))
    res = (_pcall_comm if ns else _pcall)(
        body, name="attn_full_bwd_scatter" if ns else "attn_full_bwd", grid=(Hq, nkv),
        in_specs=[chk, chk, row, row, kvs, pl.BlockSpec((None, HD, t), lambda h, j: (h // G, 0, j)), kvs] + [HBM] * ns,
        out_specs=[chk, out, out] + [HBM] * ns,
        out_shape=[_sds((Hq, nq, HD, t), BF), _sds((Hq, T, HD), BF), _sds((Hq, T, HD), BF)]
        + [_sds((7,) + g.shape[1:], g.dtype) for g in scatter],
        scratch_shapes=[pltpu.VMEM((nq, HD, t), F32), pltpu.VMEM((t, HD), F32), pltpu.VMEM((t, HD), F32)]
        + (_scatter_scratch(ns) if ns else []),
        compiler_params=_cp("arbitrary" if ns else "parallel", "arbitrary"),
    )(qT, doT, lse, delta, k, kT, v, *scatter)
    return res[0], res[1], res[2], list(res[3:])


DIL_Q = 128
DIL_K = 256
DIL_R = 64


def _vh(x, d):
    T = x.shape[0]
    return x.reshape(T // d, d, 4, HD).transpose(1, 2, 0, 3).reshape(4 * d, T // d, HD)


def _vh_inv(y, d):
    L = y.shape[1]
    return y.reshape(d, 4, L, HD).transpose(2, 0, 1, 3).reshape(L * d, 4 * HD)


def _vh_chunks(x, d):
    y = _vh(x, d)
    return y.reshape(y.shape[0], y.shape[1] // DIL_Q, DIL_Q, HD).transpose(0, 1, 3, 2)


def _chunks_to_dims(c, d):
    _, nq, R, _ = c.shape
    return c.reshape(d, 4, nq, R, DIL_Q).transpose(1, 3, 2, 4, 0).reshape(4, R, nq * DIL_Q * d)


def _dims_to_chunks(x, d):
    _, R, T = x.shape
    nq = T // (DIL_Q * d)
    return x.reshape(4, R, nq, DIL_Q, d).transpose(4, 0, 2, 1, 3).reshape(4 * d, nq, R, DIL_Q)


def _chunks_to_tokens(c, d):
    _, nq, _, _ = c.shape
    return c.reshape(d, 4, nq, HD, DIL_Q).transpose(2, 4, 0, 1, 3).reshape(nq * DIL_Q * d, 4 * HD)


def _dil_window(i, L):
    start = pl.multiple_of(jnp.clip(i * DIL_Q - DIL_R, 0, L - DIL_K), DIL_R)
    kk = start + lax.broadcasted_iota(jnp.int32, (DIL_K, 1), 0)
    qq = i * DIL_Q + lax.broadcasted_iota(jnp.int32, (1, DIL_Q), 1)
    return start, jnp.abs(kk - qq) <= DIL_R


def _tn(a, b):
    return lax.dot_general(a, b, (((0,), (0,)), ((), ())), preferred_element_type=F32)


def _dil_fwd(qT, k, v):
    V, nq, _, _ = qT.shape
    L = k.shape[1]
    assert L >= DIL_K

    unroll = 4 if nq % 4 == 0 else 1

    def body(q_ref, k_ref, v_ref, o_ref, lse_ref):
        def tile(i):
            start, mask = _dil_window(i, L)
            win = pl.ds(start, DIL_K)
            sT = jnp.where(mask, jnp.dot(k_ref[win, :], q_ref[i], preferred_element_type=F32), NEG)
            m = jnp.max(sT, axis=0, keepdims=True)
            pT = jnp.exp(sT - m).astype(BF)
            acc = _tn(v_ref[win, :], pT)
            l = jnp.max(acc[HD:HD + 8], axis=0, keepdims=True)
            o_ref[i] = acc[:HD] / l
            lse_ref[i] = m + jnp.log(l)

        def tiles(ii, carry):
            for u in range(unroll):
                tile(ii * unroll + u)
            return carry

        lax.fori_loop(0, nq // unroll, tiles, 0)

    chk = lambda r, dt: (pl.BlockSpec((None, nq, r, DIL_Q), lambda h: (h, 0, 0, 0)), _sds((V, nq, r, DIL_Q), dt))
    tok = lambda w: pl.BlockSpec((None, L, w), lambda h: (h, 0, 0))
    (o_spec, o_shape), (l_spec, l_shape) = chk(HD, F32), chk(1, F32)
    return _pcall(
        body, name=f"dil_fwd_{V // 4}", grid=(V,),
        in_specs=[chk(HD, BF)[0], tok(HD), tok(2 * HD)],
        out_specs=[o_spec, l_spec], out_shape=[o_shape, l_shape],
        compiler_params=_cp("parallel"),
    )(qT, k, v)


def _dil_merge(os_, lses):
    _, _, T = os_[0].shape
    tt = min(1024, T)
    n = len(os_)

    def body(*refs):
        o_refs, l_refs, (o_out, l_out) = refs[:n], refs[n:2 * n], refs[2 * n:]
        m = l_refs[0][...]
        for r in l_refs[1:]:
            m = jnp.maximum(m, r[...])
        ws = [jnp.exp(r[...] - m) for r in l_refs]
        tot = ws[0]
        for w_ in ws[1:]:
            tot = tot + w_
        acc = ws[0] * o_refs[0][...]
        for w_, o in zip(ws[1:], o_refs[1:]):
            acc = acc + w_ * o[...]
        o_out[...] = (acc / tot).astype(o_out.dtype)
        l_out[...] = m + jnp.log(tot)

    os_spec = pl.BlockSpec((None, HD, tt), lambda h, i: (h, 0, i))
    ls_spec = pl.BlockSpec((None, 1, tt), lambda h, i: (h, 0, i))
    return _pcall(
        body, name="dil_merge", grid=(4, T // tt),
        in_specs=[os_spec] * n + [ls_spec] * n, out_specs=[os_spec, ls_spec],
        out_shape=[_sds((4, HD, T), BF), _sds((4, 1, T), F32)],
        compiler_params=_cp("parallel", "parallel"),
    )(*os_, *lses)


def _dims_delta(doT, oT):
    _, _, T = doT.shape
    tt = min(2048, T)

    def body(do_ref, o_ref, dl_ref):
        dl_ref[...] = jnp.sum(do_ref[...].astype(F32) * o_ref[...].astype(F32), axis=0, keepdims=True)

    spec = pl.BlockSpec((None, HD, tt), lambda h, i: (h, 0, i))
    return _pcall(body, name="dims_delta", grid=(4, T // tt), in_specs=[spec, spec],
                  out_specs=pl.BlockSpec((None, 1, tt), lambda h, i: (h, 0, i)),
                  out_shape=_sds((4, 1, T), F32), compiler_params=_cp("parallel", "parallel"))(doT, oT)


def _dil_bwd(qT, k, v, doT, lse, delta):
    V, nq, _, _ = qT.shape
    L = k.shape[1]
    unroll = 2 if nq % 2 == 0 else 1

    def body(qT_ref, k_ref, v_ref, doT_ref, lse_ref, dl_ref, dq_ref, dk_ref, dv_ref, dk_acc, dv_acc):
        dk_acc[...] = jnp.zeros_like(dk_acc)
        dv_acc[...] = jnp.zeros_like(dv_acc)

        def tile(i):
            start, mask = _dil_window(i, L)
            win = pl.ds(start, DIL_K)
            kw, qT, doT = k_ref[win, :], qT_ref[i], doT_ref[i]
            sT = jnp.where(mask, jnp.dot(kw, qT, preferred_element_type=F32), NEG)
            pT = jnp.exp(sT - lse_ref[i])
            dv = _nt(pT.astype(BF), doT)
            dsT = (pT * (jnp.dot(v_ref[win, :], doT, preferred_element_type=F32) - dl_ref[i])).astype(BF)
            dk = _nt(dsT, qT)
            dq_ref[i] = _tn(kw, dsT).astype(dq_ref.dtype)
            return win, dk, dv

        def tiles(ii, carry):
            done = [tile(ii * unroll + u) for u in range(unroll)]
            for win, dk, dv in done:
                dk_acc[win, :] += dk
                dv_acc[win, :] += dv
            return carry

        lax.fori_loop(0, nq // unroll, tiles, 0)
        dk_ref[...] = dk_acc[...].astype(dk_ref.dtype)
        dv_ref[...] = dv_acc[...].astype(dv_ref.dtype)

    chk = lambda r: pl.BlockSpec((None, nq, r, DIL_Q), lambda h: (h, 0, 0, 0))
    tok = pl.BlockSpec((None, L, HD), lambda h: (h, 0, 0))
    return _pcall(
        body, name=f"dil_bwd_{V // 4}", grid=(V,),
        in_specs=[chk(HD), tok, tok, chk(HD), chk(1), chk(1)],
        out_specs=[chk(HD), tok, tok],
        out_shape=[_sds((V, nq, HD, DIL_Q), BF), _sds((V, L, HD), BF), _sds((V, L, HD), BF)],
        scratch_shapes=[pltpu.VMEM((L, HD), F32)] * 2,
        compiler_params=_cp("parallel"),
    )(qT, k, v, doT, lse, delta)


def _dilated_fwd(qa, ka, va):
    outs, lses, saved = [], [], []
    for _, d in A_CONFIGS:
        qT, k, v = _vh_chunks(qa, d), _vh(ka, d), _vh(va, d)
        oT, lse = _dil_fwd(qT, k, jnp.concatenate([v, jnp.ones_like(v)], axis=2))
        outs.append(_chunks_to_dims(oT, d))
        lses.append(_chunks_to_dims(lse, d))
        saved.append((qT, k, v))
    o, lse = _dil_merge(outs, lses)
    return o, lse, saved


def _dilated_bwd(do_t, o, lse, saved):
    doT = do_t.reshape(do_t.shape[0], 4, HD).transpose(1, 2, 0)
    delta = _dims_delta(doT, o)
    parts = []
    for (_, d), (qT, k, v) in zip(A_CONFIGS, saved):
        dqT, dk_c, dv_c = _dil_bwd(qT, k, v, _dims_to_chunks(doT, d), _dims_to_chunks(lse, d), _dims_to_chunks(delta, d))
        parts.append((_chunks_to_tokens(dqT, d), _vh_inv(dk_c, d), _vh_inv(dv_c, d)))
    total = lambda ts: sum(t.astype(F32) for t in ts).astype(do_t.dtype)
    return tuple(total(ts) for ts in zip(*parts))


NBR_Q = 128
NBR_ROWS = 10
NBR_K = NBR_ROWS * GRID_W


def _nbr_class_tiles(nq):
    return [2, 0, 1, nq - 2, nq - 1]


def _nbr_geometry(T):
    rows, nq = T // GRID_W, T // NBR_Q
    assert rows >= NBR_ROWS + 4 and nq >= 5
    kr, kc = np.divmod(np.arange(NBR_K), GRID_W)
    qr, qc = np.divmod(np.arange(NBR_Q), GRID_W)
    c0 = np.clip(qc - 8, 0, GRID_W - 16)
    col_ok = (kc[:, None] >= c0[None, :]) & (kc[:, None] < c0[None, :] + 16)
    drs, valids = [], []
    for i in _nbr_class_tiles(nq):
        start_row = int(np.clip(2 * i - 4, 0, rows - NBR_ROWS))
        r = 2 * i + qr
        r0 = np.clip(r - 4, 0, rows - 8)
        rk = start_row + kr
        row_ok = (rk[:, None] >= r0[None, :]) & (rk[:, None] < r0[None, :] + 8)
        valids.append(row_ok & col_ok)
        dr = start_row + np.arange(NBR_ROWS)[:, None] - (2 * i + np.arange(2)[None, :]) + 7
        drs.append(np.where((dr >= 0) & (dr <= 14), dr, -1))
    return np.stack(drs), np.stack(valids)


def _nbr_fold_matrices(T):
    dr, _ = _nbr_geometry(T)
    e1 = np.zeros((GRID_W * GRID_W, 128), np.float32)
    kc, qc = np.meshgrid(np.arange(GRID_W), np.arange(GRID_W), indexing="ij")
    dc = (kc - qc + 15).reshape(-1)
    keep = (dc >= 0) & (dc <= 30)
    e1[np.arange(GRID_W * GRID_W)[keep], dc[keep]] = 1.0
    n = 5 * NBR_ROWS * 2
    e2 = np.zeros((64, 4 * n), np.float32)
    for h in range(4):
        for j, d in enumerate(dr.reshape(-1)):
            if d >= 0:
                e2[h * 16 + d, h * n + j] = 1.0
    return jnp.asarray(e1), jnp.asarray(e2)


def _nbr_bias_blocks(rpb, T):
    e1, e2 = _nbr_fold_matrices(T)
    _, valid = _nbr_geometry(T)
    padded = jnp.pad(rpb, ((0, 0), (0, 1), (0, 128 - rpb.shape[2]))).reshape(64, 128)

    def body(e2t_ref, rpb_ref, e1t_ref, out_ref):
        picked = jnp.dot(e2t_ref[...], rpb_ref[...], precision=HI, preferred_element_type=F32)
        out_ref[...] = jnp.dot(picked, e1t_ref[...], precision=HI, preferred_element_type=F32)

    sub = _pcall(body, name="rpb_expand", out_shape=_sds((e2.shape[1], GRID_W * GRID_W), F32),
                 compiler_params=pltpu.CompilerParams(vmem_limit_bytes=VMEM_LIMIT))(e2.T, padded, e1.T)
    blocks = sub.reshape(4, 5, NBR_ROWS, 2, GRID_W, GRID_W).transpose(0, 1, 2, 4, 3, 5).reshape(4, 5, NBR_K, NBR_Q)
    return jnp.where(valid[None], blocks, NEG)


def _nbr_rpb_grad(dbias, T):
    e1, e2 = _nbr_fold_matrices(T)
    sub = dbias.reshape(4, 5, NBR_ROWS, GRID_W, 2, GRID_W).transpose(0, 1, 2, 4, 3, 5).reshape(-1, GRID_W * GRID_W)

    def body(e2_ref, sub_ref, e1_ref, out_ref):
        diag = jnp.dot(sub_ref[...], e1_ref[...], precision=HI, preferred_element_type=F32)
        out_ref[...] = jnp.dot(e2_ref[...], diag, precision=HI, preferred_element_type=F32)

    out = _pcall(body, name="rpb_fold", out_shape=_sds((64, 128), F32),
                 compiler_params=pltpu.CompilerParams(vmem_limit_bytes=VMEM_LIMIT))(e2, sub, e1)
    return out.reshape(4, 16, 128)[:, :15, :31]


def _nbr_tile(i, nq, T):
    start = pl.multiple_of(jnp.clip(i * NBR_Q - 4 * GRID_W, 0, T - NBR_K), NBR_Q)
    cls = jnp.where(i == 0, 1, jnp.where(i == 1, 2, jnp.where(i == nq - 2, 3, jnp.where(i == nq - 1, 4, 0))))
    return start, cls


def _nbr_fwd(qT, k, v, bias):
    H, nq, _, _ = qT.shape
    T = k.shape[1]
    unroll = 2 if nq % 2 == 0 else 1

    def body(q_ref, k_ref, v_ref, b_ref, o_ref, lse_ref):
        def tile(i):
            start, cls = _nbr_tile(i, nq, T)
            win = pl.ds(start, NBR_K)
            sT = jnp.dot(k_ref[win, :], q_ref[i], preferred_element_type=F32) + b_ref[cls]
            m = jnp.max(sT, axis=0, keepdims=True)
            pT = jnp.exp(sT - m).astype(BF)
            acc = _tn(v_ref[win, :], pT)
            l = jnp.max(acc[HD:HD + 8], axis=0, keepdims=True)
            o_ref[i] = (acc[:HD] / l).astype(o_ref.dtype)
            lse_ref[i] = m + jnp.log(l)

        def tiles(ii, carry):
            for u in range(unroll):
                tile(ii * unroll + u)
            return carry

        lax.fori_loop(0, nq // unroll, tiles, 0)

    chk = lambda r: pl.BlockSpec((None, nq, r, NBR_Q), lambda h: (h, 0, 0, 0))
    tok = lambda w: pl.BlockSpec((None, T, w), lambda h: (h, 0, 0))
    return _pcall(
        body, name="nbr_fwd", grid=(H,),
        in_specs=[chk(HD), tok(HD), tok(2 * HD), pl.BlockSpec((None, 5, NBR_K, NBR_Q), lambda h: (h, 0, 0, 0))],
        out_specs=[chk(HD), chk(1)],
        out_shape=[_sds((H, nq, HD, NBR_Q), BF), _sds((H, nq, 1, NBR_Q), F32)],
        compiler_params=_cp("parallel"),
    )(qT, k, v, bias)


def _nbr_bwd(qT, k, v, doT, lse, delta, bias):
    H, nq, _, _ = qT.shape
    T = k.shape[1]

    def body(qT_ref, k_ref, v_ref, doT_ref, lse_ref, dl_ref, b_ref, dq_ref, dk_ref, dv_ref, db_ref, dk_acc, dv_acc):
        dk_acc[...] = jnp.zeros_like(dk_acc)
        dv_acc[...] = jnp.zeros_like(dv_acc)
        db_ref[...] = jnp.zeros_like(db_ref)

        def tile(i, carry):
            start, cls = _nbr_tile(i, nq, T)
            win = pl.ds(start, NBR_K)
            kw, qT, doT = k_ref[win, :], qT_ref[i], doT_ref[i]
            sT = jnp.dot(kw, qT, preferred_element_type=F32) + b_ref[cls]
            pT = jnp.exp(sT - lse_ref[i])
            dv_acc[win, :] += _nt(pT.astype(BF), doT)
            ds = pT * (jnp.dot(v_ref[win, :], doT, preferred_element_type=F32) - dl_ref[i])
            db_ref[cls] += ds
            dsT = ds.astype(BF)
            dk_acc[win, :] += _nt(dsT, qT)
            dq_ref[i] = _tn(kw, dsT).astype(dq_ref.dtype)
            return carry

        lax.fori_loop(0, nq, tile, 0)
        dk_ref[...] = dk_acc[...].astype(dk_ref.dtype)
        dv_ref[...] = dv_acc[...].astype(dv_ref.dtype)

    chk = lambda r: pl.BlockSpec((None, nq, r, NBR_Q), lambda h: (h, 0, 0, 0))
    tok = pl.BlockSpec((None, T, HD), lambda h: (h, 0, 0))
    bsp = pl.BlockSpec((None, 5, NBR_K, NBR_Q), lambda h: (h, 0, 0, 0))
    return _pcall(
        body, name="nbr_bwd", grid=(H,),
        in_specs=[chk(HD), tok, tok, chk(HD), chk(1), chk(1), bsp],
        out_specs=[chk(HD), tok, tok, bsp],
        out_shape=[_sds((H, nq, HD, NBR_Q), BF), _sds((H, T, HD), BF), _sds((H, T, HD), BF),
                   _sds((H, 5, NBR_K, NBR_Q), F32)],
        scratch_shapes=[pltpu.VMEM((T, HD), F32)] * 2,
        compiler_params=_cp("parallel"),
    )(qT, k, v, doT, lse, delta, bias)


def _band_offsets(radius):
    offs = [0]
    for r in range(1, radius + 1):
        offs += [-r, r]
    return offs


def _dilated_bias(t):
    radius = max(w // 2 for w, _ in A_CONFIGS) // t
    tabs = []
    i = np.arange(t)
    for off in _band_offsets(radius):
        d = off * t + i[None, :] - i[:, None]
        mult = np.zeros((t, t), np.float32)
        for w, dil in A_CONFIGS:
            mult += ((d % dil) == 0) & (np.abs(d) <= w // 2)
        tabs.append(np.where(mult > 0, np.log(np.maximum(mult, 1.0)), NEG).astype(np.float32))
    return jnp.asarray(np.stack(tabs)[None]), radius


def _nbr_index(t):
    rpt = t // GRID_W
    i = np.arange(t)
    c0 = np.clip(i % GRID_W - 8, 0, GRID_W - 16)
    col_ok = ((i[None, :] % GRID_W) >= c0[:, None]) & ((i[None, :] % GRID_W) < c0[:, None] + 16)
    oks = []
    for off in _band_offsets(1):
        dr = (i[None, :] // GRID_W) - (i[:, None] // GRID_W) + rpt * off
        oks.append(col_ok & (np.abs(dr) <= 7))
    return np.stack(oks)


def _nbr_bias(rpb, t):
    rpt = t // GRID_W
    e1, e2 = _rpb_fold_matrices(t)
    ok = _nbr_index(t)
    padded = jnp.pad(rpb, ((0, 0), (0, 1), (0, 128 - rpb.shape[2]))).reshape(64, 128)

    def body(e2t_ref, rpb_ref, e1t_ref, out_ref):
        picked = jnp.dot(e2t_ref[...], rpb_ref[...], precision=HI, preferred_element_type=F32)
        out_ref[...] = jnp.dot(picked, e1t_ref[...], precision=HI, preferred_element_type=F32)

    n = e2.shape[1]
    sub = _pcall(body, name="rpb_expand", out_shape=_sds((n, GRID_W * GRID_W), F32),
                 compiler_params=pltpu.CompilerParams(vmem_limit_bytes=VMEM_LIMIT))(e2.T, padded, e1.T)
    tiles = sub.reshape(4, 3, rpt, rpt, GRID_W, GRID_W).transpose(0, 1, 2, 4, 3, 5).reshape(4, 3, t, t)
    return jnp.where(ok[None], tiles, NEG)


def _nbr_mask(qi, kb, t, rows, q_on_lanes):
    rpt = t // GRID_W
    qshape, kshape = ((1, t), (t, 1)) if q_on_lanes else ((t, 1), (1, t))
    rq = rpt * qi + lax.broadcasted_iota(jnp.int32, qshape, 1 if q_on_lanes else 0) // GRID_W
    rk = rpt * kb + lax.broadcasted_iota(jnp.int32, kshape, 0 if q_on_lanes else 1) // GRID_W
    r0 = jnp.clip(rq - 4, 0, rows - 8)
    return (rk >= r0) & (rk < r0 + 8)


def _attn_band_fwd(q, k, v, bias, radius, rowmask):
    H, T, _ = q.shape
    t = BAND_T
    nq = T // t
    Hb = bias.shape[0]
    offs = _band_offsets(radius)
    rows = T // GRID_W

    def body(q_ref, k_ref, v_ref, b_ref, o_ref, lse_ref, m_ref, l_ref, acc_ref):
        i = pl.program_id(1)
        qv = q_ref[...]
        m_ref[...] = jnp.full((t, 1), NEG, F32)
        l_ref[...] = jnp.zeros((t, 1), F32)
        acc_ref[...] = jnp.zeros((t, HD), F32)

        def tile(o, off):
            kb = i + off
            st = pl.multiple_of(kb * t, t)
            ks, vs = k_ref[pl.ds(st, t), :], v_ref[pl.ds(st, t), :]
            s = _nt(qv, ks) + b_ref[o]
            if rowmask:
                s = jnp.where(_nbr_mask(i, kb, t, rows, False), s, NEG)
            m_old = m_ref[...]
            m_new = jnp.maximum(m_old, jnp.max(s, axis=-1, keepdims=True))
            a = jnp.exp(m_old - m_new)
            p = jnp.exp(s - m_new)
            l_ref[...] = a * l_ref[...] + jnp.sum(p, axis=-1, keepdims=True)
            acc_ref[...] = a * acc_ref[...] + jnp.dot(p.astype(BF), vs, preferred_element_type=F32)
            m_ref[...] = m_new

        for o, off in enumerate(offs):
            if off == 0:
                tile(o, off)
            else:
                pl.when((i + off >= 0) & (i + off < nq))(functools.partial(tile, o, off))
        o_ref[...] = (acc_ref[...] / l_ref[...]).astype(o_ref.dtype)
        lse_ref[...] = m_ref[...] + jnp.log(l_ref[...])

    qs = pl.BlockSpec((None, t, HD), lambda h, i: (h, i, 0))
    kvs = pl.BlockSpec((None, T, HD), lambda h, i: (h, 0, 0))
    bs = pl.BlockSpec((None, len(offs), t, t), lambda h, i: (h if Hb > 1 else 0, 0, 0, 0))
    return _pcall(
        body, name="attn_band_fwd_c" if rowmask else "attn_band_fwd_a", grid=(H, nq),
        in_specs=[qs, kvs, kvs, bs],
        out_specs=[qs, pl.BlockSpec((None, t, 1), lambda h, i: (h, i, 0))],
        out_shape=[_sds((H, T, HD), BF), _sds((H, T, 1), F32)],
        scratch_shapes=[pltpu.VMEM((t, 1), F32), pltpu.VMEM((t, 1), F32), pltpu.VMEM((t, HD), F32)],
        compiler_params=_cp("parallel", "parallel"),
    )(q, k, v, bias)


def _attn_band_dq(q, k, v, o, do, lse, bias, radius, rowmask):
    H, T, _ = q.shape
    t = BAND_T
    nq = T // t
    Hb = bias.shape[0]
    offs = _band_offsets(radius)
    rows = T // GRID_W

    def body(q_ref, k_ref, v_ref, o_ref, do_ref, lse_ref, b_ref, dq_ref, dl_ref, *rest):
        db_ref = rest[0] if rowmask else None
        acc_ref = rest[-1]
        i = pl.program_id(1)
        qv, dov, lse = q_ref[...], do_ref[...], lse_ref[...]
        delta = jnp.sum(dov.astype(F32) * o_ref[...].astype(F32), axis=-1, keepdims=True)
        acc_ref[...] = jnp.zeros((t, HD), F32)
        if rowmask:
            @pl.when(i == 0)
            def _():
                db_ref[...] = jnp.zeros_like(db_ref)

        def tile(o, off):
            kb = i + off
            st = pl.multiple_of(kb * t, t)
            ks, vs = k_ref[pl.ds(st, t), :], v_ref[pl.ds(st, t), :]
            s = _nt(qv, ks) + b_ref[o]
            if rowmask:
                s = jnp.where(_nbr_mask(i, kb, t, rows, False), s, NEG)
            p = jnp.exp(s - lse)
            ds = p * (_nt(dov, vs) - delta)
            acc_ref[...] += jnp.dot(ds.astype(BF), ks, preferred_element_type=F32)
            if rowmask:
                db_ref[o] += ds

        for o, off in enumerate(offs):
            if off == 0:
                tile(o, off)
            else:
                pl.when((i + off >= 0) & (i + off < nq))(functools.partial(tile, o, off))
        dq_ref[...] = acc_ref[...]
        dl_ref[...] = delta

    qs = pl.BlockSpec((None, t, HD), lambda h, i: (h, i, 0))
    kvs = pl.BlockSpec((None, T, HD), lambda h, i: (h, 0, 0))
    cs = pl.BlockSpec((None, t, 1), lambda h, i: (h, i, 0))
    bs = pl.BlockSpec((None, len(offs), t, t), lambda h, i: (h if Hb > 1 else 0, 0, 0, 0))
    out_specs = [qs, cs]
    out_shape = [_sds((H, T, HD), F32), _sds((H, T, 1), F32)]
    if rowmask:
        out_specs.append(pl.BlockSpec((None, len(offs), t, t), lambda h, i: (h, 0, 0, 0)))
        out_shape.append(_sds((H, len(offs), t, t), F32))
    return _pcall(
        body, name="attn_band_dq_c" if rowmask else "attn_band_dq_a", grid=(H, nq),
        in_specs=[qs, kvs, kvs, qs, qs, cs, bs],
        out_specs=out_specs, out_shape=out_shape,
        scratch_shapes=[pltpu.VMEM((t, HD), F32)],
        compiler_params=_cp("parallel", "arbitrary"),
    )(q, k, v, o, do, lse, bias)


def _attn_band_dkv(q, k, v, do, lse, delta, bias_t, radius, rowmask):
    H, T, _ = q.shape
    t = BAND_T
    nq = T // t
    Hb = bias_t.shape[0]
    offs = _band_offsets(radius)
    rows = T // GRID_W
    lse_r, dl_r = _rows(lse, t), _rows(delta, t)

    def body(k_ref, v_ref, q_ref, do_ref, lse_ref, dl_ref, b_ref, dk_ref, dv_ref, dk_acc, dv_acc):
        jb = pl.program_id(1)
        kv, vv = k_ref[...], v_ref[...]
        dk_acc[...] = jnp.zeros((t, HD), F32)
        dv_acc[...] = jnp.zeros((t, HD), F32)

        def tile(o, off):
            qi = jb - off
            st = pl.multiple_of(qi * t, t)
            qs, dos = q_ref[pl.ds(st, t), :], do_ref[pl.ds(st, t), :]
            sT = _nt(kv, qs) + b_ref[o]
            if rowmask:
                sT = jnp.where(_nbr_mask(qi, jb, t, rows, True), sT, NEG)
            pT = jnp.exp(sT - lse_ref[qi])
            dv_acc[...] += jnp.dot(pT.astype(BF), dos, preferred_element_type=F32)
            dsT = pT * (_nt(vv, dos) - dl_ref[qi])
            dk_acc[...] += jnp.dot(dsT.astype(BF), qs, preferred_element_type=F32)

        for o, off in enumerate(offs):
            if off == 0:
                tile(o, off)
            else:
                pl.when((jb - off >= 0) & (jb - off < nq))(functools.partial(tile, o, off))
        dk_ref[...] = dk_acc[...]
        dv_ref[...] = dv_acc[...]

    kvs = pl.BlockSpec((None, t, HD), lambda h, j: (h, j, 0))
    qs = pl.BlockSpec((None, T, HD), lambda h, j: (h, 0, 0))
    rs = pl.BlockSpec((None, nq, 1, t), lambda h, j: (h, 0, 0, 0))
    bs = pl.BlockSpec((None, len(offs), t, t), lambda h, j: (h if Hb > 1 else 0, 0, 0, 0))
    return _pcall(
        body, name="attn_band_dkv_c" if rowmask else "attn_band_dkv_a", grid=(H, nq),
        in_specs=[kvs, kvs, qs, qs, rs, rs, bs],
        out_specs=[kvs, kvs],
        out_shape=[_sds((H, T, HD), F32)] * 2,
        scratch_shapes=[pltpu.VMEM((t, HD), F32)] * 2,
        compiler_params=_cp("parallel", "parallel"),
    )(k, v, q, do, lse_r, dl_r, bias_t)


def _rpb_fold_matrices(t):
    rpt = t // GRID_W
    e1 = np.zeros((GRID_W * GRID_W, 128), np.float32)
    ic, jc = np.meshgrid(np.arange(GRID_W), np.arange(GRID_W), indexing="ij")
    dc = (jc - ic + 15).reshape(-1)
    keep = (dc >= 0) & (dc <= 30)
    e1[np.arange(GRID_W * GRID_W)[keep], dc[keep]] = 1.0
    offs = _band_offsets(1)
    n = 4 * len(offs) * rpt * rpt
    e2 = np.zeros((64, n), np.float32)
    col = 0
    for h in range(4):
        for off in offs:
            for ib in range(rpt):
                for jb in range(rpt):
                    dr = jb - ib + rpt * off + 7
                    if 0 <= dr <= 14:
                        e2[h * 16 + dr, col] = 1.0
                    col += 1
    return jnp.asarray(e1), jnp.asarray(e2)


def _rpb_grad(dbias):
    t = dbias.shape[-1]
    rpt = t // GRID_W
    e1, e2 = _rpb_fold_matrices(t)
    sub = dbias.reshape(4, 3, rpt, GRID_W, rpt, GRID_W).transpose(0, 1, 2, 4, 3, 5)
    sub = sub.reshape(4 * 3 * rpt * rpt, GRID_W * GRID_W)

    def body(e2_ref, sub_ref, e1_ref, out_ref):
        diag = jnp.dot(sub_ref[...], e1_ref[...], precision=HI, preferred_element_type=F32)
        out_ref[...] = jnp.dot(e2_ref[...], diag, precision=HI, preferred_element_type=F32)

    out = _pcall(body, name="rpb_fold", out_shape=_sds((64, 128), F32),
                 compiler_params=pltpu.CompilerParams(vmem_limit_bytes=VMEM_LIMIT))(e2, sub, e1)
    return out.reshape(4, 16, 128)[:, :15, :31]


def _sigmoid(z):
    return 1.0 / (1.0 + jnp.exp(-z))


def _merge_fwd(oa, ob, oc, hg, bg, wa, wb, wc):
    T = oa.shape[0]
    tt = min(512, T)

    def body(oa_ref, ob_ref, oc_ref, hg_ref, bg_ref, wa_ref, wb_ref, wc_ref, out_ref):
        acc = None
        for k, (o_ref, w_ref) in enumerate(((oa_ref, wa_ref), (ob_ref, wb_ref), (oc_ref, wc_ref))):
            y = jnp.dot(o_ref[...], w_ref[...], preferred_element_type=F32)
            g = _sigmoid(hg_ref[:, D * k:D * (k + 1)] + bg_ref[:, D * k:D * (k + 1)])
            acc = g * y if acc is None else acc + g * y
        out_ref[...] = acc.astype(out_ref.dtype)

    row = lambda w: pl.BlockSpec((tt, w), lambda i: (i, 0))
    const = lambda a: pl.BlockSpec(a.shape, lambda i: (0, 0))
    return _pcall(
        body, name="merge_fwd", grid=(T // tt,),
        in_specs=[row(A_W), row(BQ_W), row(C_W), row(GATE), const(bg), const(wa), const(wb), const(wc)],
        out_specs=row(D), out_shape=_sds((T, D), BF),
        compiler_params=_cp("parallel"),
    )(oa, ob, oc, hg, bg, wa, wb, wc)


def _merge_bwd(dm, oa, ob, oc, hg, bg, wa, wb, wc):
    T = oa.shape[0]
    tt = min(256, T)

    def body(dm_ref, oa_ref, ob_ref, oc_ref, hg_ref, bg_ref, wa_ref, wb_ref, wc_ref,
             dya, dyb, dyc, doa, dob, doc, dhg, dbg):
        @pl.when(pl.program_id(0) == 0)
        def _():
            dbg[...] = jnp.zeros_like(dbg)

        dmv = dm_ref[...]
        for k, (o_ref, w_ref, dy_ref, do_ref) in enumerate(
                ((oa_ref, wa_ref, dya, doa), (ob_ref, wb_ref, dyb, dob), (oc_ref, wc_ref, dyc, doc))):
            sl = slice(D * k, D * (k + 1))
            y = jnp.dot(o_ref[...], w_ref[...], preferred_element_type=F32)
            g = _sigmoid(hg_ref[:, sl] + bg_ref[:, sl])
            dy = (dmv * g).astype(BF)
            dy_ref[...] = dy
            do_ref[...] = _nt(dy, w_ref[...]).astype(do_ref.dtype)
            dz = dmv * y * (g * (1.0 - g))
            dhg[:, sl] = dz.astype(dhg.dtype)
            dbg[:, sl] += jnp.sum(dz, axis=0, keepdims=True)

    row = lambda w: pl.BlockSpec((tt, w), lambda i: (i, 0))
    const = lambda a: pl.BlockSpec(a.shape, lambda i: (0, 0))
    return _pcall(
        body, name="merge_bwd", grid=(T // tt,),
        in_specs=[row(D), row(A_W), row(BQ_W), row(C_W), row(GATE), const(bg), const(wa), const(wb), const(wc)],
        out_specs=[row(D)] * 3 + [row(A_W), row(BQ_W), row(C_W), row(GATE),
                                  pl.BlockSpec((1, GATE), lambda i: (0, 0))],
        out_shape=[_sds((T, D), BF)] * 3 + [_sds((T, A_W), BF), _sds((T, BQ_W), BF), _sds((T, C_W), BF),
                                            _sds((T, GATE), BF), _sds((1, GATE), F32)],
        compiler_params=_cp("arbitrary"),
    )(dm, oa, ob, oc, hg, bg, wa, wb, wc)


def _lin_ln(a, w, res, g, b):
    T, K = a.shape
    tt = min(256, T)

    def body(a_ref, w_ref, res_ref, g_ref, b_ref, y_ref, yb_ref, xh_ref, rs_ref):
        u = ALPHA * res_ref[...] + jnp.dot(a_ref[...], w_ref[...], preferred_element_type=F32)
        mu = jnp.mean(u, axis=-1, keepdims=True)
        c = u - mu
        r = lax.rsqrt(jnp.mean(c * c, axis=-1, keepdims=True) + LN_EPS)
        xh = c * r
        y = xh * g_ref[...] + b_ref[...]
        y_ref[...] = y
        yb_ref[...] = y.astype(BF)
        xh_ref[...] = xh
        rs_ref[...] = r

    row = lambda w_: pl.BlockSpec((tt, w_), lambda i: (i, 0))
    const = lambda s: pl.BlockSpec(s, lambda i: (0, 0))
    return _pcall(
        body, name="lin_ln", grid=(T // tt,),
        in_specs=[row(K), const((K, D)), row(D), const((1, D)), const((1, D))],
        out_specs=[row(D), row(D), row(D), row(1)],
        out_shape=[_sds((T, D), F32), _sds((T, D), BF), _sds((T, D), F32), _sds((T, 1), F32)],
        compiler_params=_cp("parallel"),
    )(a, w, res, g, b)


def _ln_bwd(dy, xh, rs, g):
    T = dy.shape[0]
    tt = min(512, T)

    def body(dy_ref, xh_ref, rs_ref, g_ref, du_ref, dub_ref, dg_ref, db_ref):
        @pl.when(pl.program_id(0) == 0)
        def _():
            dg_ref[...] = jnp.zeros_like(dg_ref)
            db_ref[...] = jnp.zeros_like(db_ref)

        dyv, xhv = dy_ref[...], xh_ref[...]
        dg_ref[...] += jnp.sum(dyv * xhv, axis=0, keepdims=True)
        db_ref[...] += jnp.sum(dyv, axis=0, keepdims=True)
        dxh = dyv * g_ref[...]
        m1 = jnp.mean(dxh, axis=-1, keepdims=True)
        m2 = jnp.mean(dxh * xhv, axis=-1, keepdims=True)
        du = rs_ref[...] * (dxh - m1 - xhv * m2)
        du_ref[...] = du
        dub_ref[...] = du.astype(BF)

    row = lambda w_: pl.BlockSpec((tt, w_), lambda i: (i, 0))
    const = lambda s: pl.BlockSpec(s, lambda i: (0, 0))
    return _pcall(
        body, name="ln_bwd", grid=(T // tt,),
        in_specs=[row(D), row(D), row(1), const((1, D))],
        out_specs=[row(D), row(D), const((1, D)), const((1, D))],
        out_shape=[_sds((T, D), F32), _sds((T, D), BF), _sds((1, D), F32), _sds((1, D), F32)],
        compiler_params=_cp("arbitrary"),
    )(dy, xh, rs, g)


def _loss_grad(y, tgt):
    T = y.shape[0]
    tt = min(512, T)

    def body(y_ref, t_ref, dy_ref, sq_ref):
        @pl.when(pl.program_id(0) == 0)
        def _():
            sq_ref[...] = jnp.zeros_like(sq_ref)

        e = y_ref[...] - t_ref[...]
        dy_ref[...] = e * (1.0 / D)
        sq_ref[...] += jnp.sum(e * e, axis=0, keepdims=True)

    row = pl.BlockSpec((tt, D), lambda i: (i, 0))
    return _pcall(
        body, name="loss_grad", grid=(T // tt,),
        in_specs=[row, row], out_specs=[row, pl.BlockSpec((1, D), lambda i: (0, 0))],
        out_shape=[_sds((T, D), F32), _sds((1, D), F32)],
        compiler_params=_cp("arbitrary"),
    )(y, tgt)


def _position():
    return lax.axis_index("x"), lax.axis_index("y"), lax.axis_index("c")


HBM = pl.BlockSpec(memory_space=pl.ANY)


def _gather_phases(x_refs, out_refs, send, recv, loc):
    n = len(x_refs)
    x, y, c = _position()
    me, sib = (x, y, c), (x, y, 1 - c)
    chips = [(1 - x, y), (x, 1 - y), (1 - x, 1 - y)]

    def copy(a, k, block, to, src=None):
        px, py, pc = block
        dst = out_refs[a].at[4 * px + 2 * py + pc]
        return pltpu.make_async_remote_copy(
            src_ref=dst if src is None else src, dst_ref=dst,
            send_sem=send.at[a, k], recv_sem=recv.at[a, k], device_id=to, device_id_type=MESH)

    mine = [pltpu.make_async_copy(x_refs[a], out_refs[a].at[4 * x + 2 * y + c], loc.at[a]) for a in range(n)]
    first = []
    for a in range(n):
        first.append(copy(a, 0, me, sib, src=x_refs[a]))
        first += [copy(a, 1 + j, me, (*chip, c), src=x_refs[a]) for j, chip in enumerate(chips)]
    passed = [copy(a, 4 + j, (*chip, c), sib) for j, chip in enumerate(chips) for a in range(n)]

    def start():
        for cp in mine + first:
            cp.start()

    def forward():
        for j, chip in enumerate(chips):
            for a in range(n):
                copy(a, 1 + j, (*chip, c), me).wait_recv()
                copy(a, 4 + j, (*chip, c), sib).start()

    def finish():
        for a in range(n):
            copy(a, 0, sib, me).wait_recv()
            for j, chip in enumerate(chips):
                copy(a, 4 + j, (*chip, 1 - c), me).wait_recv()
        for cp in first + passed:
            cp.wait_send()
        for cp in mine:
            cp.wait()

    return start, forward, finish


def _gather_scratch(n):
    return [pltpu.SemaphoreType.DMA((n, 7)), pltpu.SemaphoreType.DMA((n, 7)), pltpu.SemaphoreType.DMA((n,))]


def _all_gather(xs, name):
    n = len(xs)

    def body(*refs):
        for phase in _gather_phases(refs[:n], refs[n:2 * n], *refs[2 * n:]):
            phase()

    return _pcall_comm(
        body, name=name,
        in_specs=[HBM] * n, out_specs=[HBM] * n,
        out_shape=[_sds((8,) + x.shape, x.dtype) for x in xs],
        scratch_shapes=_gather_scratch(n),
    )(*xs)


def _scatter_phases(g_refs, land_refs, send, recv):
    x, y, c = _position()
    peers = [(x, y, 1 - c), (1 - x, y, c), (x, 1 - y, c), (1 - x, 1 - y, c),
             (1 - x, y, 1 - c), (x, 1 - y, 1 - c), (1 - x, 1 - y, 1 - c)]
    copies = [pltpu.make_async_remote_copy(
        src_ref=g_refs[a].at[4 * px + 2 * py + pc], dst_ref=land_refs[a].at[k],
        send_sem=send.at[a, k], recv_sem=recv.at[a, k], device_id=(px, py, pc), device_id_type=MESH)
        for a in range(len(g_refs)) for k, (px, py, pc) in enumerate(peers)]

    def start():
        for cp in copies:
            cp.start()

    def finish():
        for cp in copies:
            cp.wait()

    return start, finish


def _scatter_scratch(n):
    return [pltpu.SemaphoreType.DMA((n, 7)), pltpu.SemaphoreType.DMA((n, 7))]


def _scatter(gs, name):
    n = len(gs)

    def body(*refs):
        for phase in _scatter_phases(refs[:n], refs[n:2 * n], *refs[2 * n:]):
            phase()

    return _pcall_comm(
        body, name=name,
        in_specs=[HBM] * n, out_specs=[HBM] * n,
        out_shape=[_sds((7,) + g.shape[1:], g.dtype) for g in gs],
        scratch_shapes=_scatter_scratch(n),
    )(*gs)


def _sum8(g, land, me):
    _, R, C = g.shape
    tr = min(512, R)

    def body(me_ref, g_ref, l_ref, out_ref):
        acc = g_ref[...].astype(F32)
        for k in range(7):
            acc = acc + l_ref[k].astype(F32)
        out_ref[...] = acc

    return _pcall(
        body, name="sum8",
        grid_spec=pltpu.PrefetchScalarGridSpec(
            num_scalar_prefetch=1, grid=(R // tr,),
            in_specs=[pl.BlockSpec((None, tr, C), lambda i, me_: (me_[0], i, 0)),
                      pl.BlockSpec((7, tr, C), lambda i, me_: (0, i, 0))],
            out_specs=pl.BlockSpec((tr, C), lambda i, me_: (i, 0))),
        out_shape=_sds((R, C), F32),
        compiler_params=_cp("parallel"),
    )(me, g, land)


def _adamw(g, w, m, v):
    R, C = w.shape
    tr = min(512, R)

    def body(g_ref, w_ref, m_ref, v_ref, d_out, m_out, v_out):
        d_out[...], m_out[...], v_out[...] = _adamw_math(w_ref[...], g_ref[...], m_ref[...], v_ref[...])

    blk = pl.BlockSpec((tr, C), lambda i: (i, 0))
    return _pcall(body, name="adamw", grid=(R // tr,), in_specs=[blk] * 4, out_specs=[blk] * 3,
                  out_shape=[_sds((R, C), F32)] * 3, compiler_params=_cp("parallel"))(g, w, m, v)


def _exchange_pair(gs, name):
    n = len(gs)
    hbm = pl.BlockSpec(memory_space=pl.ANY)

    def body(*refs):
        g_refs, out_refs = refs[:n], refs[n:2 * n]
        send, recv = refs[2 * n:]
        x, y, c = _position()
        copies = []
        for a in range(n):
            for p in range(4):
                copies.append(pltpu.make_async_remote_copy(
                    src_ref=g_refs[a].at[p, 1 - c], dst_ref=out_refs[a].at[p],
                    send_sem=send.at[a, p], recv_sem=recv.at[a, p],
                    device_id=(x, y, 1 - c), device_id_type=MESH))
        for cp in copies:
            cp.start()
        for cp in copies:
            cp.wait()

    return _pcall_comm(
        body, name=name,
        in_specs=[hbm] * n, out_specs=[hbm] * n,
        out_shape=[_sds((4,) + g.shape[2:], g.dtype) for g in gs],
        scratch_shapes=[pltpu.SemaphoreType.DMA((n, 4)), pltpu.SemaphoreType.DMA((n, 4))],
    )(*gs)


def _exchange_chips(ps, name):
    n = len(ps)
    hbm = pl.BlockSpec(memory_space=pl.ANY)

    def body(*refs):
        p_refs, out_refs = refs[:n], refs[n:2 * n]
        send, recv = refs[2 * n:]
        x, y, c = _position()
        chips = [(1 - x, y), (x, 1 - y), (1 - x, 1 - y)]
        copies = []
        for a in range(n):
            for j, (px, py) in enumerate(chips):
                copies.append(pltpu.make_async_remote_copy(
                    src_ref=p_refs[a].at[2 * px + py], dst_ref=out_refs[a].at[j],
                    send_sem=send.at[a, j], recv_sem=recv.at[a, j],
                    device_id=(px, py, c), device_id_type=MESH))
        for cp in copies:
            cp.start()
        for cp in copies:
            cp.wait()

    return _pcall_comm(
        body, name=name,
        in_specs=[hbm] * n, out_specs=[hbm] * n,
        out_shape=[_sds((3,) + p.shape[1:], p.dtype) for p in ps],
        scratch_shapes=[pltpu.SemaphoreType.DMA((n, 3)), pltpu.SemaphoreType.DMA((n, 3))],
    )(*ps)


def _pair_sum(g, got, core):
    _, _, R, C = g.shape
    tr = min(512, R)

    def body(core_ref, g_ref, r_ref, out_ref):
        out_ref[...] = (g_ref[...].astype(F32) + r_ref[...].astype(F32)).astype(out_ref.dtype)

    return _pcall(
        body, name="pair_sum",
        grid_spec=pltpu.PrefetchScalarGridSpec(
            num_scalar_prefetch=1, grid=(4, R // tr),
            in_specs=[pl.BlockSpec((None, None, tr, C), lambda p, i, cr: (p, cr[0], i, 0)),
                      pl.BlockSpec((None, tr, C), lambda p, i, cr: (p, i, 0))],
            out_specs=pl.BlockSpec((None, tr, C), lambda p, i, cr: (p, i, 0))),
        out_shape=_sds((4, R, C), g.dtype),
        compiler_params=_cp("parallel", "parallel"),
    )(core, g, got)


def _adamw_math(w, g, m, v):
    m = B1 * m + (1.0 - B1) * g
    v = B2 * v + (1.0 - B2) * (g * g)
    m_hat = m / (1.0 - B1 ** STEP)
    v_hat = v / (1.0 - B2 ** STEP)
    delta = -LR * (m_hat / (jnp.sqrt(v_hat) + EPS) + WD * w)
    return delta, m, v


def _chip_sum_adamw(p, got, chip, w, m, v):
    R, C = w.shape
    tr = min(512, R)

    def body(chip_ref, p_ref, r_ref, w_ref, m_ref, v_ref, g_out, d_out, m_out, v_out):
        g = ((p_ref[...].astype(F32) + r_ref[0].astype(F32)) + r_ref[1].astype(F32)) + r_ref[2].astype(F32)
        d, mn, vn = _adamw_math(w_ref[...], g, m_ref[...], v_ref[...])
        g_out[...], d_out[...], m_out[...], v_out[...] = g, d, mn, vn

    blk = pl.BlockSpec((tr, C), lambda i, ch: (i, 0))
    return _pcall(
        body, name="chip_sum_adamw",
        grid_spec=pltpu.PrefetchScalarGridSpec(
            num_scalar_prefetch=1, grid=(R // tr,),
            in_specs=[pl.BlockSpec((None, tr, C), lambda i, ch: (ch[0], i, 0)),
                      pl.BlockSpec((3, tr, C), lambda i, ch: (0, i, 0)), blk, blk, blk],
            out_specs=[blk] * 4),
        out_shape=[_sds((R, C), F32)] * 4,
        compiler_params=_cp("parallel"),
    )(chip, p, got, w, m, v)


def _small_sum_adamw(parts, w, m, v):
    _, R, C = parts.shape

    def body(p_ref, w_ref, m_ref, v_ref, g_out, d_out, m_out, v_out):
        g = p_ref[0]
        for k in range(1, 8):
            g = g + p_ref[k]
        d, mn, vn = _adamw_math(w_ref[...], g, m_ref[...], v_ref[...])
        g_out[...], d_out[...], m_out[...], v_out[...] = g, d, mn, vn

    return _pcall(body, name="small_sum_adamw", out_shape=[_sds((R, C), F32)] * 4,
                  compiler_params=pltpu.CompilerParams(vmem_limit_bytes=VMEM_LIMIT))(parts, w, m, v)


BIG = ("w_in", "w_branch_a", "w_branch_b", "w_branch_c", "w_out", "w_up", "w_down")
ROW_SHARDED = ("w_out", "w_down")
SMALL = ("b_gate", "q_norm_b", "k_norm_b", "rpb_c", "ln1_g", "ln1_b", "ln2_g", "ln2_b")
NAMES = ("w_in", "b_gate", "q_norm_b", "k_norm_b", "rpb_c", "w_branch_a", "w_branch_b", "w_branch_c",
         "w_out", "ln1_g", "ln1_b", "w_up", "w_down", "ln2_g", "ln2_b")


def _full_weight(blk, name):
    if name in ROW_SHARDED:
        return blk.reshape(-1, blk.shape[2])
    return blk.transpose(1, 0, 2).reshape(blk.shape[1], -1)


def _chunks(grad, name):
    if name in ROW_SHARDED:
        return grad.reshape(8, grad.shape[0] // 8, grad.shape[1])
    return grad.reshape(grad.shape[0], 8, grad.shape[1] // 8).transpose(1, 0, 2)


def _layer_fwd(x, xb, W, P, tabs, gm, gather=()):
    hq, = _mm(xb, W["w_qkv"], "nn", [F32], 1024, 768, 1024, name="in_qkv")
    hg, = _mm(xb, W["w_gate"], "nn", [F32], 1024, 1024, 1024, name="in_gate")
    tab_a, tab_b = tabs
    prepped = _prep_fwd(hq, tab_a, tab_b, P["qn"], P["kn"], gm)
    T = x.shape[0]
    tb = min(FULL_T, T)
    oa, lse_a, qa = _dilated_fwd(*prepped[0:3])
    qb_t, kb_t, vb_t = prepped[3:6]
    qb, kb, vb = _chunked_t(qb_t, tb), _to_heads(kb_t), _chunked_t(vb_t, min(FULL_TK, T))
    vb1 = jnp.concatenate([vb, jnp.ones(vb.shape[:2] + (V_ROWS - HD, vb.shape[3]), vb.dtype)], axis=2)
    ob, lse_b, gathered = _attn_full_fwd(qb, kb, vb1, gather)
    qc_t, kc_t, vc_t = prepped[6:9]
    qc, kc, vc = _chunked_t(qc_t, NBR_Q), _to_heads(kc_t), _to_heads(vc_t)
    bias_c = _nbr_bias_blocks(P["rpb"], T)
    oc, lse_c = _nbr_fwd(qc, kc, jnp.concatenate([vc, jnp.ones_like(vc)], axis=2), bias_c)
    oa_t, ob_t, oc_t = oa.transpose(2, 0, 1).reshape(T, A_W), _unchunk_t(ob), _unchunk_t(oc)
    qb = (qb, kb, kb.transpose(0, 2, 1), _to_heads(vb_t))
    ka = va = None
    merged = _merge_fwd(oa_t, ob_t, oc_t, hg, P["bg"], W["w_branch_a"], W["w_branch_b"], W["w_branch_c"])
    x1, x1b, xh1, rs1 = _lin_ln(merged, W["w_out"], x, P["ln1_g"], P["ln1_b"])

    def relu2(acc):
        r = jnp.maximum(acc, 0.0)
        return r * r, r

    f, r = _mm(x1b, W["w_up"], "nn", [BF, BF], 1024, 1024, 1024, epilogue=relu2, name="mlp_up")
    x2, x2b, xh2, rs2 = _lin_ln(f, W["w_down"], x1, P["ln2_g"], P["ln2_b"])
    saved = dict(xb=xb, hq=hq, hg=hg, qkv=(qa, ka, va, qb, kb, vb, qc, kc, vc), o=(oa, ob, oc),
                 lse=(lse_a, lse_b, lse_c), o_t=(oa_t, ob_t, oc_t), bias_c=bias_c, merged=merged,
                 xh1=xh1, rs1=rs1, x1b=x1b, f=f, r=r, xh2=xh2, rs2=rs2)
    return x2, x2b, saved, gathered


def _layer_bwd(dx2, S, W, P, tabs, gm, scatter=()):
    G = {}
    du2, du2b, G["ln2_g"], G["ln2_b"] = _ln_bwd(dx2, S["xh2"], S["rs2"], P["ln2_g"])
    G["w_down"], = _mm(S["f"], du2b, "tn", [F32], 1024, 1024, 512, name="dw_down")
    da, = _mm(du2b, W["w_down"], "nt", [BF], 1024, 1024, 1024,
              epilogue=lambda acc, r: (acc * (2.0 * r.astype(F32)),), extras=(S["r"],), name="d_act")
    G["w_up"], = _mm(S["x1b"], da, "tn", [F32], 1024, 1024, 512, name="dw_up")
    dx1, = _mm(da, W["w_up"], "nt", [F32], 1024, 1024, 1024,
               epilogue=lambda acc, d: (ALPHA * d + acc,), extras=(du2,), name="dx_mlp")
    du1, du1b, G["ln1_g"], G["ln1_b"] = _ln_bwd(dx1, S["xh1"], S["rs1"], P["ln1_g"])
    G["w_out"], = _mm(S["merged"], du1b, "tn", [F32], 1024, 1024, 512, name="dw_out")
    dm, = _mm(du1b, W["w_out"], "nt", [F32], 1024, 1024, 1024, name="d_merged")
    oa_t, ob_t, oc_t = S["o_t"]
    dya, dyb, dyc, doa, dob, doc, dhg, G["b_gate"] = _merge_bwd(
        dm, oa_t, ob_t, oc_t, S["hg"], P["bg"], W["w_branch_a"], W["w_branch_b"], W["w_branch_c"])
    G["w_branch_a"], = _mm(oa_t, dya, "tn", [F32], 256, 1024, 512, name="dw_branch_a")
    G["w_branch_b"], = _mm(ob_t, dyb, "tn", [F32], 512, 1024, 512, name="dw_branch_b")
    G["w_branch_c"], = _mm(oc_t, dyc, "tn", [F32], 256, 1024, 512, name="dw_branch_c")

    qa, ka, va, qb, kb, vb, qc, kc, vc = S["qkv"]
    oa, ob, oc = S["o"]
    lse_a, lse_b, lse_c = S["lse"]
    dqa, dka, dva = _dilated_bwd(doa, oa, lse_a, qa)
    qT_b, k_b, kT_b, v_b = qb
    dobT = _chunked_t(dob, ob.shape[-1])
    dqbT, dkb8, dvb8, landed = _attn_full_bwd(qT_b, k_b, kT_b, v_b, dobT, lse_b, _attn_delta(dobT, ob), scatter)
    group_sum = lambda t: t.reshape(k_b.shape[0], -1, t.shape[1], HD).sum(1)
    dqb, dkb, dvb = _unchunk_t(dqbT), _from_heads(group_sum(dkb8)), _from_heads(group_sum(dvb8))
    bias_c = S["bias_c"]
    docT = _chunked_t(doc, NBR_Q)
    dqcT, dkc, dvc, dbias_c = _nbr_bwd(qc, kc, vc, docT, lse_c, _attn_delta(docT, oc), bias_c)
    G["rpb_c"] = _nbr_rpb_grad(dbias_c, dx2.shape[0])

    tab_a, tab_b = tabs
    grads = [dqa, dka, dva, dqb, dkb, dvb, _unchunk_t(dqcT), _from_heads(dkc), _from_heads(dvc)]
    dhq, dqn, dkn = _prep_bwd(S["hq"], grads, tab_a, tab_b, P["qn"], P["kn"], gm)
    G["q_norm_b"] = dqn.reshape(BQ_W // HD, HD).sum(0)
    G["k_norm_b"] = dkn.reshape(BKV_W // HD, HD).sum(0)
    dw_qkv, = _mm(S["xb"], dhq, "tn", [F32], 1024, 768, 512, name="dw_qkv")
    dw_gate, = _mm(S["xb"], dhg, "tn", [F32], 1024, 1024, 512, name="dw_gate")
    G["w_in"] = jnp.concatenate([dw_qkv, dw_gate], axis=1)
    dx_a, = _mm(dhq, W["w_qkv"], "nt", [F32], 1024, 1024, 768,
                epilogue=lambda acc, d: (ALPHA * d + acc,), extras=(du1,), name="dx_qkv")
    dx, = _mm(dhg, W["w_gate"], "nt", [F32], 1024, 1024, 1024,
              epilogue=lambda acc, d: (d + acc,), extras=(dx_a,), name="dx_gate")
    return dx, G, landed


def _pack_small(vals):
    flat = jnp.concatenate([vals[n].reshape(-1).astype(F32) for n in SMALL])
    pad = (-flat.shape[0]) % (8 * 128)
    return jnp.pad(flat, (0, pad)).reshape(-1, 128)


def _unpack_small(packed, like):
    flat, out, off = packed.reshape(-1), {}, 0
    for n in SMALL:
        size = math.prod(like[n].shape)
        out[n] = flat[off:off + size].reshape(like[n].shape)
        off += size
    return out


def kernel(x, w_in, b_gate, q_norm_b, k_norm_b, rpb_c, w_branch_a, w_branch_b, w_branch_c, w_out, ln1_g, ln1_b, w_up, w_down, ln2_g, ln2_b, loss_target, m_w_in, m_b_gate, m_q_norm_b, m_k_norm_b, m_rpb_c, m_w_branch_a, m_w_branch_b, m_w_branch_c, m_w_out, m_ln1_g, m_ln1_b, m_w_up, m_w_down, m_ln2_g, m_ln2_b, v_w_in, v_b_gate, v_q_norm_b, v_k_norm_b, v_rpb_c, v_w_branch_a, v_w_branch_b, v_w_branch_c, v_w_out, v_ln1_g, v_ln1_b, v_w_up, v_w_down, v_ln2_g, v_ln2_b):
    w = dict(w_in=w_in, b_gate=b_gate, q_norm_b=q_norm_b, k_norm_b=k_norm_b, rpb_c=rpb_c,
             w_branch_a=w_branch_a, w_branch_b=w_branch_b, w_branch_c=w_branch_c, w_out=w_out,
             ln1_g=ln1_g, ln1_b=ln1_b, w_up=w_up, w_down=w_down, ln2_g=ln2_g, ln2_b=ln2_b)
    m = dict(w_in=m_w_in, b_gate=m_b_gate, q_norm_b=m_q_norm_b, k_norm_b=m_k_norm_b, rpb_c=m_rpb_c,
             w_branch_a=m_w_branch_a, w_branch_b=m_w_branch_b, w_branch_c=m_w_branch_c, w_out=m_w_out,
             ln1_g=m_ln1_g, ln1_b=m_ln1_b, w_up=m_w_up, w_down=m_w_down, ln2_g=m_ln2_g, ln2_b=m_ln2_b)
    v = dict(w_in=v_w_in, b_gate=v_b_gate, q_norm_b=v_q_norm_b, k_norm_b=v_k_norm_b, rpb_c=v_rpb_c,
             w_branch_a=v_w_branch_a, w_branch_b=v_w_branch_b, w_branch_c=v_w_branch_c, w_out=v_w_out,
             ln1_g=v_ln1_g, ln1_b=v_ln1_b, w_up=v_w_up, w_down=v_w_down, ln2_g=v_ln2_g, ln2_b=v_ln2_b)
    T = x.shape[1]
    xc, yc, cc = _position()

    flat2 = lambda a: a.reshape(-1, a.shape[-1])
    shards = {n: w[n].astype(BF) for n in BIG}
    first = _all_gather([shards[n][0] for n in BIG], "gather_layer0")
    later = [flat2(shards[n][1:]) for n in BIG]

    tabs = _rope_tables(T)
    gm = _group_mean_matrix()

    def whole(blocks):
        W = {n: _full_weight(blk, n) for n, blk in zip(BIG, blocks)}
        W["w_qkv"], W["w_gate"] = W["w_in"][:, :QKV], W["w_in"][:, QKV:]
        return W

    Ps = [dict(qn=jnp.tile(q_norm_b[l][None], (1, 2)), kn=jnp.tile(k_norm_b[l][None], (1, 2)),
               rpb=rpb_c[l], bg=b_gate[l][None], ln1_g=ln1_g[l][None], ln1_b=ln1_b[l][None],
               ln2_g=ln2_g[l][None], ln2_b=ln2_b[l][None]) for l in range(DEPTH)]

    h = x[0]
    hb = h.astype(BF)
    Ws, saved = [whole(first)], []
    for l in range(DEPTH):
        h, hb, S, gathered = _layer_fwd(h, hb, Ws[l], Ps[l], tabs, gm, later if l == 0 else ())
        saved.append(S)
        if l == 0:
            rest = [g.reshape(8, DEPTH - 1, -1, g.shape[2]) for g in gathered]
            Ws += [whole([g[:, i] for g in rest]) for i in range(DEPTH - 1)]
    dy, sq = _loss_grad(h, loss_target[0])
    loss = lax.psum(0.5 / D * jnp.sum(sq), AXES)

    grads, chunks, landed = [None] * DEPTH, [None] * DEPTH, [None] * DEPTH
    for l in reversed(range(DEPTH)):
        dy, grads[l], arrived = _layer_bwd(dy, saved[l], Ws[l], Ps[l], tabs, gm, chunks[l + 1] if l + 1 < DEPTH else ())
        if l + 1 < DEPTH:
            landed[l + 1] = arrived
        chunks[l] = [_chunks(grads[l][n], n).astype(BF) for n in BIG]
    landed[0] = _scatter(chunks[0], "scatter_layer0")
    grad_x = dy[None]
    me = (4 * xc + 2 * yc + cc).reshape(1).astype(jnp.int32)
    out_g, out_d, out_m, out_v = {}, {}, {}, {}
    for a, n in enumerate(BIG):
        g = jnp.concatenate([_sum8(chunks[l][a], landed[l][a], me) for l in range(DEPTH)], axis=0)
        res = (g,) + tuple(_adamw(g, flat2(w[n]), flat2(m[n]), flat2(v[n])))
        out_g[n], out_d[n], out_m[n], out_v[n] = [t.reshape(w[n].shape) for t in res]

    part = _pack_small({n: jnp.stack([grads[l][n].reshape(w[n].shape[1:]) for l in range(DEPTH)]) for n in SMALL})
    parts, = _all_gather([part], "gather_small_grads")
    res = _small_sum_adamw(parts, _pack_small(w), _pack_small(m), _pack_small(v))
    for dst, packed in zip((out_g, out_d, out_m, out_v), res):
        dst.update(_unpack_small(packed, w))

    return (loss, grad_x, *[out_g[n] for n in NAMES], *[out_d[n] for n in NAMES],
            *[out_m[n] for n in NAMES], *[out_v[n] for n in NAMES])
```

```python
import functools
import math

import numpy as np
import jax
import jax.numpy as jnp
from jax import lax
from jax.experimental import pallas as pl
from jax.experimental.pallas import tpu as pltpu

F32 = jnp.float32
BF = jnp.bfloat16
HI = lax.Precision.HIGHEST
NEG = -1e30
MESH = pl.DeviceIdType.MESH
AXES = ("x", "y", "c")

D = 1024
DEPTH = 4
HD = 64
A_W, BQ_W, BKV_W, C_W = 256, 512, 128, 256
QKV = 2304
GATE = 3072
D_FF = 4096
GRID_W = 64
ALPHA = (2 * DEPTH) ** 0.25
LN_EPS = 1e-5
RMS_EPS = 1e-6
SCALE = HD ** -0.5
ROPE_THETA = 500000.0
AXIAL_THETA = 10000.0
A_CONFIGS = ((128, 1), (512, 4), (2048, 16))
LR, B1, B2, EPS, WD, STEP = 0.001, 0.9, 0.999, 1e-08, 0.01, 10

VMEM_LIMIT = 56 * 1024 * 1024
BAND_T = 256


def _pcall(body, **kw):
    return pl.pallas_call(body, **kw)


def _pcall_comm(body, **kw):
    return pl.pallas_call(body, **kw)


def _cp(*sem):
    return pltpu.CompilerParams(dimension_semantics=sem, vmem_limit_bytes=VMEM_LIMIT)


def _sds(shape, dtype):
    return jax.ShapeDtypeStruct(shape, dtype)


def _mm(a, b, dims, outs, tm, tn, tk, epilogue=None, extras=(), name="mm"):
    if dims == "tn":
        K, M = a.shape
    else:
        M, K = a.shape
    N = b.shape[0] if dims == "nt" else b.shape[1]
    tm, tn, tk = min(tm, M), min(tn, N), min(tk, K)
    assert M % tm == 0 and N % tn == 0 and K % tk == 0, (name, M, N, K, tm, tn, tk)
    nk = K // tk
    ne, no = len(extras), len(outs)

    def body(a_ref, b_ref, *rest):
        extra_refs, out_refs = rest[:ne], rest[ne:ne + no]
        av, bv = a_ref[...].astype(BF), b_ref[...].astype(BF)
        if dims == "nn":
            p = jnp.dot(av, bv, preferred_element_type=F32)
        elif dims == "nt":
            p = lax.dot_general(av, bv, (((1,), (1,)), ((), ())), preferred_element_type=F32)
        else:
            p = lax.dot_general(av, bv, (((0,), (0,)), ((), ())), preferred_element_type=F32)

        def finish(acc):
            res = epilogue(acc, *[r[...] for r in extra_refs]) if epilogue else (acc,)
            for o, r in zip(out_refs, res):
                o[...] = r.astype(o.dtype)

        if nk == 1:
            finish(p)
        else:
            acc_ref = rest[-1]
            k = pl.program_id(2)

            @pl.when(k == 0)
            def _():
                acc_ref[...] = p

            @pl.when(k > 0)
            def _():
                acc_ref[...] += p

            @pl.when(k == nk - 1)
            def _():
                finish(acc_ref[...])

    if dims == "tn":
        a_spec = pl.BlockSpec((tk, tm), lambda i, j, k: (k, i))
    else:
        a_spec = pl.BlockSpec((tm, tk), lambda i, j, k: (i, k))
    if dims == "nt":
        b_spec = pl.BlockSpec((tn, tk), lambda i, j, k: (j, k))
    else:
        b_spec = pl.BlockSpec((tk, tn), lambda i, j, k: (k, j))
    o_spec = pl.BlockSpec((tm, tn), lambda i, j, k: (i, j))
    res = _pcall(
        body, name=name, grid=(M // tm, N // tn, nk),
        in_specs=[a_spec, b_spec] + [o_spec] * ne,
        out_specs=[o_spec] * no,
        out_shape=[_sds((M, N), dt) for dt in outs],
        scratch_shapes=[pltpu.VMEM((tm, tn), F32)] if nk > 1 else [],
        compiler_params=_cp("parallel", "parallel", "arbitrary"),
    )(a, b, *extras)
    return res


def _rope_tables(T):
    pos = jnp.arange(T)

    def cs(p, theta, half):
        inv = theta ** (-jnp.arange(half, dtype=F32) / half)
        ang = p.astype(F32)[:, None] * inv[None, :]
        return jnp.cos(ang), jnp.sin(ang)

    ca, sa = cs(pos, ROPE_THETA, 8)
    one, zero, z8 = jnp.ones((T, 48), F32), jnp.zeros((T, 48), F32), jnp.zeros((T, 8), F32)
    tab_a = [jnp.concatenate(t, 1) for t in ([ca, ca, one], [-sa, z8, zero], [z8, sa, zero])]
    cr, sr = cs(pos // GRID_W, AXIAL_THETA, 16)
    cc, sc = cs(pos % GRID_W, AXIAL_THETA, 16)
    z16 = jnp.zeros((T, 16), F32)
    tab_b = [jnp.concatenate(t, 1) for t in ([cr, cr, cc, cc], [-sr, z16, -sc, z16], [z16, sr, z16, sc])]
    return [jnp.tile(t, (1, 2)) for t in tab_a], [jnp.tile(t, (1, 2)) for t in tab_b]


def _rot(x, C, S1, S2, k):
    return x * C + pltpu.roll(x, 128 - k, 1) * S1 + pltpu.roll(x, k, 1) * S2


def _rot_t(d, C, S1, S2, k):
    return d * C + pltpu.roll(d * S1, k, 1) + pltpu.roll(d * S2, 128 - k, 1)


def _group_mean_matrix():
    m = np.zeros((128, 128), np.float32)
    m[:64, :64] = 1.0 / 64
    m[64:, 64:] = 1.0 / 64
    return jnp.asarray(m)


def _prep_fwd(hq, tab_a, tab_b, qn, kn, gm):
    T = hq.shape[0]
    tt = min(256, T)
    widths = [A_W, A_W, A_W, BQ_W, BKV_W, BKV_W, C_W, C_W, C_W]

    def body(h_ref, ca, s1a, s2a, cb, s1b, s2b, qn_ref, kn_ref, gm_ref,
             qa, ka, va, qb, kb, vb, qc, kc, vc):
        def col(off, j):
            return h_ref[:, off + 128 * j: off + 128 * (j + 1)]

        for j in range(2):
            sl = slice(128 * j, 128 * (j + 1))
            qa[:, sl] = (_rot(col(0, j), ca[...], s1a[...], s2a[...], 8) * SCALE).astype(qa.dtype)
            ka[:, sl] = _rot(col(256, j), ca[...], s1a[...], s2a[...], 8).astype(ka.dtype)
            va[:, sl] = col(512, j).astype(va.dtype)
            qc[:, sl] = (col(1536, j) * SCALE).astype(qc.dtype)
            kc[:, sl] = col(1792, j).astype(kc.dtype)
            vc[:, sl] = col(2048, j).astype(vc.dtype)

        def normed(x, w):
            ms = jnp.dot(x * x, gm_ref[...], precision=HI, preferred_element_type=F32)
            return x * lax.rsqrt(ms + RMS_EPS) * w

        for j in range(4):
            y = normed(col(768, j), qn_ref[...])
            qb[:, 128 * j:128 * (j + 1)] = (_rot(y, cb[...], s1b[...], s2b[...], 16) * SCALE).astype(qb.dtype)
        y = normed(col(1280, 0), kn_ref[...])
        kb[...] = _rot(y, cb[...], s1b[...], s2b[...], 16).astype(kb.dtype)
        vb[...] = col(1408, 0).astype(vb.dtype)

    row = lambda w: pl.BlockSpec((tt, w), lambda i: (i, 0))
    const = lambda s: pl.BlockSpec(s, lambda i: (0, 0))
    return _pcall(
        body, name="prep_fwd", grid=(T // tt,),
        in_specs=[row(QKV)] + [row(128)] * 6 + [const((1, 128))] * 2 + [const((128, 128))],
        out_specs=[row(w) for w in widths],
        out_shape=[_sds((T, w), BF) for w in widths],
        compiler_params=_cp("parallel"),
    )(hq, *tab_a, *tab_b, qn, kn, gm)


def _prep_bwd(hq, grads, tab_a, tab_b, qn, kn, gm):
    T = hq.shape[0]
    tt = min(256, T)
    widths = [A_W, A_W, A_W, BQ_W, BKV_W, BKV_W, C_W, C_W, C_W]

    def body(h_ref, dqa, dka, dva, dqb, dkb, dvb, dqc, dkc, dvc,
             ca, s1a, s2a, cb, s1b, s2b, qn_ref, kn_ref, gm_ref, dh, dqn, dkn):
        i = pl.program_id(0)

        @pl.when(i == 0)
        def _():
            dqn[...] = jnp.zeros_like(dqn)
            dkn[...] = jnp.zeros_like(dkn)

        def put(off, j, val):
            dh[:, off + 128 * j: off + 128 * (j + 1)] = val.astype(dh.dtype)

        for j in range(2):
            sl = slice(128 * j, 128 * (j + 1))
            put(0, j, _rot_t(dqa[:, sl] * SCALE, ca[...], s1a[...], s2a[...], 8))
            put(256, j, _rot_t(dka[:, sl], ca[...], s1a[...], s2a[...], 8))
            put(512, j, dva[:, sl])
            put(1536, j, dqc[:, sl] * SCALE)
            put(1792, j, dkc[:, sl])
            put(2048, j, dvc[:, sl])

        def norm_bwd(x, w, e):
            ms = jnp.dot(x * x, gm_ref[...], precision=HI, preferred_element_type=F32)
            r = lax.rsqrt(ms + RMS_EPS)
            n = x * r
            dn = e * w
            proj = jnp.dot(dn * n, gm_ref[...], precision=HI, preferred_element_type=F32)
            return r * (dn - n * proj), jnp.sum(e * n, axis=0, keepdims=True)

        for j in range(4):
            sl = slice(128 * j, 128 * (j + 1))
            e = _rot_t(dqb[:, sl] * SCALE, cb[...], s1b[...], s2b[...], 16)
            dx, dw = norm_bwd(h_ref[:, 768 + 128 * j: 768 + 128 * (j + 1)], qn_ref[...], e)
            put(768, j, dx)
            dqn[:, sl] += dw
        e = _rot_t(dkb[...], cb[...], s1b[...], s2b[...], 16)
        dx, dw = norm_bwd(h_ref[:, 1280:1408], kn_ref[...], e)
        put(1280, 0, dx)
        dkn[...] += dw
        put(1408, 0, dvb[...])

    row = lambda w: pl.BlockSpec((tt, w), lambda i: (i, 0))
    const = lambda s: pl.BlockSpec(s, lambda i: (0, 0))
    return _pcall(
        body, name="prep_bwd", grid=(T // tt,),
        in_specs=[row(QKV)] + [row(w) for w in widths] + [row(128)] * 6
        + [const((1, 128))] * 2 + [const((128, 128))],
        out_specs=[row(QKV), const((1, BQ_W)), const((1, BKV_W))],
        out_shape=[_sds((T, QKV), BF), _sds((1, BQ_W), F32), _sds((1, BKV_W), F32)],
        compiler_params=_cp("arbitrary"),
    )(hq, *grads, *tab_a, *tab_b, qn, kn, gm)


def _to_heads(x):
    T, W = x.shape
    return x.reshape(T, W // HD, HD).transpose(1, 0, 2)


def _from_heads(x):
    H, T, _ = x.shape
    return x.transpose(1, 0, 2).reshape(T, H * HD)


def _rows(x, t):
    H, T, _ = x.shape
    return x.reshape(H, T // t, 1, t)


def _nt(a, b):
    return lax.dot_general(a, b, (((1,), (1,)), ((), ())), preferred_element_type=F32)


def _chunked_t(x, t):
    T, W = x.shape
    return x.reshape(T // t, t, W // HD, HD).transpose(2, 0, 3, 1)


def _unchunk_t(x):
    H, n, _, t = x.shape
    return x.transpose(1, 3, 0, 2).reshape(n * t, H * HD)


FULL_T = 512
FULL_TK = 512
FULL_HEADS = 1
V_ROWS = 72


def _phased(phases, grid):
    step = pl.program_id(0) * grid[1] + pl.program_id(1)
    last = grid[0] * grid[1] - 1
    for p, phase in enumerate(phases):
        pl.when(step == p * last // (len(phases) - 1))(phase)


def _attn_full_fwd(qT, k, vT1, gather=()):
    Hq, nq, _, t = qT.shape
    Hk, T, _ = k.shape
    G = Hq // Hk
    nk, tk = vT1.shape[1], vT1.shape[3]

    HB = FULL_HEADS
    assert G % HB == 0
    ng = len(gather)
    grid = (Hq // HB, nq)

    def body(q_ref, k_ref, v_ref, *rest):
        x_refs, (o_ref, lse_ref), rest = rest[:ng], rest[ng:ng + 2], rest[ng + 2:]
        out_refs, acc_refs, sems = rest[:ng], rest[ng:ng + HB], rest[ng + HB:]
        if ng:
            _phased(_gather_phases(x_refs, out_refs, *sems), grid)
        for acc_ref in acc_refs:
            acc_ref[...] = jnp.zeros((V_ROWS, t), F32)

        def scores(j, b):
            sT = jnp.dot(k_ref[pl.ds(pl.multiple_of(j * tk, tk), tk), :], q_ref[b], preferred_element_type=F32)
            return sT, jnp.max(sT, axis=0, keepdims=True)

        def update(j, b, scored, m_old):
            sT, m_tile = scored
            m_new = jnp.maximum(m_old, m_tile)
            pT = jnp.exp(sT - m_new).astype(BF)
            acc_refs[b][...] = (jnp.exp(m_old - m_new) * acc_refs[b][...]
                                + jnp.dot(v_ref[j], pT, preferred_element_type=F32))
            return m_new

        def step(j, carry):
            ms, ss = carry
            nxt = jnp.minimum(j + 1, nk - 1)
            new_s = tuple(scores(nxt, b) for b in range(HB))
            new_m = tuple(update(j, b, ss[b], ms[b]) for b in range(HB))
            return new_m, new_s

        init = (tuple(jnp.full((1, t), NEG, F32) for _ in range(HB)), tuple(scores(0, b) for b in range(HB)))
        ms, _ = lax.fori_loop(0, nk, step, init)
        for b in range(HB):
            l = acc_refs[b][pl.ds(HD, 1), :]
            o_ref[b] = (acc_refs[b][pl.ds(0, HD), :] / l).astype(o_ref.dtype)
            lse_ref[b] = ms[b] + jnp.log(l)

    qs = pl.BlockSpec((HB, None, HD, t), lambda h, i: (h, i, 0, 0))
    res = (_pcall_comm if ng else _pcall)(
        body, name="attn_full_fwd_gather" if ng else "attn_full_fwd", grid=grid,
        in_specs=[qs, pl.BlockSpec((None, T, HD), lambda h, i: (h * HB // G, 0, 0)),
                  pl.BlockSpec((None, nk, V_ROWS, tk), lambda h, i: (h * HB // G, 0, 0, 0))] + [HBM] * ng,
        out_specs=[qs, pl.BlockSpec((HB, None, 1, t), lambda h, i: (h, i, 0, 0))] + [HBM] * ng,
        out_shape=[_sds((Hq, nq, HD, t), BF), _sds((Hq, nq, 1, t), F32)]
        + [_sds((8,) + x.shape, x.dtype) for x in gather],
        scratch_shapes=[pltpu.VMEM((V_ROWS, t), F32)] * HB + (_gather_scratch(ng) if ng else []),
        compiler_params=_cp("arbitrary", "arbitrary") if ng else _cp("parallel", "parallel"),
    )(qT, k, vT1, *gather)
    return res[0], res[1], list(res[2:])


def _attn_delta(doT, oT):
    Hq, nq, _, t = doT.shape

    def body(do_ref, o_ref, dl_ref):
        dl_ref[...] = jnp.sum(do_ref[...].astype(F32) * o_ref[...].astype(F32), axis=1, keepdims=True)

    qs = pl.BlockSpec((None, nq, HD, t), lambda h: (h, 0, 0, 0))
    rs = pl.BlockSpec((None, nq, 1, t), lambda h: (h, 0, 0, 0))
    return _pcall(body, name="attn_delta", grid=(Hq,), in_specs=[qs, qs], out_specs=rs,
                  out_shape=_sds((Hq, nq, 1, t), F32), compiler_params=_cp("parallel"))(doT, oT)


def _attn_full_bwd(qT, k, kT, v, doT, lse, delta, scatter=()):
    Hq, nq, _, t = qT.shape
    Hk, T, _ = k.shape
    G = Hq // Hk
    nkv = T // t
    ns = len(scatter)

    def body(qT_ref, doT_ref, lse_ref, dl_ref, k_ref, kT_ref, v_ref, *rest):
        g_refs, (dq_ref, dk_ref, dv_ref), rest = rest[:ns], rest[ns:ns + 3], rest[ns + 3:]
        land_refs, (dq_acc, dk_acc, dv_acc), sems = rest[:ns], rest[ns:ns + 3], rest[ns + 3:]
        if ns:
            _phased(_scatter_phases(g_refs, land_refs, *sems), (Hq, nkv))
        j = pl.program_id(1)

        @pl.when(j == 0)
        def _():
            dq_acc[...] = jnp.zeros_like(dq_acc)

        kv, kTv, vv = k_ref[...], kT_ref[...], v_ref[...]
        dk_acc[...] = jnp.zeros((t, HD), F32)
        dv_acc[...] = jnp.zeros((t, HD), F32)

        def step(i, carry):
            qT, doT = qT_ref[i], doT_ref[i]
            pT = jnp.exp(jnp.dot(kv, qT, preferred_element_type=F32) - lse_ref[i])
            dv_acc[...] += _nt(pT.astype(BF), doT)
            dsT = (pT * (jnp.dot(vv, doT, preferred_element_type=F32) - dl_ref[i])).astype(BF)
            dk_acc[...] += _nt(dsT, qT)
            dq_acc[i] += jnp.dot(kTv, dsT, preferred_element_type=F32)
            return carry

        lax.fori_loop(0, nq, step, 0)
        dk_ref[...] = dk_acc[...].astype(dk_ref.dtype)
        dv_ref[...] = dv_acc[...].astype(dv_ref.dtype)

        @pl.when(j == nkv - 1)
        def _():
            dq_ref[...] = dq_acc[...].astype(dq_ref.dtype)

    chk = pl.BlockSpec((None, nq, HD, t), lambda h, j: (h, 0, 0, 0))
    row = pl.BlockSpec((None, nq, 1, t), lambda h, j: (h, 0, 0, 0))
    kvs = pl.BlockSpec((None, t, HD), lambda h, j: (h // G, j, 0))
    out = pl.BlockSpec((None, t, HD), lambda h, j: (h, j, 0))
    res = (_pcall_comm if ns else _pcall)(
        body, name="attn_full_bwd_scatter" if ns else "attn_full_bwd", grid=(Hq, nkv),
        in_specs=[chk, chk, row, row, kvs, pl.BlockSpec((None, HD, t), lambda h, j: (h // G, 0, j)), kvs] + [HBM] * ns,
        out_specs=[chk, out, out] + [HBM] * ns,
        out_shape=[_sds((Hq, nq, HD, t), BF), _sds((Hq, T, HD), BF), _sds((Hq, T, HD), BF)]
        + [_sds((7,) + g.shape[1:], g.dtype) for g in scatter],
        scratch_shapes=[pltpu.VMEM((nq, HD, t), F32), pltpu.VMEM((t, HD), F32), pltpu.VMEM((t, HD), F32)]
        + (_scatter_scratch(ns) if ns else []),
        compiler_params=_cp("arbitrary" if ns else "parallel", "arbitrary"),
    )(qT, doT, lse, delta, k, kT, v, *scatter)
    return res[0], res[1], res[2], list(res[3:])


DIL_Q = 128
DIL_K = 256
DIL_R = 64


def _vh(x, d):
    T = x.shape[0]
    return x.reshape(T // d, d, 4, HD).transpose(1, 2, 0, 3).reshape(4 * d, T // d, HD)


def _vh_inv(y, d):
    L = y.shape[1]
    return y.reshape(d, 4, L, HD).transpose(2, 0, 1, 3).reshape(L * d, 4 * HD)


def _vh_chunks(x, d):
    y = _vh(x, d)
    return y.reshape(y.shape[0], y.shape[1] // DIL_Q, DIL_Q, HD).transpose(0, 1, 3, 2)


def _chunks_to_dims(c, d):
    _, nq, R, _ = c.shape
    return c.reshape(d, 4, nq, R, DIL_Q).transpose(1, 3, 2, 4, 0).reshape(4, R, nq * DIL_Q * d)


def _dims_to_chunks(x, d):
    _, R, T = x.shape
    nq = T // (DIL_Q * d)
    return x.reshape(4, R, nq, DIL_Q, d).transpose(4, 0, 2, 1, 3).reshape(4 * d, nq, R, DIL_Q)


def _chunks_to_tokens(c, d):
    _, nq, _, _ = c.shape
    return c.reshape(d, 4, nq, HD, DIL_Q).transpose(2, 4, 0, 1, 3).reshape(nq * DIL_Q * d, 4 * HD)


def _dil_window(i, L):
    start = pl.multiple_of(jnp.clip(i * DIL_Q - DIL_R, 0, L - DIL_K), DIL_R)
    kk = start + lax.broadcasted_iota(jnp.int32, (DIL_K, 1), 0)
    qq = i * DIL_Q + lax.broadcasted_iota(jnp.int32, (1, DIL_Q), 1)
    return start, jnp.abs(kk - qq) <= DIL_R


def _tn(a, b):
    return lax.dot_general(a, b, (((0,), (0,)), ((), ())), preferred_element_type=F32)


def _dil_fwd(qT, k, v):
    V, nq, _, _ = qT.shape
    L = k.shape[1]
    assert L >= DIL_K

    unroll = 4 if nq % 4 == 0 else 1

    def body(q_ref, k_ref, v_ref, o_ref, lse_ref):
        def tile(i):
            start, mask = _dil_window(i, L)
            win = pl.ds(start, DIL_K)
            sT = jnp.where(mask, jnp.dot(k_ref[win, :], q_ref[i], preferred_element_type=F32), NEG)
            m = jnp.max(sT, axis=0, keepdims=True)
            pT = jnp.exp(sT - m).astype(BF)
            acc = _tn(v_ref[win, :], pT)
            l = jnp.max(acc[HD:HD + 8], axis=0, keepdims=True)
            o_ref[i] = acc[:HD] / l
            lse_ref[i] = m + jnp.log(l)

        def tiles(ii, carry):
            for u in range(unroll):
                tile(ii * unroll + u)
            return carry

        lax.fori_loop(0, nq // unroll, tiles, 0)

    chk = lambda r, dt: (pl.BlockSpec((None, nq, r, DIL_Q), lambda h: (h, 0, 0, 0)), _sds((V, nq, r, DIL_Q), dt))
    tok = lambda w: pl.BlockSpec((None, L, w), lambda h: (h, 0, 0))
    (o_spec, o_shape), (l_spec, l_shape) = chk(HD, F32), chk(1, F32)
    return _pcall(
        body, name=f"dil_fwd_{V // 4}", grid=(V,),
        in_specs=[chk(HD, BF)[0], tok(HD), tok(2 * HD)],
        out_specs=[o_spec, l_spec], out_shape=[o_shape, l_shape],
        compiler_params=_cp("parallel"),
    )(qT, k, v)


def _dil_merge(os_, lses):
    _, _, T = os_[0].shape
    tt = min(1024, T)
    n = len(os_)

    def body(*refs):
        o_refs, l_refs, (o_out, l_out) = refs[:n], refs[n:2 * n], refs[2 * n:]
        m = l_refs[0][...]
        for r in l_refs[1:]:
            m = jnp.maximum(m, r[...])
        ws = [jnp.exp(r[...] - m) for r in l_refs]
        tot = ws[0]
        for w_ in ws[1:]:
            tot = tot + w_
        acc = ws[0] * o_refs[0][...]
        for w_, o in zip(ws[1:], o_refs[1:]):
            acc = acc + w_ * o[...]
        o_out[...] = (acc / tot).astype(o_out.dtype)
        l_out[...] = m + jnp.log(tot)

    os_spec = pl.BlockSpec((None, HD, tt), lambda h, i: (h, 0, i))
    ls_spec = pl.BlockSpec((None, 1, tt), lambda h, i: (h, 0, i))
    return _pcall(
        body, name="dil_merge", grid=(4, T // tt),
        in_specs=[os_spec] * n + [ls_spec] * n, out_specs=[os_spec, ls_spec],
        out_shape=[_sds((4, HD, T), BF), _sds((4, 1, T), F32)],
        compiler_params=_cp("parallel", "parallel"),
    )(*os_, *lses)


def _dims_delta(doT, oT):
    _, _, T = doT.shape
    tt = min(2048, T)

    def body(do_ref, o_ref, dl_ref):
        dl_ref[...] = jnp.sum(do_ref[...].astype(F32) * o_ref[...].astype(F32), axis=0, keepdims=True)

    spec = pl.BlockSpec((None, HD, tt), lambda h, i: (h, 0, i))
    return _pcall(body, name="dims_delta", grid=(4, T // tt), in_specs=[spec, spec],
                  out_specs=pl.BlockSpec((None, 1, tt), lambda h, i: (h, 0, i)),
                  out_shape=_sds((4, 1, T), F32), compiler_params=_cp("parallel", "parallel"))(doT, oT)


def _dil_bwd(qT, k, v, doT, lse, delta):
    V, nq, _, _ = qT.shape
    L = k.shape[1]
    unroll = 2 if nq % 2 == 0 else 1

    def body(qT_ref, k_ref, v_ref, doT_ref, lse_ref, dl_ref, dq_ref, dk_ref, dv_ref, dk_acc, dv_acc):
        dk_acc[...] = jnp.zeros_like(dk_acc)
        dv_acc[...] = jnp.zeros_like(dv_acc)

        def tile(i):
            start, mask = _dil_window(i, L)
            win = pl.ds(start, DIL_K)
            kw, qT, doT = k_ref[win, :], qT_ref[i], doT_ref[i]
            sT = jnp.where(mask, jnp.dot(kw, qT, preferred_element_type=F32), NEG)
            pT = jnp.exp(sT - lse_ref[i])
            dv = _nt(pT.astype(BF), doT)
            dsT = (pT * (jnp.dot(v_ref[win, :], doT, preferred_element_type=F32) - dl_ref[i])).astype(BF)
            dk = _nt(dsT, qT)
            dq_ref[i] = _tn(kw, dsT).astype(dq_ref.dtype)
            return win, dk, dv

        def tiles(ii, carry):
            done = [tile(ii * unroll + u) for u in range(unroll)]
            for win, dk, dv in done:
                dk_acc[win, :] += dk
                dv_acc[win, :] += dv
            return carry

        lax.fori_loop(0, nq // unroll, tiles, 0)
        dk_ref[...] = dk_acc[...].astype(dk_ref.dtype)
        dv_ref[...] = dv_acc[...].astype(dv_ref.dtype)

    chk = lambda r: pl.BlockSpec((None, nq, r, DIL_Q), lambda h: (h, 0, 0, 0))
    tok = pl.BlockSpec((None, L, HD), lambda h: (h, 0, 0))
    return _pcall(
        body, name=f"dil_bwd_{V // 4}", grid=(V,),
        in_specs=[chk(HD), tok, tok, chk(HD), chk(1), chk(1)],
        out_specs=[chk(HD), tok, tok],
        out_shape=[_sds((V, nq, HD, DIL_Q), BF), _sds((V, L, HD), BF), _sds((V, L, HD), BF)],
        scratch_shapes=[pltpu.VMEM((L, HD), F32)] * 2,
        compiler_params=_cp("parallel"),
    )(qT, k, v, doT, lse, delta)


def _dilated_fwd(qa, ka, va):
    outs, lses, saved = [], [], []
    for _, d in A_CONFIGS:
        qT, k, v = _vh_chunks(qa, d), _vh(ka, d), _vh(va, d)
        oT, lse = _dil_fwd(qT, k, jnp.concatenate([v, jnp.ones_like(v)], axis=2))
        outs.append(_chunks_to_dims(oT, d))
        lses.append(_chunks_to_dims(lse, d))
        saved.append((qT, k, v))
    o, lse = _dil_merge(outs, lses)
    return o, lse, saved


def _dilated_bwd(do_t, o, lse, saved):
    doT = do_t.reshape(do_t.shape[0], 4, HD).transpose(1, 2, 0)
    delta = _dims_delta(doT, o)
    parts = []
    for (_, d), (qT, k, v) in zip(A_CONFIGS, saved):
        dqT, dk_c, dv_c = _dil_bwd(qT, k, v, _dims_to_chunks(doT, d), _dims_to_chunks(lse, d), _dims_to_chunks(delta, d))
        parts.append((_chunks_to_tokens(dqT, d), _vh_inv(dk_c, d), _vh_inv(dv_c, d)))
    total = lambda ts: sum(t.astype(F32) for t in ts).astype(do_t.dtype)
    return tuple(total(ts) for ts in zip(*parts))


NBR_Q = 128
NBR_ROWS = 10
NBR_K = NBR_ROWS * GRID_W


def _nbr_class_tiles(nq):
    return [2, 0, 1, nq - 2, nq - 1]


def _nbr_geometry(T):
    rows, nq = T // GRID_W, T // NBR_Q
    assert rows >= NBR_ROWS + 4 and nq >= 5
    kr, kc = np.divmod(np.arange(NBR_K), GRID_W)
    qr, qc = np.divmod(np.arange(NBR_Q), GRID_W)
    c0 = np.clip(qc - 8, 0, GRID_W - 16)
    col_ok = (kc[:, None] >= c0[None, :]) & (kc[:, None] < c0[None, :] + 16)
    drs, valids = [], []
    for i in _nbr_class_tiles(nq):
        start_row = int(np.clip(2 * i - 4, 0, rows - NBR_ROWS))
        r = 2 * i + qr
        r0 = np.clip(r - 4, 0, rows - 8)
        rk = start_row + kr
        row_ok = (rk[:, None] >= r0[None, :]) & (rk[:, None] < r0[None, :] + 8)
        valids.append(row_ok & col_ok)
        dr = start_row + np.arange(NBR_ROWS)[:, None] - (2 * i + np.arange(2)[None, :]) + 7
        drs.append(np.where((dr >= 0) & (dr <= 14), dr, -1))
    return np.stack(drs), np.stack(valids)


def _nbr_fold_matrices(T):
    dr, _ = _nbr_geometry(T)
    e1 = np.zeros((GRID_W * GRID_W, 128), np.float32)
    kc, qc = np.meshgrid(np.arange(GRID_W), np.arange(GRID_W), indexing="ij")
    dc = (kc - qc + 15).reshape(-1)
    keep = (dc >= 0) & (dc <= 30)
    e1[np.arange(GRID_W * GRID_W)[keep], dc[keep]] = 1.0
    n = 5 * NBR_ROWS * 2
    e2 = np.zeros((64, 4 * n), np.float32)
    for h in range(4):
        for j, d in enumerate(dr.reshape(-1)):
            if d >= 0:
                e2[h * 16 + d, h * n + j] = 1.0
    return jnp.asarray(e1), jnp.asarray(e2)


def _nbr_bias_blocks(rpb, T):
    e1, e2 = _nbr_fold_matrices(T)
    _, valid = _nbr_geometry(T)
    padded = jnp.pad(rpb, ((0, 0), (0, 1), (0, 128 - rpb.shape[2]))).reshape(64, 128)

    def body(e2t_ref, rpb_ref, e1t_ref, out_ref):
        picked = jnp.dot(e2t_ref[...], rpb_ref[...], precision=HI, preferred_element_type=F32)
        out_ref[...] = jnp.dot(picked, e1t_ref[...], precision=HI, preferred_element_type=F32)

    sub = _pcall(body, name="rpb_expand", out_shape=_sds((e2.shape[1], GRID_W * GRID_W), F32),
                 compiler_params=pltpu.CompilerParams(vmem_limit_bytes=VMEM_LIMIT))(e2.T, padded, e1.T)
    blocks = sub.reshape(4, 5, NBR_ROWS, 2, GRID_W, GRID_W).transpose(0, 1, 2, 4, 3, 5).reshape(4, 5, NBR_K, NBR_Q)
    return jnp.where(valid[None], blocks, NEG)


def _nbr_rpb_grad(dbias, T):
    e1, e2 = _nbr_fold_matrices(T)
    sub = dbias.reshape(4, 5, NBR_ROWS, GRID_W, 2, GRID_W).transpose(0, 1, 2, 4, 3, 5).reshape(-1, GRID_W * GRID_W)

    def body(e2_ref, sub_ref, e1_ref, out_ref):
        diag = jnp.dot(sub_ref[...], e1_ref[...], precision=HI, preferred_element_type=F32)
        out_ref[...] = jnp.dot(e2_ref[...], diag, precision=HI, preferred_element_type=F32)

    out = _pcall(body, name="rpb_fold", out_shape=_sds((64, 128), F32),
                 compiler_params=pltpu.CompilerParams(vmem_limit_bytes=VMEM_LIMIT))(e2, sub, e1)
    return out.reshape(4, 16, 128)[:, :15, :31]


def _nbr_tile(i, nq, T):
    start = pl.multiple_of(jnp.clip(i * NBR_Q - 4 * GRID_W, 0, T - NBR_K), NBR_Q)
    cls = jnp.where(i == 0, 1, jnp.where(i == 1, 2, jnp.where(i == nq - 2, 3, jnp.where(i == nq - 1, 4, 0))))
    return start, cls


def _nbr_fwd(qT, k, v, bias):
    H, nq, _, _ = qT.shape
    T = k.shape[1]
    unroll = 2 if nq % 2 == 0 else 1

    def body(q_ref, k_ref, v_ref, b_ref, o_ref, lse_ref):
        def tile(i):
            start, cls = _nbr_tile(i, nq, T)
            win = pl.ds(start, NBR_K)
            sT = jnp.dot(k_ref[win, :], q_ref[i], preferred_element_type=F32) + b_ref[cls]
            m = jnp.max(sT, axis=0, keepdims=True)
            pT = jnp.exp(sT - m).astype(BF)
            acc = _tn(v_ref[win, :], pT)
            l = jnp.max(acc[HD:HD + 8], axis=0, keepdims=True)
            o_ref[i] = (acc[:HD] / l).astype(o_ref.dtype)
            lse_ref[i] = m + jnp.log(l)

        def tiles(ii, carry):
            for u in range(unroll):
                tile(ii * unroll + u)
            return carry

        lax.fori_loop(0, nq // unroll, tiles, 0)

    chk = lambda r: pl.BlockSpec((None, nq, r, NBR_Q), lambda h: (h, 0, 0, 0))
    tok = lambda w: pl.BlockSpec((None, T, w), lambda h: (h, 0, 0))
    return _pcall(
        body, name="nbr_fwd", grid=(H,),
        in_specs=[chk(HD), tok(HD), tok(2 * HD), pl.BlockSpec((None, 5, NBR_K, NBR_Q), lambda h: (h, 0, 0, 0))],
        out_specs=[chk(HD), chk(1)],
        out_shape=[_sds((H, nq, HD, NBR_Q), BF), _sds((H, nq, 1, NBR_Q), F32)],
        compiler_params=_cp("parallel"),
    )(qT, k, v, bias)


def _nbr_bwd(qT, k, v, doT, lse, delta, bias):
    H, nq, _, _ = qT.shape
    T = k.shape[1]

    def body(qT_ref, k_ref, v_ref, doT_ref, lse_ref, dl_ref, b_ref, dq_ref, dk_ref, dv_ref, db_ref, dk_acc, dv_acc):
        dk_acc[...] = jnp.zeros_like(dk_acc)
        dv_acc[...] = jnp.zeros_like(dv_acc)
        db_ref[...] = jnp.zeros_like(db_ref)

        def tile(i, carry):
            start, cls = _nbr_tile(i, nq, T)
            win = pl.ds(start, NBR_K)
            kw, qT, doT = k_ref[win, :], qT_ref[i], doT_ref[i]
            sT = jnp.dot(kw, qT, preferred_element_type=F32) + b_ref[cls]
            pT = jnp.exp(sT - lse_ref[i])
            dv_acc[win, :] += _nt(pT.astype(BF), doT)
            ds = pT * (jnp.dot(v_ref[win, :], doT, preferred_element_type=F32) - dl_ref[i])
            db_ref[cls] += ds
            dsT = ds.astype(BF)
            dk_acc[win, :] += _nt(dsT, qT)
            dq_ref[i] = _tn(kw, dsT).astype(dq_ref.dtype)
            return carry

        lax.fori_loop(0, nq, tile, 0)
        dk_ref[...] = dk_acc[...].astype(dk_ref.dtype)
        dv_ref[...] = dv_acc[...].astype(dv_ref.dtype)

    chk = lambda r: pl.BlockSpec((None, nq, r, NBR_Q), lambda h: (h, 0, 0, 0))
    tok = pl.BlockSpec((None, T, HD), lambda h: (h, 0, 0))
    bsp = pl.BlockSpec((None, 5, NBR_K, NBR_Q), lambda h: (h, 0, 0, 0))
    return _pcall(
        body, name="nbr_bwd", grid=(H,),
        in_specs=[chk(HD), tok, tok, chk(HD), chk(1), chk(1), bsp],
        out_specs=[chk(HD), tok, tok, bsp],
        out_shape=[_sds((H, nq, HD, NBR_Q), BF), _sds((H, T, HD), BF), _sds((H, T, HD), BF),
                   _sds((H, 5, NBR_K, NBR_Q), F32)],
        scratch_shapes=[pltpu.VMEM((T, HD), F32)] * 2,
        compiler_params=_cp("parallel"),
    )(qT, k, v, doT, lse, delta, bias)


def _band_offsets(radius):
    offs = [0]
    for r in range(1, radius + 1):
        offs += [-r, r]
    return offs


def _dilated_bias(t):
    radius = max(w // 2 for w, _ in A_CONFIGS) // t
    tabs = []
    i = np.arange(t)
    for off in _band_offsets(radius):
        d = off * t + i[None, :] - i[:, None]
        mult = np.zeros((t, t), np.float32)
        for w, dil in A_CONFIGS:
            mult += ((d % dil) == 0) & (np.abs(d) <= w // 2)
        tabs.append(np.where(mult > 0, np.log(np.maximum(mult, 1.0)), NEG).astype(np.float32))
    return jnp.asarray(np.stack(tabs)[None]), radius


def _nbr_index(t):
    rpt = t // GRID_W
    i = np.arange(t)
    c0 = np.clip(i % GRID_W - 8, 0, GRID_W - 16)
    col_ok = ((i[None, :] % GRID_W) >= c0[:, None]) & ((i[None, :] % GRID_W) < c0[:, None] + 16)
    oks = []
    for off in _band_offsets(1):
        dr = (i[None, :] // GRID_W) - (i[:, None] // GRID_W) + rpt * off
        oks.append(col_ok & (np.abs(dr) <= 7))
    return np.stack(oks)


def _nbr_bias(rpb, t):
    rpt = t // GRID_W
    e1, e2 = _rpb_fold_matrices(t)
    ok = _nbr_index(t)
    padded = jnp.pad(rpb, ((0, 0), (0, 1), (0, 128 - rpb.shape[2]))).reshape(64, 128)

    def body(e2t_ref, rpb_ref, e1t_ref, out_ref):
        picked = jnp.dot(e2t_ref[...], rpb_ref[...], precision=HI, preferred_element_type=F32)
        out_ref[...] = jnp.dot(picked, e1t_ref[...], precision=HI, preferred_element_type=F32)

    n = e2.shape[1]
    sub = _pcall(body, name="rpb_expand", out_shape=_sds((n, GRID_W * GRID_W), F32),
                 compiler_params=pltpu.CompilerParams(vmem_limit_bytes=VMEM_LIMIT))(e2.T, padded, e1.T)
    tiles = sub.reshape(4, 3, rpt, rpt, GRID_W, GRID_W).transpose(0, 1, 2, 4, 3, 5).reshape(4, 3, t, t)
    return jnp.where(ok[None], tiles, NEG)


def _nbr_mask(qi, kb, t, rows, q_on_lanes):
    rpt = t // GRID_W
    qshape, kshape = ((1, t), (t, 1)) if q_on_lanes else ((t, 1), (1, t))
    rq = rpt * qi + lax.broadcasted_iota(jnp.int32, qshape, 1 if q_on_lanes else 0) // GRID_W
    rk = rpt * kb + lax.broadcasted_iota(jnp.int32, kshape, 0 if q_on_lanes else 1) // GRID_W
    r0 = jnp.clip(rq - 4, 0, rows - 8)
    return (rk >= r0) & (rk < r0 + 8)


def _attn_band_fwd(q, k, v, bias, radius, rowmask):
    H, T, _ = q.shape
    t = BAND_T
    nq = T // t
    Hb = bias.shape[0]
    offs = _band_offsets(radius)
    rows = T // GRID_W

    def body(q_ref, k_ref, v_ref, b_ref, o_ref, lse_ref, m_ref, l_ref, acc_ref):
        i = pl.program_id(1)
        qv = q_ref[...]
        m_ref[...] = jnp.full((t, 1), NEG, F32)
        l_ref[...] = jnp.zeros((t, 1), F32)
        acc_ref[...] = jnp.zeros((t, HD), F32)

        def tile(o, off):
            kb = i + off
            st = pl.multiple_of(kb * t, t)
            ks, vs = k_ref[pl.ds(st, t), :], v_ref[pl.ds(st, t), :]
            s = _nt(qv, ks) + b_ref[o]
            if rowmask:
                s = jnp.where(_nbr_mask(i, kb, t, rows, False), s, NEG)
            m_old = m_ref[...]
            m_new = jnp.maximum(m_old, jnp.max(s, axis=-1, keepdims=True))
            a = jnp.exp(m_old - m_new)
            p = jnp.exp(s - m_new)
            l_ref[...] = a * l_ref[...] + jnp.sum(p, axis=-1, keepdims=True)
            acc_ref[...] = a * acc_ref[...] + jnp.dot(p.astype(BF), vs, preferred_element_type=F32)
            m_ref[...] = m_new

        for o, off in enumerate(offs):
            if off == 0:
                tile(o, off)
            else:
                pl.when((i + off >= 0) & (i + off < nq))(functools.partial(tile, o, off))
        o_ref[...] = (acc_ref[...] / l_ref[...]).astype(o_ref.dtype)
        lse_ref[...] = m_ref[...] + jnp.log(l_ref[...])

    qs = pl.BlockSpec((None, t, HD), lambda h, i: (h, i, 0))
    kvs = pl.BlockSpec((None, T, HD), lambda h, i: (h, 0, 0))
    bs = pl.BlockSpec((None, len(offs), t, t), lambda h, i: (h if Hb > 1 else 0, 0, 0, 0))
    return _pcall(
        body, name="attn_band_fwd_c" if rowmask else "attn_band_fwd_a", grid=(H, nq),
        in_specs=[qs, kvs, kvs, bs],
        out_specs=[qs, pl.BlockSpec((None, t, 1), lambda h, i: (h, i, 0))],
        out_shape=[_sds((H, T, HD), BF), _sds((H, T, 1), F32)],
        scratch_shapes=[pltpu.VMEM((t, 1), F32), pltpu.VMEM((t, 1), F32), pltpu.VMEM((t, HD), F32)],
        compiler_params=_cp("parallel", "parallel"),
    )(q, k, v, bias)


def _attn_band_dq(q, k, v, o, do, lse, bias, radius, rowmask):
    H, T, _ = q.shape
    t = BAND_T
    nq = T // t
    Hb = bias.shape[0]
    offs = _band_offsets(radius)
    rows = T // GRID_W

    def body(q_ref, k_ref, v_ref, o_ref, do_ref, lse_ref, b_ref, dq_ref, dl_ref, *rest):
        db_ref = rest[0] if rowmask else None
        acc_ref = rest[-1]
        i = pl.program_id(1)
        qv, dov, lse = q_ref[...], do_ref[...], lse_ref[...]
        delta = jnp.sum(dov.astype(F32) * o_ref[...].astype(F32), axis=-1, keepdims=True)
        acc_ref[...] = jnp.zeros((t, HD), F32)
        if rowmask:
            @pl.when(i == 0)
            def _():
                db_ref[...] = jnp.zeros_like(db_ref)

        def tile(o, off):
            kb = i + off
            st = pl.multiple_of(kb * t, t)
            ks, vs = k_ref[pl.ds(st, t), :], v_ref[pl.ds(st, t), :]
            s = _nt(qv, ks) + b_ref[o]
            if rowmask:
                s = jnp.where(_nbr_mask(i, kb, t, rows, False), s, NEG)
            p = jnp.exp(s - lse)
            ds = p * (_nt(dov, vs) - delta)
            acc_ref[...] += jnp.dot(ds.astype(BF), ks, preferred_element_type=F32)
            if rowmask:
                db_ref[o] += ds

        for o, off in enumerate(offs):
            if off == 0:
                tile(o, off)
            else:
                pl.when((i + off >= 0) & (i + off < nq))(functools.partial(tile, o, off))
        dq_ref[...] = acc_ref[...]
        dl_ref[...] = delta

    qs = pl.BlockSpec((None, t, HD), lambda h, i: (h, i, 0))
    kvs = pl.BlockSpec((None, T, HD), lambda h, i: (h, 0, 0))
    cs = pl.BlockSpec((None, t, 1), lambda h, i: (h, i, 0))
    bs = pl.BlockSpec((None, len(offs), t, t), lambda h, i: (h if Hb > 1 else 0, 0, 0, 0))
    out_specs = [qs, cs]
    out_shape = [_sds((H, T, HD), F32), _sds((H, T, 1), F32)]
    if rowmask:
        out_specs.append(pl.BlockSpec((None, len(offs), t, t), lambda h, i: (h, 0, 0, 0)))
        out_shape.append(_sds((H, len(offs), t, t), F32))
    return _pcall(
        body, name="attn_band_dq_c" if rowmask else "attn_band_dq_a", grid=(H, nq),
        in_specs=[qs, kvs, kvs, qs, qs, cs, bs],
        out_specs=out_specs, out_shape=out_shape,
        scratch_shapes=[pltpu.VMEM((t, HD), F32)],
        compiler_params=_cp("parallel", "arbitrary"),
    )(q, k, v, o, do, lse, bias)


def _attn_band_dkv(q, k, v, do, lse, delta, bias_t, radius, rowmask):
    H, T, _ = q.shape
    t = BAND_T
    nq = T // t
    Hb = bias_t.shape[0]
    offs = _band_offsets(radius)
    rows = T // GRID_W
    lse_r, dl_r = _rows(lse, t), _rows(delta, t)

    def body(k_ref, v_ref, q_ref, do_ref, lse_ref, dl_ref, b_ref, dk_ref, dv_ref, dk_acc, dv_acc):
        jb = pl.program_id(1)
        kv, vv = k_ref[...], v_ref[...]
        dk_acc[...] = jnp.zeros((t, HD), F32)
        dv_acc[...] = jnp.zeros((t, HD), F32)

        def tile(o, off):
            qi = jb - off
            st = pl.multiple_of(qi * t, t)
            qs, dos = q_ref[pl.ds(st, t), :], do_ref[pl.ds(st, t), :]
            sT = _nt(kv, qs) + b_ref[o]
            if rowmask:
                sT = jnp.where(_nbr_mask(qi, jb, t, rows, True), sT, NEG)
            pT = jnp.exp(sT - lse_ref[qi])
            dv_acc[...] += jnp.dot(pT.astype(BF), dos, preferred_element_type=F32)
            dsT = pT * (_nt(vv, dos) - dl_ref[qi])
            dk_acc[...] += jnp.dot(dsT.astype(BF), qs, preferred_element_type=F32)

        for o, off in enumerate(offs):
            if off == 0:
                tile(o, off)
            else:
                pl.when((jb - off >= 0) & (jb - off < nq))(functools.partial(tile, o, off))
        dk_ref[...] = dk_acc[...]
        dv_ref[...] = dv_acc[...]

    kvs = pl.BlockSpec((None, t, HD), lambda h, j: (h, j, 0))
    qs = pl.BlockSpec((None, T, HD), lambda h, j: (h, 0, 0))
    rs = pl.BlockSpec((None, nq, 1, t), lambda h, j: (h, 0, 0, 0))
    bs = pl.BlockSpec((None, len(offs), t, t), lambda h, j: (h if Hb > 1 else 0, 0, 0, 0))
    return _pcall(
        body, name="attn_band_dkv_c" if rowmask else "attn_band_dkv_a", grid=(H, nq),
        in_specs=[kvs, kvs, qs, qs, rs, rs, bs],
        out_specs=[kvs, kvs],
        out_shape=[_sds((H, T, HD), F32)] * 2,
        scratch_shapes=[pltpu.VMEM((t, HD), F32)] * 2,
        compiler_params=_cp("parallel", "parallel"),
    )(k, v, q, do, lse_r, dl_r, bias_t)


def _rpb_fold_matrices(t):
    rpt = t // GRID_W
    e1 = np.zeros((GRID_W * GRID_W, 128), np.float32)
    ic, jc = np.meshgrid(np.arange(GRID_W), np.arange(GRID_W), indexing="ij")
    dc = (jc - ic + 15).reshape(-1)
    keep = (dc >= 0) & (dc <= 30)
    e1[np.arange(GRID_W * GRID_W)[keep], dc[keep]] = 1.0
    offs = _band_offsets(1)
    n = 4 * len(offs) * rpt * rpt
    e2 = np.zeros((64, n), np.float32)
    col = 0
    for h in range(4):
        for off in offs:
            for ib in range(rpt):
                for jb in range(rpt):
                    dr = jb - ib + rpt * off + 7
                    if 0 <= dr <= 14:
                        e2[h * 16 + dr, col] = 1.0
                    col += 1
    return jnp.asarray(e1), jnp.asarray(e2)


def _rpb_grad(dbias):
    t = dbias.shape[-1]
    rpt = t // GRID_W
    e1, e2 = _rpb_fold_matrices(t)
    sub = dbias.reshape(4, 3, rpt, GRID_W, rpt, GRID_W).transpose(0, 1, 2, 4, 3, 5)
    sub = sub.reshape(4 * 3 * rpt * rpt, GRID_W * GRID_W)

    def body(e2_ref, sub_ref, e1_ref, out_ref):
        diag = jnp.dot(sub_ref[...], e1_ref[...], precision=HI, preferred_element_type=F32)
        out_ref[...] = jnp.dot(e2_ref[...], diag, precision=HI, preferred_element_type=F32)

    out = _pcall(body, name="rpb_fold", out_shape=_sds((64, 128), F32),
                 compiler_params=pltpu.CompilerParams(vmem_limit_bytes=VMEM_LIMIT))(e2, sub, e1)
    return out.reshape(4, 16, 128)[:, :15, :31]


def _sigmoid(z):
    return 1.0 / (1.0 + jnp.exp(-z))


def _merge_fwd(oa, ob, oc, hg, bg, wa, wb, wc):
    T = oa.shape[0]
    tt = min(512, T)

    def body(oa_ref, ob_ref, oc_ref, hg_ref, bg_ref, wa_ref, wb_ref, wc_ref, out_ref):
        acc = None
        for k, (o_ref, w_ref) in enumerate(((oa_ref, wa_ref), (ob_ref, wb_ref), (oc_ref, wc_ref))):
            y = jnp.dot(o_ref[...], w_ref[...], preferred_element_type=F32)
            g = _sigmoid(hg_ref[:, D * k:D * (k + 1)] + bg_ref[:, D * k:D * (k + 1)])
            acc = g * y if acc is None else acc + g * y
        out_ref[...] = acc.astype(out_ref.dtype)

    row = lambda w: pl.BlockSpec((tt, w), lambda i: (i, 0))
    const = lambda a: pl.BlockSpec(a.shape, lambda i: (0, 0))
    return _pcall(
        body, name="merge_fwd", grid=(T // tt,),
        in_specs=[row(A_W), row(BQ_W), row(C_W), row(GATE), const(bg), const(wa), const(wb), const(wc)],
        out_specs=row(D), out_shape=_sds((T, D), BF),
        compiler_params=_cp("parallel"),
    )(oa, ob, oc, hg, bg, wa, wb, wc)


def _merge_bwd(dm, oa, ob, oc, hg, bg, wa, wb, wc):
    T = oa.shape[0]
    tt = min(256, T)

    def body(dm_ref, oa_ref, ob_ref, oc_ref, hg_ref, bg_ref, wa_ref, wb_ref, wc_ref,
             dya, dyb, dyc, doa, dob, doc, dhg, dbg):
        @pl.when(pl.program_id(0) == 0)
        def _():
            dbg[...] = jnp.zeros_like(dbg)

        dmv = dm_ref[...]
        for k, (o_ref, w_ref, dy_ref, do_ref) in enumerate(
                ((oa_ref, wa_ref, dya, doa), (ob_ref, wb_ref, dyb, dob), (oc_ref, wc_ref, dyc, doc))):
            sl = slice(D * k, D * (k + 1))
            y = jnp.dot(o_ref[...], w_ref[...], preferred_element_type=F32)
            g = _sigmoid(hg_ref[:, sl] + bg_ref[:, sl])
            dy = (dmv * g).astype(BF)
            dy_ref[...] = dy
            do_ref[...] = _nt(dy, w_ref[...]).astype(do_ref.dtype)
            dz = dmv * y * (g * (1.0 - g))
            dhg[:, sl] = dz.astype(dhg.dtype)
            dbg[:, sl] += jnp.sum(dz, axis=0, keepdims=True)

    row = lambda w: pl.BlockSpec((tt, w), lambda i: (i, 0))
    const = lambda a: pl.BlockSpec(a.shape, lambda i: (0, 0))
    return _pcall(
        body, name="merge_bwd", grid=(T // tt,),
        in_specs=[row(D), row(A_W), row(BQ_W), row(C_W), row(GATE), const(bg), const(wa), const(wb), const(wc)],
        out_specs=[row(D)] * 3 + [row(A_W), row(BQ_W), row(C_W), row(GATE),
                                  pl.BlockSpec((1, GATE), lambda i: (0, 0))],
        out_shape=[_sds((T, D), BF)] * 3 + [_sds((T, A_W), BF), _sds((T, BQ_W), BF), _sds((T, C_W), BF),
                                            _sds((T, GATE), BF), _sds((1, GATE), F32)],
        compiler_params=_cp("arbitrary"),
    )(dm, oa, ob, oc, hg, bg, wa, wb, wc)


def _lin_ln(a, w, res, g, b):
    T, K = a.shape
    tt = min(256, T)

    def body(a_ref, w_ref, res_ref, g_ref, b_ref, y_ref, yb_ref, xh_ref, rs_ref):
        u = ALPHA * res_ref[...] + jnp.dot(a_ref[...], w_ref[...], preferred_element_type=F32)
        mu = jnp.mean(u, axis=-1, keepdims=True)
        c = u - mu
        r = lax.rsqrt(jnp.mean(c * c, axis=-1, keepdims=True) + LN_EPS)
        xh = c * r
        y = xh * g_ref[...] + b_ref[...]
        y_ref[...] = y
        yb_ref[...] = y.astype(BF)
        xh_ref[...] = xh
        rs_ref[...] = r

    row = lambda w_: pl.BlockSpec((tt, w_), lambda i: (i, 0))
    const = lambda s: pl.BlockSpec(s, lambda i: (0, 0))
    return _pcall(
        body, name="lin_ln", grid=(T // tt,),
        in_specs=[row(K), const((K, D)), row(D), const((1, D)), const((1, D))],
        out_specs=[row(D), row(D), row(D), row(1)],
        out_shape=[_sds((T, D), F32), _sds((T, D), BF), _sds((T, D), F32), _sds((T, 1), F32)],
        compiler_params=_cp("parallel"),
    )(a, w, res, g, b)


def _ln_bwd(dy, xh, rs, g):
    T = dy.shape[0]
    tt = min(512, T)

    def body(dy_ref, xh_ref, rs_ref, g_ref, du_ref, dub_ref, dg_ref, db_ref):
        @pl.when(pl.program_id(0) == 0)
        def _():
            dg_ref[...] = jnp.zeros_like(dg_ref)
            db_ref[...] = jnp.zeros_like(db_ref)

        dyv, xhv = dy_ref[...], xh_ref[...]
        dg_ref[...] += jnp.sum(dyv * xhv, axis=0, keepdims=True)
        db_ref[...] += jnp.sum(dyv, axis=0, keepdims=True)
        dxh = dyv * g_ref[...]
        m1 = jnp.mean(dxh, axis=-1, keepdims=True)
        m2 = jnp.mean(dxh * xhv, axis=-1, keepdims=True)
        du = rs_ref[...] * (dxh - m1 - xhv * m2)
        du_ref[...] = du
        dub_ref[...] = du.astype(BF)

    row = lambda w_: pl.BlockSpec((tt, w_), lambda i: (i, 0))
    const = lambda s: pl.BlockSpec(s, lambda i: (0, 0))
    return _pcall(
        body, name="ln_bwd", grid=(T // tt,),
        in_specs=[row(D), row(D), row(1), const((1, D))],
        out_specs=[row(D), row(D), const((1, D)), const((1, D))],
        out_shape=[_sds((T, D), F32), _sds((T, D), BF), _sds((1, D), F32), _sds((1, D), F32)],
        compiler_params=_cp("arbitrary"),
    )(dy, xh, rs, g)


def _loss_grad(y, tgt):
    T = y.shape[0]
    tt = min(512, T)

    def body(y_ref, t_ref, dy_ref, sq_ref):
        @pl.when(pl.program_id(0) == 0)
        def _():
            sq_ref[...] = jnp.zeros_like(sq_ref)

        e = y_ref[...] - t_ref[...]
        dy_ref[...] = e * (1.0 / D)
        sq_ref[...] += jnp.sum(e * e, axis=0, keepdims=True)

    row = pl.BlockSpec((tt, D), lambda i: (i, 0))
    return _pcall(
        body, name="loss_grad", grid=(T // tt,),
        in_specs=[row, row], out_specs=[row, pl.BlockSpec((1, D), lambda i: (0, 0))],
        out_shape=[_sds((T, D), F32), _sds((1, D), F32)],
        compiler_params=_cp("arbitrary"),
    )(y, tgt)


def _position():
    return lax.axis_index("x"), lax.axis_index("y"), lax.axis_index("c")


HBM = pl.BlockSpec(memory_space=pl.ANY)


def _gather_phases(x_refs, out_refs, send, recv, loc):
    n = len(x_refs)
    x, y, c = _position()
    me, sib = (x, y, c), (x, y, 1 - c)
    chips = [(1 - x, y), (x, 1 - y), (1 - x, 1 - y)]

    def copy(a, k, block, to, src=None):
        px, py, pc = block
        dst = out_refs[a].at[4 * px + 2 * py + pc]
        return pltpu.make_async_remote_copy(
            src_ref=dst if src is None else src, dst_ref=dst,
            send_sem=send.at[a, k], recv_sem=recv.at[a, k], device_id=to, device_id_type=MESH)

    mine = [pltpu.make_async_copy(x_refs[a], out_refs[a].at[4 * x + 2 * y + c], loc.at[a]) for a in range(n)]
    first = []
    for a in range(n):
        first.append(copy(a, 0, me, sib, src=x_refs[a]))
        first += [copy(a, 1 + j, me, (*chip, c), src=x_refs[a]) for j, chip in enumerate(chips)]
    passed = [copy(a, 4 + j, (*chip, c), sib) for j, chip in enumerate(chips) for a in range(n)]

    def start():
        for cp in mine + first:
            cp.start()

    def forward():
        for j, chip in enumerate(chips):
            for a in range(n):
                copy(a, 1 + j, (*chip, c), me).wait_recv()
                copy(a, 4 + j, (*chip, c), sib).start()

    def finish():
        for a in range(n):
            copy(a, 0, sib, me).wait_recv()
            for j, chip in enumerate(chips):
                copy(a, 4 + j, (*chip, 1 - c), me).wait_recv()
        for cp in first + passed:
            cp.wait_send()
        for cp in mine:
            cp.wait()

    return start, forward, finish


def _gather_scratch(n):
    return [pltpu.SemaphoreType.DMA((n, 7)), pltpu.SemaphoreType.DMA((n, 7)), pltpu.SemaphoreType.DMA((n,))]


def _all_gather(xs, name):
    n = len(xs)

    def body(*refs):
        for phase in _gather_phases(refs[:n], refs[n:2 * n], *refs[2 * n:]):
            phase()

    return _pcall_comm(
        body, name=name,
        in_specs=[HBM] * n, out_specs=[HBM] * n,
        out_shape=[_sds((8,) + x.shape, x.dtype) for x in xs],
        scratch_shapes=_gather_scratch(n),
    )(*xs)


def _scatter_phases(g_refs, land_refs, send, recv):
    x, y, c = _position()
    peers = [(x, y, 1 - c), (1 - x, y, c), (x, 1 - y, c), (1 - x, 1 - y, c),
             (1 - x, y, 1 - c), (x, 1 - y, 1 - c), (1 - x, 1 - y, 1 - c)]
    copies = [pltpu.make_async_remote_copy(
        src_ref=g_refs[a].at[4 * px + 2 * py + pc], dst_ref=land_refs[a].at[k],
        send_sem=send.at[a, k], recv_sem=recv.at[a, k], device_id=(px, py, pc), device_id_type=MESH)
        for a in range(len(g_refs)) for k, (px, py, pc) in enumerate(peers)]

    def start():
        for cp in copies:
            cp.start()

    def finish():
        for cp in copies:
            cp.wait()

    return start, finish


def _scatter_scratch(n):
    return [pltpu.SemaphoreType.DMA((n, 7)), pltpu.SemaphoreType.DMA((n, 7))]


def _scatter(gs, name):
    n = len(gs)

    def body(*refs):
        for phase in _scatter_phases(refs[:n], refs[n:2 * n], *refs[2 * n:]):
            phase()

    return _pcall_comm(
        body, name=name,
        in_specs=[HBM] * n, out_specs=[HBM] * n,
        out_shape=[_sds((7,) + g.shape[1:], g.dtype) for g in gs],
        scratch_shapes=_scatter_scratch(n),
    )(*gs)


def _sum8(g, land, me):
    _, R, C = g.shape
    tr = min(512, R)

    def body(me_ref, g_ref, l_ref, out_ref):
        acc = g_ref[...].astype(F32)
        for k in range(7):
            acc = acc + l_ref[k].astype(F32)
        out_ref[...] = acc

    return _pcall(
        body, name="sum8",
        grid_spec=pltpu.PrefetchScalarGridSpec(
            num_scalar_prefetch=1, grid=(R // tr,),
            in_specs=[pl.BlockSpec((None, tr, C), lambda i, me_: (me_[0], i, 0)),
                      pl.BlockSpec((7, tr, C), lambda i, me_: (0, i, 0))],
            out_specs=pl.BlockSpec((tr, C), lambda i, me_: (i, 0))),
        out_shape=_sds((R, C), F32),
        compiler_params=_cp("parallel"),
    )(me, g, land)


def _adamw(g, w, m, v):
    R, C = w.shape
    tr = min(512, R)

    def body(g_ref, w_ref, m_ref, v_ref, d_out, m_out, v_out):
        d_out[...], m_out[...], v_out[...] = _adamw_math(w_ref[...], g_ref[...], m_ref[...], v_ref[...])

    blk = pl.BlockSpec((tr, C), lambda i: (i, 0))
    return _pcall(body, name="adamw", grid=(R // tr,), in_specs=[blk] * 4, out_specs=[blk] * 3,
                  out_shape=[_sds((R, C), F32)] * 3, compiler_params=_cp("parallel"))(g, w, m, v)


def _exchange_pair(gs, name):
    n = len(gs)
    hbm = pl.BlockSpec(memory_space=pl.ANY)

    def body(*refs):
        g_refs, out_refs = refs[:n], refs[n:2 * n]
        send, recv = refs[2 * n:]
        x, y, c = _position()
        copies = []
        for a in range(n):
            for p in range(4):
                copies.append(pltpu.make_async_remote_copy(
                    src_ref=g_refs[a].at[p, 1 - c], dst_ref=out_refs[a].at[p],
                    send_sem=send.at[a, p], recv_sem=recv.at[a, p],
                    device_id=(x, y, 1 - c), device_id_type=MESH))
        for cp in copies:
            cp.start()
        for cp in copies:
            cp.wait()

    return _pcall_comm(
        body, name=name,
        in_specs=[hbm] * n, out_specs=[hbm] * n,
        out_shape=[_sds((4,) + g.shape[2:], g.dtype) for g in gs],
        scratch_shapes=[pltpu.SemaphoreType.DMA((n, 4)), pltpu.SemaphoreType.DMA((n, 4))],
    )(*gs)


def _exchange_chips(ps, name):
    n = len(ps)
    hbm = pl.BlockSpec(memory_space=pl.ANY)

    def body(*refs):
        p_refs, out_refs = refs[:n], refs[n:2 * n]
        send, recv = refs[2 * n:]
        x, y, c = _position()
        chips = [(1 - x, y), (x, 1 - y), (1 - x, 1 - y)]
        copies = []
        for a in range(n):
            for j, (px, py) in enumerate(chips):
                copies.append(pltpu.make_async_remote_copy(
                    src_ref=p_refs[a].at[2 * px + py], dst_ref=out_refs[a].at[j],
                    send_sem=send.at[a, j], recv_sem=recv.at[a, j],
                    device_id=(px, py, c), device_id_type=MESH))
        for cp in copies:
            cp.start()
        for cp in copies:
            cp.wait()

    return _pcall_comm(
        body, name=name,
        in_specs=[hbm] * n, out_specs=[hbm] * n,
        out_shape=[_sds((3,) + p.shape[1:], p.dtype) for p in ps],
        scratch_shapes=[pltpu.SemaphoreType.DMA((n, 3)), pltpu.SemaphoreType.DMA((n, 3))],
    )(*ps)


def _pair_sum(g, got, core):
    _, _, R, C = g.shape
    tr = min(512, R)

    def body(core_ref, g_ref, r_ref, out_ref):
        out_ref[...] = (g_ref[...].astype(F32) + r_ref[...].astype(F32)).astype(out_ref.dtype)

    return _pcall(
        body, name="pair_sum",
        grid_spec=pltpu.PrefetchScalarGridSpec(
            num_scalar_prefetch=1, grid=(4, R // tr),
            in_specs=[pl.BlockSpec((None, None, tr, C), lambda p, i, cr: (p, cr[0], i, 0)),
                      pl.BlockSpec((None, tr, C), lambda p, i, cr: (p, i, 0))],
            out_specs=pl.BlockSpec((None, tr, C), lambda p, i, cr: (p, i, 0))),
        out_shape=_sds((4, R, C), g.dtype),
        compiler_params=_cp("parallel", "parallel"),
    )(core, g, got)


def _adamw_math(w, g, m, v):
    m = B1 * m + (1.0 - B1) * g
    v = B2 * v + (1.0 - B2) * (g * g)
    m_hat = m / (1.0 - B1 ** STEP)
    v_hat = v / (1.0 - B2 ** STEP)
    delta = -LR * (m_hat / (jnp.sqrt(v_hat) + EPS) + WD * w)
    return delta, m, v


def _chip_sum_adamw(p, got, chip, w, m, v):
    R, C = w.shape
    tr = min(512, R)

    def body(chip_ref, p_ref, r_ref, w_ref, m_ref, v_ref, g_out, d_out, m_out, v_out):
        g = ((p_ref[...].astype(F32) + r_ref[0].astype(F32)) + r_ref[1].astype(F32)) + r_ref[2].astype(F32)
        d, mn, vn = _adamw_math(w_ref[...], g, m_ref[...], v_ref[...])
        g_out[...], d_out[...], m_out[...], v_out[...] = g, d, mn, vn

    blk = pl.BlockSpec((tr, C), lambda i, ch: (i, 0))
    return _pcall(
        body, name="chip_sum_adamw",
        grid_spec=pltpu.PrefetchScalarGridSpec(
            num_scalar_prefetch=1, grid=(R // tr,),
            in_specs=[pl.BlockSpec((None, tr, C), lambda i, ch: (ch[0], i, 0)),
                      pl.BlockSpec((3, tr, C), lambda i, ch: (0, i, 0)), blk, blk, blk],
            out_specs=[blk] * 4),
        out_shape=[_sds((R, C), F32)] * 4,
        compiler_params=_cp("parallel"),
    )(chip, p, got, w, m, v)


def _small_sum_adamw(parts, w, m, v):
    _, R, C = parts.shape

    def body(p_ref, w_ref, m_ref, v_ref, g_out, d_out, m_out, v_out):
        g = p_ref[0]
        for k in range(1, 8):
            g = g + p_ref[k]
        d, mn, vn = _adamw_math(w_ref[...], g, m_ref[...], v_ref[...])
        g_out[...], d_out[...], m_out[...], v_out[...] = g, d, mn, vn

    return _pcall(body, name="small_sum_adamw", out_shape=[_sds((R, C), F32)] * 4,
                  compiler_params=pltpu.CompilerParams(vmem_limit_bytes=VMEM_LIMIT))(parts, w, m, v)


BIG = ("w_in", "w_branch_a", "w_branch_b", "w_branch_c", "w_out", "w_up", "w_down")
ROW_SHARDED = ("w_out", "w_down")
EARLY = ("w_down", "w_up", "w_out", "w_branch_a", "w_branch_b", "w_branch_c")
SMALL = ("b_gate", "q_norm_b", "k_norm_b", "rpb_c", "ln1_g", "ln1_b", "ln2_g", "ln2_b")
NAMES = ("w_in", "b_gate", "q_norm_b", "k_norm_b", "rpb_c", "w_branch_a", "w_branch_b", "w_branch_c",
         "w_out", "ln1_g", "ln1_b", "w_up", "w_down", "ln2_g", "ln2_b")


def _full_weight(blk, name):
    if name in ROW_SHARDED:
        return blk.reshape(-1, blk.shape[2])
    return blk.transpose(1, 0, 2).reshape(blk.shape[1], -1)


def _chunks(grad, name):
    if name in ROW_SHARDED:
        return grad.reshape(8, grad.shape[0] // 8, grad.shape[1])
    return grad.reshape(grad.shape[0], 8, grad.shape[1] // 8).transpose(1, 0, 2)


def _layer_fwd(x, xb, W, P, tabs, gm, gather=(), late=None):
    hq, = _mm(xb, W["w_qkv"], "nn", [F32], 1024, 768, 1024, name="in_qkv")
    hg, = _mm(xb, W["w_gate"], "nn", [F32], 1024, 1024, 1024, name="in_gate")
    tab_a, tab_b = tabs
    prepped = _prep_fwd(hq, tab_a, tab_b, P["qn"], P["kn"], gm)
    T = x.shape[0]
    tb = min(FULL_T, T)
    oa, lse_a, qa = _dilated_fwd(*prepped[0:3])
    qb_t, kb_t, vb_t = prepped[3:6]
    qb, kb, vb = _chunked_t(qb_t, tb), _to_heads(kb_t), _chunked_t(vb_t, min(FULL_TK, T))
    vb1 = jnp.concatenate([vb, jnp.ones(vb.shape[:2] + (V_ROWS - HD, vb.shape[3]), vb.dtype)], axis=2)
    ob, lse_b, gathered = _attn_full_fwd(qb, kb, vb1, gather)
    if late is not None:
        W = {**W, **late(gathered)}
    qc_t, kc_t, vc_t = prepped[6:9]
    qc, kc, vc = _chunked_t(qc_t, NBR_Q), _to_heads(kc_t), _to_heads(vc_t)
    bias_c = _nbr_bias_blocks(P["rpb"], T)
    oc, lse_c = _nbr_fwd(qc, kc, jnp.concatenate([vc, jnp.ones_like(vc)], axis=2), bias_c)
    oa_t, ob_t, oc_t = oa.transpose(2, 0, 1).reshape(T, A_W), _unchunk_t(ob), _unchunk_t(oc)
    qb = (qb, kb, kb.transpose(0, 2, 1), _to_heads(vb_t))
    ka = va = None
    merged = _merge_fwd(oa_t, ob_t, oc_t, hg, P["bg"], W["w_branch_a"], W["w_branch_b"], W["w_branch_c"])
    x1, x1b, xh1, rs1 = _lin_ln(merged, W["w_out"], x, P["ln1_g"], P["ln1_b"])

    def relu2(acc):
        r = jnp.maximum(acc, 0.0)
        return r * r, r

    f, r = _mm(x1b, W["w_up"], "nn", [BF, BF], 1024, 1024, 1024, epilogue=relu2, name="mlp_up")
    x2, x2b, xh2, rs2 = _lin_ln(f, W["w_down"], x1, P["ln2_g"], P["ln2_b"])
    saved = dict(xb=xb, hq=hq, hg=hg, qkv=(qa, ka, va, qb, kb, vb, qc, kc, vc), o=(oa, ob, oc),
                 lse=(lse_a, lse_b, lse_c), o_t=(oa_t, ob_t, oc_t), bias_c=bias_c, merged=merged,
                 xh1=xh1, rs1=rs1, x1b=x1b, f=f, r=r, xh2=xh2, rs2=rs2)
    return x2, x2b, saved, W


def _layer_bwd(dx2, S, W, P, tabs, gm, scatter=()):
    G = {}
    du2, du2b, G["ln2_g"], G["ln2_b"] = _ln_bwd(dx2, S["xh2"], S["rs2"], P["ln2_g"])
    G["w_down"], = _mm(S["f"], du2b, "tn", [F32], 1024, 1024, 512, name="dw_down")
    da, = _mm(du2b, W["w_down"], "nt", [BF], 1024, 1024, 1024,
              epilogue=lambda acc, r: (acc * (2.0 * r.astype(F32)),), extras=(S["r"],), name="d_act")
    G["w_up"], = _mm(S["x1b"], da, "tn", [F32], 1024, 1024, 512, name="dw_up")
    dx1, = _mm(da, W["w_up"], "nt", [F32], 1024, 1024, 1024,
               epilogue=lambda acc, d: (ALPHA * d + acc,), extras=(du2,), name="dx_mlp")
    du1, du1b, G["ln1_g"], G["ln1_b"] = _ln_bwd(dx1, S["xh1"], S["rs1"], P["ln1_g"])
    G["w_out"], = _mm(S["merged"], du1b, "tn", [F32], 1024, 1024, 512, name="dw_out")
    dm, = _mm(du1b, W["w_out"], "nt", [F32], 1024, 1024, 1024, name="d_merged")
    oa_t, ob_t, oc_t = S["o_t"]
    dya, dyb, dyc, doa, dob, doc, dhg, G["b_gate"] = _merge_bwd(
        dm, oa_t, ob_t, oc_t, S["hg"], P["bg"], W["w_branch_a"], W["w_branch_b"], W["w_branch_c"])
    G["w_branch_a"], = _mm(oa_t, dya, "tn", [F32], 256, 1024, 512, name="dw_branch_a")
    G["w_branch_b"], = _mm(ob_t, dyb, "tn", [F32], 512, 1024, 512, name="dw_branch_b")
    G["w_branch_c"], = _mm(oc_t, dyc, "tn", [F32], 256, 1024, 512, name="dw_branch_c")

    qa, ka, va, qb, kb, vb, qc, kc, vc = S["qkv"]
    oa, ob, oc = S["o"]
    lse_a, lse_b, lse_c = S["lse"]
    dqa, dka, dva = _dilated_bwd(doa, oa, lse_a, qa)
    qT_b, k_b, kT_b, v_b = qb
    dobT = _chunked_t(dob, ob.shape[-1])
    early = {n: _chunks(G[n], n).astype(BF) for n in EARLY}
    dqbT, dkb8, dvb8, landed = _attn_full_bwd(qT_b, k_b, kT_b, v_b, dobT, lse_b, _attn_delta(dobT, ob),
                                              list(scatter) + [early[n] for n in EARLY])
    landed = (landed[:len(scatter)], dict(zip(EARLY, landed[len(scatter):])), early)
    group_sum = lambda t: t.reshape(k_b.shape[0], -1, t.shape[1], HD).sum(1)
    dqb, dkb, dvb = _unchunk_t(dqbT), _from_heads(group_sum(dkb8)), _from_heads(group_sum(dvb8))
    bias_c = S["bias_c"]
    docT = _chunked_t(doc, NBR_Q)
    dqcT, dkc, dvc, dbias_c = _nbr_bwd(qc, kc, vc, docT, lse_c, _attn_delta(docT, oc), bias_c)
    G["rpb_c"] = _nbr_rpb_grad(dbias_c, dx2.shape[0])

    tab_a, tab_b = tabs
    grads = [dqa, dka, dva, dqb, dkb, dvb, _unchunk_t(dqcT), _from_heads(dkc), _from_heads(dvc)]
    dhq, dqn, dkn = _prep_bwd(S["hq"], grads, tab_a, tab_b, P["qn"], P["kn"], gm)
    G["q_norm_b"] = dqn.reshape(BQ_W // HD, HD).sum(0)
    G["k_norm_b"] = dkn.reshape(BKV_W // HD, HD).sum(0)
    dw_qkv, = _mm(S["xb"], dhq, "tn", [F32], 1024, 768, 512, name="dw_qkv")
    dw_gate, = _mm(S["xb"], dhg, "tn", [F32], 1024, 1024, 512, name="dw_gate")
    G["w_in"] = jnp.concatenate([dw_qkv, dw_gate], axis=1)
    dx_a, = _mm(dhq, W["w_qkv"], "nt", [F32], 1024, 1024, 768,
                epilogue=lambda acc, d: (ALPHA * d + acc,), extras=(du1,), name="dx_qkv")
    dx, = _mm(dhg, W["w_gate"], "nt", [F32], 1024, 1024, 1024,
              epilogue=lambda acc, d: (d + acc,), extras=(dx_a,), name="dx_gate")
    return dx, G, landed


def _pack_small(vals):
    flat = jnp.concatenate([vals[n].reshape(-1).astype(F32) for n in SMALL])
    pad = (-flat.shape[0]) % (8 * 128)
    return jnp.pad(flat, (0, pad)).reshape(-1, 128)


def _unpack_small(packed, like):
    flat, out, off = packed.reshape(-1), {}, 0
    for n in SMALL:
        size = math.prod(like[n].shape)
        out[n] = flat[off:off + size].reshape(like[n].shape)
        off += size
    return out


def kernel(x, w_in, b_gate, q_norm_b, k_norm_b, rpb_c, w_branch_a, w_branch_b, w_branch_c, w_out, ln1_g, ln1_b, w_up, w_down, ln2_g, ln2_b, loss_target, m_w_in, m_b_gate, m_q_norm_b, m_k_norm_b, m_rpb_c, m_w_branch_a, m_w_branch_b, m_w_branch_c, m_w_out, m_ln1_g, m_ln1_b, m_w_up, m_w_down, m_ln2_g, m_ln2_b, v_w_in, v_b_gate, v_q_norm_b, v_k_norm_b, v_rpb_c, v_w_branch_a, v_w_branch_b, v_w_branch_c, v_w_out, v_ln1_g, v_ln1_b, v_w_up, v_w_down, v_ln2_g, v_ln2_b):
    w = dict(w_in=w_in, b_gate=b_gate, q_norm_b=q_norm_b, k_norm_b=k_norm_b, rpb_c=rpb_c,
             w_branch_a=w_branch_a, w_branch_b=w_branch_b, w_branch_c=w_branch_c, w_out=w_out,
             ln1_g=ln1_g, ln1_b=ln1_b, w_up=w_up, w_down=w_down, ln2_g=ln2_g, ln2_b=ln2_b)
    m = dict(w_in=m_w_in, b_gate=m_b_gate, q_norm_b=m_q_norm_b, k_norm_b=m_k_norm_b, rpb_c=m_rpb_c,
             w_branch_a=m_w_branch_a, w_branch_b=m_w_branch_b, w_branch_c=m_w_branch_c, w_out=m_w_out,
             ln1_g=m_ln1_g, ln1_b=m_ln1_b, w_up=m_w_up, w_down=m_w_down, ln2_g=m_ln2_g, ln2_b=m_ln2_b)
    v = dict(w_in=v_w_in, b_gate=v_b_gate, q_norm_b=v_q_norm_b, k_norm_b=v_k_norm_b, rpb_c=v_rpb_c,
             w_branch_a=v_w_branch_a, w_branch_b=v_w_branch_b, w_branch_c=v_w_branch_c, w_out=v_w_out,
             ln1_g=v_ln1_g, ln1_b=v_ln1_b, w_up=v_w_up, w_down=v_w_down, ln2_g=v_ln2_g, ln2_b=v_ln2_b)
    T = x.shape[1]
    xc, yc, cc = _position()

    flat2 = lambda a: a.reshape(-1, a.shape[-1])
    shards = {n: w[n].astype(BF) for n in BIG}
    w_in0, = _all_gather([shards["w_in"][0]], "gather_w_in0")
    riding = [shards[n][0] for n in BIG[1:]] + [flat2(shards[n][1:]) for n in BIG]

    tabs = _rope_tables(T)
    gm = _group_mean_matrix()

    def in_proj(blk):
        w_in_l = _full_weight(blk, "w_in")
        return {"w_qkv": w_in_l[:, :QKV], "w_gate": w_in_l[:, QKV:]}

    Ws = [in_proj(w_in0)]

    def late(gathered):
        rest = [g.reshape(8, DEPTH - 1, -1, g.shape[2]) for g in gathered[len(BIG) - 1:]]
        for i in range(DEPTH - 1):
            W = {n: _full_weight(g[:, i], n) for n, g in zip(BIG[1:], rest[1:])}
            Ws.append({**W, **in_proj(rest[0][:, i])})
        return {n: _full_weight(g, n) for n, g in zip(BIG[1:], gathered)}

    Ps = [dict(qn=jnp.tile(q_norm_b[l][None], (1, 2)), kn=jnp.tile(k_norm_b[l][None], (1, 2)),
               rpb=rpb_c[l], bg=b_gate[l][None], ln1_g=ln1_g[l][None], ln1_b=ln1_b[l][None],
               ln2_g=ln2_g[l][None], ln2_b=ln2_b[l][None]) for l in range(DEPTH)]

    h = x[0]
    hb = h.astype(BF)
    saved = []
    for l in range(DEPTH):
        h, hb, S, Ws[l] = _layer_fwd(h, hb, Ws[l], Ps[l], tabs, gm, *((riding, late) if l == 0 else ()))
        saved.append(S)
    dy, sq = _loss_grad(h, loss_target[0])
    loss = lax.psum(0.5 / D * jnp.sum(sq), AXES)

    grads = [None] * DEPTH
    chunks, landed = [{} for _ in range(DEPTH)], [{} for _ in range(DEPTH)]
    for l in reversed(range(DEPTH)):
        above = [chunks[l + 1]["w_in"]] if l + 1 < DEPTH else []
        dy, grads[l], (arrived, landed_early, early) = _layer_bwd(dy, saved[l], Ws[l], Ps[l], tabs, gm, above)
        if above:
            landed[l + 1]["w_in"], = arrived
        chunks[l].update(early)
        landed[l].update(landed_early)
        chunks[l]["w_in"] = _chunks(grads[l]["w_in"], "w_in").astype(BF)
    landed[0]["w_in"], = _scatter([chunks[0]["w_in"]], "scatter_w_in0")
    grad_x = dy[None]
    me = (4 * xc + 2 * yc + cc).reshape(1).astype(jnp.int32)
    out_g, out_d, out_m, out_v = {}, {}, {}, {}
    for n in BIG:
        g = jnp.concatenate([_sum8(chunks[l][n], landed[l][n], me) for l in range(DEPTH)], axis=0)
        res = (g,) + tuple(_adamw(g, flat2(w[n]), flat2(m[n]), flat2(v[n])))
        out_g[n], out_d[n], out_m[n], out_v[n] = [t.reshape(w[n].shape) for t in res]

    part = _pack_small({n: jnp.stack([grads[l][n].reshape(w[n].shape[1:]) for l in range(DEPTH)]) for n in SMALL})
    parts, = _all_gather([part], "gather_small_grads")
    res = _small_sum_adamw(parts, _pack_small(w), _pack_small(m), _pack_small(v))
    for dst, packed in zip((out_g, out_d, out_m, out_v), res):
        dst.update(_unpack_small(packed, w))

    return (loss, grad_x, *[out_g[n] for n in NAMES], *[out_d[n] for n in NAMES],
            *[out_m[n] for n in NAMES], *[out_v[n] for n in NAMES])
```

```python
import functools
import math

import numpy as np
import jax
import jax.numpy as jnp
from jax import lax
from jax.experimental import pallas as pl
from jax.experimental.pallas import tpu as pltpu

F32 = jnp.float32
BF = jnp.bfloat16
HI = lax.Precision.HIGHEST
NEG = -1e30
MESH = pl.DeviceIdType.MESH
AXES = ("x", "y", "c")

D = 1024
DEPTH = 4
HD = 64
A_W, BQ_W, BKV_W, C_W = 256, 512, 128, 256
QKV = 2304
GATE = 3072
D_FF = 4096
GRID_W = 64
ALPHA = (2 * DEPTH) ** 0.25
LN_EPS = 1e-5
RMS_EPS = 1e-6
SCALE = HD ** -0.5
ROPE_THETA = 500000.0
AXIAL_THETA = 10000.0
A_CONFIGS = ((128, 1), (512, 4), (2048, 16))
LR, B1, B2, EPS, WD, STEP = 0.001, 0.9, 0.999, 1e-08, 0.01, 10

VMEM_LIMIT = 56 * 1024 * 1024
BAND_T = 256


def _pcall(body, **kw):
    return pl.pallas_call(body, **kw)


def _pcall_comm(body, **kw):
    return pl.pallas_call(body, **kw)


def _cp(*sem):
    return pltpu.CompilerParams(dimension_semantics=sem, vmem_limit_bytes=VMEM_LIMIT)


def _sds(shape, dtype):
    return jax.ShapeDtypeStruct(shape, dtype)


def _mm(a, b, dims, outs, tm, tn, tk, epilogue=None, extras=(), name="mm"):
    if dims == "tn":
        K, M = a.shape
    else:
        M, K = a.shape
    N = b.shape[0] if dims == "nt" else b.shape[1]
    tm, tn, tk = min(tm, M), min(tn, N), min(tk, K)
    assert M % tm == 0 and N % tn == 0 and K % tk == 0, (name, M, N, K, tm, tn, tk)
    nk = K // tk
    ne, no = len(extras), len(outs)

    def body(a_ref, b_ref, *rest):
        extra_refs, out_refs = rest[:ne], rest[ne:ne + no]
        av, bv = a_ref[...].astype(BF), b_ref[...].astype(BF)
        if dims == "nn":
            p = jnp.dot(av, bv, preferred_element_type=F32)
        elif dims == "nt":
            p = lax.dot_general(av, bv, (((1,), (1,)), ((), ())), preferred_element_type=F32)
        else:
            p = lax.dot_general(av, bv, (((0,), (0,)), ((), ())), preferred_element_type=F32)

        def finish(acc):
            res = epilogue(acc, *[r[...] for r in extra_refs]) if epilogue else (acc,)
            for o, r in zip(out_refs, res):
                o[...] = r.astype(o.dtype)

        if nk == 1:
            finish(p)
        else:
            acc_ref = rest[-1]
            k = pl.program_id(2)

            @pl.when(k == 0)
            def _():
                acc_ref[...] = p

            @pl.when(k > 0)
            def _():
                acc_ref[...] += p

            @pl.when(k == nk - 1)
            def _():
                finish(acc_ref[...])

    if dims == "tn":
        a_spec = pl.BlockSpec((tk, tm), lambda i, j, k: (k, i))
    else:
        a_spec = pl.BlockSpec((tm, tk), lambda i, j, k: (i, k))
    if dims == "nt":
        b_spec = pl.BlockSpec((tn, tk), lambda i, j, k: (j, k))
    else:
        b_spec = pl.BlockSpec((tk, tn), lambda i, j, k: (k, j))
    o_spec = pl.BlockSpec((tm, tn), lambda i, j, k: (i, j))
    res = _pcall(
        body, name=name, grid=(M // tm, N // tn, nk),
        in_specs=[a_spec, b_spec] + [o_spec] * ne,
        out_specs=[o_spec] * no,
        out_shape=[_sds((M, N), dt) for dt in outs],
        scratch_shapes=[pltpu.VMEM((tm, tn), F32)] if nk > 1 else [],
        compiler_params=_cp("parallel", "parallel", "arbitrary"),
    )(a, b, *extras)
    return res


def _rope_tables(T):
    pos = jnp.arange(T)

    def cs(p, theta, half):
        inv = theta ** (-jnp.arange(half, dtype=F32) / half)
        ang = p.astype(F32)[:, None] * inv[None, :]
        return jnp.cos(ang), jnp.sin(ang)

    ca, sa = cs(pos, ROPE_THETA, 8)
    one, zero, z8 = jnp.ones((T, 48), F32), jnp.zeros((T, 48), F32), jnp.zeros((T, 8), F32)
    tab_a = [jnp.concatenate(t, 1) for t in ([ca, ca, one], [-sa, z8, zero], [z8, sa, zero])]
    cr, sr = cs(pos // GRID_W, AXIAL_THETA, 16)
    cc, sc = cs(pos % GRID_W, AXIAL_THETA, 16)
    z16 = jnp.zeros((T, 16), F32)
    tab_b = [jnp.concatenate(t, 1) for t in ([cr, cr, cc, cc], [-sr, z16, -sc, z16], [z16, sr, z16, sc])]
    return [jnp.tile(t, (1, 2)) for t in tab_a], [jnp.tile(t, (1, 2)) for t in tab_b]


def _rot(x, C, S1, S2, k):
    return x * C + pltpu.roll(x, 128 - k, 1) * S1 + pltpu.roll(x, k, 1) * S2


def _rot_t(d, C, S1, S2, k):
    return d * C + pltpu.roll(d * S1, k, 1) + pltpu.roll(d * S2, 128 - k, 1)


def _group_mean_matrix():
    m = np.zeros((128, 128), np.float32)
    m[:64, :64] = 1.0 / 64
    m[64:, 64:] = 1.0 / 64
    return jnp.asarray(m)


def _prep_fwd(hq, tab_a, tab_b, qn, kn, gm):
    T = hq.shape[0]
    tt = min(256, T)
    widths = [A_W, A_W, A_W, BQ_W, BKV_W, BKV_W, C_W, C_W, C_W]

    def body(h_ref, ca, s1a, s2a, cb, s1b, s2b, qn_ref, kn_ref, gm_ref,
             qa, ka, va, qb, kb, vb, qc, kc, vc):
        def col(off, j):
            return h_ref[:, off + 128 * j: off + 128 * (j + 1)]

        for j in range(2):
            sl = slice(128 * j, 128 * (j + 1))
            qa[:, sl] = (_rot(col(0, j), ca[...], s1a[...], s2a[...], 8) * SCALE).astype(qa.dtype)
            ka[:, sl] = _rot(col(256, j), ca[...], s1a[...], s2a[...], 8).astype(ka.dtype)
            va[:, sl] = col(512, j).astype(va.dtype)
            qc[:, sl] = (col(1536, j) * SCALE).astype(qc.dtype)
            kc[:, sl] = col(1792, j).astype(kc.dtype)
            vc[:, sl] = col(2048, j).astype(vc.dtype)

        def normed(x, w):
            ms = jnp.dot(x * x, gm_ref[...], precision=HI, preferred_element_type=F32)
            return x * lax.rsqrt(ms + RMS_EPS) * w

        for j in range(4):
            y = normed(col(768, j), qn_ref[...])
            qb[:, 128 * j:128 * (j + 1)] = (_rot(y, cb[...], s1b[...], s2b[...], 16) * SCALE).astype(qb.dtype)
        y = normed(col(1280, 0), kn_ref[...])
        kb[...] = _rot(y, cb[...], s1b[...], s2b[...], 16).astype(kb.dtype)
        vb[...] = col(1408, 0).astype(vb.dtype)

    row = lambda w: pl.BlockSpec((tt, w), lambda i: (i, 0))
    const = lambda s: pl.BlockSpec(s, lambda i: (0, 0))
    return _pcall(
        body, name="prep_fwd", grid=(T // tt,),
        in_specs=[row(QKV)] + [row(128)] * 6 + [const((1, 128))] * 2 + [const((128, 128))],
        out_specs=[row(w) for w in widths],
        out_shape=[_sds((T, w), BF) for w in widths],
        compiler_params=_cp("parallel"),
    )(hq, *tab_a, *tab_b, qn, kn, gm)


def _prep_bwd(hq, grads, tab_a, tab_b, qn, kn, gm):
    T = hq.shape[0]
    tt = min(256, T)
    widths = [A_W, A_W, A_W, BQ_W, BKV_W, BKV_W, C_W, C_W, C_W]

    def body(h_ref, dqa, dka, dva, dqb, dkb, dvb, dqc, dkc, dvc,
             ca, s1a, s2a, cb, s1b, s2b, qn_ref, kn_ref, gm_ref, dh, dqn, dkn):
        i = pl.program_id(0)

        @pl.when(i == 0)
        def _():
            dqn[...] = jnp.zeros_like(dqn)
            dkn[...] = jnp.zeros_like(dkn)

        def put(off, j, val):
            dh[:, off + 128 * j: off + 128 * (j + 1)] = val.astype(dh.dtype)

        for j in range(2):
            sl = slice(128 * j, 128 * (j + 1))
            put(0, j, _rot_t(dqa[:, sl] * SCALE, ca[...], s1a[...], s2a[...], 8))
            put(256, j, _rot_t(dka[:, sl], ca[...], s1a[...], s2a[...], 8))
            put(512, j, dva[:, sl])
            put(1536, j, dqc[:, sl] * SCALE)
            put(1792, j, dkc[:, sl])
            put(2048, j, dvc[:, sl])

        def norm_bwd(x, w, e):
            ms = jnp.dot(x * x, gm_ref[...], precision=HI, preferred_element_type=F32)
            r = lax.rsqrt(ms + RMS_EPS)
            n = x * r
            dn = e * w
            proj = jnp.dot(dn * n, gm_ref[...], precision=HI, preferred_element_type=F32)
            return r * (dn - n * proj), jnp.sum(e * n, axis=0, keepdims=True)

        for j in range(4):
            sl = slice(128 * j, 128 * (j + 1))
            e = _rot_t(dqb[:, sl] * SCALE, cb[...], s1b[...], s2b[...], 16)
            dx, dw = norm_bwd(h_ref[:, 768 + 128 * j: 768 + 128 * (j + 1)], qn_ref[...], e)
            put(768, j, dx)
            dqn[:, sl] += dw
        e = _rot_t(dkb[...], cb[...], s1b[...], s2b[...], 16)
        dx, dw = norm_bwd(h_ref[:, 1280:1408], kn_ref[...], e)
        put(1280, 0, dx)
        dkn[...] += dw
        put(1408, 0, dvb[...])

    row = lambda w, c=0: pl.BlockSpec((tt, w), lambda i: (i, c))
    const = lambda s: pl.BlockSpec(s, lambda i: (0, 0))
    return _pcall(
        body, name="prep_bwd", grid=(T // tt,),
        in_specs=[row(QKV)] + [row(A_W, c) for c in range(3)] + [row(w) for w in widths[3:]] + [row(128)] * 6
        + [const((1, 128))] * 2 + [const((128, 128))],
        out_specs=[row(QKV), const((1, BQ_W)), const((1, BKV_W))],
        out_shape=[_sds((T, QKV), BF), _sds((1, BQ_W), F32), _sds((1, BKV_W), F32)],
        compiler_params=_cp("arbitrary"),
    )(hq, grads[0], grads[0], grads[0], *grads[1:], *tab_a, *tab_b, qn, kn, gm)


def _to_heads(x):
    T, W = x.shape
    return x.reshape(T, W // HD, HD).transpose(1, 0, 2)


def _from_heads(x):
    H, T, _ = x.shape
    return x.transpose(1, 0, 2).reshape(T, H * HD)


def _rows(x, t):
    H, T, _ = x.shape
    return x.reshape(H, T // t, 1, t)


def _nt(a, b):
    return lax.dot_general(a, b, (((1,), (1,)), ((), ())), preferred_element_type=F32)


def _chunked_t(x, t):
    T, W = x.shape
    return x.reshape(T // t, t, W // HD, HD).transpose(2, 0, 3, 1)


def _unchunk_t(x):
    H, n, _, t = x.shape
    return x.transpose(1, 3, 0, 2).reshape(n * t, H * HD)


FULL_T = 512
FULL_TK = 512
FULL_HEADS = 1
V_ROWS = 72


def _phased(phases, grid, at):
    step = pl.program_id(0) * grid[1] + pl.program_id(1)
    last = grid[0] * grid[1] - 1
    for phase, frac in zip(phases, at):
        pl.when(step == int(round(frac * last)))(phase)


def _attn_full_fwd(qT, k, vT1, gather=()):
    Hq, nq, _, t = qT.shape
    Hk, T, _ = k.shape
    G = Hq // Hk
    nk, tk = vT1.shape[1], vT1.shape[3]

    HB = FULL_HEADS
    assert G % HB == 0
    ng = len(gather)
    grid = (Hq // HB, nq)

    def body(q_ref, k_ref, v_ref, *rest):
        x_refs, (o_ref, lse_ref), rest = rest[:ng], rest[ng:ng + 2], rest[ng + 2:]
        out_refs, acc_refs, sems = rest[:ng], rest[ng:ng + HB], rest[ng + HB:]
        if ng:
            _phased(_gather_phases(x_refs, out_refs, *sems), grid, (0.0, 0.75, 1.0))
        for acc_ref in acc_refs:
            acc_ref[...] = jnp.zeros((V_ROWS, t), F32)

        def scores(j, b):
            sT = jnp.dot(k_ref[pl.ds(pl.multiple_of(j * tk, tk), tk), :], q_ref[b], preferred_element_type=F32)
            return sT, jnp.max(sT, axis=0, keepdims=True)

        def update(j, b, scored, m_old):
            sT, m_tile = scored
            m_new = jnp.maximum(m_old, m_tile)
            pT = jnp.exp(sT - m_new).astype(BF)
            acc_refs[b][...] = (jnp.exp(m_old - m_new) * acc_refs[b][...]
                                + jnp.dot(v_ref[j], pT, preferred_element_type=F32))
            return m_new

        def step(j, carry):
            ms, ss = carry
            nxt = jnp.minimum(j + 1, nk - 1)
            new_s = tuple(scores(nxt, b) for b in range(HB))
            new_m = tuple(update(j, b, ss[b], ms[b]) for b in range(HB))
            return new_m, new_s

        init = (tuple(jnp.full((1, t), NEG, F32) for _ in range(HB)), tuple(scores(0, b) for b in range(HB)))
        ms, _ = lax.fori_loop(0, nk, step, init)
        for b in range(HB):
            l = acc_refs[b][pl.ds(HD, 1), :]
            o_ref[b] = (acc_refs[b][pl.ds(0, HD), :] / l).astype(o_ref.dtype)
            lse_ref[b] = ms[b] + jnp.log(l)

    qs = pl.BlockSpec((HB, None, HD, t), lambda h, i: (h, i, 0, 0))
    res = (_pcall_comm if ng else _pcall)(
        body, name="attn_full_fwd_gather" if ng else "attn_full_fwd", grid=grid,
        in_specs=[qs, pl.BlockSpec((None, T, HD), lambda h, i: (h * HB // G, 0, 0)),
                  pl.BlockSpec((None, nk, V_ROWS, tk), lambda h, i: (h * HB // G, 0, 0, 0))] + [HBM] * ng,
        out_specs=[qs, pl.BlockSpec((HB, None, 1, t), lambda h, i: (h, i, 0, 0))] + [HBM] * ng,
        out_shape=[_sds((Hq, nq, HD, t), BF), _sds((Hq, nq, 1, t), F32)]
        + [_sds((8,) + x.shape, x.dtype) for x in gather],
        scratch_shapes=[pltpu.VMEM((V_ROWS, t), F32)] * HB + (_gather_scratch(ng) if ng else []),
        compiler_params=_cp("arbitrary", "arbitrary") if ng else _cp("parallel", "parallel"),
    )(qT, k, vT1, *gather)
    return res[0], res[1], list(res[2:])


def _attn_delta(doT, oT):
    Hq, nq, _, t = doT.shape

    def body(do_ref, o_ref, dl_ref):
        dl_ref[...] = jnp.sum(do_ref[...].astype(F32) * o_ref[...].astype(F32), axis=1, keepdims=True)

    qs = pl.BlockSpec((None, nq, HD, t), lambda h: (h, 0, 0, 0))
    rs = pl.BlockSpec((None, nq, 1, t), lambda h: (h, 0, 0, 0))
    return _pcall(body, name="attn_delta", grid=(Hq,), in_specs=[qs, qs], out_specs=rs,
                  out_shape=_sds((Hq, nq, 1, t), F32), compiler_params=_cp("parallel"))(doT, oT)


def _attn_full_bwd(qT, k, kT, v, doT, lse, delta, scatter=()):
    Hq, nq, _, t = qT.shape
    Hk, T, _ = k.shape
    G = Hq // Hk
    nkv = T // t
    ns = len(scatter)

    def body(qT_ref, doT_ref, lse_ref, dl_ref, k_ref, kT_ref, v_ref, *rest):
        g_refs, (dq_ref, dk_ref, dv_ref), rest = rest[:ns], rest[ns:ns + 3], rest[ns + 3:]
        land_refs, (dq_acc, dk_acc, dv_acc), sems = rest[:ns], rest[ns:ns + 3], rest[ns + 3:]
        if ns:
            _phased(_scatter_phases(g_refs, land_refs, *sems), (Hq, nkv), (0.0, 1.0))
        j = pl.program_id(1)

        @pl.when(j == 0)
        def _():
            dq_acc[...] = jnp.zeros_like(dq_acc)

        kv, kTv, vv = k_ref[...], kT_ref[...], v_ref[...]
        dk_acc[...] = jnp.zeros((t, HD), F32)
        dv_acc[...] = jnp.zeros((t, HD), F32)

        def step(i, carry):
            qT, doT = qT_ref[i], doT_ref[i]
            pT = jnp.exp(jnp.dot(kv, qT, preferred_element_type=F32) - lse_ref[i])
            dv_acc[...] += _nt(pT.astype(BF), doT)
            dsT = (pT * (jnp.dot(vv, doT, preferred_element_type=F32) - dl_ref[i])).astype(BF)
            dk_acc[...] += _nt(dsT, qT)
            dq_acc[i] += jnp.dot(kTv, dsT, preferred_element_type=F32)
            return carry

        lax.fori_loop(0, nq, step, 0)
        dk_ref[...] = dk_acc[...].astype(dk_ref.dtype)
        dv_ref[...] = dv_acc[...].astype(dv_ref.dtype)

        @pl.when(j == nkv - 1)
        def _():
            dq_ref[...] = dq_acc[...].astype(dq_ref.dtype)

    chk = pl.BlockSpec((None, nq, HD, t), lambda h, j: (h, 0, 0, 0))
    row = pl.BlockSpec((None, nq, 1, t), lambda h, j: (h, 0, 0, 0))
    kvs = pl.BlockSpec((None, t, HD), lambda h, j: (h // G, j, 0))
    out = pl.BlockSpec((None, t, HD), lambda h, j: (h, j, 0))
    res = (_pcall_comm if ns else _pcall)(
        body, name="attn_full_bwd_scatter" if ns else "attn_full_bwd", grid=(Hq, nkv),
        in_specs=[chk, chk, row, row, kvs, pl.BlockSpec((None, HD, t), lambda h, j: (h // G, 0, j)), kvs] + [HBM] * ns,
        out_specs=[chk, out, out] + [HBM] * ns,
        out_shape=[_sds((Hq, nq, HD, t), BF), _sds((Hq, T, HD), BF), _sds((Hq, T, HD), BF)]
        + [_sds((7,) + g.shape[1:], g.dtype) for g in scatter],
        scratch_shapes=[pltpu.VMEM((nq, HD, t), F32), pltpu.VMEM((t, HD), F32), pltpu.VMEM((t, HD), F32)]
        + (_scatter_scratch(ns) if ns else []),
        compiler_params=_cp("arbitrary" if ns else "parallel", "arbitrary"),
    )(qT, doT, lse, delta, k, kT, v, *scatter)
    return res[0], res[1], res[2], list(res[3:])


DIL_Q = 128
DIL_K = 256
DIL_R = 64


def _vh(x, d):
    T = x.shape[0]
    return x.reshape(T // d, d, 4, HD).transpose(1, 2, 0, 3).reshape(4 * d, T // d, HD)


def _vh_inv(y, d):
    L = y.shape[1]
    return y.reshape(d, 4, L, HD).transpose(2, 0, 1, 3).reshape(L * d, 4 * HD)


def _vh_chunks(x, d):
    y = _vh(x, d)
    return y.reshape(y.shape[0], y.shape[1] // DIL_Q, DIL_Q, HD).transpose(0, 1, 3, 2)


def _chunks_to_dims(c, d):
    _, nq, R, _ = c.shape
    return c.reshape(d, 4, nq, R, DIL_Q).transpose(1, 3, 2, 4, 0).reshape(4, R, nq * DIL_Q * d)


def _dims_to_chunks(x, d):
    _, R, T = x.shape
    nq = T // (DIL_Q * d)
    return x.reshape(4, R, nq, DIL_Q, d).transpose(4, 0, 2, 1, 3).reshape(4 * d, nq, R, DIL_Q)


def _chunks_to_tokens(c, d):
    _, nq, _, _ = c.shape
    return c.reshape(d, 4, nq, HD, DIL_Q).transpose(2, 4, 0, 1, 3).reshape(nq * DIL_Q * d, 4 * HD)


def _dil_window(i, L):
    start = pl.multiple_of(jnp.clip(i * DIL_Q - DIL_R, 0, L - DIL_K), DIL_R)
    kk = start + lax.broadcasted_iota(jnp.int32, (DIL_K, 1), 0)
    qq = i * DIL_Q + lax.broadcasted_iota(jnp.int32, (1, DIL_Q), 1)
    return start, jnp.abs(kk - qq) <= DIL_R


def _tn(a, b):
    return lax.dot_general(a, b, (((0,), (0,)), ((), ())), preferred_element_type=F32)


def _dil_fwd(qT, k, v):
    V, nq, _, _ = qT.shape
    L = k.shape[1]
    assert L >= DIL_K

    unroll = 4 if nq % 4 == 0 else 1

    def body(q_ref, k_ref, v_ref, o_ref, lse_ref):
        def tile(i):
            start, mask = _dil_window(i, L)
            win = pl.ds(start, DIL_K)
            sT = jnp.where(mask, jnp.dot(k_ref[win, :], q_ref[i], preferred_element_type=F32), NEG)
            m = jnp.max(sT, axis=0, keepdims=True)
            pT = jnp.exp(sT - m).astype(BF)
            acc = _tn(v_ref[win, :], pT)
            l = jnp.max(acc[HD:HD + 8], axis=0, keepdims=True)
            o_ref[i] = acc[:HD] / l
            lse_ref[i] = m + jnp.log(l)

        def tiles(ii, carry):
            for u in range(unroll):
                tile(ii * unroll + u)
            return carry

        lax.fori_loop(0, nq // unroll, tiles, 0)

    chk = lambda r, dt: (pl.BlockSpec((None, nq, r, DIL_Q), lambda h: (h, 0, 0, 0)), _sds((V, nq, r, DIL_Q), dt))
    tok = lambda w: pl.BlockSpec((None, L, w), lambda h: (h, 0, 0))
    (o_spec, o_shape), (l_spec, l_shape) = chk(HD, F32), chk(1, F32)
    return _pcall(
        body, name=f"dil_fwd_{V // 4}", grid=(V,),
        in_specs=[chk(HD, BF)[0], tok(HD), tok(2 * HD)],
        out_specs=[o_spec, l_spec], out_shape=[o_shape, l_shape],
        compiler_params=_cp("parallel"),
    )(qT, k, v)


def _dil_merge(os_, lses):
    _, _, T = os_[0].shape
    tt = min(1024, T)
    n = len(os_)

    def body(*refs):
        o_refs, l_refs, (o_out, l_out) = refs[:n], refs[n:2 * n], refs[2 * n:]
        m = l_refs[0][...]
        for r in l_refs[1:]:
            m = jnp.maximum(m, r[...])
        ws = [jnp.exp(r[...] - m) for r in l_refs]
        tot = ws[0]
        for w_ in ws[1:]:
            tot = tot + w_
        acc = ws[0] * o_refs[0][...]
        for w_, o in zip(ws[1:], o_refs[1:]):
            acc = acc + w_ * o[...]
        o_out[...] = (acc / tot).astype(o_out.dtype)
        l_out[...] = m + jnp.log(tot)

    os_spec = pl.BlockSpec((None, HD, tt), lambda h, i: (h, 0, i))
    ls_spec = pl.BlockSpec((None, 1, tt), lambda h, i: (h, 0, i))
    return _pcall(
        body, name="dil_merge", grid=(4, T // tt),
        in_specs=[os_spec] * n + [ls_spec] * n, out_specs=[os_spec, ls_spec],
        out_shape=[_sds((4, HD, T), BF), _sds((4, 1, T), F32)],
        compiler_params=_cp("parallel", "parallel"),
    )(*os_, *lses)


def _dims_delta(doT, oT):
    _, _, T = doT.shape
    tt = min(2048, T)

    def body(do_ref, o_ref, dl_ref):
        dl_ref[...] = jnp.sum(do_ref[...].astype(F32) * o_ref[...].astype(F32), axis=0, keepdims=True)

    spec = pl.BlockSpec((None, HD, tt), lambda h, i: (h, 0, i))
    return _pcall(body, name="dims_delta", grid=(4, T // tt), in_specs=[spec, spec],
                  out_specs=pl.BlockSpec((None, 1, tt), lambda h, i: (h, 0, i)),
                  out_shape=_sds((4, 1, T), F32), compiler_params=_cp("parallel", "parallel"))(doT, oT)


def _dil_bwd(qT, k, v, doT, lse, delta):
    V, nq, _, _ = qT.shape
    L = k.shape[1]
    unroll = 2 if nq % 2 == 0 else 1

    def body(qT_ref, k_ref, v_ref, doT_ref, lse_ref, dl_ref, out_ref, dk_acc, dv_acc):
        dk_acc[...] = jnp.zeros_like(dk_acc)
        dv_acc[...] = jnp.zeros_like(dv_acc)

        def tile(i):
            start, mask = _dil_window(i, L)
            win = pl.ds(start, DIL_K)
            kw, qT, doT = k_ref[win, :], qT_ref[i], doT_ref[i]
            sT = jnp.where(mask, jnp.dot(kw, qT, preferred_element_type=F32), NEG)
            pT = jnp.exp(sT - lse_ref[i])
            dv = _nt(pT.astype(BF), doT)
            dsT = (pT * (jnp.dot(v_ref[win, :], doT, preferred_element_type=F32) - dl_ref[i])).astype(BF)
            dk = _nt(dsT, qT)
            out_ref[0, pl.ds(pl.multiple_of(i * DIL_Q, DIL_Q), DIL_Q), :] = _tn(dsT, kw).astype(out_ref.dtype)
            return win, dk, dv

        def tiles(ii, carry):
            done = [tile(ii * unroll + u) for u in range(unroll)]
            for win, dk, dv in done:
                dk_acc[win, :] += dk
                dv_acc[win, :] += dv
            return carry

        lax.fori_loop(0, nq // unroll, tiles, 0)
        out_ref[1] = dk_acc[...].astype(out_ref.dtype)
        out_ref[2] = dv_acc[...].astype(out_ref.dtype)

    chk = lambda r: pl.BlockSpec((None, nq, r, DIL_Q), lambda h: (h, 0, 0, 0))
    tok = pl.BlockSpec((None, L, HD), lambda h: (h, 0, 0))
    return _pcall(
        body, name=f"dil_bwd_{V // 4}", grid=(V,),
        in_specs=[chk(HD), tok, tok, chk(HD), chk(1), chk(1)],
        out_specs=pl.BlockSpec((None, 3, L, HD), lambda h: (h, 0, 0, 0)),
        out_shape=_sds((V, 3, L, HD), BF),
        scratch_shapes=[pltpu.VMEM((L, HD), F32)] * 2,
        compiler_params=_cp("parallel"),
    )(qT, k, v, doT, lse, delta)


def _dilated_fwd(qa, ka, va):
    outs, lses, saved = [], [], []
    for _, d in A_CONFIGS:
        qT, k, v = _vh_chunks(qa, d), _vh(ka, d), _vh(va, d)
        oT, lse = _dil_fwd(qT, k, jnp.concatenate([v, jnp.ones_like(v)], axis=2))
        outs.append(_chunks_to_dims(oT, d))
        lses.append(_chunks_to_dims(lse, d))
        saved.append((qT, k, v))
    o, lse = _dil_merge(outs, lses)
    return o, lse, saved


def _dilated_bwd(do_t, o, lse, saved):
    T = do_t.shape[0]
    doT = do_t.reshape(T, 4, HD).transpose(1, 2, 0)
    delta = _dims_delta(doT, o)
    total = None
    for (_, d), (qT, k, v) in zip(A_CONFIGS, saved):
        g = _dil_bwd(qT, k, v, _dims_to_chunks(doT, d), _dims_to_chunks(lse, d), _dims_to_chunks(delta, d))
        g = g.reshape(d, 4, 3, T // d, HD).transpose(3, 0, 2, 1, 4).reshape(T, 3 * A_W).astype(F32)
        total = g if total is None else total + g
    return total.astype(do_t.dtype)


NBR_Q = 128
NBR_ROWS = 10
NBR_K = NBR_ROWS * GRID_W


def _nbr_class_tiles(nq):
    return [2, 0, 1, nq - 2, nq - 1]


def _nbr_geometry(T):
    rows, nq = T // GRID_W, T // NBR_Q
    assert rows >= NBR_ROWS + 4 and nq >= 5
    kr, kc = np.divmod(np.arange(NBR_K), GRID_W)
    qr, qc = np.divmod(np.arange(NBR_Q), GRID_W)
    c0 = np.clip(qc - 8, 0, GRID_W - 16)
    col_ok = (kc[:, None] >= c0[None, :]) & (kc[:, None] < c0[None, :] + 16)
    drs, valids = [], []
    for i in _nbr_class_tiles(nq):
        start_row = int(np.clip(2 * i - 4, 0, rows - NBR_ROWS))
        r = 2 * i + qr
        r0 = np.clip(r - 4, 0, rows - 8)
        rk = start_row + kr
        row_ok = (rk[:, None] >= r0[None, :]) & (rk[:, None] < r0[None, :] + 8)
        valids.append(row_ok & col_ok)
        dr = start_row + np.arange(NBR_ROWS)[:, None] - (2 * i + np.arange(2)[None, :]) + 7
        drs.append(np.where((dr >= 0) & (dr <= 14), dr, -1))
    return np.stack(drs), np.stack(valids)


def _nbr_fold_matrices(T):
    dr, _ = _nbr_geometry(T)
    e1 = np.zeros((GRID_W * GRID_W, 128), np.float32)
    kc, qc = np.meshgrid(np.arange(GRID_W), np.arange(GRID_W), indexing="ij")
    dc = (kc - qc + 15).reshape(-1)
    keep = (dc >= 0) & (dc <= 30)
    e1[np.arange(GRID_W * GRID_W)[keep], dc[keep]] = 1.0
    n = 5 * NBR_ROWS * 2
    e2 = np.zeros((64, 4 * n), np.float32)
    for h in range(4):
        for j, d in enumerate(dr.reshape(-1)):
            if d >= 0:
                e2[h * 16 + d, h * n + j] = 1.0
    return jnp.asarray(e1), jnp.asarray(e2)


def _nbr_bias_blocks(rpb, T):
    e1, e2 = _nbr_fold_matrices(T)
    _, valid = _nbr_geometry(T)
    padded = jnp.pad(rpb, ((0, 0), (0, 1), (0, 128 - rpb.shape[2]))).reshape(64, 128)

    def body(e2t_ref, rpb_ref, e1t_ref, out_ref):
        picked = jnp.dot(e2t_ref[...], rpb_ref[...], precision=HI, preferred_element_type=F32)
        out_ref[...] = jnp.dot(picked, e1t_ref[...], precision=HI, preferred_element_type=F32)

    sub = _pcall(body, name="rpb_expand", out_shape=_sds((e2.shape[1], GRID_W * GRID_W), F32),
                 compiler_params=pltpu.CompilerParams(vmem_limit_bytes=VMEM_LIMIT))(e2.T, padded, e1.T)
    blocks = sub.reshape(4, 5, NBR_ROWS, 2, GRID_W, GRID_W).transpose(0, 1, 2, 4, 3, 5).reshape(4, 5, NBR_K, NBR_Q)
    return jnp.where(valid[None], blocks, NEG)


def _nbr_rpb_grad(dbias, T):
    e1, e2 = _nbr_fold_matrices(T)
    sub = dbias.reshape(4, 5, NBR_ROWS, GRID_W, 2, GRID_W).transpose(0, 1, 2, 4, 3, 5).reshape(-1, GRID_W * GRID_W)

    def body(e2_ref, sub_ref, e1_ref, out_ref):
        diag = jnp.dot(sub_ref[...], e1_ref[...], precision=HI, preferred_element_type=F32)
        out_ref[...] = jnp.dot(e2_ref[...], diag, precision=HI, preferred_element_type=F32)

    out = _pcall(body, name="rpb_fold", out_shape=_sds((64, 128), F32),
                 compiler_params=pltpu.CompilerParams(vmem_limit_bytes=VMEM_LIMIT))(e2, sub, e1)
    return out.reshape(4, 16, 128)[:, :15, :31]


def _nbr_tile(i, nq, T):
    start = pl.multiple_of(jnp.clip(i * NBR_Q - 4 * GRID_W, 0, T - NBR_K), NBR_Q)
    cls = jnp.where(i == 0, 1, jnp.where(i == 1, 2, jnp.where(i == nq - 2, 3, jnp.where(i == nq - 1, 4, 0))))
    return start, cls


def _nbr_fwd(qT, k, v, bias):
    H, nq, _, _ = qT.shape
    T = k.shape[1]
    unroll = 2 if nq % 2 == 0 else 1

    def body(q_ref, k_ref, v_ref, b_ref, o_ref, lse_ref):
        def tile(i):
            start, cls = _nbr_tile(i, nq, T)
            win = pl.ds(start, NBR_K)
            sT = jnp.dot(k_ref[win, :], q_ref[i], preferred_element_type=F32) + b_ref[cls]
            m = jnp.max(sT, axis=0, keepdims=True)
            pT = jnp.exp(sT - m).astype(BF)
            acc = _tn(v_ref[win, :], pT)
            l = jnp.max(acc[HD:HD + 8], axis=0, keepdims=True)
            o_ref[i] = (acc[:HD] / l).astype(o_ref.dtype)
            lse_ref[i] = m + jnp.log(l)

        def tiles(ii, carry):
            for u in range(unroll):
                tile(ii * unroll + u)
            return carry

        lax.fori_loop(0, nq // unroll, tiles, 0)

    chk = lambda r: pl.BlockSpec((None, nq, r, NBR_Q), lambda h: (h, 0, 0, 0))
    tok = lambda w: pl.BlockSpec((None, T, w), lambda h: (h, 0, 0))
    return _pcall(
        body, name="nbr_fwd", grid=(H,),
        in_specs=[chk(HD), tok(HD), tok(2 * HD), pl.BlockSpec((None, 5, NBR_K, NBR_Q), lambda h: (h, 0, 0, 0))],
        out_specs=[chk(HD), chk(1)],
        out_shape=[_sds((H, nq, HD, NBR_Q), BF), _sds((H, nq, 1, NBR_Q), F32)],
        compiler_params=_cp("parallel"),
    )(qT, k, v, bias)


def _nbr_bwd(qT, k, v, doT, lse, delta, bias):
    H, nq, _, _ = qT.shape
    T = k.shape[1]

    def body(qT_ref, k_ref, v_ref, doT_ref, lse_ref, dl_ref, b_ref, dq_ref, dk_ref, dv_ref, db_ref, dk_acc, dv_acc):
        dk_acc[...] = jnp.zeros_like(dk_acc)
        dv_acc[...] = jnp.zeros_like(dv_acc)
        db_ref[...] = jnp.zeros_like(db_ref)

        def tile(i, carry):
            start, cls = _nbr_tile(i, nq, T)
            win = pl.ds(start, NBR_K)
            kw, qT, doT = k_ref[win, :], qT_ref[i], doT_ref[i]
            sT = jnp.dot(kw, qT, preferred_element_type=F32) + b_ref[cls]
            pT = jnp.exp(sT - lse_ref[i])
            dv_acc[win, :] += _nt(pT.astype(BF), doT)
            ds = pT * (jnp.dot(v_ref[win, :], doT, preferred_element_type=F32) - dl_ref[i])
            db_ref[cls] += ds
            dsT = ds.astype(BF)
            dk_acc[win, :] += _nt(dsT, qT)
            dq_ref[i] = _tn(kw, dsT).astype(dq_ref.dtype)
            return carry

        lax.fori_loop(0, nq, tile, 0)
        dk_ref[...] = dk_acc[...].astype(dk_ref.dtype)
        dv_ref[...] = dv_acc[...].astype(dv_ref.dtype)

    chk = lambda r: pl.BlockSpec((None, nq, r, NBR_Q), lambda h: (h, 0, 0, 0))
    tok = pl.BlockSpec((None, T, HD), lambda h: (h, 0, 0))
    bsp = pl.BlockSpec((None, 5, NBR_K, NBR_Q), lambda h: (h, 0, 0, 0))
    return _pcall(
        body, name="nbr_bwd", grid=(H,),
        in_specs=[chk(HD), tok, tok, chk(HD), chk(1), chk(1), bsp],
        out_specs=[chk(HD), tok, tok, bsp],
        out_shape=[_sds((H, nq, HD, NBR_Q), BF), _sds((H, T, HD), BF), _sds((H, T, HD), BF),
                   _sds((H, 5, NBR_K, NBR_Q), F32)],
        scratch_shapes=[pltpu.VMEM((T, HD), F32)] * 2,
        compiler_params=_cp("parallel"),
    )(qT, k, v, doT, lse, delta, bias)


def _band_offsets(radius):
    offs = [0]
    for r in range(1, radius + 1):
        offs += [-r, r]
    return offs


def _dilated_bias(t):
    radius = max(w // 2 for w, _ in A_CONFIGS) // t
    tabs = []
    i = np.arange(t)
    for off in _band_offsets(radius):
        d = off * t + i[None, :] - i[:, None]
        mult = np.zeros((t, t), np.float32)
        for w, dil in A_CONFIGS:
            mult += ((d % dil) == 0) & (np.abs(d) <= w // 2)
        tabs.append(np.where(mult > 0, np.log(np.maximum(mult, 1.0)), NEG).astype(np.float32))
    return jnp.asarray(np.stack(tabs)[None]), radius


def _nbr_index(t):
    rpt = t // GRID_W
    i = np.arange(t)
    c0 = np.clip(i % GRID_W - 8, 0, GRID_W - 16)
    col_ok = ((i[None, :] % GRID_W) >= c0[:, None]) & ((i[None, :] % GRID_W) < c0[:, None] + 16)
    oks = []
    for off in _band_offsets(1):
        dr = (i[None, :] // GRID_W) - (i[:, None] // GRID_W) + rpt * off
        oks.append(col_ok & (np.abs(dr) <= 7))
    return np.stack(oks)


def _nbr_bias(rpb, t):
    rpt = t // GRID_W
    e1, e2 = _rpb_fold_matrices(t)
    ok = _nbr_index(t)
    padded = jnp.pad(rpb, ((0, 0), (0, 1), (0, 128 - rpb.shape[2]))).reshape(64, 128)

    def body(e2t_ref, rpb_ref, e1t_ref, out_ref):
        picked = jnp.dot(e2t_ref[...], rpb_ref[...], precision=HI, preferred_element_type=F32)
        out_ref[...] = jnp.dot(picked, e1t_ref[...], precision=HI, preferred_element_type=F32)

    n = e2.shape[1]
    sub = _pcall(body, name="rpb_expand", out_shape=_sds((n, GRID_W * GRID_W), F32),
                 compiler_params=pltpu.CompilerParams(vmem_limit_bytes=VMEM_LIMIT))(e2.T, padded, e1.T)
    tiles = sub.reshape(4, 3, rpt, rpt, GRID_W, GRID_W).transpose(0, 1, 2, 4, 3, 5).reshape(4, 3, t, t)
    return jnp.where(ok[None], tiles, NEG)


def _nbr_mask(qi, kb, t, rows, q_on_lanes):
    rpt = t // GRID_W
    qshape, kshape = ((1, t), (t, 1)) if q_on_lanes else ((t, 1), (1, t))
    rq = rpt * qi + lax.broadcasted_iota(jnp.int32, qshape, 1 if q_on_lanes else 0) // GRID_W
    rk = rpt * kb + lax.broadcasted_iota(jnp.int32, kshape, 0 if q_on_lanes else 1) // GRID_W
    r0 = jnp.clip(rq - 4, 0, rows - 8)
    return (rk >= r0) & (rk < r0 + 8)


def _attn_band_fwd(q, k, v, bias, radius, rowmask):
    H, T, _ = q.shape
    t = BAND_T
    nq = T // t
    Hb = bias.shape[0]
    offs = _band_offsets(radius)
    rows = T // GRID_W

    def body(q_ref, k_ref, v_ref, b_ref, o_ref, lse_ref, m_ref, l_ref, acc_ref):
        i = pl.program_id(1)
        qv = q_ref[...]
        m_ref[...] = jnp.full((t, 1), NEG, F32)
        l_ref[...] = jnp.zeros((t, 1), F32)
        acc_ref[...] = jnp.zeros((t, HD), F32)

        def tile(o, off):
            kb = i + off
            st = pl.multiple_of(kb * t, t)
            ks, vs = k_ref[pl.ds(st, t), :], v_ref[pl.ds(st, t), :]
            s = _nt(qv, ks) + b_ref[o]
            if rowmask:
                s = jnp.where(_nbr_mask(i, kb, t, rows, False), s, NEG)
            m_old = m_ref[...]
            m_new = jnp.maximum(m_old, jnp.max(s, axis=-1, keepdims=True))
            a = jnp.exp(m_old - m_new)
            p = jnp.exp(s - m_new)
            l_ref[...] = a * l_ref[...] + jnp.sum(p, axis=-1, keepdims=True)
            acc_ref[...] = a * acc_ref[...] + jnp.dot(p.astype(BF), vs, preferred_element_type=F32)
            m_ref[...] = m_new

        for o, off in enumerate(offs):
            if off == 0:
                tile(o, off)
            else:
                pl.when((i + off >= 0) & (i + off < nq))(functools.partial(tile, o, off))
        o_ref[...] = (acc_ref[...] / l_ref[...]).astype(o_ref.dtype)
        lse_ref[...] = m_ref[...] + jnp.log(l_ref[...])

    qs = pl.BlockSpec((None, t, HD), lambda h, i: (h, i, 0))
    kvs = pl.BlockSpec((None, T, HD), lambda h, i: (h, 0, 0))
    bs = pl.BlockSpec((None, len(offs), t, t), lambda h, i: (h if Hb > 1 else 0, 0, 0, 0))
    return _pcall(
        body, name="attn_band_fwd_c" if rowmask else "attn_band_fwd_a", grid=(H, nq),
        in_specs=[qs, kvs, kvs, bs],
        out_specs=[qs, pl.BlockSpec((None, t, 1), lambda h, i: (h, i, 0))],
        out_shape=[_sds((H, T, HD), BF), _sds((H, T, 1), F32)],
        scratch_shapes=[pltpu.VMEM((t, 1), F32), pltpu.VMEM((t, 1), F32), pltpu.VMEM((t, HD), F32)],
        compiler_params=_cp("parallel", "parallel"),
    )(q, k, v, bias)


def _attn_band_dq(q, k, v, o, do, lse, bias, radius, rowmask):
    H, T, _ = q.shape
    t = BAND_T
    nq = T // t
    Hb = bias.shape[0]
    offs = _band_offsets(radius)
    rows = T // GRID_W

    def body(q_ref, k_ref, v_ref, o_ref, do_ref, lse_ref, b_ref, dq_ref, dl_ref, *rest):
        db_ref = rest[0] if rowmask else None
        acc_ref = rest[-1]
        i = pl.program_id(1)
        qv, dov, lse = q_ref[...], do_ref[...], lse_ref[...]
        delta = jnp.sum(dov.astype(F32) * o_ref[...].astype(F32), axis=-1, keepdims=True)
        acc_ref[...] = jnp.zeros((t, HD), F32)
        if rowmask:
            @pl.when(i == 0)
            def _():
                db_ref[...] = jnp.zeros_like(db_ref)

        def tile(o, off):
            kb = i + off
            st = pl.multiple_of(kb * t, t)
            ks, vs = k_ref[pl.ds(st, t), :], v_ref[pl.ds(st, t), :]
            s = _nt(qv, ks) + b_ref[o]
            if rowmask:
                s = jnp.where(_nbr_mask(i, kb, t, rows, False), s, NEG)
            p = jnp.exp(s - lse)
            ds = p * (_nt(dov, vs) - delta)
            acc_ref[...] += jnp.dot(ds.astype(BF), ks, preferred_element_type=F32)
            if rowmask:
                db_ref[o] += ds

        for o, off in enumerate(offs):
            if off == 0:
                tile(o, off)
            else:
                pl.when((i + off >= 0) & (i + off < nq))(functools.partial(tile, o, off))
        dq_ref[...] = acc_ref[...]
        dl_ref[...] = delta

    qs = pl.BlockSpec((None, t, HD), lambda h, i: (h, i, 0))
    kvs = pl.BlockSpec((None, T, HD), lambda h, i: (h, 0, 0))
    cs = pl.BlockSpec((None, t, 1), lambda h, i: (h, i, 0))
    bs = pl.BlockSpec((None, len(offs), t, t), lambda h, i: (h if Hb > 1 else 0, 0, 0, 0))
    out_specs = [qs, cs]
    out_shape = [_sds((H, T, HD), F32), _sds((H, T, 1), F32)]
    if rowmask:
        out_specs.append(pl.BlockSpec((None, len(offs), t, t), lambda h, i: (h, 0, 0, 0)))
        out_shape.append(_sds((H, len(offs), t, t), F32))
    return _pcall(
        body, name="attn_band_dq_c" if rowmask else "attn_band_dq_a", grid=(H, nq),
        in_specs=[qs, kvs, kvs, qs, qs, cs, bs],
        out_specs=out_specs, out_shape=out_shape,
        scratch_shapes=[pltpu.VMEM((t, HD), F32)],
        compiler_params=_cp("parallel", "arbitrary"),
    )(q, k, v, o, do, lse, bias)


def _attn_band_dkv(q, k, v, do, lse, delta, bias_t, radius, rowmask):
    H, T, _ = q.shape
    t = BAND_T
    nq = T // t
    Hb = bias_t.shape[0]
    offs = _band_offsets(radius)
    rows = T // GRID_W
    lse_r, dl_r = _rows(lse, t), _rows(delta, t)

    def body(k_ref, v_ref, q_ref, do_ref, lse_ref, dl_ref, b_ref, dk_ref, dv_ref, dk_acc, dv_acc):
        jb = pl.program_id(1)
        kv, vv = k_ref[...], v_ref[...]
        dk_acc[...] = jnp.zeros((t, HD), F32)
        dv_acc[...] = jnp.zeros((t, HD), F32)

        def tile(o, off):
            qi = jb - off
            st = pl.multiple_of(qi * t, t)
            qs, dos = q_ref[pl.ds(st, t), :], do_ref[pl.ds(st, t), :]
            sT = _nt(kv, qs) + b_ref[o]
            if rowmask:
                sT = jnp.where(_nbr_mask(qi, jb, t, rows, True), sT, NEG)
            pT = jnp.exp(sT - lse_ref[qi])
            dv_acc[...] += jnp.dot(pT.astype(BF), dos, preferred_element_type=F32)
            dsT = pT * (_nt(vv, dos) - dl_ref[qi])
            dk_acc[...] += jnp.dot(dsT.astype(BF), qs, preferred_element_type=F32)

        for o, off in enumerate(offs):
            if off == 0:
                tile(o, off)
            else:
                pl.when((jb - off >= 0) & (jb - off < nq))(functools.partial(tile, o, off))
        dk_ref[...] = dk_acc[...]
        dv_ref[...] = dv_acc[...]

    kvs = pl.BlockSpec((None, t, HD), lambda h, j: (h, j, 0))
    qs = pl.BlockSpec((None, T, HD), lambda h, j: (h, 0, 0))
    rs = pl.BlockSpec((None, nq, 1, t), lambda h, j: (h, 0, 0, 0))
    bs = pl.BlockSpec((None, len(offs), t, t), lambda h, j: (h if Hb > 1 else 0, 0, 0, 0))
    return _pcall(
        body, name="attn_band_dkv_c" if rowmask else "attn_band_dkv_a", grid=(H, nq),
        in_specs=[kvs, kvs, qs, qs, rs, rs, bs],
        out_specs=[kvs, kvs],
        out_shape=[_sds((H, T, HD), F32)] * 2,
        scratch_shapes=[pltpu.VMEM((t, HD), F32)] * 2,
        compiler_params=_cp("parallel", "parallel"),
    )(k, v, q, do, lse_r, dl_r, bias_t)


def _rpb_fold_matrices(t):
    rpt = t // GRID_W
    e1 = np.zeros((GRID_W * GRID_W, 128), np.float32)
    ic, jc = np.meshgrid(np.arange(GRID_W), np.arange(GRID_W), indexing="ij")
    dc = (jc - ic + 15).reshape(-1)
    keep = (dc >= 0) & (dc <= 30)
    e1[np.arange(GRID_W * GRID_W)[keep], dc[keep]] = 1.0
    offs = _band_offsets(1)
    n = 4 * len(offs) * rpt * rpt
    e2 = np.zeros((64, n), np.float32)
    col = 0
    for h in range(4):
        for off in offs:
            for ib in range(rpt):
                for jb in range(rpt):
                    dr = jb - ib + rpt * off + 7
                    if 0 <= dr <= 14:
                        e2[h * 16 + dr, col] = 1.0
                    col += 1
    return jnp.asarray(e1), jnp.asarray(e2)


def _rpb_grad(dbias):
    t = dbias.shape[-1]
    rpt = t // GRID_W
    e1, e2 = _rpb_fold_matrices(t)
    sub = dbias.reshape(4, 3, rpt, GRID_W, rpt, GRID_W).transpose(0, 1, 2, 4, 3, 5)
    sub = sub.reshape(4 * 3 * rpt * rpt, GRID_W * GRID_W)

    def body(e2_ref, sub_ref, e1_ref, out_ref):
        diag = jnp.dot(sub_ref[...], e1_ref[...], precision=HI, preferred_element_type=F32)
        out_ref[...] = jnp.dot(e2_ref[...], diag, precision=HI, preferred_element_type=F32)

    out = _pcall(body, name="rpb_fold", out_shape=_sds((64, 128), F32),
                 compiler_params=pltpu.CompilerParams(vmem_limit_bytes=VMEM_LIMIT))(e2, sub, e1)
    return out.reshape(4, 16, 128)[:, :15, :31]


def _sigmoid(z):
    return 1.0 / (1.0 + jnp.exp(-z))


def _merge_fwd(oa, ob, oc, hg, bg, wa, wb, wc):
    T = oa.shape[0]
    tt = min(512, T)

    def body(oa_ref, ob_ref, oc_ref, hg_ref, bg_ref, wa_ref, wb_ref, wc_ref, out_ref):
        acc = None
        for k, (o_ref, w_ref) in enumerate(((oa_ref, wa_ref), (ob_ref, wb_ref), (oc_ref, wc_ref))):
            y = jnp.dot(o_ref[...], w_ref[...], preferred_element_type=F32)
            g = _sigmoid(hg_ref[:, D * k:D * (k + 1)] + bg_ref[:, D * k:D * (k + 1)])
            acc = g * y if acc is None else acc + g * y
        out_ref[...] = acc.astype(out_ref.dtype)

    row = lambda w: pl.BlockSpec((tt, w), lambda i: (i, 0))
    const = lambda a: pl.BlockSpec(a.shape, lambda i: (0, 0))
    return _pcall(
        body, name="merge_fwd", grid=(T // tt,),
        in_specs=[row(A_W), row(BQ_W), row(C_W), row(GATE), const(bg), const(wa), const(wb), const(wc)],
        out_specs=row(D), out_shape=_sds((T, D), BF),
        compiler_params=_cp("parallel"),
    )(oa, ob, oc, hg, bg, wa, wb, wc)


def _merge_bwd(dm, oa, ob, oc, hg, bg, wa, wb, wc):
    T = oa.shape[0]
    tt = min(256, T)

    def body(dm_ref, oa_ref, ob_ref, oc_ref, hg_ref, bg_ref, wa_ref, wb_ref, wc_ref,
             dya, dyb, dyc, doa, dob, doc, dhg, dbg):
        @pl.when(pl.program_id(0) == 0)
        def _():
            dbg[...] = jnp.zeros_like(dbg)

        dmv = dm_ref[...]
        for k, (o_ref, w_ref, dy_ref, do_ref) in enumerate(
                ((oa_ref, wa_ref, dya, doa), (ob_ref, wb_ref, dyb, dob), (oc_ref, wc_ref, dyc, doc))):
            sl = slice(D * k, D * (k + 1))
            y = jnp.dot(o_ref[...], w_ref[...], preferred_element_type=F32)
            g = _sigmoid(hg_ref[:, sl] + bg_ref[:, sl])
            dy = (dmv * g).astype(BF)
            dy_ref[...] = dy
            do_ref[...] = _nt(dy, w_ref[...]).astype(do_ref.dtype)
            dz = dmv * y * (g * (1.0 - g))
            dhg[:, sl] = dz.astype(dhg.dtype)
            dbg[:, sl] += jnp.sum(dz, axis=0, keepdims=True)

    row = lambda w: pl.BlockSpec((tt, w), lambda i: (i, 0))
    const = lambda a: pl.BlockSpec(a.shape, lambda i: (0, 0))
    return _pcall(
        body, name="merge_bwd", grid=(T // tt,),
        in_specs=[row(D), row(A_W), row(BQ_W), row(C_W), row(GATE), const(bg), const(wa), const(wb), const(wc)],
        out_specs=[row(D)] * 3 + [row(A_W), row(BQ_W), row(C_W), row(GATE),
                                  pl.BlockSpec((1, GATE), lambda i: (0, 0))],
        out_shape=[_sds((T, D), BF)] * 3 + [_sds((T, A_W), BF), _sds((T, BQ_W), BF), _sds((T, C_W), BF),
                                            _sds((T, GATE), BF), _sds((1, GATE), F32)],
        compiler_params=_cp("arbitrary"),
    )(dm, oa, ob, oc, hg, bg, wa, wb, wc)


def _lin_ln(a, w, res, g, b):
    T, K = a.shape
    tt = min(256, T)

    def body(a_ref, w_ref, res_ref, g_ref, b_ref, y_ref, yb_ref, xh_ref, rs_ref):
        u = ALPHA * res_ref[...] + jnp.dot(a_ref[...], w_ref[...], preferred_element_type=F32)
        mu = jnp.mean(u, axis=-1, keepdims=True)
        c = u - mu
        r = lax.rsqrt(jnp.mean(c * c, axis=-1, keepdims=True) + LN_EPS)
        xh = c * r
        y = xh * g_ref[...] + b_ref[...]
        y_ref[...] = y
        yb_ref[...] = y.astype(BF)
        xh_ref[...] = xh
        rs_ref[...] = r

    row = lambda w_: pl.BlockSpec((tt, w_), lambda i: (i, 0))
    const = lambda s: pl.BlockSpec(s, lambda i: (0, 0))
    return _pcall(
        body, name="lin_ln", grid=(T // tt,),
        in_specs=[row(K), const((K, D)), row(D), const((1, D)), const((1, D))],
        out_specs=[row(D), row(D), row(D), row(1)],
        out_shape=[_sds((T, D), F32), _sds((T, D), BF), _sds((T, D), F32), _sds((T, 1), F32)],
        compiler_params=_cp("parallel"),
    )(a, w, res, g, b)


def _ln_bwd(dy, xh, rs, g):
    T = dy.shape[0]
    tt = min(512, T)

    def body(dy_ref, xh_ref, rs_ref, g_ref, du_ref, dub_ref, dg_ref, db_ref):
        @pl.when(pl.program_id(0) == 0)
        def _():
            dg_ref[...] = jnp.zeros_like(dg_ref)
            db_ref[...] = jnp.zeros_like(db_ref)

        dyv, xhv = dy_ref[...], xh_ref[...]
        dg_ref[...] += jnp.sum(dyv * xhv, axis=0, keepdims=True)
        db_ref[...] += jnp.sum(dyv, axis=0, keepdims=True)
        dxh = dyv * g_ref[...]
        m1 = jnp.mean(dxh, axis=-1, keepdims=True)
        m2 = jnp.mean(dxh * xhv, axis=-1, keepdims=True)
        du = rs_ref[...] * (dxh - m1 - xhv * m2)
        du_ref[...] = du
        dub_ref[...] = du.astype(BF)

    row = lambda w_: pl.BlockSpec((tt, w_), lambda i: (i, 0))
    const = lambda s: pl.BlockSpec(s, lambda i: (0, 0))
    return _pcall(
        body, name="ln_bwd", grid=(T // tt,),
        in_specs=[row(D), row(D), row(1), const((1, D))],
        out_specs=[row(D), row(D), const((1, D)), const((1, D))],
        out_shape=[_sds((T, D), F32), _sds((T, D), BF), _sds((1, D), F32), _sds((1, D), F32)],
        compiler_params=_cp("arbitrary"),
    )(dy, xh, rs, g)


def _loss_grad(y, tgt):
    T = y.shape[0]
    tt = min(512, T)

    def body(y_ref, t_ref, dy_ref, sq_ref):
        @pl.when(pl.program_id(0) == 0)
        def _():
            sq_ref[...] = jnp.zeros_like(sq_ref)

        e = y_ref[...] - t_ref[...]
        dy_ref[...] = e * (1.0 / D)
        sq_ref[...] += jnp.sum(e * e, axis=0, keepdims=True)

    row = pl.BlockSpec((tt, D), lambda i: (i, 0))
    return _pcall(
        body, name="loss_grad", grid=(T // tt,),
        in_specs=[row, row], out_specs=[row, pl.BlockSpec((1, D), lambda i: (0, 0))],
        out_shape=[_sds((T, D), F32), _sds((1, D), F32)],
        compiler_params=_cp("arbitrary"),
    )(y, tgt)


def _position():
    return lax.axis_index("x"), lax.axis_index("y"), lax.axis_index("c")


HBM = pl.BlockSpec(memory_space=pl.ANY)


def _gather_phases(x_refs, out_refs, send, recv, loc):
    n = len(x_refs)
    x, y, c = _position()
    me, sib = (x, y, c), (x, y, 1 - c)
    chips = [(1 - x, y), (x, 1 - y), (1 - x, 1 - y)]

    def copy(a, k, block, to, src=None):
        px, py, pc = block
        dst = out_refs[a].at[4 * px + 2 * py + pc]
        return pltpu.make_async_remote_copy(
            src_ref=dst if src is None else src, dst_ref=dst,
            send_sem=send.at[a, k], recv_sem=recv.at[a, k], device_id=to, device_id_type=MESH)

    mine = [pltpu.make_async_copy(x_refs[a], out_refs[a].at[4 * x + 2 * y + c], loc.at[a]) for a in range(n)]
    first = []
    for a in range(n):
        first.append(copy(a, 0, me, sib, src=x_refs[a]))
        first += [copy(a, 1 + j, me, (*chip, c), src=x_refs[a]) for j, chip in enumerate(chips)]
    passed = [copy(a, 4 + j, (*chip, c), sib) for j, chip in enumerate(chips) for a in range(n)]

    def start():
        for cp in mine + first:
            cp.start()

    def forward():
        for j, chip in enumerate(chips):
            for a in range(n):
                copy(a, 1 + j, (*chip, c), me).wait_recv()
                copy(a, 4 + j, (*chip, c), sib).start()

    def finish():
        for a in range(n):
            copy(a, 0, sib, me).wait_recv()
            for j, chip in enumerate(chips):
                copy(a, 4 + j, (*chip, 1 - c), me).wait_recv()
        for cp in first + passed:
            cp.wait_send()
        for cp in mine:
            cp.wait()

    return start, forward, finish


def _gather_scratch(n):
    return [pltpu.SemaphoreType.DMA((n, 7)), pltpu.SemaphoreType.DMA((n, 7)), pltpu.SemaphoreType.DMA((n,))]


def _all_gather(xs, name):
    n = len(xs)

    def body(*refs):
        for phase in _gather_phases(refs[:n], refs[n:2 * n], *refs[2 * n:]):
            phase()

    return _pcall_comm(
        body, name=name,
        in_specs=[HBM] * n, out_specs=[HBM] * n,
        out_shape=[_sds((8,) + x.shape, x.dtype) for x in xs],
        scratch_shapes=_gather_scratch(n),
    )(*xs)


def _scatter_phases(g_refs, land_refs, send, recv):
    x, y, c = _position()
    peers = [(x, y, 1 - c), (1 - x, y, c), (x, 1 - y, c), (1 - x, 1 - y, c),
             (1 - x, y, 1 - c), (x, 1 - y, 1 - c), (1 - x, 1 - y, 1 - c)]
    copies = [pltpu.make_async_remote_copy(
        src_ref=g_refs[a].at[4 * px + 2 * py + pc], dst_ref=land_refs[a].at[k],
        send_sem=send.at[a, k], recv_sem=recv.at[a, k], device_id=(px, py, pc), device_id_type=MESH)
        for a in range(len(g_refs)) for k, (px, py, pc) in enumerate(peers)]

    def start():
        for cp in copies:
            cp.start()

    def finish():
        for cp in copies:
            cp.wait()

    return start, finish


def _scatter_scratch(n):
    return [pltpu.SemaphoreType.DMA((n, 7)), pltpu.SemaphoreType.DMA((n, 7))]


def _scatter(gs, name):
    n = len(gs)

    def body(*refs):
        for phase in _scatter_phases(refs[:n], refs[n:2 * n], *refs[2 * n:]):
            phase()

    return _pcall_comm(
        body, name=name,
        in_specs=[HBM] * n, out_specs=[HBM] * n,
        out_shape=[_sds((7,) + g.shape[1:], g.dtype) for g in gs],
        scratch_shapes=_scatter_scratch(n),
    )(*gs)


def _sum8(g, land, me):
    _, R, C = g.shape
    tr = min(512, R)

    def body(me_ref, g_ref, l_ref, out_ref):
        acc = g_ref[...].astype(F32)
        for k in range(7):
            acc = acc + l_ref[k].astype(F32)
        out_ref[...] = acc

    return _pcall(
        body, name="sum8",
        grid_spec=pltpu.PrefetchScalarGridSpec(
            num_scalar_prefetch=1, grid=(R // tr,),
            in_specs=[pl.BlockSpec((None, tr, C), lambda i, me_: (me_[0], i, 0)),
                      pl.BlockSpec((7, tr, C), lambda i, me_: (0, i, 0))],
            out_specs=pl.BlockSpec((tr, C), lambda i, me_: (i, 0))),
        out_shape=_sds((R, C), F32),
        compiler_params=_cp("parallel"),
    )(me, g, land)


def _adamw(g, w, m, v):
    R, C = w.shape
    tr = min(512, R)

    def body(g_ref, w_ref, m_ref, v_ref, d_out, m_out, v_out):
        d_out[...], m_out[...], v_out[...] = _adamw_math(w_ref[...], g_ref[...], m_ref[...], v_ref[...])

    blk = pl.BlockSpec((tr, C), lambda i: (i, 0))
    return _pcall(body, name="adamw", grid=(R // tr,), in_specs=[blk] * 4, out_specs=[blk] * 3,
                  out_shape=[_sds((R, C), F32)] * 3, compiler_params=_cp("parallel"))(g, w, m, v)


def _exchange_pair(gs, name):
    n = len(gs)
    hbm = pl.BlockSpec(memory_space=pl.ANY)

    def body(*refs):
        g_refs, out_refs = refs[:n], refs[n:2 * n]
        send, recv = refs[2 * n:]
        x, y, c = _position()
        copies = []
        for a in range(n):
            for p in range(4):
                copies.append(pltpu.make_async_remote_copy(
                    src_ref=g_refs[a].at[p, 1 - c], dst_ref=out_refs[a].at[p],
                    send_sem=send.at[a, p], recv_sem=recv.at[a, p],
                    device_id=(x, y, 1 - c), device_id_type=MESH))
        for cp in copies:
            cp.start()
        for cp in copies:
            cp.wait()

    return _pcall_comm(
        body, name=name,
        in_specs=[hbm] * n, out_specs=[hbm] * n,
        out_shape=[_sds((4,) + g.shape[2:], g.dtype) for g in gs],
        scratch_shapes=[pltpu.SemaphoreType.DMA((n, 4)), pltpu.SemaphoreType.DMA((n, 4))],
    )(*gs)


def _exchange_chips(ps, name):
    n = len(ps)
    hbm = pl.BlockSpec(memory_space=pl.ANY)

    def body(*refs):
        p_refs, out_refs = refs[:n], refs[n:2 * n]
        send, recv = refs[2 * n:]
        x, y, c = _position()
        chips = [(1 - x, y), (x, 1 - y), (1 - x, 1 - y)]
        copies = []
        for a in range(n):
            for j, (px, py) in enumerate(chips):
                copies.append(pltpu.make_async_remote_copy(
                    src_ref=p_refs[a].at[2 * px + py], dst_ref=out_refs[a].at[j],
                    send_sem=send.at[a, j], recv_sem=recv.at[a, j],
                    device_id=(px, py, c), device_id_type=MESH))
        for cp in copies:
            cp.start()
        for cp in copies:
            cp.wait()

    return _pcall_comm(
        body, name=name,
        in_specs=[hbm] * n, out_specs=[hbm] * n,
        out_shape=[_sds((3,) + p.shape[1:], p.dtype) for p in ps],
        scratch_shapes=[pltpu.SemaphoreType.DMA((n, 3)), pltpu.SemaphoreType.DMA((n, 3))],
    )(*ps)


def _pair_sum(g, got, core):
    _, _, R, C = g.shape
    tr = min(512, R)

    def body(core_ref, g_ref, r_ref, out_ref):
        out_ref[...] = (g_ref[...].astype(F32) + r_ref[...].astype(F32)).astype(out_ref.dtype)

    return _pcall(
        body, name="pair_sum",
        grid_spec=pltpu.PrefetchScalarGridSpec(
            num_scalar_prefetch=1, grid=(4, R // tr),
            in_specs=[pl.BlockSpec((None, None, tr, C), lambda p, i, cr: (p, cr[0], i, 0)),
                      pl.BlockSpec((None, tr, C), lambda p, i, cr: (p, i, 0))],
            out_specs=pl.BlockSpec((None, tr, C), lambda p, i, cr: (p, i, 0))),
        out_shape=_sds((4, R, C), g.dtype),
        compiler_params=_cp("parallel", "parallel"),
    )(core, g, got)


def _adamw_math(w, g, m, v):
    m = B1 * m + (1.0 - B1) * g
    v = B2 * v + (1.0 - B2) * (g * g)
    m_hat = m / (1.0 - B1 ** STEP)
    v_hat = v / (1.0 - B2 ** STEP)
    delta = -LR * (m_hat / (jnp.sqrt(v_hat) + EPS) + WD * w)
    return delta, m, v


def _chip_sum_adamw(p, got, chip, w, m, v):
    R, C = w.shape
    tr = min(512, R)

    def body(chip_ref, p_ref, r_ref, w_ref, m_ref, v_ref, g_out, d_out, m_out, v_out):
        g = ((p_ref[...].astype(F32) + r_ref[0].astype(F32)) + r_ref[1].astype(F32)) + r_ref[2].astype(F32)
        d, mn, vn = _adamw_math(w_ref[...], g, m_ref[...], v_ref[...])
        g_out[...], d_out[...], m_out[...], v_out[...] = g, d, mn, vn

    blk = pl.BlockSpec((tr, C), lambda i, ch: (i, 0))
    return _pcall(
        body, name="chip_sum_adamw",
        grid_spec=pltpu.PrefetchScalarGridSpec(
            num_scalar_prefetch=1, grid=(R // tr,),
            in_specs=[pl.BlockSpec((None, tr, C), lambda i, ch: (ch[0], i, 0)),
                      pl.BlockSpec((3, tr, C), lambda i, ch: (0, i, 0)), blk, blk, blk],
            out_specs=[blk] * 4),
        out_shape=[_sds((R, C), F32)] * 4,
        compiler_params=_cp("parallel"),
    )(chip, p, got, w, m, v)


def _small_sum_adamw(parts, w, m, v):
    _, R, C = parts.shape

    def body(p_ref, w_ref, m_ref, v_ref, g_out, d_out, m_out, v_out):
        g = p_ref[0]
        for k in range(1, 8):
            g = g + p_ref[k]
        d, mn, vn = _adamw_math(w_ref[...], g, m_ref[...], v_ref[...])
        g_out[...], d_out[...], m_out[...], v_out[...] = g, d, mn, vn

    return _pcall(body, name="small_sum_adamw", out_shape=[_sds((R, C), F32)] * 4,
                  compiler_params=pltpu.CompilerParams(vmem_limit_bytes=VMEM_LIMIT))(parts, w, m, v)


BIG = ("w_in", "w_branch_a", "w_branch_b", "w_branch_c", "w_out", "w_up", "w_down")
ROW_SHARDED = ("w_out", "w_down")
EARLY = ("w_down", "w_up", "w_out", "w_branch_a", "w_branch_b", "w_branch_c")
SMALL = ("b_gate", "q_norm_b", "k_norm_b", "rpb_c", "ln1_g", "ln1_b", "ln2_g", "ln2_b")
NAMES = ("w_in", "b_gate", "q_norm_b", "k_norm_b", "rpb_c", "w_branch_a", "w_branch_b", "w_branch_c",
         "w_out", "ln1_g", "ln1_b", "w_up", "w_down", "ln2_g", "ln2_b")


def _full_weight(blk, name):
    if name in ROW_SHARDED:
        return blk.reshape(-1, blk.shape[2])
    return blk.transpose(1, 0, 2).reshape(blk.shape[1], -1)


def _chunks(grad, name):
    if name in ROW_SHARDED:
        return grad.reshape(8, grad.shape[0] // 8, grad.shape[1])
    return grad.reshape(grad.shape[0], 8, grad.shape[1] // 8).transpose(1, 0, 2)


def _layer_fwd(x, xb, W, P, tabs, gm, gather=(), late=None):
    hq, = _mm(xb, W["w_qkv"], "nn", [F32], 1024, 768, 1024, name="in_qkv")
    hg, = _mm(xb, W["w_gate"], "nn", [F32], 1024, 1024, 1024, name="in_gate")
    tab_a, tab_b = tabs
    prepped = _prep_fwd(hq, tab_a, tab_b, P["qn"], P["kn"], gm)
    T = x.shape[0]
    tb = min(FULL_T, T)
    oa, lse_a, qa = _dilated_fwd(*prepped[0:3])
    qb_t, kb_t, vb_t = prepped[3:6]
    qb, kb, vb = _chunked_t(qb_t, tb), _to_heads(kb_t), _chunked_t(vb_t, min(FULL_TK, T))
    vb1 = jnp.concatenate([vb, jnp.ones(vb.shape[:2] + (V_ROWS - HD, vb.shape[3]), vb.dtype)], axis=2)
    ob, lse_b, gathered = _attn_full_fwd(qb, kb, vb1, gather)
    if late is not None:
        W = {**W, **late(gathered)}
    qc_t, kc_t, vc_t = prepped[6:9]
    qc, kc, vc = _chunked_t(qc_t, NBR_Q), _to_heads(kc_t), _to_heads(vc_t)
    bias_c = _nbr_bias_blocks(P["rpb"], T)
    oc, lse_c = _nbr_fwd(qc, kc, jnp.concatenate([vc, jnp.ones_like(vc)], axis=2), bias_c)
    oa_t, ob_t, oc_t = oa.transpose(2, 0, 1).reshape(T, A_W), _unchunk_t(ob), _unchunk_t(oc)
    qb = (qb, kb, kb.transpose(0, 2, 1), _to_heads(vb_t))
    ka = va = None
    merged = _merge_fwd(oa_t, ob_t, oc_t, hg, P["bg"], W["w_branch_a"], W["w_branch_b"], W["w_branch_c"])
    x1, x1b, xh1, rs1 = _lin_ln(merged, W["w_out"], x, P["ln1_g"], P["ln1_b"])

    def relu2(acc):
        r = jnp.maximum(acc, 0.0)
        return r * r, r

    f, r = _mm(x1b, W["w_up"], "nn", [BF, BF], 1024, 1024, 1024, epilogue=relu2, name="mlp_up")
    x2, x2b, xh2, rs2 = _lin_ln(f, W["w_down"], x1, P["ln2_g"], P["ln2_b"])
    saved = dict(xb=xb, hq=hq, hg=hg, qkv=(qa, ka, va, qb, kb, vb, qc, kc, vc), o=(oa, ob, oc),
                 lse=(lse_a, lse_b, lse_c), o_t=(oa_t, ob_t, oc_t), bias_c=bias_c, merged=merged,
                 xh1=xh1, rs1=rs1, x1b=x1b, f=f, r=r, xh2=xh2, rs2=rs2)
    return x2, x2b, saved, W


def _layer_bwd(dx2, S, W, P, tabs, gm, scatter=()):
    G = {}
    du2, du2b, G["ln2_g"], G["ln2_b"] = _ln_bwd(dx2, S["xh2"], S["rs2"], P["ln2_g"])
    G["w_down"], = _mm(S["f"], du2b, "tn", [F32], 1024, 1024, 512, name="dw_down")
    da, = _mm(du2b, W["w_down"], "nt", [BF], 1024, 1024, 1024,
              epilogue=lambda acc, r: (acc * (2.0 * r.astype(F32)),), extras=(S["r"],), name="d_act")
    G["w_up"], = _mm(S["x1b"], da, "tn", [F32], 1024, 1024, 512, name="dw_up")
    dx1, = _mm(da, W["w_up"], "nt", [F32], 1024, 1024, 1024,
               epilogue=lambda acc, d: (ALPHA * d + acc,), extras=(du2,), name="dx_mlp")
    du1, du1b, G["ln1_g"], G["ln1_b"] = _ln_bwd(dx1, S["xh1"], S["rs1"], P["ln1_g"])
    G["w_out"], = _mm(S["merged"], du1b, "tn", [F32], 1024, 1024, 512, name="dw_out")
    dm, = _mm(du1b, W["w_out"], "nt", [F32], 1024, 1024, 1024, name="d_merged")
    oa_t, ob_t, oc_t = S["o_t"]
    dya, dyb, dyc, doa, dob, doc, dhg, G["b_gate"] = _merge_bwd(
        dm, oa_t, ob_t, oc_t, S["hg"], P["bg"], W["w_branch_a"], W["w_branch_b"], W["w_branch_c"])
    G["w_branch_a"], = _mm(oa_t, dya, "tn", [F32], 256, 1024, 512, name="dw_branch_a")
    G["w_branch_b"], = _mm(ob_t, dyb, "tn", [F32], 512, 1024, 512, name="dw_branch_b")
    G["w_branch_c"], = _mm(oc_t, dyc, "tn", [F32], 256, 1024, 512, name="dw_branch_c")

    qa, ka, va, qb, kb, vb, qc, kc, vc = S["qkv"]
    oa, ob, oc = S["o"]
    lse_a, lse_b, lse_c = S["lse"]
    dqkv_a = _dilated_bwd(doa, oa, lse_a, qa)
    qT_b, k_b, kT_b, v_b = qb
    dobT = _chunked_t(dob, ob.shape[-1])
    early = {n: _chunks(G[n], n).astype(BF) for n in EARLY}
    dqbT, dkb8, dvb8, landed = _attn_full_bwd(qT_b, k_b, kT_b, v_b, dobT, lse_b, _attn_delta(dobT, ob),
                                              list(scatter) + [early[n] for n in EARLY])
    landed = (landed[:len(scatter)], dict(zip(EARLY, landed[len(scatter):])), early)
    group_sum = lambda t: t.reshape(k_b.shape[0], -1, t.shape[1], HD).sum(1)
    dqb, dkb, dvb = _unchunk_t(dqbT), _from_heads(group_sum(dkb8)), _from_heads(group_sum(dvb8))
    bias_c = S["bias_c"]
    docT = _chunked_t(doc, NBR_Q)
    dqcT, dkc, dvc, dbias_c = _nbr_bwd(qc, kc, vc, docT, lse_c, _attn_delta(docT, oc), bias_c)
    G["rpb_c"] = _nbr_rpb_grad(dbias_c, dx2.shape[0])

    tab_a, tab_b = tabs
    grads = [dqkv_a, dqb, dkb, dvb, _unchunk_t(dqcT), _from_heads(dkc), _from_heads(dvc)]
    dhq, dqn, dkn = _prep_bwd(S["hq"], grads, tab_a, tab_b, P["qn"], P["kn"], gm)
    G["q_norm_b"] = dqn.reshape(BQ_W // HD, HD).sum(0)
    G["k_norm_b"] = dkn.reshape(BKV_W // HD, HD).sum(0)
    dw_qkv, = _mm(S["xb"], dhq, "tn", [F32], 1024, 768, 512, name="dw_qkv")
    dw_gate, = _mm(S["xb"], dhg, "tn", [F32], 1024, 1024, 512, name="dw_gate")
    G["w_in"] = jnp.concatenate([dw_qkv, dw_gate], axis=1)
    dx_a, = _mm(dhq, W["w_qkv"], "nt", [F32], 1024, 1024, 768,
                epilogue=lambda acc, d: (ALPHA * d + acc,), extras=(du1,), name="dx_qkv")
    dx, = _mm(dhg, W["w_gate"], "nt", [F32], 1024, 1024, 1024,
              epilogue=lambda acc, d: (d + acc,), extras=(dx_a,), name="dx_gate")
    return dx, G, landed


def _pack_small(vals):
    flat = jnp.concatenate([vals[n].reshape(-1).astype(F32) for n in SMALL])
    pad = (-flat.shape[0]) % (8 * 128)
    return jnp.pad(flat, (0, pad)).reshape(-1, 128)


def _unpack_small(packed, like):
    flat, out, off = packed.reshape(-1), {}, 0
    for n in SMALL:
        size = math.prod(like[n].shape)
        out[n] = flat[off:off + size].reshape(like[n].shape)
        off += size
    return out


def kernel(x, w_in, b_gate, q_norm_b, k_norm_b, rpb_c, w_branch_a, w_branch_b, w_branch_c, w_out, ln1_g, ln1_b, w_up, w_down, ln2_g, ln2_b, loss_target, m_w_in, m_b_gate, m_q_norm_b, m_k_norm_b, m_rpb_c, m_w_branch_a, m_w_branch_b, m_w_branch_c, m_w_out, m_ln1_g, m_ln1_b, m_w_up, m_w_down, m_ln2_g, m_ln2_b, v_w_in, v_b_gate, v_q_norm_b, v_k_norm_b, v_rpb_c, v_w_branch_a, v_w_branch_b, v_w_branch_c, v_w_out, v_ln1_g, v_ln1_b, v_w_up, v_w_down, v_ln2_g, v_ln2_b):
    w = dict(w_in=w_in, b_gate=b_gate, q_norm_b=q_norm_b, k_norm_b=k_norm_b, rpb_c=rpb_c,
             w_branch_a=w_branch_a, w_branch_b=w_branch_b, w_branch_c=w_branch_c, w_out=w_out,
             ln1_g=ln1_g, ln1_b=ln1_b, w_up=w_up, w_down=w_down, ln2_g=ln2_g, ln2_b=ln2_b)
    m = dict(w_in=m_w_in, b_gate=m_b_gate, q_norm_b=m_q_norm_b, k_norm_b=m_k_norm_b, rpb_c=m_rpb_c,
             w_branch_a=m_w_branch_a, w_branch_b=m_w_branch_b, w_branch_c=m_w_branch_c, w_out=m_w_out,
             ln1_g=m_ln1_g, ln1_b=m_ln1_b, w_up=m_w_up, w_down=m_w_down, ln2_g=m_ln2_g, ln2_b=m_ln2_b)
    v = dict(w_in=v_w_in, b_gate=v_b_gate, q_norm_b=v_q_norm_b, k_norm_b=v_k_norm_b, rpb_c=v_rpb_c,
             w_branch_a=v_w_branch_a, w_branch_b=v_w_branch_b, w_branch_c=v_w_branch_c, w_out=v_w_out,
             ln1_g=v_ln1_g, ln1_b=v_ln1_b, w_up=v_w_up, w_down=v_w_down, ln2_g=v_ln2_g, ln2_b=v_ln2_b)
    T = x.shape[1]
    xc, yc, cc = _position()

    flat2 = lambda a: a.reshape(-1, a.shape[-1])
    shards = {n: w[n].astype(BF) for n in BIG}
    w_in0, = _all_gather([shards["w_in"][0]], "gather_w_in0")
    riding = [shards[n][0] for n in BIG[1:]] + [flat2(shards[n][1:]) for n in BIG]

    tabs = _rope_tables(T)
    gm = _group_mean_matrix()

    def in_proj(blk):
        w_in_l = _full_weight(blk, "w_in")
        return {"w_qkv": w_in_l[:, :QKV], "w_gate": w_in_l[:, QKV:]}

    Ws = [in_proj(w_in0)]

    def late(gathered):
        rest = [g.reshape(8, DEPTH - 1, -1, g.shape[2]) for g in gathered[len(BIG) - 1:]]
        for i in range(DEPTH - 1):
            W = {n: _full_weight(g[:, i], n) for n, g in zip(BIG[1:], rest[1:])}
            Ws.append({**W, **in_proj(rest[0][:, i])})
        return {n: _full_weight(g, n) for n, g in zip(BIG[1:], gathered)}

    Ps = [dict(qn=jnp.tile(q_norm_b[l][None], (1, 2)), kn=jnp.tile(k_norm_b[l][None], (1, 2)),
               rpb=rpb_c[l], bg=b_gate[l][None], ln1_g=ln1_g[l][None], ln1_b=ln1_b[l][None],
               ln2_g=ln2_g[l][None], ln2_b=ln2_b[l][None]) for l in range(DEPTH)]

    h = x[0]
    hb = h.astype(BF)
    saved = []
    for l in range(DEPTH):
        h, hb, S, Ws[l] = _layer_fwd(h, hb, Ws[l], Ps[l], tabs, gm, *((riding, late) if l == 0 else ()))
        saved.append(S)
    dy, sq = _loss_grad(h, loss_target[0])
    loss = lax.psum(0.5 / D * jnp.sum(sq), AXES)

    grads = [None] * DEPTH
    chunks, landed = [{} for _ in range(DEPTH)], [{} for _ in range(DEPTH)]
    for l in reversed(range(DEPTH)):
        above = [chunks[l + 1]["w_in"]] if l + 1 < DEPTH else []
        dy, grads[l], (arrived, landed_early, early) = _layer_bwd(dy, saved[l], Ws[l], Ps[l], tabs, gm, above)
        if above:
            landed[l + 1]["w_in"], = arrived
        chunks[l].update(early)
        landed[l].update(landed_early)
        chunks[l]["w_in"] = _chunks(grads[l]["w_in"], "w_in").astype(BF)
    landed[0]["w_in"], = _scatter([chunks[0]["w_in"]], "scatter_w_in0")
    grad_x = dy[None]
    me = (4 * xc + 2 * yc + cc).reshape(1).astype(jnp.int32)
    out_g, out_d, out_m, out_v = {}, {}, {}, {}
    for n in BIG:
        g = jnp.concatenate([_sum8(chunks[l][n], landed[l][n], me) for l in range(DEPTH)], axis=0)
        res = (g,) + tuple(_adamw(g, flat2(w[n]), flat2(m[n]), flat2(v[n])))
        out_g[n], out_d[n], out_m[n], out_v[n] = [t.reshape(w[n].shape) for t in res]

    part = _pack_small({n: jnp.stack([grads[l][n].reshape(w[n].shape[1:]) for l in range(DEPTH)]) for n in SMALL})
    parts, = _all_gather([part], "gather_small_grads")
    res = _small_sum_adamw(parts, _pack_small(w), _pack_small(m), _pack_small(v))
    for dst, packed in zip((out_g, out_d, out_m, out_v), res):
        dst.update(_unpack_small(packed, w))

    return (loss, grad_x, *[out_g[n] for n in NAMES], *[out_d[n] for n in NAMES],
            *[out_m[n] for n in NAMES], *[out_v[n] for n in NAMES])
```

```python
import math

import numpy as np
import jax
import jax.numpy as jnp
from jax import lax
from jax.experimental import pallas as pl
from jax.experimental.pallas import tpu as pltpu

F32 = jnp.float32
BF = jnp.bfloat16
HI = lax.Precision.HIGHEST
NEG = -1e30
MESH = pl.DeviceIdType.MESH
AXES = ("x", "y", "c")

D = 1024
DEPTH = 4
HD = 64
A_W, BQ_W, BKV_W, C_W = 256, 512, 128, 256
QKV = 2304
GATE = 3072
D_FF = 4096
GRID_W = 64
ALPHA = (2 * DEPTH) ** 0.25
LN_EPS = 1e-5
RMS_EPS = 1e-6
SCALE = HD ** -0.5
ROPE_THETA = 500000.0
AXIAL_THETA = 10000.0
A_CONFIGS = ((128, 1), (512, 4), (2048, 16))
LR, B1, B2, EPS, WD, STEP = 0.001, 0.9, 0.999, 1e-08, 0.01, 10

VMEM_LIMIT = 56 * 1024 * 1024


def _pcall(body, **kw):
    return pl.pallas_call(body, **kw)


def _pcall_comm(body, **kw):
    return pl.pallas_call(body, **kw)


def _cp(*sem):
    return pltpu.CompilerParams(dimension_semantics=sem, vmem_limit_bytes=VMEM_LIMIT)


def _sds(shape, dtype):
    return jax.ShapeDtypeStruct(shape, dtype)


def _mm(a, b, dims, outs, tm, tn, tk, epilogue=None, extras=(), name="mm"):
    if dims == "tn":
        K, M = a.shape
    else:
        M, K = a.shape
    N = b.shape[0] if dims == "nt" else b.shape[1]
    tm, tn, tk = min(tm, M), min(tn, N), min(tk, K)
    assert M % tm == 0 and N % tn == 0 and K % tk == 0, (name, M, N, K, tm, tn, tk)
    nk = K // tk
    ne, no = len(extras), len(outs)

    def body(a_ref, b_ref, *rest):
        extra_refs, out_refs = rest[:ne], rest[ne:ne + no]
        av, bv = a_ref[...].astype(BF), b_ref[...].astype(BF)
        if dims == "nn":
            p = jnp.dot(av, bv, preferred_element_type=F32)
        elif dims == "nt":
            p = lax.dot_general(av, bv, (((1,), (1,)), ((), ())), preferred_element_type=F32)
        else:
            p = lax.dot_general(av, bv, (((0,), (0,)), ((), ())), preferred_element_type=F32)

        def finish(acc):
            res = epilogue(acc, *[r[...] for r in extra_refs]) if epilogue else (acc,)
            for o, r in zip(out_refs, res):
                o[...] = r.astype(o.dtype)

        if nk == 1:
            finish(p)
        else:
            acc_ref = rest[-1]
            k = pl.program_id(2)

            @pl.when(k == 0)
            def _():
                acc_ref[...] = p

            @pl.when(k > 0)
            def _():
                acc_ref[...] += p

            @pl.when(k == nk - 1)
            def _():
                finish(acc_ref[...])

    if dims == "tn":
        a_spec = pl.BlockSpec((tk, tm), lambda i, j, k: (k, i))
    else:
        a_spec = pl.BlockSpec((tm, tk), lambda i, j, k: (i, k))
    if dims == "nt":
        b_spec = pl.BlockSpec((tn, tk), lambda i, j, k: (j, k))
    else:
        b_spec = pl.BlockSpec((tk, tn), lambda i, j, k: (k, j))
    o_spec = pl.BlockSpec((tm, tn), lambda i, j, k: (i, j))
    res = _pcall(
        body, name=name, grid=(M // tm, N // tn, nk),
        in_specs=[a_spec, b_spec] + [o_spec] * ne,
        out_specs=[o_spec] * no,
        out_shape=[_sds((M, N), dt) for dt in outs],
        scratch_shapes=[pltpu.VMEM((tm, tn), F32)] if nk > 1 else [],
        compiler_params=_cp("parallel", "parallel", "arbitrary"),
    )(a, b, *extras)
    return res


def _rope_tables(T):
    pos = jnp.arange(T)

    def cs(p, theta, half):
        inv = theta ** (-jnp.arange(half, dtype=F32) / half)
        ang = p.astype(F32)[:, None] * inv[None, :]
        return jnp.cos(ang), jnp.sin(ang)

    ca, sa = cs(pos, ROPE_THETA, 8)
    one, zero, z8 = jnp.ones((T, 48), F32), jnp.zeros((T, 48), F32), jnp.zeros((T, 8), F32)
    tab_a = [jnp.concatenate(t, 1) for t in ([ca, ca, one], [-sa, z8, zero], [z8, sa, zero])]
    cr, sr = cs(pos // GRID_W, AXIAL_THETA, 16)
    cc, sc = cs(pos % GRID_W, AXIAL_THETA, 16)
    z16 = jnp.zeros((T, 16), F32)
    tab_b = [jnp.concatenate(t, 1) for t in ([cr, cr, cc, cc], [-sr, z16, -sc, z16], [z16, sr, z16, sc])]
    return [jnp.tile(t, (1, 2)) for t in tab_a], [jnp.tile(t, (1, 2)) for t in tab_b]


def _rot(x, C, S1, S2, k):
    return x * C + pltpu.roll(x, 128 - k, 1) * S1 + pltpu.roll(x, k, 1) * S2


def _rot_t(d, C, S1, S2, k):
    return d * C + pltpu.roll(d * S1, k, 1) + pltpu.roll(d * S2, 128 - k, 1)


def _group_mean_matrix():
    m = np.zeros((128, 128), np.float32)
    m[:64, :64] = 1.0 / 64
    m[64:, 64:] = 1.0 / 64
    return jnp.asarray(m)


def _prep_fwd(hq, tab_a, tab_b, qn, kn, gm):
    T = hq.shape[0]
    tt = min(256, T)
    widths = [A_W, A_W, A_W, BQ_W, BKV_W, BKV_W, C_W, C_W, C_W]

    def body(h_ref, ca, s1a, s2a, cb, s1b, s2b, qn_ref, kn_ref, gm_ref,
             qa, ka, va, qb, kb, vb, qc, kc, vc):
        def col(off, j):
            return h_ref[:, off + 128 * j: off + 128 * (j + 1)]

        for j in range(2):
            sl = slice(128 * j, 128 * (j + 1))
            qa[:, sl] = (_rot(col(0, j), ca[...], s1a[...], s2a[...], 8) * SCALE).astype(qa.dtype)
            ka[:, sl] = _rot(col(256, j), ca[...], s1a[...], s2a[...], 8).astype(ka.dtype)
            va[:, sl] = col(512, j).astype(va.dtype)
            qc[:, sl] = (col(1536, j) * SCALE).astype(qc.dtype)
            kc[:, sl] = col(1792, j).astype(kc.dtype)
            vc[:, sl] = col(2048, j).astype(vc.dtype)

        def normed(x, w):
            ms = jnp.dot(x * x, gm_ref[...], precision=HI, preferred_element_type=F32)
            return x * lax.rsqrt(ms + RMS_EPS) * w

        for j in range(4):
            y = normed(col(768, j), qn_ref[...])
            qb[:, 128 * j:128 * (j + 1)] = (_rot(y, cb[...], s1b[...], s2b[...], 16) * SCALE).astype(qb.dtype)
        y = normed(col(1280, 0), kn_ref[...])
        kb[...] = _rot(y, cb[...], s1b[...], s2b[...], 16).astype(kb.dtype)
        vb[...] = col(1408, 0).astype(vb.dtype)

    row = lambda w: pl.BlockSpec((tt, w), lambda i: (i, 0))
    const = lambda s: pl.BlockSpec(s, lambda i: (0, 0))
    return _pcall(
        body, name="prep_fwd", grid=(T // tt,),
        in_specs=[row(QKV)] + [row(128)] * 6 + [const((1, 128))] * 2 + [const((128, 128))],
        out_specs=[row(w) for w in widths],
        out_shape=[_sds((T, w), BF) for w in widths],
        compiler_params=_cp("parallel"),
    )(hq, *tab_a, *tab_b, qn, kn, gm)


def _prep_bwd(hq, grads, tab_a, tab_b, qn, kn, gm):
    T = hq.shape[0]
    tt = min(256, T)
    widths = [A_W, A_W, A_W, BQ_W, BKV_W, BKV_W, C_W, C_W, C_W]

    def body(h_ref, dqa, dka, dva, dqb, dkb, dvb, dqc, dkc, dvc,
             ca, s1a, s2a, cb, s1b, s2b, qn_ref, kn_ref, gm_ref, dh, dqn, dkn):
        i = pl.program_id(0)

        @pl.when(i == 0)
        def _():
            dqn[...] = jnp.zeros_like(dqn)
            dkn[...] = jnp.zeros_like(dkn)

        def put(off, j, val):
            dh[:, off + 128 * j: off + 128 * (j + 1)] = val.astype(dh.dtype)

        for j in range(2):
            sl = slice(128 * j, 128 * (j + 1))
            put(0, j, _rot_t(dqa[:, sl] * SCALE, ca[...], s1a[...], s2a[...], 8))
            put(256, j, _rot_t(dka[:, sl], ca[...], s1a[...], s2a[...], 8))
            put(512, j, dva[:, sl])
            put(1536, j, dqc[:, sl] * SCALE)
            put(1792, j, dkc[:, sl])
            put(2048, j, dvc[:, sl])

        def norm_bwd(x, w, e):
            ms = jnp.dot(x * x, gm_ref[...], precision=HI, preferred_element_type=F32)
            r = lax.rsqrt(ms + RMS_EPS)
            n = x * r
            dn = e * w
            proj = jnp.dot(dn * n, gm_ref[...], precision=HI, preferred_element_type=F32)
            return r * (dn - n * proj), jnp.sum(e * n, axis=0, keepdims=True)

        for j in range(4):
            sl = slice(128 * j, 128 * (j + 1))
            e = _rot_t(dqb[:, sl] * SCALE, cb[...], s1b[...], s2b[...], 16)
            dx, dw = norm_bwd(h_ref[:, 768 + 128 * j: 768 + 128 * (j + 1)], qn_ref[...], e)
            put(768, j, dx)
            dqn[:, sl] += dw
        e = _rot_t(dkb[...], cb[...], s1b[...], s2b[...], 16)
        dx, dw = norm_bwd(h_ref[:, 1280:1408], kn_ref[...], e)
        put(1280, 0, dx)
        dkn[...] += dw
        put(1408, 0, dvb[...])

    row = lambda w, c=0: pl.BlockSpec((tt, w), lambda i: (i, c))
    const = lambda s: pl.BlockSpec(s, lambda i: (0, 0))
    return _pcall(
        body, name="prep_bwd", grid=(T // tt,),
        in_specs=[row(QKV)] + [row(A_W, c) for c in range(3)] + [row(w) for w in widths[3:]] + [row(128)] * 6
        + [const((1, 128))] * 2 + [const((128, 128))],
        out_specs=[row(QKV), const((1, BQ_W)), const((1, BKV_W))],
        out_shape=[_sds((T, QKV), BF), _sds((1, BQ_W), F32), _sds((1, BKV_W), F32)],
        compiler_params=_cp("arbitrary"),
    )(hq, grads[0], grads[0], grads[0], *grads[1:], *tab_a, *tab_b, qn, kn, gm)


def _to_heads(x):
    T, W = x.shape
    return x.reshape(T, W // HD, HD).transpose(1, 0, 2)


def _from_heads(x):
    H, T, _ = x.shape
    return x.transpose(1, 0, 2).reshape(T, H * HD)


def _nt(a, b):
    return lax.dot_general(a, b, (((1,), (1,)), ((), ())), preferred_element_type=F32)


def _chunked_t(x, t):
    T, W = x.shape
    return x.reshape(T // t, t, W // HD, HD).transpose(2, 0, 3, 1)


def _unchunk_t(x):
    H, n, _, t = x.shape
    return x.transpose(1, 3, 0, 2).reshape(n * t, H * HD)


FULL_T = 512
FULL_TK = 512
FULL_HEADS = 1
V_ROWS = 72


def _phased(phases, grid, at):
    step = pl.program_id(0) * grid[1] + pl.program_id(1)
    last = grid[0] * grid[1] - 1
    for phase, frac in zip(phases, at):
        pl.when(step == int(round(frac * last)))(phase)


def _attn_full_fwd(qT, k, vT1, gather=()):
    Hq, nq, _, t = qT.shape
    Hk, T, _ = k.shape
    G = Hq // Hk
    nk, tk = vT1.shape[1], vT1.shape[3]

    HB = FULL_HEADS
    assert G % HB == 0
    ng = len(gather)
    grid = (Hq // HB, nq)

    def body(q_ref, k_ref, v_ref, *rest):
        x_refs, (o_ref, lse_ref), rest = rest[:ng], rest[ng:ng + 2], rest[ng + 2:]
        out_refs, acc_refs, sems = rest[:ng], rest[ng:ng + HB], rest[ng + HB:]
        if ng:
            _phased(_gather_phases(x_refs, out_refs, *sems), grid, (0.0, 0.75, 1.0))
        for acc_ref in acc_refs:
            acc_ref[...] = jnp.zeros((V_ROWS, t), F32)

        def scores(j, b):
            sT = jnp.dot(k_ref[pl.ds(pl.multiple_of(j * tk, tk), tk), :], q_ref[b], preferred_element_type=F32)
            return sT, jnp.max(sT, axis=0, keepdims=True)

        def update(j, b, scored, m_old):
            sT, m_tile = scored
            m_new = jnp.maximum(m_old, m_tile)
            pT = jnp.exp(sT - m_new).astype(BF)
            acc_refs[b][...] = (jnp.exp(m_old - m_new) * acc_refs[b][...]
                                + jnp.dot(v_ref[j], pT, preferred_element_type=F32))
            return m_new

        def step(j, carry):
            ms, ss = carry
            nxt = jnp.minimum(j + 1, nk - 1)
            new_s = tuple(scores(nxt, b) for b in range(HB))
            new_m = tuple(update(j, b, ss[b], ms[b]) for b in range(HB))
            return new_m, new_s

        init = (tuple(jnp.full((1, t), NEG, F32) for _ in range(HB)), tuple(scores(0, b) for b in range(HB)))
        ms, _ = lax.fori_loop(0, nk, step, init)
        for b in range(HB):
            l = acc_refs[b][pl.ds(HD, 1), :]
            o_ref[b] = (acc_refs[b][pl.ds(0, HD), :] / l).astype(o_ref.dtype)
            lse_ref[b] = ms[b] + jnp.log(l)

    qs = pl.BlockSpec((HB, None, HD, t), lambda h, i: (h, i, 0, 0))
    res = (_pcall_comm if ng else _pcall)(
        body, name="attn_full_fwd_gather" if ng else "attn_full_fwd", grid=grid,
        in_specs=[qs, pl.BlockSpec((None, T, HD), lambda h, i: (h * HB // G, 0, 0)),
                  pl.BlockSpec((None, nk, V_ROWS, tk), lambda h, i: (h * HB // G, 0, 0, 0))] + [HBM] * ng,
        out_specs=[qs, pl.BlockSpec((HB, None, 1, t), lambda h, i: (h, i, 0, 0))] + [HBM] * ng,
        out_shape=[_sds((Hq, nq, HD, t), BF), _sds((Hq, nq, 1, t), F32)]
        + [_sds((8,) + x.shape, x.dtype) for x in gather],
        scratch_shapes=[pltpu.VMEM((V_ROWS, t), F32)] * HB + (_gather_scratch(ng) if ng else []),
        compiler_params=_cp("arbitrary", "arbitrary") if ng else _cp("parallel", "parallel"),
    )(qT, k, vT1, *gather)
    return res[0], res[1], list(res[2:])


def _attn_delta(doT, oT):
    Hq, nq, _, t = doT.shape

    def body(do_ref, o_ref, dl_ref):
        dl_ref[...] = jnp.sum(do_ref[...].astype(F32) * o_ref[...].astype(F32), axis=1, keepdims=True)

    qs = pl.BlockSpec((None, nq, HD, t), lambda h: (h, 0, 0, 0))
    rs = pl.BlockSpec((None, nq, 1, t), lambda h: (h, 0, 0, 0))
    return _pcall(body, name="attn_delta", grid=(Hq,), in_specs=[qs, qs], out_specs=rs,
                  out_shape=_sds((Hq, nq, 1, t), F32), compiler_params=_cp("parallel"))(doT, oT)


def _attn_full_bwd(qT, k, kT, v, doT, lse, delta, scatter=()):
    Hq, nq, _, t = qT.shape
    Hk, T, _ = k.shape
    G = Hq // Hk
    nkv = T // t
    ns = len(scatter)

    def body(qT_ref, doT_ref, lse_ref, dl_ref, k_ref, kT_ref, v_ref, *rest):
        g_refs, (dq_ref, dk_ref, dv_ref), rest = rest[:ns], rest[ns:ns + 3], rest[ns + 3:]
        land_refs, (dq_acc, dk_acc, dv_acc), sems = rest[:ns], rest[ns:ns + 3], rest[ns + 3:]
        if ns:
            _phased(_scatter_phases(g_refs, land_refs, *sems), (Hq, nkv), (0.0, 1.0))
        j = pl.program_id(1)

        @pl.when(j == 0)
        def _():
            dq_acc[...] = jnp.zeros_like(dq_acc)

        kv, kTv, vv = k_ref[...], kT_ref[...], v_ref[...]
        dk_acc[...] = jnp.zeros((t, HD), F32)
        dv_acc[...] = jnp.zeros((t, HD), F32)

        def step(i, carry):
            qT, doT = qT_ref[i], doT_ref[i]
            pT = jnp.exp(jnp.dot(kv, qT, preferred_element_type=F32) - lse_ref[i])
            dv_acc[...] += _nt(pT.astype(BF), doT)
            dsT = (pT * (jnp.dot(vv, doT, preferred_element_type=F32) - dl_ref[i])).astype(BF)
            dk_acc[...] += _nt(dsT, qT)
            dq_acc[i] += jnp.dot(kTv, dsT, preferred_element_type=F32)
            return carry

        lax.fori_loop(0, nq, step, 0)
        dk_ref[...] = dk_acc[...].astype(dk_ref.dtype)
        dv_ref[...] = dv_acc[...].astype(dv_ref.dtype)

        @pl.when(j == nkv - 1)
        def _():
            dq_ref[...] = dq_acc[...].astype(dq_ref.dtype)

    chk = pl.BlockSpec((None, nq, HD, t), lambda h, j: (h, 0, 0, 0))
    row = pl.BlockSpec((None, nq, 1, t), lambda h, j: (h, 0, 0, 0))
    kvs = pl.BlockSpec((None, t, HD), lambda h, j: (h // G, j, 0))
    out = pl.BlockSpec((None, t, HD), lambda h, j: (h, j, 0))
    res = (_pcall_comm if ns else _pcall)(
        body, name="attn_full_bwd_scatter" if ns else "attn_full_bwd", grid=(Hq, nkv),
        in_specs=[chk, chk, row, row, kvs, pl.BlockSpec((None, HD, t), lambda h, j: (h // G, 0, j)), kvs] + [HBM] * ns,
        out_specs=[chk, out, out] + [HBM] * ns,
        out_shape=[_sds((Hq, nq, HD, t), BF), _sds((Hq, T, HD), BF), _sds((Hq, T, HD), BF)]
        + [_sds((7,) + g.shape[1:], g.dtype) for g in scatter],
        scratch_shapes=[pltpu.VMEM((nq, HD, t), F32), pltpu.VMEM((t, HD), F32), pltpu.VMEM((t, HD), F32)]
        + (_scatter_scratch(ns) if ns else []),
        compiler_params=_cp("arbitrary" if ns else "parallel", "arbitrary"),
    )(qT, doT, lse, delta, k, kT, v, *scatter)
    return res[0], res[1], res[2], list(res[3:])


DIL_Q = 128
DIL_K = 256
DIL_R = 64


def _vh(x, d):
    T = x.shape[0]
    return x.reshape(T // d, d, 4, HD).transpose(1, 2, 0, 3).reshape(4 * d, T // d, HD)


def _vh_chunks(x, d):
    y = _vh(x, d)
    return y.reshape(y.shape[0], y.shape[1] // DIL_Q, DIL_Q, HD).transpose(0, 1, 3, 2)


def _chunks_to_dims(c, d):
    _, nq, R, _ = c.shape
    return c.reshape(d, 4, nq, R, DIL_Q).transpose(1, 3, 2, 4, 0).reshape(4, R, nq * DIL_Q * d)


def _dims_to_chunks(x, d):
    _, R, T = x.shape
    nq = T // (DIL_Q * d)
    return x.reshape(4, R, nq, DIL_Q, d).transpose(4, 0, 2, 1, 3).reshape(4 * d, nq, R, DIL_Q)


def _dil_window(i, L):
    start = pl.multiple_of(jnp.clip(i * DIL_Q - DIL_R, 0, L - DIL_K), DIL_R)
    kk = start + lax.broadcasted_iota(jnp.int32, (DIL_K, 1), 0)
    qq = i * DIL_Q + lax.broadcasted_iota(jnp.int32, (1, DIL_Q), 1)
    return start, jnp.abs(kk - qq) <= DIL_R


def _tn(a, b):
    return lax.dot_general(a, b, (((0,), (0,)), ((), ())), preferred_element_type=F32)


def _dil_fwd(qT, k, v):
    V, nq, _, _ = qT.shape
    L = k.shape[1]
    assert L >= DIL_K

    unroll = 4 if nq % 4 == 0 else 1

    def body(q_ref, k_ref, v_ref, o_ref, lse_ref):
        def tile(i):
            start, mask = _dil_window(i, L)
            win = pl.ds(start, DIL_K)
            sT = jnp.where(mask, jnp.dot(k_ref[win, :], q_ref[i], preferred_element_type=F32), NEG)
            m = jnp.max(sT, axis=0, keepdims=True)
            pT = jnp.exp(sT - m).astype(BF)
            acc = _tn(v_ref[win, :], pT)
            l = jnp.max(acc[HD:HD + 8], axis=0, keepdims=True)
            o_ref[i] = acc[:HD] / l
            lse_ref[i] = m + jnp.log(l)

        def tiles(ii, carry):
            for u in range(unroll):
                tile(ii * unroll + u)
            return carry

        lax.fori_loop(0, nq // unroll, tiles, 0)

    chk = lambda r, dt: (pl.BlockSpec((None, nq, r, DIL_Q), lambda h: (h, 0, 0, 0)), _sds((V, nq, r, DIL_Q), dt))
    tok = lambda w: pl.BlockSpec((None, L, w), lambda h: (h, 0, 0))
    (o_spec, o_shape), (l_spec, l_shape) = chk(HD, F32), chk(1, F32)
    return _pcall(
        body, name=f"dil_fwd_{V // 4}", grid=(V,),
        in_specs=[chk(HD, BF)[0], tok(HD), tok(2 * HD)],
        out_specs=[o_spec, l_spec], out_shape=[o_shape, l_shape],
        compiler_params=_cp("parallel"),
    )(qT, k, v)


def _dil_merge(os_, lses):
    _, _, T = os_[0].shape
    tt = min(1024, T)
    n = len(os_)

    def body(*refs):
        o_refs, l_refs, (o_out, l_out) = refs[:n], refs[n:2 * n], refs[2 * n:]
        m = l_refs[0][...]
        for r in l_refs[1:]:
            m = jnp.maximum(m, r[...])
        ws = [jnp.exp(r[...] - m) for r in l_refs]
        tot = ws[0]
        for w_ in ws[1:]:
            tot = tot + w_
        acc = ws[0] * o_refs[0][...]
        for w_, o in zip(ws[1:], o_refs[1:]):
            acc = acc + w_ * o[...]
        o_out[...] = (acc / tot).astype(o_out.dtype)
        l_out[...] = m + jnp.log(tot)

    os_spec = pl.BlockSpec((None, HD, tt), lambda h, i: (h, 0, i))
    ls_spec = pl.BlockSpec((None, 1, tt), lambda h, i: (h, 0, i))
    return _pcall(
        body, name="dil_merge", grid=(4, T // tt),
        in_specs=[os_spec] * n + [ls_spec] * n, out_specs=[os_spec, ls_spec],
        out_shape=[_sds((4, HD, T), BF), _sds((4, 1, T), F32)],
        compiler_params=_cp("parallel", "parallel"),
    )(*os_, *lses)


def _dims_delta(doT, oT):
    _, _, T = doT.shape
    tt = min(2048, T)

    def body(do_ref, o_ref, dl_ref):
        dl_ref[...] = jnp.sum(do_ref[...].astype(F32) * o_ref[...].astype(F32), axis=0, keepdims=True)

    spec = pl.BlockSpec((None, HD, tt), lambda h, i: (h, 0, i))
    return _pcall(body, name="dims_delta", grid=(4, T // tt), in_specs=[spec, spec],
                  out_specs=pl.BlockSpec((None, 1, tt), lambda h, i: (h, 0, i)),
                  out_shape=_sds((4, 1, T), F32), compiler_params=_cp("parallel", "parallel"))(doT, oT)


def _dil_bwd(qT, k, v, doT, lse, delta):
    V, nq, _, _ = qT.shape
    L = k.shape[1]
    unroll = 2 if nq % 2 == 0 else 1

    def body(qT_ref, k_ref, v_ref, doT_ref, lse_ref, dl_ref, out_ref, dk_acc, dv_acc):
        dk_acc[...] = jnp.zeros_like(dk_acc)
        dv_acc[...] = jnp.zeros_like(dv_acc)

        def tile(i):
            start, mask = _dil_window(i, L)
            win = pl.ds(start, DIL_K)
            kw, qT, doT = k_ref[win, :], qT_ref[i], doT_ref[i]
            sT = jnp.where(mask, jnp.dot(kw, qT, preferred_element_type=F32), NEG)
            pT = jnp.exp(sT - lse_ref[i])
            dv = _nt(pT.astype(BF), doT)
            dsT = (pT * (jnp.dot(v_ref[win, :], doT, preferred_element_type=F32) - dl_ref[i])).astype(BF)
            dk = _nt(dsT, qT)
            out_ref[0, pl.ds(pl.multiple_of(i * DIL_Q, DIL_Q), DIL_Q), :] = _tn(dsT, kw).astype(out_ref.dtype)
            return win, dk, dv

        def tiles(ii, carry):
            done = [tile(ii * unroll + u) for u in range(unroll)]
            for win, dk, dv in done:
                dk_acc[win, :] += dk
                dv_acc[win, :] += dv
            return carry

        lax.fori_loop(0, nq // unroll, tiles, 0)
        out_ref[1] = dk_acc[...].astype(out_ref.dtype)
        out_ref[2] = dv_acc[...].astype(out_ref.dtype)

    chk = lambda r: pl.BlockSpec((None, nq, r, DIL_Q), lambda h: (h, 0, 0, 0))
    tok = pl.BlockSpec((None, L, HD), lambda h: (h, 0, 0))
    return _pcall(
        body, name=f"dil_bwd_{V // 4}", grid=(V,),
        in_specs=[chk(HD), tok, tok, chk(HD), chk(1), chk(1)],
        out_specs=pl.BlockSpec((None, 3, L, HD), lambda h: (h, 0, 0, 0)),
        out_shape=_sds((V, 3, L, HD), BF),
        scratch_shapes=[pltpu.VMEM((L, HD), F32)] * 2,
        compiler_params=_cp("parallel"),
    )(qT, k, v, doT, lse, delta)


def _dilated_fwd(qa, ka, va):
    outs, lses, saved = [], [], []
    for _, d in A_CONFIGS:
        qT, k, v = _vh_chunks(qa, d), _vh(ka, d), _vh(va, d)
        oT, lse = _dil_fwd(qT, k, jnp.concatenate([v, jnp.ones_like(v)], axis=2))
        outs.append(_chunks_to_dims(oT, d))
        lses.append(_chunks_to_dims(lse, d))
        saved.append((qT, k, v))
    o, lse = _dil_merge(outs, lses)
    return o, lse, saved


def _dilated_bwd(do_t, o, lse, saved):
    T = do_t.shape[0]
    doT = do_t.reshape(T, 4, HD).transpose(1, 2, 0)
    delta = _dims_delta(doT, o)
    total = None
    for (_, d), (qT, k, v) in zip(A_CONFIGS, saved):
        g = _dil_bwd(qT, k, v, _dims_to_chunks(doT, d), _dims_to_chunks(lse, d), _dims_to_chunks(delta, d))
        g = g.reshape(d, 4, 3, T // d, HD).transpose(3, 0, 2, 1, 4).reshape(T, 3 * A_W).astype(F32)
        total = g if total is None else total + g
    return total.astype(do_t.dtype)


NBR_Q = 128
NBR_ROWS = 10
NBR_K = NBR_ROWS * GRID_W


def _nbr_class_tiles(nq):
    return [2, 0, 1, nq - 2, nq - 1]


def _nbr_geometry(T):
    rows, nq = T // GRID_W, T // NBR_Q
    assert rows >= NBR_ROWS + 4 and nq >= 5
    kr, kc = np.divmod(np.arange(NBR_K), GRID_W)
    qr, qc = np.divmod(np.arange(NBR_Q), GRID_W)
    c0 = np.clip(qc - 8, 0, GRID_W - 16)
    col_ok = (kc[:, None] >= c0[None, :]) & (kc[:, None] < c0[None, :] + 16)
    drs, valids = [], []
    for i in _nbr_class_tiles(nq):
        start_row = int(np.clip(2 * i - 4, 0, rows - NBR_ROWS))
        r = 2 * i + qr
        r0 = np.clip(r - 4, 0, rows - 8)
        rk = start_row + kr
        row_ok = (rk[:, None] >= r0[None, :]) & (rk[:, None] < r0[None, :] + 8)
        valids.append(row_ok & col_ok)
        dr = start_row + np.arange(NBR_ROWS)[:, None] - (2 * i + np.arange(2)[None, :]) + 7
        drs.append(np.where((dr >= 0) & (dr <= 14), dr, -1))
    return np.stack(drs), np.stack(valids)


def _nbr_fold_matrices(T):
    dr, _ = _nbr_geometry(T)
    e1 = np.zeros((GRID_W * GRID_W, 128), np.float32)
    kc, qc = np.meshgrid(np.arange(GRID_W), np.arange(GRID_W), indexing="ij")
    dc = (kc - qc + 15).reshape(-1)
    keep = (dc >= 0) & (dc <= 30)
    e1[np.arange(GRID_W * GRID_W)[keep], dc[keep]] = 1.0
    n = 5 * NBR_ROWS * 2
    e2 = np.zeros((64, 4 * n), np.float32)
    for h in range(4):
        for j, d in enumerate(dr.reshape(-1)):
            if d >= 0:
                e2[h * 16 + d, h * n + j] = 1.0
    return jnp.asarray(e1), jnp.asarray(e2)


def _nbr_bias_blocks(rpb, T):
    e1, e2 = _nbr_fold_matrices(T)
    _, valid = _nbr_geometry(T)
    padded = jnp.pad(rpb, ((0, 0), (0, 1), (0, 128 - rpb.shape[2]))).reshape(64, 128)

    def body(e2t_ref, rpb_ref, e1t_ref, out_ref):
        picked = jnp.dot(e2t_ref[...], rpb_ref[...], precision=HI, preferred_element_type=F32)
        out_ref[...] = jnp.dot(picked, e1t_ref[...], precision=HI, preferred_element_type=F32)

    sub = _pcall(body, name="rpb_expand", out_shape=_sds((e2.shape[1], GRID_W * GRID_W), F32),
                 compiler_params=pltpu.CompilerParams(vmem_limit_bytes=VMEM_LIMIT))(e2.T, padded, e1.T)
    blocks = sub.reshape(4, 5, NBR_ROWS, 2, GRID_W, GRID_W).transpose(0, 1, 2, 4, 3, 5).reshape(4, 5, NBR_K, NBR_Q)
    return jnp.where(valid[None], blocks, NEG)


def _nbr_rpb_grad(dbias, T):
    e1, e2 = _nbr_fold_matrices(T)
    sub = dbias.reshape(4, 5, NBR_ROWS, GRID_W, 2, GRID_W).transpose(0, 1, 2, 4, 3, 5).reshape(-1, GRID_W * GRID_W)

    def body(e2_ref, sub_ref, e1_ref, out_ref):
        diag = jnp.dot(sub_ref[...], e1_ref[...], precision=HI, preferred_element_type=F32)
        out_ref[...] = jnp.dot(e2_ref[...], diag, precision=HI, preferred_element_type=F32)

    out = _pcall(body, name="rpb_fold", out_shape=_sds((64, 128), F32),
                 compiler_params=pltpu.CompilerParams(vmem_limit_bytes=VMEM_LIMIT))(e2, sub, e1)
    return out.reshape(4, 16, 128)[:, :15, :31]


def _nbr_tile(i, nq, T):
    start = pl.multiple_of(jnp.clip(i * NBR_Q - 4 * GRID_W, 0, T - NBR_K), NBR_Q)
    cls = jnp.where(i == 0, 1, jnp.where(i == 1, 2, jnp.where(i == nq - 2, 3, jnp.where(i == nq - 1, 4, 0))))
    return start, cls


def _nbr_fwd(qT, k, v, bias):
    H, nq, _, _ = qT.shape
    T = k.shape[1]
    unroll = 2 if nq % 2 == 0 else 1

    def body(q_ref, k_ref, v_ref, b_ref, o_ref, lse_ref):
        def tile(i):
            start, cls = _nbr_tile(i, nq, T)
            win = pl.ds(start, NBR_K)
            sT = jnp.dot(k_ref[win, :], q_ref[i], preferred_element_type=F32) + b_ref[cls]
            m = jnp.max(sT, axis=0, keepdims=True)
            pT = jnp.exp(sT - m).astype(BF)
            acc = _tn(v_ref[win, :], pT)
            l = jnp.max(acc[HD:HD + 8], axis=0, keepdims=True)
            o_ref[i] = (acc[:HD] / l).astype(o_ref.dtype)
            lse_ref[i] = m + jnp.log(l)

        def tiles(ii, carry):
            for u in range(unroll):
                tile(ii * unroll + u)
            return carry

        lax.fori_loop(0, nq // unroll, tiles, 0)

    chk = lambda r: pl.BlockSpec((None, nq, r, NBR_Q), lambda h: (h, 0, 0, 0))
    tok = lambda w: pl.BlockSpec((None, T, w), lambda h: (h, 0, 0))
    return _pcall(
        body, name="nbr_fwd", grid=(H,),
        in_specs=[chk(HD), tok(HD), tok(2 * HD), pl.BlockSpec((None, 5, NBR_K, NBR_Q), lambda h: (h, 0, 0, 0))],
        out_specs=[chk(HD), chk(1)],
        out_shape=[_sds((H, nq, HD, NBR_Q), BF), _sds((H, nq, 1, NBR_Q), F32)],
        compiler_params=_cp("parallel"),
    )(qT, k, v, bias)


def _nbr_bwd(qT, k, v, doT, lse, delta, bias):
    H, nq, _, _ = qT.shape
    T = k.shape[1]

    def body(qT_ref, k_ref, v_ref, doT_ref, lse_ref, dl_ref, b_ref, dq_ref, dk_ref, dv_ref, db_ref, dk_acc, dv_acc):
        dk_acc[...] = jnp.zeros_like(dk_acc)
        dv_acc[...] = jnp.zeros_like(dv_acc)
        db_ref[...] = jnp.zeros_like(db_ref)

        def tile(i, carry):
            start, cls = _nbr_tile(i, nq, T)
            win = pl.ds(start, NBR_K)
            kw, qT, doT = k_ref[win, :], qT_ref[i], doT_ref[i]
            sT = jnp.dot(kw, qT, preferred_element_type=F32) + b_ref[cls]
            pT = jnp.exp(sT - lse_ref[i])
            dv_acc[win, :] += _nt(pT.astype(BF), doT)
            ds = pT * (jnp.dot(v_ref[win, :], doT, preferred_element_type=F32) - dl_ref[i])
            db_ref[cls] += ds
            dsT = ds.astype(BF)
            dk_acc[win, :] += _nt(dsT, qT)
            dq_ref[i] = _tn(kw, dsT).astype(dq_ref.dtype)
            return carry

        lax.fori_loop(0, nq, tile, 0)
        dk_ref[...] = dk_acc[...].astype(dk_ref.dtype)
        dv_ref[...] = dv_acc[...].astype(dv_ref.dtype)

    chk = lambda r: pl.BlockSpec((None, nq, r, NBR_Q), lambda h: (h, 0, 0, 0))
    tok = pl.BlockSpec((None, T, HD), lambda h: (h, 0, 0))
    bsp = pl.BlockSpec((None, 5, NBR_K, NBR_Q), lambda h: (h, 0, 0, 0))
    return _pcall(
        body, name="nbr_bwd", grid=(H,),
        in_specs=[chk(HD), tok, tok, chk(HD), chk(1), chk(1), bsp],
        out_specs=[chk(HD), tok, tok, bsp],
        out_shape=[_sds((H, nq, HD, NBR_Q), BF), _sds((H, T, HD), BF), _sds((H, T, HD), BF),
                   _sds((H, 5, NBR_K, NBR_Q), F32)],
        scratch_shapes=[pltpu.VMEM((T, HD), F32)] * 2,
        compiler_params=_cp("parallel"),
    )(qT, k, v, doT, lse, delta, bias)


def _sigmoid(z):
    return 1.0 / (1.0 + jnp.exp(-z))


def _merge_fwd(oa, ob, oc, hg, bg, wa, wb, wc):
    T = oa.shape[0]
    tt = min(512, T)

    def body(oa_ref, ob_ref, oc_ref, hg_ref, bg_ref, wa_ref, wb_ref, wc_ref, out_ref):
        acc = None
        for k, (o_ref, w_ref) in enumerate(((oa_ref, wa_ref), (ob_ref, wb_ref), (oc_ref, wc_ref))):
            y = jnp.dot(o_ref[...], w_ref[...], preferred_element_type=F32)
            g = _sigmoid(hg_ref[:, D * k:D * (k + 1)] + bg_ref[:, D * k:D * (k + 1)])
            acc = g * y if acc is None else acc + g * y
        out_ref[...] = acc.astype(out_ref.dtype)

    row = lambda w: pl.BlockSpec((tt, w), lambda i: (i, 0))
    const = lambda a: pl.BlockSpec(a.shape, lambda i: (0, 0))
    return _pcall(
        body, name="merge_fwd", grid=(T // tt,),
        in_specs=[row(A_W), row(BQ_W), row(C_W), row(GATE), const(bg), const(wa), const(wb), const(wc)],
        out_specs=row(D), out_shape=_sds((T, D), BF),
        compiler_params=_cp("parallel"),
    )(oa, ob, oc, hg, bg, wa, wb, wc)


def _merge_bwd(dm, oa, ob, oc, hg, bg, wa, wb, wc):
    T = oa.shape[0]
    tt = min(256, T)

    def body(dm_ref, oa_ref, ob_ref, oc_ref, hg_ref, bg_ref, wa_ref, wb_ref, wc_ref,
             dya, dyb, dyc, doa, dob, doc, dhg, dbg):
        @pl.when(pl.program_id(0) == 0)
        def _():
            dbg[...] = jnp.zeros_like(dbg)

        dmv = dm_ref[...]
        for k, (o_ref, w_ref, dy_ref, do_ref) in enumerate(
                ((oa_ref, wa_ref, dya, doa), (ob_ref, wb_ref, dyb, dob), (oc_ref, wc_ref, dyc, doc))):
            sl = slice(D * k, D * (k + 1))
            y = jnp.dot(o_ref[...], w_ref[...], preferred_element_type=F32)
            g = _sigmoid(hg_ref[:, sl] + bg_ref[:, sl])
            dy = (dmv * g).astype(BF)
            dy_ref[...] = dy
            do_ref[...] = _nt(dy, w_ref[...]).astype(do_ref.dtype)
            dz = dmv * y * (g * (1.0 - g))
            dhg[:, sl] = dz.astype(dhg.dtype)
            dbg[:, sl] += jnp.sum(dz, axis=0, keepdims=True)

    row = lambda w: pl.BlockSpec((tt, w), lambda i: (i, 0))
    const = lambda a: pl.BlockSpec(a.shape, lambda i: (0, 0))
    return _pcall(
        body, name="merge_bwd", grid=(T // tt,),
        in_specs=[row(D), row(A_W), row(BQ_W), row(C_W), row(GATE), const(bg), const(wa), const(wb), const(wc)],
        out_specs=[row(D)] * 3 + [row(A_W), row(BQ_W), row(C_W), row(GATE),
                                  pl.BlockSpec((1, GATE), lambda i: (0, 0))],
        out_shape=[_sds((T, D), BF)] * 3 + [_sds((T, A_W), BF), _sds((T, BQ_W), BF), _sds((T, C_W), BF),
                                            _sds((T, GATE), BF), _sds((1, GATE), F32)],
        compiler_params=_cp("arbitrary"),
    )(dm, oa, ob, oc, hg, bg, wa, wb, wc)


def _lin_ln(a, w, res, g, b):
    T, K = a.shape
    tt = min(256, T)

    def body(a_ref, w_ref, res_ref, g_ref, b_ref, y_ref, yb_ref, xh_ref, rs_ref):
        u = ALPHA * res_ref[...] + jnp.dot(a_ref[...], w_ref[...], preferred_element_type=F32)
        mu = jnp.mean(u, axis=-1, keepdims=True)
        c = u - mu
        r = lax.rsqrt(jnp.mean(c * c, axis=-1, keepdims=True) + LN_EPS)
        xh = c * r
        y = xh * g_ref[...] + b_ref[...]
        y_ref[...] = y
        yb_ref[...] = y.astype(BF)
        xh_ref[...] = xh
        rs_ref[...] = r

    row = lambda w_: pl.BlockSpec((tt, w_), lambda i: (i, 0))
    const = lambda s: pl.BlockSpec(s, lambda i: (0, 0))
    return _pcall(
        body, name="lin_ln", grid=(T // tt,),
        in_specs=[row(K), const((K, D)), row(D), const((1, D)), const((1, D))],
        out_specs=[row(D), row(D), row(D), row(1)],
        out_shape=[_sds((T, D), F32), _sds((T, D), BF), _sds((T, D), F32), _sds((T, 1), F32)],
        compiler_params=_cp("parallel"),
    )(a, w, res, g, b)


def _ln_bwd(dy, xh, rs, g):
    T = dy.shape[0]
    tt = min(512, T)

    def body(dy_ref, xh_ref, rs_ref, g_ref, du_ref, dub_ref, dg_ref, db_ref):
        @pl.when(pl.program_id(0) == 0)
        def _():
            dg_ref[...] = jnp.zeros_like(dg_ref)
            db_ref[...] = jnp.zeros_like(db_ref)

        dyv, xhv = dy_ref[...], xh_ref[...]
        dg_ref[...] += jnp.sum(dyv * xhv, axis=0, keepdims=True)
        db_ref[...] += jnp.sum(dyv, axis=0, keepdims=True)
        dxh = dyv * g_ref[...]
        m1 = jnp.mean(dxh, axis=-1, keepdims=True)
        m2 = jnp.mean(dxh * xhv, axis=-1, keepdims=True)
        du = rs_ref[...] * (dxh - m1 - xhv * m2)
        du_ref[...] = du
        dub_ref[...] = du.astype(BF)

    row = lambda w_: pl.BlockSpec((tt, w_), lambda i: (i, 0))
    const = lambda s: pl.BlockSpec(s, lambda i: (0, 0))
    return _pcall(
        body, name="ln_bwd", grid=(T // tt,),
        in_specs=[row(D), row(D), row(1), const((1, D))],
        out_specs=[row(D), row(D), const((1, D)), const((1, D))],
        out_shape=[_sds((T, D), F32), _sds((T, D), BF), _sds((1, D), F32), _sds((1, D), F32)],
        compiler_params=_cp("arbitrary"),
    )(dy, xh, rs, g)


def _loss_grad(y, tgt):
    T = y.shape[0]
    tt = min(512, T)

    def body(y_ref, t_ref, dy_ref, sq_ref):
        @pl.when(pl.program_id(0) == 0)
        def _():
            sq_ref[...] = jnp.zeros_like(sq_ref)

        e = y_ref[...] - t_ref[...]
        dy_ref[...] = e * (1.0 / D)
        sq_ref[...] += jnp.sum(e * e, axis=0, keepdims=True)

    row = pl.BlockSpec((tt, D), lambda i: (i, 0))
    return _pcall(
        body, name="loss_grad", grid=(T // tt,),
        in_specs=[row, row], out_specs=[row, pl.BlockSpec((1, D), lambda i: (0, 0))],
        out_shape=[_sds((T, D), F32), _sds((1, D), F32)],
        compiler_params=_cp("arbitrary"),
    )(y, tgt)


def _position():
    return lax.axis_index("x"), lax.axis_index("y"), lax.axis_index("c")


HBM = pl.BlockSpec(memory_space=pl.ANY)


def _gather_phases(x_refs, out_refs, send, recv, loc):
    n = len(x_refs)
    x, y, c = _position()
    me, sib = (x, y, c), (x, y, 1 - c)
    chips = [(1 - x, y), (x, 1 - y), (1 - x, 1 - y)]

    def copy(a, k, block, to, src=None):
        px, py, pc = block
        dst = out_refs[a].at[4 * px + 2 * py + pc]
        return pltpu.make_async_remote_copy(
            src_ref=dst if src is None else src, dst_ref=dst,
            send_sem=send.at[a, k], recv_sem=recv.at[a, k], device_id=to, device_id_type=MESH)

    mine = [pltpu.make_async_copy(x_refs[a], out_refs[a].at[4 * x + 2 * y + c], loc.at[a]) for a in range(n)]
    first = []
    for a in range(n):
        first.append(copy(a, 0, me, sib, src=x_refs[a]))
        first += [copy(a, 1 + j, me, (*chip, c), src=x_refs[a]) for j, chip in enumerate(chips)]
    passed = [copy(a, 4 + j, (*chip, c), sib) for j, chip in enumerate(chips) for a in range(n)]

    def start():
        for cp in mine + first:
            cp.start()

    def forward():
        for j, chip in enumerate(chips):
            for a in range(n):
                copy(a, 1 + j, (*chip, c), me).wait_recv()
                copy(a, 4 + j, (*chip, c), sib).start()

    def finish():
        for a in range(n):
            copy(a, 0, sib, me).wait_recv()
            for j, chip in enumerate(chips):
                copy(a, 4 + j, (*chip, 1 - c), me).wait_recv()
        for cp in first + passed:
            cp.wait_send()
        for cp in mine:
            cp.wait()

    return start, forward, finish


def _gather_scratch(n):
    return [pltpu.SemaphoreType.DMA((n, 7)), pltpu.SemaphoreType.DMA((n, 7)), pltpu.SemaphoreType.DMA((n,))]


def _all_gather(xs, name):
    n = len(xs)

    def body(*refs):
        for phase in _gather_phases(refs[:n], refs[n:2 * n], *refs[2 * n:]):
            phase()

    return _pcall_comm(
        body, name=name,
        in_specs=[HBM] * n, out_specs=[HBM] * n,
        out_shape=[_sds((8,) + x.shape, x.dtype) for x in xs],
        scratch_shapes=_gather_scratch(n),
    )(*xs)


def _scatter_phases(g_refs, land_refs, send, recv):
    x, y, c = _position()
    peers = [(x, y, 1 - c), (1 - x, y, c), (x, 1 - y, c), (1 - x, 1 - y, c),
             (1 - x, y, 1 - c), (x, 1 - y, 1 - c), (1 - x, 1 - y, 1 - c)]
    copies = [pltpu.make_async_remote_copy(
        src_ref=g_refs[a].at[4 * px + 2 * py + pc], dst_ref=land_refs[a].at[k],
        send_sem=send.at[a, k], recv_sem=recv.at[a, k], device_id=(px, py, pc), device_id_type=MESH)
        for a in range(len(g_refs)) for k, (px, py, pc) in enumerate(peers)]

    def start():
        for cp in copies:
            cp.start()

    def finish():
        for cp in copies:
            cp.wait()

    return start, finish


def _scatter_scratch(n):
    return [pltpu.SemaphoreType.DMA((n, 7)), pltpu.SemaphoreType.DMA((n, 7))]


def _scatter(gs, name):
    n = len(gs)

    def body(*refs):
        for phase in _scatter_phases(refs[:n], refs[n:2 * n], *refs[2 * n:]):
            phase()

    return _pcall_comm(
        body, name=name,
        in_specs=[HBM] * n, out_specs=[HBM] * n,
        out_shape=[_sds((7,) + g.shape[1:], g.dtype) for g in gs],
        scratch_shapes=_scatter_scratch(n),
    )(*gs)


def _sum8(g, land, me):
    _, R, C = g.shape
    tr = min(512, R)

    def body(me_ref, g_ref, l_ref, out_ref):
        acc = g_ref[...].astype(F32)
        for k in range(7):
            acc = acc + l_ref[k].astype(F32)
        out_ref[...] = acc

    return _pcall(
        body, name="sum8",
        grid_spec=pltpu.PrefetchScalarGridSpec(
            num_scalar_prefetch=1, grid=(R // tr,),
            in_specs=[pl.BlockSpec((None, tr, C), lambda i, me_: (me_[0], i, 0)),
                      pl.BlockSpec((7, tr, C), lambda i, me_: (0, i, 0))],
            out_specs=pl.BlockSpec((tr, C), lambda i, me_: (i, 0))),
        out_shape=_sds((R, C), F32),
        compiler_params=_cp("parallel"),
    )(me, g, land)


def _adamw(g, w, m, v):
    R, C = w.shape
    tr = min(512, R)

    def body(g_ref, w_ref, m_ref, v_ref, d_out, m_out, v_out):
        d_out[...], m_out[...], v_out[...] = _adamw_math(w_ref[...], g_ref[...], m_ref[...], v_ref[...])

    blk = pl.BlockSpec((tr, C), lambda i: (i, 0))
    return _pcall(body, name="adamw", grid=(R // tr,), in_specs=[blk] * 4, out_specs=[blk] * 3,
                  out_shape=[_sds((R, C), F32)] * 3, compiler_params=_cp("parallel"))(g, w, m, v)


def _adamw_math(w, g, m, v):
    m = B1 * m + (1.0 - B1) * g
    v = B2 * v + (1.0 - B2) * (g * g)
    m_hat = m / (1.0 - B1 ** STEP)
    v_hat = v / (1.0 - B2 ** STEP)
    delta = -LR * (m_hat / (jnp.sqrt(v_hat) + EPS) + WD * w)
    return delta, m, v


def _small_sum_adamw(parts, w, m, v):
    _, R, C = parts.shape

    def body(p_ref, w_ref, m_ref, v_ref, g_out, d_out, m_out, v_out):
        g = p_ref[0]
        for k in range(1, 8):
            g = g + p_ref[k]
        d, mn, vn = _adamw_math(w_ref[...], g, m_ref[...], v_ref[...])
        g_out[...], d_out[...], m_out[...], v_out[...] = g, d, mn, vn

    return _pcall(body, name="small_sum_adamw", out_shape=[_sds((R, C), F32)] * 4,
                  compiler_params=pltpu.CompilerParams(vmem_limit_bytes=VMEM_LIMIT))(parts, w, m, v)


BIG = ("w_in", "w_branch_a", "w_branch_b", "w_branch_c", "w_out", "w_up", "w_down")
ROW_SHARDED = ("w_out", "w_down")
EARLY = ("w_down", "w_up", "w_out", "w_branch_a", "w_branch_b", "w_branch_c")
SMALL = ("b_gate", "q_norm_b", "k_norm_b", "rpb_c", "ln1_g", "ln1_b", "ln2_g", "ln2_b")
NAMES = ("w_in", "b_gate", "q_norm_b", "k_norm_b", "rpb_c", "w_branch_a", "w_branch_b", "w_branch_c",
         "w_out", "ln1_g", "ln1_b", "w_up", "w_down", "ln2_g", "ln2_b")


def _full_weight(blk, name):
    if name in ROW_SHARDED:
        return blk.reshape(-1, blk.shape[2])
    return blk.transpose(1, 0, 2).reshape(blk.shape[1], -1)


def _chunks(grad, name):
    if name in ROW_SHARDED:
        return grad.reshape(8, grad.shape[0] // 8, grad.shape[1])
    return grad.reshape(grad.shape[0], 8, grad.shape[1] // 8).transpose(1, 0, 2)


def _layer_fwd(x, xb, W, P, tabs, gm, gather=(), late=None):
    hq, = _mm(xb, W["w_qkv"], "nn", [F32], 1024, 768, 1024, name="in_qkv")
    hg, = _mm(xb, W["w_gate"], "nn", [F32], 1024, 1024, 1024, name="in_gate")
    tab_a, tab_b = tabs
    prepped = _prep_fwd(hq, tab_a, tab_b, P["qn"], P["kn"], gm)
    T = x.shape[0]
    tb = min(FULL_T, T)
    oa, lse_a, qa = _dilated_fwd(*prepped[0:3])
    qb_t, kb_t, vb_t = prepped[3:6]
    qb, kb, vb = _chunked_t(qb_t, tb), _to_heads(kb_t), _chunked_t(vb_t, min(FULL_TK, T))
    vb1 = jnp.concatenate([vb, jnp.ones(vb.shape[:2] + (V_ROWS - HD, vb.shape[3]), vb.dtype)], axis=2)
    ob, lse_b, gathered = _attn_full_fwd(qb, kb, vb1, gather)
    if late is not None:
        W = {**W, **late(gathered)}
    qc_t, kc_t, vc_t = prepped[6:9]
    qc, kc, vc = _chunked_t(qc_t, NBR_Q), _to_heads(kc_t), _to_heads(vc_t)
    bias_c = _nbr_bias_blocks(P["rpb"], T)
    oc, lse_c = _nbr_fwd(qc, kc, jnp.concatenate([vc, jnp.ones_like(vc)], axis=2), bias_c)
    oa_t, ob_t, oc_t = oa.transpose(2, 0, 1).reshape(T, A_W), _unchunk_t(ob), _unchunk_t(oc)
    qb = (qb, kb, kb.transpose(0, 2, 1), _to_heads(vb_t))
    ka = va = None
    merged = _merge_fwd(oa_t, ob_t, oc_t, hg, P["bg"], W["w_branch_a"], W["w_branch_b"], W["w_branch_c"])
    x1, x1b, xh1, rs1 = _lin_ln(merged, W["w_out"], x, P["ln1_g"], P["ln1_b"])

    def relu2(acc):
        r = jnp.maximum(acc, 0.0)
        return r * r, r

    f, r = _mm(x1b, W["w_up"], "nn", [BF, BF], 1024, 1024, 1024, epilogue=relu2, name="mlp_up")
    x2, x2b, xh2, rs2 = _lin_ln(f, W["w_down"], x1, P["ln2_g"], P["ln2_b"])
    saved = dict(xb=xb, hq=hq, hg=hg, qkv=(qa, ka, va, qb, kb, vb, qc, kc, vc), o=(oa, ob, oc),
                 lse=(lse_a, lse_b, lse_c), o_t=(oa_t, ob_t, oc_t), bias_c=bias_c, merged=merged,
                 xh1=xh1, rs1=rs1, x1b=x1b, f=f, r=r, xh2=xh2, rs2=rs2)
    return x2, x2b, saved, W


def _layer_bwd(dx2, S, W, P, tabs, gm, scatter=()):
    G = {}
    du2, du2b, G["ln2_g"], G["ln2_b"] = _ln_bwd(dx2, S["xh2"], S["rs2"], P["ln2_g"])
    G["w_down"], = _mm(S["f"], du2b, "tn", [F32], 1024, 1024, 1024, name="dw_down")
    da, = _mm(du2b, W["w_down"], "nt", [BF], 1024, 1024, 1024,
              epilogue=lambda acc, r: (acc * (2.0 * r.astype(F32)),), extras=(S["r"],), name="d_act")
    G["w_up"], = _mm(S["x1b"], da, "tn", [F32], 1024, 1024, 1024, name="dw_up")
    dx1, = _mm(da, W["w_up"], "nt", [F32], 1024, 1024, 1024,
               epilogue=lambda acc, d: (ALPHA * d + acc,), extras=(du2,), name="dx_mlp")
    du1, du1b, G["ln1_g"], G["ln1_b"] = _ln_bwd(dx1, S["xh1"], S["rs1"], P["ln1_g"])
    G["w_out"], = _mm(S["merged"], du1b, "tn", [F32], 1024, 1024, 512, name="dw_out")
    dm, = _mm(du1b, W["w_out"], "nt", [F32], 1024, 1024, 1024, name="d_merged")
    oa_t, ob_t, oc_t = S["o_t"]
    dya, dyb, dyc, doa, dob, doc, dhg, G["b_gate"] = _merge_bwd(
        dm, oa_t, ob_t, oc_t, S["hg"], P["bg"], W["w_branch_a"], W["w_branch_b"], W["w_branch_c"])
    G["w_branch_a"], = _mm(oa_t, dya, "tn", [F32], 256, 1024, 512, name="dw_branch_a")
    G["w_branch_b"], = _mm(ob_t, dyb, "tn", [F32], 512, 1024, 512, name="dw_branch_b")
    G["w_branch_c"], = _mm(oc_t, dyc, "tn", [F32], 256, 1024, 512, name="dw_branch_c")

    qa, ka, va, qb, kb, vb, qc, kc, vc = S["qkv"]
    oa, ob, oc = S["o"]
    lse_a, lse_b, lse_c = S["lse"]
    dqkv_a = _dilated_bwd(doa, oa, lse_a, qa)
    qT_b, k_b, kT_b, v_b = qb
    dobT = _chunked_t(dob, ob.shape[-1])
    early = {n: _chunks(G[n], n).astype(BF) for n in EARLY}
    dqbT, dkb8, dvb8, landed = _attn_full_bwd(qT_b, k_b, kT_b, v_b, dobT, lse_b, _attn_delta(dobT, ob),
                                              list(scatter) + [early[n] for n in EARLY])
    landed = (landed[:len(scatter)], dict(zip(EARLY, landed[len(scatter):])), early)
    group_sum = lambda t: t.reshape(k_b.shape[0], -1, t.shape[1], HD).sum(1)
    dqb, dkb, dvb = _unchunk_t(dqbT), _from_heads(group_sum(dkb8)), _from_heads(group_sum(dvb8))
    bias_c = S["bias_c"]
    docT = _chunked_t(doc, NBR_Q)
    dqcT, dkc, dvc, dbias_c = _nbr_bwd(qc, kc, vc, docT, lse_c, _attn_delta(docT, oc), bias_c)
    G["rpb_c"] = _nbr_rpb_grad(dbias_c, dx2.shape[0])

    tab_a, tab_b = tabs
    grads = [dqkv_a, dqb, dkb, dvb, _unchunk_t(dqcT), _from_heads(dkc), _from_heads(dvc)]
    dhq, dqn, dkn = _prep_bwd(S["hq"], grads, tab_a, tab_b, P["qn"], P["kn"], gm)
    G["q_norm_b"] = dqn.reshape(BQ_W // HD, HD).sum(0)
    G["k_norm_b"] = dkn.reshape(BKV_W // HD, HD).sum(0)
    dw_qkv, = _mm(S["xb"], dhq, "tn", [F32], 1024, 768, 1024, name="dw_qkv")
    dw_gate, = _mm(S["xb"], dhg, "tn", [F32], 1024, 1024, 1024, name="dw_gate")
    G["w_in"] = jnp.concatenate([dw_qkv, dw_gate], axis=1)
    dx_a, = _mm(dhq, W["w_qkv"], "nt", [F32], 1024, 1024, 768,
                epilogue=lambda acc, d: (ALPHA * d + acc,), extras=(du1,), name="dx_qkv")
    dx, = _mm(dhg, W["w_gate"], "nt", [F32], 1024, 1024, 1024,
              epilogue=lambda acc, d: (d + acc,), extras=(dx_a,), name="dx_gate")
    return dx, G, landed


def _pack_small(vals):
    flat = jnp.concatenate([vals[n].reshape(-1).astype(F32) for n in SMALL])
    pad = (-flat.shape[0]) % (8 * 128)
    return jnp.pad(flat, (0, pad)).reshape(-1, 128)


def _unpack_small(packed, like):
    flat, out, off = packed.reshape(-1), {}, 0
    for n in SMALL:
        size = math.prod(like[n].shape)
        out[n] = flat[off:off + size].reshape(like[n].shape)
        off += size
    return out


def kernel(x, w_in, b_gate, q_norm_b, k_norm_b, rpb_c, w_branch_a, w_branch_b, w_branch_c, w_out, ln1_g, ln1_b, w_up, w_down, ln2_g, ln2_b, loss_target, m_w_in, m_b_gate, m_q_norm_b, m_k_norm_b, m_rpb_c, m_w_branch_a, m_w_branch_b, m_w_branch_c, m_w_out, m_ln1_g, m_ln1_b, m_w_up, m_w_down, m_ln2_g, m_ln2_b, v_w_in, v_b_gate, v_q_norm_b, v_k_norm_b, v_rpb_c, v_w_branch_a, v_w_branch_b, v_w_branch_c, v_w_out, v_ln1_g, v_ln1_b, v_w_up, v_w_down, v_ln2_g, v_ln2_b):
    w = dict(w_in=w_in, b_gate=b_gate, q_norm_b=q_norm_b, k_norm_b=k_norm_b, rpb_c=rpb_c,
             w_branch_a=w_branch_a, w_branch_b=w_branch_b, w_branch_c=w_branch_c, w_out=w_out,
             ln1_g=ln1_g, ln1_b=ln1_b, w_up=w_up, w_down=w_down, ln2_g=ln2_g, ln2_b=ln2_b)
    m = dict(w_in=m_w_in, b_gate=m_b_gate, q_norm_b=m_q_norm_b, k_norm_b=m_k_norm_b, rpb_c=m_rpb_c,
             w_branch_a=m_w_branch_a, w_branch_b=m_w_branch_b, w_branch_c=m_w_branch_c, w_out=m_w_out,
             ln1_g=m_ln1_g, ln1_b=m_ln1_b, w_up=m_w_up, w_down=m_w_down, ln2_g=m_ln2_g, ln2_b=m_ln2_b)
    v = dict(w_in=v_w_in, b_gate=v_b_gate, q_norm_b=v_q_norm_b, k_norm_b=v_k_norm_b, rpb_c=v_rpb_c,
             w_branch_a=v_w_branch_a, w_branch_b=v_w_branch_b, w_branch_c=v_w_branch_c, w_out=v_w_out,
             ln1_g=v_ln1_g, ln1_b=v_ln1_b, w_up=v_w_up, w_down=v_w_down, ln2_g=v_ln2_g, ln2_b=v_ln2_b)
    T = x.shape[1]
    xc, yc, cc = _position()

    flat2 = lambda a: a.reshape(-1, a.shape[-1])
    shards = {n: w[n].astype(BF) for n in BIG}
    w_in0, = _all_gather([shards["w_in"][0]], "gather_w_in0")
    riding = [shards[n][0] for n in BIG[1:]] + [flat2(shards[n][1:]) for n in BIG]

    tabs = _rope_tables(T)
    gm = _group_mean_matrix()

    def in_proj(blk):
        w_in_l = _full_weight(blk, "w_in")
        return {"w_qkv": w_in_l[:, :QKV], "w_gate": w_in_l[:, QKV:]}

    Ws = [in_proj(w_in0)]

    def late(gathered):
        rest = [g.reshape(8, DEPTH - 1, -1, g.shape[2]) for g in gathered[len(BIG) - 1:]]
        for i in range(DEPTH - 1):
            W = {n: _full_weight(g[:, i], n) for n, g in zip(BIG[1:], rest[1:])}
            Ws.append({**W, **in_proj(rest[0][:, i])})
        return {n: _full_weight(g, n) for n, g in zip(BIG[1:], gathered)}

    Ps = [dict(qn=jnp.tile(q_norm_b[l][None], (1, 2)), kn=jnp.tile(k_norm_b[l][None], (1, 2)),
               rpb=rpb_c[l], bg=b_gate[l][None], ln1_g=ln1_g[l][None], ln1_b=ln1_b[l][None],
               ln2_g=ln2_g[l][None], ln2_b=ln2_b[l][None]) for l in range(DEPTH)]

    h = x[0]
    hb = h.astype(BF)
    saved = []
    for l in range(DEPTH):
        h, hb, S, Ws[l] = _layer_fwd(h, hb, Ws[l], Ps[l], tabs, gm, *((riding, late) if l == 0 else ()))
        saved.append(S)
    dy, sq = _loss_grad(h, loss_target[0])
    loss = lax.psum(0.5 / D * jnp.sum(sq), AXES)

    grads = [None] * DEPTH
    chunks, landed = [{} for _ in range(DEPTH)], [{} for _ in range(DEPTH)]
    for l in reversed(range(DEPTH)):
        above = [chunks[l + 1]["w_in"]] if l + 1 < DEPTH else []
        dy, grads[l], (arrived, landed_early, early) = _layer_bwd(dy, saved[l], Ws[l], Ps[l], tabs, gm, above)
        if above:
            landed[l + 1]["w_in"], = arrived
        chunks[l].update(early)
        landed[l].update(landed_early)
        chunks[l]["w_in"] = _chunks(grads[l]["w_in"], "w_in").astype(BF)
    landed[0]["w_in"], = _scatter([chunks[0]["w_in"]], "scatter_w_in0")
    grad_x = dy[None]
    me = (4 * xc + 2 * yc + cc).reshape(1).astype(jnp.int32)
    out_g, out_d, out_m, out_v = {}, {}, {}, {}
    for n in BIG:
        g = jnp.concatenate([_sum8(chunks[l][n], landed[l][n], me) for l in range(DEPTH)], axis=0)
        res = (g,) + tuple(_adamw(g, flat2(w[n]), flat2(m[n]), flat2(v[n])))
        out_g[n], out_d[n], out_m[n], out_v[n] = [t.reshape(w[n].shape) for t in res]

    part = _pack_small({n: jnp.stack([grads[l][n].reshape(w[n].shape[1:]) for l in range(DEPTH)]) for n in SMALL})
    parts, = _all_gather([part], "gather_small_grads")
    res = _small_sum_adamw(parts, _pack_small(w), _pack_small(m), _pack_small(v))
    for dst, packed in zip((out_g, out_d, out_m, out_v), res):
        dst.update(_unpack_small(packed, w))

    return (loss, grad_x, *[out_g[n] for n in NAMES], *[out_d[n] for n in NAMES],
            *[out_m[n] for n in NAMES], *[out_v[n] for n in NAMES])
```

```python
import math

import numpy as np
import jax
import jax.numpy as jnp
from jax import lax
from jax.experimental import pallas as pl
from jax.experimental.pallas import tpu as pltpu

F32 = jnp.float32
BF = jnp.bfloat16
HI = lax.Precision.HIGHEST
NEG = -1e30
MESH = pl.DeviceIdType.MESH
AXES = ("x", "y", "c")

D = 1024
DEPTH = 4
HD = 64
A_W, BQ_W, BKV_W, C_W = 256, 512, 128, 256
QKV = 2304
GATE = 3072
D_FF = 4096
GRID_W = 64
ALPHA = (2 * DEPTH) ** 0.25
LN_EPS = 1e-5
RMS_EPS = 1e-6
SCALE = HD ** -0.5
ROPE_THETA = 500000.0
AXIAL_THETA = 10000.0
A_CONFIGS = ((128, 1), (512, 4), (2048, 16))
LR, B1, B2, EPS, WD, STEP = 0.001, 0.9, 0.999, 1e-08, 0.01, 10

VMEM_LIMIT = 56 * 1024 * 1024


def _pcall(body, **kw):
    return pl.pallas_call(body, **kw)


def _pcall_comm(body, **kw):
    return pl.pallas_call(body, **kw)


def _cp(*sem):
    return pltpu.CompilerParams(dimension_semantics=sem, vmem_limit_bytes=VMEM_LIMIT)


def _sds(shape, dtype):
    return jax.ShapeDtypeStruct(shape, dtype)


def _mm(a, b, dims, outs, tm, tn, tk, epilogue=None, extras=(), name="mm"):
    if dims == "tn":
        K, M = a.shape
    else:
        M, K = a.shape
    N = b.shape[0] if dims == "nt" else b.shape[1]
    tm, tn, tk = min(tm, M), min(tn, N), min(tk, K)
    assert M % tm == 0 and N % tn == 0 and K % tk == 0, (name, M, N, K, tm, tn, tk)
    nk = K // tk
    ne, no = len(extras), len(outs)

    def body(a_ref, b_ref, *rest):
        extra_refs, out_refs = rest[:ne], rest[ne:ne + no]
        av, bv = a_ref[...].astype(BF), b_ref[...].astype(BF)
        if dims == "nn":
            p = jnp.dot(av, bv, preferred_element_type=F32)
        elif dims == "nt":
            p = lax.dot_general(av, bv, (((1,), (1,)), ((), ())), preferred_element_type=F32)
        else:
            p = lax.dot_general(av, bv, (((0,), (0,)), ((), ())), preferred_element_type=F32)

        def finish(acc):
            res = epilogue(acc, *[r[...] for r in extra_refs]) if epilogue else (acc,)
            for o, r in zip(out_refs, res):
                o[...] = r.astype(o.dtype)

        if nk == 1:
            finish(p)
        else:
            acc_ref = rest[-1]
            k = pl.program_id(2)

            @pl.when(k == 0)
            def _():
                acc_ref[...] = p

            @pl.when(k > 0)
            def _():
                acc_ref[...] += p

            @pl.when(k == nk - 1)
            def _():
                finish(acc_ref[...])

    if dims == "tn":
        a_spec = pl.BlockSpec((tk, tm), lambda i, j, k: (k, i))
    else:
        a_spec = pl.BlockSpec((tm, tk), lambda i, j, k: (i, k))
    if dims == "nt":
        b_spec = pl.BlockSpec((tn, tk), lambda i, j, k: (j, k))
    else:
        b_spec = pl.BlockSpec((tk, tn), lambda i, j, k: (k, j))
    o_spec = pl.BlockSpec((tm, tn), lambda i, j, k: (i, j))
    res = _pcall(
        body, name=name, grid=(M // tm, N // tn, nk),
        in_specs=[a_spec, b_spec] + [o_spec] * ne,
        out_specs=[o_spec] * no,
        out_shape=[_sds((M, N), dt) for dt in outs],
        scratch_shapes=[pltpu.VMEM((tm, tn), F32)] if nk > 1 else [],
        compiler_params=_cp("parallel", "parallel", "arbitrary"),
    )(a, b, *extras)
    return res


def _rope_tables(T):
    pos = jnp.arange(T)

    def cs(p, theta, half):
        inv = theta ** (-jnp.arange(half, dtype=F32) / half)
        ang = p.astype(F32)[:, None] * inv[None, :]
        return jnp.cos(ang), jnp.sin(ang)

    ca, sa = cs(pos, ROPE_THETA, 8)
    one, zero, z8 = jnp.ones((T, 48), F32), jnp.zeros((T, 48), F32), jnp.zeros((T, 8), F32)
    tab_a = [jnp.concatenate(t, 1) for t in ([ca, ca, one], [-sa, z8, zero], [z8, sa, zero])]
    cr, sr = cs(pos // GRID_W, AXIAL_THETA, 16)
    cc, sc = cs(pos % GRID_W, AXIAL_THETA, 16)
    z16 = jnp.zeros((T, 16), F32)
    tab_b = [jnp.concatenate(t, 1) for t in ([cr, cr, cc, cc], [-sr, z16, -sc, z16], [z16, sr, z16, sc])]
    return [jnp.tile(t, (1, 2)) for t in tab_a], [jnp.tile(t, (1, 2)) for t in tab_b]


def _rot(x, C, S1, S2, k):
    return x * C + pltpu.roll(x, 128 - k, 1) * S1 + pltpu.roll(x, k, 1) * S2


def _rot_t(d, C, S1, S2, k):
    return d * C + pltpu.roll(d * S1, k, 1) + pltpu.roll(d * S2, 128 - k, 1)


def _group_mean_matrix():
    m = np.zeros((128, 128), np.float32)
    m[:64, :64] = 1.0 / 64
    m[64:, 64:] = 1.0 / 64
    return jnp.asarray(m)


def _prep_fwd(hq, tab_a, tab_b, qn, kn, gm):
    T = hq.shape[0]
    tt = min(256, T)
    widths = [A_W, A_W, A_W, BQ_W, BKV_W, BKV_W, C_W, C_W, C_W]

    def body(h_ref, ca, s1a, s2a, cb, s1b, s2b, qn_ref, kn_ref, gm_ref,
             qa, ka, va, qb, kb, vb, qc, kc, vc):
        def col(off, j):
            return h_ref[:, off + 128 * j: off + 128 * (j + 1)]

        for j in range(2):
            sl = slice(128 * j, 128 * (j + 1))
            qa[:, sl] = (_rot(col(0, j), ca[...], s1a[...], s2a[...], 8) * SCALE).astype(qa.dtype)
            ka[:, sl] = _rot(col(256, j), ca[...], s1a[...], s2a[...], 8).astype(ka.dtype)
            va[:, sl] = col(512, j).astype(va.dtype)
            qc[:, sl] = (col(1536, j) * SCALE).astype(qc.dtype)
            kc[:, sl] = col(1792, j).astype(kc.dtype)
            vc[:, sl] = col(2048, j).astype(vc.dtype)

        def normed(x, w):
            ms = jnp.dot(x * x, gm_ref[...], precision=HI, preferred_element_type=F32)
            return x * lax.rsqrt(ms + RMS_EPS) * w

        for j in range(4):
            y = normed(col(768, j), qn_ref[...])
            qb[:, 128 * j:128 * (j + 1)] = (_rot(y, cb[...], s1b[...], s2b[...], 16) * SCALE).astype(qb.dtype)
        y = normed(col(1280, 0), kn_ref[...])
        kb[...] = _rot(y, cb[...], s1b[...], s2b[...], 16).astype(kb.dtype)
        vb[...] = col(1408, 0).astype(vb.dtype)

    row = lambda w: pl.BlockSpec((tt, w), lambda i: (i, 0))
    const = lambda s: pl.BlockSpec(s, lambda i: (0, 0))
    return _pcall(
        body, name="prep_fwd", grid=(T // tt,),
        in_specs=[row(QKV)] + [row(128)] * 6 + [const((1, 128))] * 2 + [const((128, 128))],
        out_specs=[row(w) for w in widths],
        out_shape=[_sds((T, w), BF) for w in widths],
        compiler_params=_cp("parallel"),
    )(hq, *tab_a, *tab_b, qn, kn, gm)


def _prep_bwd(hq, grads, tab_a, tab_b, qn, kn, gm):
    T = hq.shape[0]
    tt = min(256, T)
    widths = [A_W, A_W, A_W, BQ_W, BKV_W, BKV_W, C_W, C_W, C_W]

    def body(h_ref, dqa, dka, dva, dqb, dkb, dvb, dqc, dkc, dvc,
             ca, s1a, s2a, cb, s1b, s2b, qn_ref, kn_ref, gm_ref, dh, dqn, dkn):
        i = pl.program_id(0)

        @pl.when(i == 0)
        def _():
            dqn[...] = jnp.zeros_like(dqn)
            dkn[...] = jnp.zeros_like(dkn)

        def put(off, j, val):
            dh[:, off + 128 * j: off + 128 * (j + 1)] = val.astype(dh.dtype)

        for j in range(2):
            sl = slice(128 * j, 128 * (j + 1))
            put(0, j, _rot_t(dqa[:, sl] * SCALE, ca[...], s1a[...], s2a[...], 8))
            put(256, j, _rot_t(dka[:, sl], ca[...], s1a[...], s2a[...], 8))
            put(512, j, dva[:, sl])
            put(1536, j, dqc[:, sl] * SCALE)
            put(1792, j, dkc[:, sl])
            put(2048, j, dvc[:, sl])

        def norm_bwd(x, w, e):
            ms = jnp.dot(x * x, gm_ref[...], precision=HI, preferred_element_type=F32)
            r = lax.rsqrt(ms + RMS_EPS)
            n = x * r
            dn = e * w
            proj = jnp.dot(dn * n, gm_ref[...], precision=HI, preferred_element_type=F32)
            return r * (dn - n * proj), jnp.sum(e * n, axis=0, keepdims=True)

        for j in range(4):
            sl = slice(128 * j, 128 * (j + 1))
            e = _rot_t(dqb[:, sl] * SCALE, cb[...], s1b[...], s2b[...], 16)
            dx, dw = norm_bwd(h_ref[:, 768 + 128 * j: 768 + 128 * (j + 1)], qn_ref[...], e)
            put(768, j, dx)
            dqn[:, sl] += dw
        e = _rot_t(dkb[...], cb[...], s1b[...], s2b[...], 16)
        dx, dw = norm_bwd(h_ref[:, 1280:1408], kn_ref[...], e)
        put(1280, 0, dx)
        dkn[...] += dw
        put(1408, 0, dvb[...])

    row = lambda w, c=0: pl.BlockSpec((tt, w), lambda i: (i, c))
    const = lambda s: pl.BlockSpec(s, lambda i: (0, 0))
    return _pcall(
        body, name="prep_bwd", grid=(T // tt,),
        in_specs=[row(QKV)] + [row(A_W, c) for c in range(3)] + [row(w) for w in widths[3:]] + [row(128)] * 6
        + [const((1, 128))] * 2 + [const((128, 128))],
        out_specs=[row(QKV), const((1, BQ_W)), const((1, BKV_W))],
        out_shape=[_sds((T, QKV), BF), _sds((1, BQ_W), F32), _sds((1, BKV_W), F32)],
        compiler_params=_cp("arbitrary"),
    )(hq, grads[0], grads[0], grads[0], *grads[1:], *tab_a, *tab_b, qn, kn, gm)


def _to_heads(x):
    T, W = x.shape
    return x.reshape(T, W // HD, HD).transpose(1, 0, 2)


def _from_heads(x):
    H, T, _ = x.shape
    return x.transpose(1, 0, 2).reshape(T, H * HD)


def _nt(a, b):
    return lax.dot_general(a, b, (((1,), (1,)), ((), ())), preferred_element_type=F32)


def _chunked_t(x, t):
    T, W = x.shape
    return x.reshape(T // t, t, W // HD, HD).transpose(2, 0, 3, 1)


def _unchunk_t(x):
    H, n, _, t = x.shape
    return x.transpose(1, 3, 0, 2).reshape(n * t, H * HD)


FULL_T = 512
FULL_TK = 512
FULL_HEADS = 1
V_ROWS = 72


def _phased(phases, grid, at):
    step = pl.program_id(0) * grid[1] + pl.program_id(1)
    last = grid[0] * grid[1] - 1
    for phase, frac in zip(phases, at):
        pl.when(step == int(round(frac * last)))(phase)


def _attn_full_fwd(qT, k, vT1, gather=()):
    Hq, nq, _, t = qT.shape
    Hk, T, _ = k.shape
    G = Hq // Hk
    nk, tk = vT1.shape[1], vT1.shape[3]

    HB = FULL_HEADS
    assert G % HB == 0
    ng = len(gather)
    grid = (Hq // HB, nq)

    def body(q_ref, k_ref, v_ref, *rest):
        x_refs, (o_ref, lse_ref), rest = rest[:ng], rest[ng:ng + 2], rest[ng + 2:]
        out_refs, acc_refs, sems = rest[:ng], rest[ng:ng + HB], rest[ng + HB:]
        if ng:
            _phased(_gather_phases(x_refs, out_refs, *sems), grid, (0.0, 0.75, 1.0))
        for acc_ref in acc_refs:
            acc_ref[...] = jnp.zeros((V_ROWS, t), F32)

        def scores(j, b):
            sT = jnp.dot(k_ref[pl.ds(pl.multiple_of(j * tk, tk), tk), :], q_ref[b], preferred_element_type=F32)
            return sT, jnp.max(sT, axis=0, keepdims=True)

        def update(j, b, scored, m_old):
            sT, m_tile = scored
            m_new = jnp.maximum(m_old, m_tile)
            pT = jnp.exp(sT - m_new).astype(BF)
            acc_refs[b][...] = (jnp.exp(m_old - m_new) * acc_refs[b][...]
                                + jnp.dot(v_ref[j], pT, preferred_element_type=F32))
            return m_new

        def step(j, carry):
            ms, ss = carry
            nxt = jnp.minimum(j + 1, nk - 1)
            new_s = tuple(scores(nxt, b) for b in range(HB))
            new_m = tuple(update(j, b, ss[b], ms[b]) for b in range(HB))
            return new_m, new_s

        init = (tuple(jnp.full((1, t), NEG, F32) for _ in range(HB)), tuple(scores(0, b) for b in range(HB)))
        ms, _ = lax.fori_loop(0, nk, step, init)
        for b in range(HB):
            l = acc_refs[b][pl.ds(HD, 1), :]
            o_ref[b] = (acc_refs[b][pl.ds(0, HD), :] / l).astype(o_ref.dtype)
            lse_ref[b] = ms[b] + jnp.log(l)

    qs = pl.BlockSpec((HB, None, HD, t), lambda h, i: (h, i, 0, 0))
    res = (_pcall_comm if ng else _pcall)(
        body, name="attn_full_fwd_gather" if ng else "attn_full_fwd", grid=grid,
        in_specs=[qs, pl.BlockSpec((None, T, HD), lambda h, i: (h * HB // G, 0, 0)),
                  pl.BlockSpec((None, nk, V_ROWS, tk), lambda h, i: (h * HB // G, 0, 0, 0))] + [HBM] * ng,
        out_specs=[qs, pl.BlockSpec((HB, None, 1, t), lambda h, i: (h, i, 0, 0))] + [HBM] * ng,
        out_shape=[_sds((Hq, nq, HD, t), BF), _sds((Hq, nq, 1, t), F32)]
        + [_sds((8,) + x.shape, x.dtype) for x in gather],
        scratch_shapes=[pltpu.VMEM((V_ROWS, t), F32)] * HB + (_gather_scratch(ng) if ng else []),
        compiler_params=_cp("arbitrary", "arbitrary") if ng else _cp("parallel", "parallel"),
    )(qT, k, vT1, *gather)
    return res[0], res[1], list(res[2:])


def _attn_delta(doT, oT):
    Hq, nq, _, t = doT.shape

    def body(do_ref, o_ref, dl_ref):
        dl_ref[...] = jnp.sum(do_ref[...].astype(F32) * o_ref[...].astype(F32), axis=1, keepdims=True)

    qs = pl.BlockSpec((None, nq, HD, t), lambda h: (h, 0, 0, 0))
    rs = pl.BlockSpec((None, nq, 1, t), lambda h: (h, 0, 0, 0))
    return _pcall(body, name="attn_delta", grid=(Hq,), in_specs=[qs, qs], out_specs=rs,
                  out_shape=_sds((Hq, nq, 1, t), F32), compiler_params=_cp("parallel"))(doT, oT)


def _attn_full_bwd(qT, k, kT, v, doT, lse, delta, scatter=()):
    Hq, nq, _, t = qT.shape
    Hk, T, _ = k.shape
    G = Hq // Hk
    nkv = T // t
    ns = len(scatter)

    def body(qT_ref, doT_ref, lse_ref, dl_ref, k_ref, kT_ref, v_ref, *rest):
        g_refs, (dq_ref, dk_ref, dv_ref), rest = rest[:ns], rest[ns:ns + 3], rest[ns + 3:]
        land_refs, (dq_acc, dk_acc, dv_acc), sems = rest[:ns], rest[ns:ns + 3], rest[ns + 3:]
        if ns:
            _phased(_scatter_phases(g_refs, land_refs, *sems), (Hq, nkv), (0.0, 1.0))
        j = pl.program_id(1)

        @pl.when(j == 0)
        def _():
            dq_acc[...] = jnp.zeros_like(dq_acc)

        kv, kTv, vv = k_ref[...], kT_ref[...], v_ref[...]
        dk_acc[...] = jnp.zeros((t, HD), F32)
        dv_acc[...] = jnp.zeros((t, HD), F32)

        def step(i, carry):
            qT, doT = qT_ref[i], doT_ref[i]
            pT = jnp.exp(jnp.dot(kv, qT, preferred_element_type=F32) - lse_ref[i])
            dv_acc[...] += _nt(pT.astype(BF), doT)
            dsT = (pT * (jnp.dot(vv, doT, preferred_element_type=F32) - dl_ref[i])).astype(BF)
            dk_acc[...] += _nt(dsT, qT)
            dq_acc[i] += jnp.dot(kTv, dsT, preferred_element_type=F32)
            return carry

        lax.fori_loop(0, nq, step, 0)
        dk_ref[...] = dk_acc[...].astype(dk_ref.dtype)
        dv_ref[...] = dv_acc[...].astype(dv_ref.dtype)

        @pl.when(j == nkv - 1)
        def _():
            dq_ref[...] = dq_acc[...].astype(dq_ref.dtype)

    chk = pl.BlockSpec((None, nq, HD, t), lambda h, j: (h, 0, 0, 0))
    row = pl.BlockSpec((None, nq, 1, t), lambda h, j: (h, 0, 0, 0))
    kvs = pl.BlockSpec((None, t, HD), lambda h, j: (h // G, j, 0))
    out = pl.BlockSpec((None, t, HD), lambda h, j: (h, j, 0))
    res = (_pcall_comm if ns else _pcall)(
        body, name="attn_full_bwd_scatter" if ns else "attn_full_bwd", grid=(Hq, nkv),
        in_specs=[chk, chk, row, row, kvs, pl.BlockSpec((None, HD, t), lambda h, j: (h // G, 0, j)), kvs] + [HBM] * ns,
        out_specs=[chk, out, out] + [HBM] * ns,
        out_shape=[_sds((Hq, nq, HD, t), BF), _sds((Hq, T, HD), BF), _sds((Hq, T, HD), BF)]
        + [_sds((7,) + g.shape[1:], g.dtype) for g in scatter],
        scratch_shapes=[pltpu.VMEM((nq, HD, t), F32), pltpu.VMEM((t, HD), F32), pltpu.VMEM((t, HD), F32)]
        + (_scatter_scratch(ns) if ns else []),
        compiler_params=_cp("arbitrary" if ns else "parallel", "arbitrary"),
    )(qT, doT, lse, delta, k, kT, v, *scatter)
    return res[0], res[1], res[2], list(res[3:])


DIL_Q = 128
DIL_K = 256
DIL_R = 64


def _vh(x, d):
    T = x.shape[0]
    return x.reshape(T // d, d, 4, HD).transpose(1, 2, 0, 3).reshape(4 * d, T // d, HD)


def _vh_chunks(x, d):
    y = _vh(x, d)
    return y.reshape(y.shape[0], y.shape[1] // DIL_Q, DIL_Q, HD).transpose(0, 1, 3, 2)


def _chunks_to_dims(c, d):
    _, nq, R, _ = c.shape
    return c.reshape(d, 4, nq, R, DIL_Q).transpose(1, 3, 2, 4, 0).reshape(4, R, nq * DIL_Q * d)


def _dims_to_chunks(x, d):
    _, R, T = x.shape
    nq = T // (DIL_Q * d)
    return x.reshape(4, R, nq, DIL_Q, d).transpose(4, 0, 2, 1, 3).reshape(4 * d, nq, R, DIL_Q)


def _dil_window(i, L):
    start = pl.multiple_of(jnp.clip(i * DIL_Q - DIL_R, 0, L - DIL_K), DIL_R)
    kk = start + lax.broadcasted_iota(jnp.int32, (DIL_K, 1), 0)
    qq = i * DIL_Q + lax.broadcasted_iota(jnp.int32, (1, DIL_Q), 1)
    return start, jnp.abs(kk - qq) <= DIL_R


def _tn(a, b):
    return lax.dot_general(a, b, (((0,), (0,)), ((), ())), preferred_element_type=F32)


def _dil_fwd(qT, k, v):
    V, nq, _, _ = qT.shape
    L = k.shape[1]
    assert L >= DIL_K

    unroll = 4 if nq % 4 == 0 else 1

    def body(q_ref, k_ref, v_ref, o_ref, lse_ref):
        def tile(i):
            start, mask = _dil_window(i, L)
            win = pl.ds(start, DIL_K)
            sT = jnp.where(mask, jnp.dot(k_ref[win, :], q_ref[i], preferred_element_type=F32), NEG)
            m = jnp.max(sT, axis=0, keepdims=True)
            pT = jnp.exp(sT - m).astype(BF)
            acc = _tn(v_ref[win, :], pT)
            l = jnp.max(acc[HD:HD + 8], axis=0, keepdims=True)
            o_ref[i] = acc[:HD] / l
            lse_ref[i] = m + jnp.log(l)

        def tiles(ii, carry):
            for u in range(unroll):
                tile(ii * unroll + u)
            return carry

        lax.fori_loop(0, nq // unroll, tiles, 0)

    chk = lambda r, dt: (pl.BlockSpec((None, nq, r, DIL_Q), lambda h: (h, 0, 0, 0)), _sds((V, nq, r, DIL_Q), dt))
    tok = lambda w: pl.BlockSpec((None, L, w), lambda h: (h, 0, 0))
    (o_spec, o_shape), (l_spec, l_shape) = chk(HD, F32), chk(1, F32)
    return _pcall(
        body, name=f"dil_fwd_{V // 4}", grid=(V,),
        in_specs=[chk(HD, BF)[0], tok(HD), tok(2 * HD)],
        out_specs=[o_spec, l_spec], out_shape=[o_shape, l_shape],
        compiler_params=_cp("parallel"),
    )(qT, k, v)


def _dil_merge(os_, lses):
    _, _, T = os_[0].shape
    tt = min(1024, T)
    n = len(os_)

    def body(*refs):
        o_refs, l_refs, (o_out, l_out) = refs[:n], refs[n:2 * n], refs[2 * n:]
        m = l_refs[0][...]
        for r in l_refs[1:]:
            m = jnp.maximum(m, r[...])
        ws = [jnp.exp(r[...] - m) for r in l_refs]
        tot = ws[0]
        for w_ in ws[1:]:
            tot = tot + w_
        acc = ws[0] * o_refs[0][...]
        for w_, o in zip(ws[1:], o_refs[1:]):
            acc = acc + w_ * o[...]
        o_out[...] = (acc / tot).astype(o_out.dtype)
        l_out[...] = m + jnp.log(tot)

    os_spec = pl.BlockSpec((None, HD, tt), lambda h, i: (h, 0, i))
    ls_spec = pl.BlockSpec((None, 1, tt), lambda h, i: (h, 0, i))
    return _pcall(
        body, name="dil_merge", grid=(4, T // tt),
        in_specs=[os_spec] * n + [ls_spec] * n, out_specs=[os_spec, ls_spec],
        out_shape=[_sds((4, HD, T), BF), _sds((4, 1, T), F32)],
        compiler_params=_cp("parallel", "parallel"),
    )(*os_, *lses)


def _dims_delta(doT, oT):
    _, _, T = doT.shape
    tt = min(2048, T)

    def body(do_ref, o_ref, dl_ref):
        dl_ref[...] = jnp.sum(do_ref[...].astype(F32) * o_ref[...].astype(F32), axis=0, keepdims=True)

    spec = pl.BlockSpec((None, HD, tt), lambda h, i: (h, 0, i))
    return _pcall(body, name="dims_delta", grid=(4, T // tt), in_specs=[spec, spec],
                  out_specs=pl.BlockSpec((None, 1, tt), lambda h, i: (h, 0, i)),
                  out_shape=_sds((4, 1, T), F32), compiler_params=_cp("parallel", "parallel"))(doT, oT)


def _dil_bwd(qT, k, v, doT, lse, delta):
    V, nq, _, _ = qT.shape
    L = k.shape[1]
    unroll = 2 if nq % 2 == 0 else 1

    def body(qT_ref, k_ref, v_ref, doT_ref, lse_ref, dl_ref, out_ref, dk_acc, dv_acc):
        dk_acc[...] = jnp.zeros_like(dk_acc)
        dv_acc[...] = jnp.zeros_like(dv_acc)

        def tile(i):
            start, mask = _dil_window(i, L)
            win = pl.ds(start, DIL_K)
            kw, qT, doT = k_ref[win, :], qT_ref[i], doT_ref[i]
            sT = jnp.where(mask, jnp.dot(kw, qT, preferred_element_type=F32), NEG)
            pT = jnp.exp(sT - lse_ref[i])
            dv = _nt(pT.astype(BF), doT)
            dsT = (pT * (jnp.dot(v_ref[win, :], doT, preferred_element_type=F32) - dl_ref[i])).astype(BF)
            dk = _nt(dsT, qT)
            out_ref[0, pl.ds(pl.multiple_of(i * DIL_Q, DIL_Q), DIL_Q), :] = _tn(dsT, kw).astype(out_ref.dtype)
            return win, dk, dv

        def tiles(ii, carry):
            done = [tile(ii * unroll + u) for u in range(unroll)]
            for win, dk, dv in done:
                dk_acc[win, :] += dk
                dv_acc[win, :] += dv
            return carry

        lax.fori_loop(0, nq // unroll, tiles, 0)
        out_ref[1] = dk_acc[...].astype(out_ref.dtype)
        out_ref[2] = dv_acc[...].astype(out_ref.dtype)

    chk = lambda r: pl.BlockSpec((None, nq, r, DIL_Q), lambda h: (h, 0, 0, 0))
    tok = pl.BlockSpec((None, L, HD), lambda h: (h, 0, 0))
    return _pcall(
        body, name=f"dil_bwd_{V // 4}", grid=(V,),
        in_specs=[chk(HD), tok, tok, chk(HD), chk(1), chk(1)],
        out_specs=pl.BlockSpec((None, 3, L, HD), lambda h: (h, 0, 0, 0)),
        out_shape=_sds((V, 3, L, HD), BF),
        scratch_shapes=[pltpu.VMEM((L, HD), F32)] * 2,
        compiler_params=_cp("parallel"),
    )(qT, k, v, doT, lse, delta)


def _dilated_fwd(qa, ka, va):
    outs, lses, saved = [], [], []
    for _, d in A_CONFIGS:
        qT, k, v = _vh_chunks(qa, d), _vh(ka, d), _vh(va, d)
        oT, lse = _dil_fwd(qT, k, jnp.concatenate([v, jnp.ones_like(v)], axis=2))
        outs.append(_chunks_to_dims(oT, d))
        lses.append(_chunks_to_dims(lse, d))
        saved.append((qT, k, v))
    o, lse = _dil_merge(outs, lses)
    return o, lse, saved


def _dilated_bwd(do_t, o, lse, saved):
    T = do_t.shape[0]
    doT = do_t.reshape(T, 4, HD).transpose(1, 2, 0)
    delta = _dims_delta(doT, o)
    total = None
    for (_, d), (qT, k, v) in zip(A_CONFIGS, saved):
        g = _dil_bwd(qT, k, v, _dims_to_chunks(doT, d), _dims_to_chunks(lse, d), _dims_to_chunks(delta, d))
        g = g.reshape(d, 4, 3, T // d, HD).transpose(3, 0, 2, 1, 4).reshape(T, 3 * A_W).astype(F32)
        total = g if total is None else total + g
    return total.astype(do_t.dtype)


NBR_Q = 128
NBR_ROWS = 10
NBR_K = NBR_ROWS * GRID_W


def _nbr_class_tiles(nq):
    return [2, 0, 1, nq - 2, nq - 1]


def _nbr_geometry(T):
    rows, nq = T // GRID_W, T // NBR_Q
    assert rows >= NBR_ROWS + 4 and nq >= 5
    kr, kc = np.divmod(np.arange(NBR_K), GRID_W)
    qr, qc = np.divmod(np.arange(NBR_Q), GRID_W)
    c0 = np.clip(qc - 8, 0, GRID_W - 16)
    col_ok = (kc[:, None] >= c0[None, :]) & (kc[:, None] < c0[None, :] + 16)
    drs, valids = [], []
    for i in _nbr_class_tiles(nq):
        start_row = int(np.clip(2 * i - 4, 0, rows - NBR_ROWS))
        r = 2 * i + qr
        r0 = np.clip(r - 4, 0, rows - 8)
        rk = start_row + kr
        row_ok = (rk[:, None] >= r0[None, :]) & (rk[:, None] < r0[None, :] + 8)
        valids.append(row_ok & col_ok)
        dr = start_row + np.arange(NBR_ROWS)[:, None] - (2 * i + np.arange(2)[None, :]) + 7
        drs.append(np.where((dr >= 0) & (dr <= 14), dr, -1))
    return np.stack(drs), np.stack(valids)


def _nbr_fold_matrices(T):
    dr, _ = _nbr_geometry(T)
    e1 = np.zeros((GRID_W * GRID_W, 128), np.float32)
    kc, qc = np.meshgrid(np.arange(GRID_W), np.arange(GRID_W), indexing="ij")
    dc = (kc - qc + 15).reshape(-1)
    keep = (dc >= 0) & (dc <= 30)
    e1[np.arange(GRID_W * GRID_W)[keep], dc[keep]] = 1.0
    n = 5 * NBR_ROWS * 2
    e2 = np.zeros((64, 4 * n), np.float32)
    for h in range(4):
        for j, d in enumerate(dr.reshape(-1)):
            if d >= 0:
                e2[h * 16 + d, h * n + j] = 1.0
    return jnp.asarray(e1), jnp.asarray(e2)


def _nbr_bias_blocks(rpb, T):
    e1, e2 = _nbr_fold_matrices(T)
    _, valid = _nbr_geometry(T)
    padded = jnp.pad(rpb, ((0, 0), (0, 1), (0, 128 - rpb.shape[2]))).reshape(64, 128)

    def body(e2t_ref, rpb_ref, e1t_ref, out_ref):
        picked = jnp.dot(e2t_ref[...], rpb_ref[...], precision=HI, preferred_element_type=F32)
        out_ref[...] = jnp.dot(picked, e1t_ref[...], precision=HI, preferred_element_type=F32)

    sub = _pcall(body, name="rpb_expand", out_shape=_sds((e2.shape[1], GRID_W * GRID_W), F32),
                 compiler_params=pltpu.CompilerParams(vmem_limit_bytes=VMEM_LIMIT))(e2.T, padded, e1.T)
    blocks = sub.reshape(4, 5, NBR_ROWS, 2, GRID_W, GRID_W).transpose(0, 1, 2, 4, 3, 5).reshape(4, 5, NBR_K, NBR_Q)
    return jnp.where(valid[None], blocks, NEG)


def _nbr_rpb_grad(dbias, T):
    e1, e2 = _nbr_fold_matrices(T)
    sub = dbias.reshape(4, 5, NBR_ROWS, GRID_W, 2, GRID_W).transpose(0, 1, 2, 4, 3, 5).reshape(-1, GRID_W * GRID_W)

    def body(e2_ref, sub_ref, e1_ref, out_ref):
        diag = jnp.dot(sub_ref[...], e1_ref[...], precision=HI, preferred_element_type=F32)
        out_ref[...] = jnp.dot(e2_ref[...], diag, precision=HI, preferred_element_type=F32)

    out = _pcall(body, name="rpb_fold", out_shape=_sds((64, 128), F32),
                 compiler_params=pltpu.CompilerParams(vmem_limit_bytes=VMEM_LIMIT))(e2, sub, e1)
    return out.reshape(4, 16, 128)[:, :15, :31]


def _nbr_tile(i, nq, T):
    start = pl.multiple_of(jnp.clip(i * NBR_Q - 4 * GRID_W, 0, T - NBR_K), NBR_Q)
    cls = jnp.where(i == 0, 1, jnp.where(i == 1, 2, jnp.where(i == nq - 2, 3, jnp.where(i == nq - 1, 4, 0))))
    return start, cls


def _nbr_fwd(qT, k, v, bias):
    H, nq, _, _ = qT.shape
    T = k.shape[1]
    unroll = 2 if nq % 2 == 0 else 1

    def body(q_ref, k_ref, v_ref, b_ref, o_ref, lse_ref):
        def tile(i):
            start, cls = _nbr_tile(i, nq, T)
            win = pl.ds(start, NBR_K)
            sT = jnp.dot(k_ref[win, :], q_ref[i], preferred_element_type=F32) + b_ref[cls]
            m = jnp.max(sT, axis=0, keepdims=True)
            pT = jnp.exp(sT - m).astype(BF)
            acc = _tn(v_ref[win, :], pT)
            l = jnp.max(acc[HD:HD + 8], axis=0, keepdims=True)
            o_ref[i] = (acc[:HD] / l).astype(o_ref.dtype)
            lse_ref[i] = m + jnp.log(l)

        def tiles(ii, carry):
            for u in range(unroll):
                tile(ii * unroll + u)
            return carry

        lax.fori_loop(0, nq // unroll, tiles, 0)

    chk = lambda r: pl.BlockSpec((None, nq, r, NBR_Q), lambda h: (h, 0, 0, 0))
    tok = lambda w: pl.BlockSpec((None, T, w), lambda h: (h, 0, 0))
    return _pcall(
        body, name="nbr_fwd", grid=(H,),
        in_specs=[chk(HD), tok(HD), tok(2 * HD), pl.BlockSpec((None, 5, NBR_K, NBR_Q), lambda h: (h, 0, 0, 0))],
        out_specs=[chk(HD), chk(1)],
        out_shape=[_sds((H, nq, HD, NBR_Q), BF), _sds((H, nq, 1, NBR_Q), F32)],
        compiler_params=_cp("parallel"),
    )(qT, k, v, bias)


def _nbr_bwd(qT, k, v, doT, lse, delta, bias):
    H, nq, _, _ = qT.shape
    T = k.shape[1]

    def body(qT_ref, k_ref, v_ref, doT_ref, lse_ref, dl_ref, b_ref, dq_ref, dk_ref, dv_ref, db_ref, dk_acc, dv_acc):
        dk_acc[...] = jnp.zeros_like(dk_acc)
        dv_acc[...] = jnp.zeros_like(dv_acc)
        db_ref[...] = jnp.zeros_like(db_ref)

        def tile(i, carry):
            start, cls = _nbr_tile(i, nq, T)
            win = pl.ds(start, NBR_K)
            kw, qT, doT = k_ref[win, :], qT_ref[i], doT_ref[i]
            sT = jnp.dot(kw, qT, preferred_element_type=F32) + b_ref[cls]
            pT = jnp.exp(sT - lse_ref[i])
            dv_acc[win, :] += _nt(pT.astype(BF), doT)
            ds = pT * (jnp.dot(v_ref[win, :], doT, preferred_element_type=F32) - dl_ref[i])
            db_ref[cls] += ds
            dsT = ds.astype(BF)
            dk_acc[win, :] += _nt(dsT, qT)
            dq_ref[i] = _tn(kw, dsT).astype(dq_ref.dtype)
            return carry

        lax.fori_loop(0, nq, tile, 0)
        dk_ref[...] = dk_acc[...].astype(dk_ref.dtype)
        dv_ref[...] = dv_acc[...].astype(dv_ref.dtype)

    chk = lambda r: pl.BlockSpec((None, nq, r, NBR_Q), lambda h: (h, 0, 0, 0))
    tok = pl.BlockSpec((None, T, HD), lambda h: (h, 0, 0))
    bsp = pl.BlockSpec((None, 5, NBR_K, NBR_Q), lambda h: (h, 0, 0, 0))
    return _pcall(
        body, name="nbr_bwd", grid=(H,),
        in_specs=[chk(HD), tok, tok, chk(HD), chk(1), chk(1), bsp],
        out_specs=[chk(HD), tok, tok, bsp],
        out_shape=[_sds((H, nq, HD, NBR_Q), BF), _sds((H, T, HD), BF), _sds((H, T, HD), BF),
                   _sds((H, 5, NBR_K, NBR_Q), F32)],
        scratch_shapes=[pltpu.VMEM((T, HD), F32)] * 2,
        compiler_params=_cp("parallel"),
    )(qT, k, v, doT, lse, delta, bias)


def _sigmoid(z):
    return 1.0 / (1.0 + jnp.exp(-z))


def _merge_fwd(oa, ob, oc, hg, bg, wa, wb, wc):
    T = oa.shape[0]
    tt = min(512, T)

    def body(oa_ref, ob_ref, oc_ref, hg_ref, bg_ref, wa_ref, wb_ref, wc_ref, out_ref):
        acc = None
        for k, (o_ref, w_ref) in enumerate(((oa_ref, wa_ref), (ob_ref, wb_ref), (oc_ref, wc_ref))):
            y = jnp.dot(o_ref[...], w_ref[...], preferred_element_type=F32)
            g = _sigmoid(hg_ref[:, D * k:D * (k + 1)] + bg_ref[:, D * k:D * (k + 1)])
            acc = g * y if acc is None else acc + g * y
        out_ref[...] = acc.astype(out_ref.dtype)

    row = lambda w: pl.BlockSpec((tt, w), lambda i: (i, 0))
    const = lambda a: pl.BlockSpec(a.shape, lambda i: (0, 0))
    return _pcall(
        body, name="merge_fwd", grid=(T // tt,),
        in_specs=[row(A_W), row(BQ_W), row(C_W), row(GATE), const(bg), const(wa), const(wb), const(wc)],
        out_specs=row(D), out_shape=_sds((T, D), BF),
        compiler_params=_cp("parallel"),
    )(oa, ob, oc, hg, bg, wa, wb, wc)


def _merge_bwd(dm, oa, ob, oc, hg, bg, wa, wb, wc):
    T = oa.shape[0]
    tt = min(256, T)

    def body(dm_ref, oa_ref, ob_ref, oc_ref, hg_ref, bg_ref, wa_ref, wb_ref, wc_ref,
             dya, dyb, dyc, doa, dob, doc, dhg, dbg):
        @pl.when(pl.program_id(0) == 0)
        def _():
            dbg[...] = jnp.zeros_like(dbg)

        dmv = dm_ref[...]
        for k, (o_ref, w_ref, dy_ref, do_ref) in enumerate(
                ((oa_ref, wa_ref, dya, doa), (ob_ref, wb_ref, dyb, dob), (oc_ref, wc_ref, dyc, doc))):
            sl = slice(D * k, D * (k + 1))
            y = jnp.dot(o_ref[...], w_ref[...], preferred_element_type=F32)
            g = _sigmoid(hg_ref[:, sl] + bg_ref[:, sl])
            dy = (dmv * g).astype(BF)
            dy_ref[...] = dy
            do_ref[...] = _nt(dy, w_ref[...]).astype(do_ref.dtype)
            dz = dmv * y * (g * (1.0 - g))
            dhg[:, sl] = dz.astype(dhg.dtype)
            dbg[:, sl] += jnp.sum(dz, axis=0, keepdims=True)

    row = lambda w: pl.BlockSpec((tt, w), lambda i: (i, 0))
    const = lambda a: pl.BlockSpec(a.shape, lambda i: (0, 0))
    return _pcall(
        body, name="merge_bwd", grid=(T // tt,),
        in_specs=[row(D), row(A_W), row(BQ_W), row(C_W), row(GATE), const(bg), const(wa), const(wb), const(wc)],
        out_specs=[row(D)] * 3 + [row(A_W), row(BQ_W), row(C_W), row(GATE),
                                  pl.BlockSpec((1, GATE), lambda i: (0, 0))],
        out_shape=[_sds((T, D), BF)] * 3 + [_sds((T, A_W), BF), _sds((T, BQ_W), BF), _sds((T, C_W), BF),
                                            _sds((T, GATE), BF), _sds((1, GATE), F32)],
        compiler_params=_cp("arbitrary"),
    )(dm, oa, ob, oc, hg, bg, wa, wb, wc)


def _lin_ln(a, w, res, g, b):
    T, K = a.shape
    tt = min(256, T)

    def body(a_ref, w_ref, res_ref, g_ref, b_ref, y_ref, yb_ref, xh_ref, rs_ref):
        u = ALPHA * res_ref[...] + jnp.dot(a_ref[...], w_ref[...], preferred_element_type=F32)
        mu = jnp.mean(u, axis=-1, keepdims=True)
        c = u - mu
        r = lax.rsqrt(jnp.mean(c * c, axis=-1, keepdims=True) + LN_EPS)
        xh = c * r
        y = xh * g_ref[...] + b_ref[...]
        y_ref[...] = y
        yb_ref[...] = y.astype(BF)
        xh_ref[...] = xh
        rs_ref[...] = r

    row = lambda w_: pl.BlockSpec((tt, w_), lambda i: (i, 0))
    const = lambda s: pl.BlockSpec(s, lambda i: (0, 0))
    return _pcall(
        body, name="lin_ln", grid=(T // tt,),
        in_specs=[row(K), const((K, D)), row(D), const((1, D)), const((1, D))],
        out_specs=[row(D), row(D), row(D), row(1)],
        out_shape=[_sds((T, D), F32), _sds((T, D), BF), _sds((T, D), F32), _sds((T, 1), F32)],
        compiler_params=_cp("parallel"),
    )(a, w, res, g, b)


def _ln_bwd(dy, xh, rs, g):
    T = dy.shape[0]
    tt = min(512, T)

    def body(dy_ref, xh_ref, rs_ref, g_ref, du_ref, dub_ref, dg_ref, db_ref):
        @pl.when(pl.program_id(0) == 0)
        def _():
            dg_ref[...] = jnp.zeros_like(dg_ref)
            db_ref[...] = jnp.zeros_like(db_ref)

        dyv, xhv = dy_ref[...], xh_ref[...]
        dg_ref[...] += jnp.sum(dyv * xhv, axis=0, keepdims=True)
        db_ref[...] += jnp.sum(dyv, axis=0, keepdims=True)
        dxh = dyv * g_ref[...]
        m1 = jnp.mean(dxh, axis=-1, keepdims=True)
        m2 = jnp.mean(dxh * xhv, axis=-1, keepdims=True)
        du = rs_ref[...] * (dxh - m1 - xhv * m2)
        du_ref[...] = du
        dub_ref[...] = du.astype(BF)

    row = lambda w_: pl.BlockSpec((tt, w_), lambda i: (i, 0))
    const = lambda s: pl.BlockSpec(s, lambda i: (0, 0))
    return _pcall(
        body, name="ln_bwd", grid=(T // tt,),
        in_specs=[row(D), row(D), row(1), const((1, D))],
        out_specs=[row(D), row(D), const((1, D)), const((1, D))],
        out_shape=[_sds((T, D), F32), _sds((T, D), BF), _sds((1, D), F32), _sds((1, D), F32)],
        compiler_params=_cp("arbitrary"),
    )(dy, xh, rs, g)


def _loss_grad(y, tgt):
    T = y.shape[0]
    tt = min(512, T)

    def body(y_ref, t_ref, dy_ref, sq_ref):
        @pl.when(pl.program_id(0) == 0)
        def _():
            sq_ref[...] = jnp.zeros_like(sq_ref)

        e = y_ref[...] - t_ref[...]
        dy_ref[...] = e * (1.0 / D)
        sq_ref[...] += jnp.sum(e * e, axis=0, keepdims=True)

    row = pl.BlockSpec((tt, D), lambda i: (i, 0))
    return _pcall(
        body, name="loss_grad", grid=(T // tt,),
        in_specs=[row, row], out_specs=[row, pl.BlockSpec((1, D), lambda i: (0, 0))],
        out_shape=[_sds((T, D), F32), _sds((1, D), F32)],
        compiler_params=_cp("arbitrary"),
    )(y, tgt)


def _position():
    return lax.axis_index("x"), lax.axis_index("y"), lax.axis_index("c")


HBM = pl.BlockSpec(memory_space=pl.ANY)


def _gather_phases(x_refs, out_refs, send, recv, loc):
    n = len(x_refs)
    x, y, c = _position()
    me, sib = (x, y, c), (x, y, 1 - c)
    chips = [(1 - x, y), (x, 1 - y), (1 - x, 1 - y)]

    def copy(a, k, block, to, src=None):
        px, py, pc = block
        dst = out_refs[a].at[4 * px + 2 * py + pc]
        return pltpu.make_async_remote_copy(
            src_ref=dst if src is None else src, dst_ref=dst,
            send_sem=send.at[a, k], recv_sem=recv.at[a, k], device_id=to, device_id_type=MESH)

    mine = [pltpu.make_async_copy(x_refs[a], out_refs[a].at[4 * x + 2 * y + c], loc.at[a]) for a in range(n)]
    first = []
    for a in range(n):
        first.append(copy(a, 0, me, sib, src=x_refs[a]))
        first += [copy(a, 1 + j, me, (*chip, c), src=x_refs[a]) for j, chip in enumerate(chips)]
    passed = [copy(a, 4 + j, (*chip, c), sib) for j, chip in enumerate(chips) for a in range(n)]

    def start():
        for cp in mine + first:
            cp.start()

    def forward():
        for j, chip in enumerate(chips):
            for a in range(n):
                copy(a, 1 + j, (*chip, c), me).wait_recv()
                copy(a, 4 + j, (*chip, c), sib).start()

    def finish():
        for a in range(n):
            copy(a, 0, sib, me).wait_recv()
            for j, chip in enumerate(chips):
                copy(a, 4 + j, (*chip, 1 - c), me).wait_recv()
        for cp in first + passed:
            cp.wait_send()
        for cp in mine:
            cp.wait()

    return start, forward, finish


def _gather_scratch(n):
    return [pltpu.SemaphoreType.DMA((n, 7)), pltpu.SemaphoreType.DMA((n, 7)), pltpu.SemaphoreType.DMA((n,))]


def _all_gather(xs, name):
    n = len(xs)

    def body(*refs):
        for phase in _gather_phases(refs[:n], refs[n:2 * n], *refs[2 * n:]):
            phase()

    return _pcall_comm(
        body, name=name,
        in_specs=[HBM] * n, out_specs=[HBM] * n,
        out_shape=[_sds((8,) + x.shape, x.dtype) for x in xs],
        scratch_shapes=_gather_scratch(n),
    )(*xs)


def _scatter_phases(g_refs, land_refs, send, recv):
    x, y, c = _position()
    peers = [(x, y, 1 - c), (1 - x, y, c), (x, 1 - y, c), (1 - x, 1 - y, c),
             (1 - x, y, 1 - c), (x, 1 - y, 1 - c), (1 - x, 1 - y, 1 - c)]
    copies = [pltpu.make_async_remote_copy(
        src_ref=g_refs[a].at[4 * px + 2 * py + pc], dst_ref=land_refs[a].at[k],
        send_sem=send.at[a, k], recv_sem=recv.at[a, k], device_id=(px, py, pc), device_id_type=MESH)
        for a in range(len(g_refs)) for k, (px, py, pc) in enumerate(peers)]

    def start():
        for cp in copies:
            cp.start()

    def finish():
        for cp in copies:
            cp.wait()

    return start, finish


def _scatter_scratch(n):
    return [pltpu.SemaphoreType.DMA((n, 7)), pltpu.SemaphoreType.DMA((n, 7))]


def _scatter(gs, name):
    n = len(gs)

    def body(*refs):
        for phase in _scatter_phases(refs[:n], refs[n:2 * n], *refs[2 * n:]):
            phase()

    return _pcall_comm(
        body, name=name,
        in_specs=[HBM] * n, out_specs=[HBM] * n,
        out_shape=[_sds((7,) + g.shape[1:], g.dtype) for g in gs],
        scratch_shapes=_scatter_scratch(n),
    )(*gs)


def _sum8(g, land, me):
    _, R, C = g.shape
    tr = min(512, R)

    def body(me_ref, g_ref, l_ref, out_ref):
        acc = g_ref[...].astype(F32)
        for k in range(7):
            acc = acc + l_ref[k].astype(F32)
        out_ref[...] = acc

    return _pcall(
        body, name="sum8",
        grid_spec=pltpu.PrefetchScalarGridSpec(
            num_scalar_prefetch=1, grid=(R // tr,),
            in_specs=[pl.BlockSpec((None, tr, C), lambda i, me_: (me_[0], i, 0)),
                      pl.BlockSpec((7, tr, C), lambda i, me_: (0, i, 0))],
            out_specs=pl.BlockSpec((tr, C), lambda i, me_: (i, 0))),
        out_shape=_sds((R, C), F32),
        compiler_params=_cp("parallel"),
    )(me, g, land)


def _adamw(g, w, m, v):
    R, C = w.shape
    tr = min(512, R)

    def body(g_ref, w_ref, m_ref, v_ref, d_out, m_out, v_out):
        d_out[...], m_out[...], v_out[...] = _adamw_math(w_ref[...], g_ref[...], m_ref[...], v_ref[...])

    blk = pl.BlockSpec((tr, C), lambda i: (i, 0))
    return _pcall(body, name="adamw", grid=(R // tr,), in_specs=[blk] * 4, out_specs=[blk] * 3,
                  out_shape=[_sds((R, C), F32)] * 3, compiler_params=_cp("parallel"))(g, w, m, v)


def _adamw_math(w, g, m, v):
    m = B1 * m + (1.0 - B1) * g
    v = B2 * v + (1.0 - B2) * (g * g)
    m_hat = m / (1.0 - B1 ** STEP)
    v_hat = v / (1.0 - B2 ** STEP)
    delta = -LR * (m_hat / (jnp.sqrt(v_hat) + EPS) + WD * w)
    return delta, m, v


def _small_sum_adamw(parts, w, m, v):
    _, R, C = parts.shape

    def body(p_ref, w_ref, m_ref, v_ref, g_out, d_out, m_out, v_out):
        g = p_ref[0]
        for k in range(1, 8):
            g = g + p_ref[k]
        d, mn, vn = _adamw_math(w_ref[...], g, m_ref[...], v_ref[...])
        g_out[...], d_out[...], m_out[...], v_out[...] = g, d, mn, vn

    return _pcall(body, name="small_sum_adamw", out_shape=[_sds((R, C), F32)] * 4,
                  compiler_params=pltpu.CompilerParams(vmem_limit_bytes=VMEM_LIMIT))(parts, w, m, v)


BIG = ("w_in", "w_branch_a", "w_branch_b", "w_branch_c", "w_out", "w_up", "w_down")
ROW_SHARDED = ("w_out", "w_down")
EARLY = ("w_down", "w_up", "w_out", "w_branch_a", "w_branch_b", "w_branch_c")
SMALL = ("b_gate", "q_norm_b", "k_norm_b", "rpb_c", "ln1_g", "ln1_b", "ln2_g", "ln2_b")
NAMES = ("w_in", "b_gate", "q_norm_b", "k_norm_b", "rpb_c", "w_branch_a", "w_branch_b", "w_branch_c",
         "w_out", "ln1_g", "ln1_b", "w_up", "w_down", "ln2_g", "ln2_b")


def _full_weight(blk, name):
    if name in ROW_SHARDED:
        return blk.reshape(-1, blk.shape[2])
    return blk.transpose(1, 0, 2).reshape(blk.shape[1], -1)


def _chunks(grad, name):
    if name in ROW_SHARDED:
        return grad.reshape(8, grad.shape[0] // 8, grad.shape[1])
    return grad.reshape(grad.shape[0], 8, grad.shape[1] // 8).transpose(1, 0, 2)


def _layer_fwd(x, xb, W, P, tabs, gm, gather=(), late=None):
    hq, = _mm(xb, W["w_qkv"], "nn", [F32], 1024, 768, 1024, name="in_qkv")
    hg, = _mm(xb, W["w_gate"], "nn", [F32], 1024, 1024, 1024, name="in_gate")
    tab_a, tab_b = tabs
    prepped = _prep_fwd(hq, tab_a, tab_b, P["qn"], P["kn"], gm)
    T = x.shape[0]
    tb = min(FULL_T, T)
    oa, lse_a, qa = _dilated_fwd(*prepped[0:3])
    qb_t, kb_t, vb_t = prepped[3:6]
    qb, kb, vb = _chunked_t(qb_t, tb), _to_heads(kb_t), _chunked_t(vb_t, min(FULL_TK, T))
    vb1 = jnp.concatenate([vb, jnp.ones(vb.shape[:2] + (V_ROWS - HD, vb.shape[3]), vb.dtype)], axis=2)
    ob, lse_b, gathered = _attn_full_fwd(qb, kb, vb1, gather)
    if late is not None:
        W = {**W, **late(gathered)}
    qc_t, kc_t, vc_t = prepped[6:9]
    qc, kc, vc = _chunked_t(qc_t, NBR_Q), _to_heads(kc_t), _to_heads(vc_t)
    bias_c = _nbr_bias_blocks(P["rpb"], T)
    oc, lse_c = _nbr_fwd(qc, kc, jnp.concatenate([vc, jnp.ones_like(vc)], axis=2), bias_c)
    oa_t, ob_t, oc_t = oa.transpose(2, 0, 1).reshape(T, A_W), _unchunk_t(ob), _unchunk_t(oc)
    qb = (qb, kb, kb.transpose(0, 2, 1), _to_heads(vb_t))
    ka = va = None
    merged = _merge_fwd(oa_t, ob_t, oc_t, hg, P["bg"], W["w_branch_a"], W["w_branch_b"], W["w_branch_c"])
    x1, x1b, xh1, rs1 = _lin_ln(merged, W["w_out"], x, P["ln1_g"], P["ln1_b"])

    def relu2(acc):
        r = jnp.maximum(acc, 0.0)
        return r * r, r

    f, r = _mm(x1b, W["w_up"], "nn", [BF, BF], 1024, 1024, 1024, epilogue=relu2, name="mlp_up")
    x2, x2b, xh2, rs2 = _lin_ln(f, W["w_down"], x1, P["ln2_g"], P["ln2_b"])
    saved = dict(xb=xb, hq=hq, hg=hg, qkv=(qa, ka, va, qb, kb, vb, qc, kc, vc), o=(oa, ob, oc),
                 lse=(lse_a, lse_b, lse_c), o_t=(oa_t, ob_t, oc_t), bias_c=bias_c, merged=merged,
                 xh1=xh1, rs1=rs1, x1b=x1b, f=f, r=r, xh2=xh2, rs2=rs2)
    return x2, x2b, saved, W


def _layer_bwd(dx2, S, W, P, tabs, gm, scatter=()):
    G = {}
    du2, du2b, G["ln2_g"], G["ln2_b"] = _ln_bwd(dx2, S["xh2"], S["rs2"], P["ln2_g"])
    G["w_down"], = _mm(S["f"], du2b, "tn", [F32], 1024, 1024, 1024, name="dw_down")
    da, = _mm(du2b, W["w_down"], "nt", [BF], 1024, 1024, 1024,
              epilogue=lambda acc, r: (acc * (2.0 * r.astype(F32)),), extras=(S["r"],), name="d_act")
    G["w_up"], = _mm(S["x1b"], da, "tn", [F32], 1024, 1024, 1024, name="dw_up")
    dx1, = _mm(da, W["w_up"], "nt", [F32], 1024, 1024, 1024,
               epilogue=lambda acc, d: (ALPHA * d + acc,), extras=(du2,), name="dx_mlp")
    du1, du1b, G["ln1_g"], G["ln1_b"] = _ln_bwd(dx1, S["xh1"], S["rs1"], P["ln1_g"])
    G["w_out"], = _mm(S["merged"], du1b, "tn", [F32], 1024, 1024, 1024, name="dw_out")
    dm, = _mm(du1b, W["w_out"], "nt", [F32], 1024, 1024, 1024, name="d_merged")
    oa_t, ob_t, oc_t = S["o_t"]
    dya, dyb, dyc, doa, dob, doc, dhg, G["b_gate"] = _merge_bwd(
        dm, oa_t, ob_t, oc_t, S["hg"], P["bg"], W["w_branch_a"], W["w_branch_b"], W["w_branch_c"])
    G["w_branch_a"], = _mm(oa_t, dya, "tn", [F32], 256, 1024, 1024, name="dw_branch_a")
    G["w_branch_b"], = _mm(ob_t, dyb, "tn", [F32], 512, 1024, 1024, name="dw_branch_b")
    G["w_branch_c"], = _mm(oc_t, dyc, "tn", [F32], 256, 1024, 1024, name="dw_branch_c")

    qa, ka, va, qb, kb, vb, qc, kc, vc = S["qkv"]
    oa, ob, oc = S["o"]
    lse_a, lse_b, lse_c = S["lse"]
    dqkv_a = _dilated_bwd(doa, oa, lse_a, qa)
    qT_b, k_b, kT_b, v_b = qb
    dobT = _chunked_t(dob, ob.shape[-1])
    early = {n: _chunks(G[n], n).astype(BF) for n in EARLY}
    dqbT, dkb8, dvb8, landed = _attn_full_bwd(qT_b, k_b, kT_b, v_b, dobT, lse_b, _attn_delta(dobT, ob),
                                              list(scatter) + [early[n] for n in EARLY])
    landed = (landed[:len(scatter)], dict(zip(EARLY, landed[len(scatter):])), early)
    group_sum = lambda t: t.reshape(k_b.shape[0], -1, t.shape[1], HD).sum(1)
    dqb, dkb, dvb = _unchunk_t(dqbT), _from_heads(group_sum(dkb8)), _from_heads(group_sum(dvb8))
    bias_c = S["bias_c"]
    docT = _chunked_t(doc, NBR_Q)
    dqcT, dkc, dvc, dbias_c = _nbr_bwd(qc, kc, vc, docT, lse_c, _attn_delta(docT, oc), bias_c)
    G["rpb_c"] = _nbr_rpb_grad(dbias_c, dx2.shape[0])

    tab_a, tab_b = tabs
    grads = [dqkv_a, dqb, dkb, dvb, _unchunk_t(dqcT), _from_heads(dkc), _from_heads(dvc)]
    dhq, dqn, dkn = _prep_bwd(S["hq"], grads, tab_a, tab_b, P["qn"], P["kn"], gm)
    G["q_norm_b"] = dqn.reshape(BQ_W // HD, HD).sum(0)
    G["k_norm_b"] = dkn.reshape(BKV_W // HD, HD).sum(0)
    dw_qkv, = _mm(S["xb"], dhq, "tn", [F32], 1024, 768, 1024, name="dw_qkv")
    dw_gate, = _mm(S["xb"], dhg, "tn", [F32], 1024, 1024, 1024, name="dw_gate")
    G["w_in"] = jnp.concatenate([dw_qkv, dw_gate], axis=1)
    dx_a, = _mm(dhq, W["w_qkv"], "nt", [F32], 1024, 1024, 768,
                epilogue=lambda acc, d: (ALPHA * d + acc,), extras=(du1,), name="dx_qkv")
    dx, = _mm(dhg, W["w_gate"], "nt", [F32], 1024, 1024, 1024,
              epilogue=lambda acc, d: (d + acc,), extras=(dx_a,), name="dx_gate")
    return dx, G, landed


def _pack_small(vals):
    flat = jnp.concatenate([vals[n].reshape(-1).astype(F32) for n in SMALL])
    pad = (-flat.shape[0]) % (8 * 128)
    return jnp.pad(flat, (0, pad)).reshape(-1, 128)


def _unpack_small(packed, like):
    flat, out, off = packed.reshape(-1), {}, 0
    for n in SMALL:
        size = math.prod(like[n].shape)
        out[n] = flat[off:off + size].reshape(like[n].shape)
        off += size
    return out


def kernel(x, w_in, b_gate, q_norm_b, k_norm_b, rpb_c, w_branch_a, w_branch_b, w_branch_c, w_out, ln1_g, ln1_b, w_up, w_down, ln2_g, ln2_b, loss_target, m_w_in, m_b_gate, m_q_norm_b, m_k_norm_b, m_rpb_c, m_w_branch_a, m_w_branch_b, m_w_branch_c, m_w_out, m_ln1_g, m_ln1_b, m_w_up, m_w_down, m_ln2_g, m_ln2_b, v_w_in, v_b_gate, v_q_norm_b, v_k_norm_b, v_rpb_c, v_w_branch_a, v_w_branch_b, v_w_branch_c, v_w_out, v_ln1_g, v_ln1_b, v_w_up, v_w_down, v_ln2_g, v_ln2_b):
    w = dict(w_in=w_in, b_gate=b_gate, q_norm_b=q_norm_b, k_norm_b=k_norm_b, rpb_c=rpb_c,
             w_branch_a=w_branch_a, w_branch_b=w_branch_b, w_branch_c=w_branch_c, w_out=w_out,
             ln1_g=ln1_g, ln1_b=ln1_b, w_up=w_up, w_down=w_down, ln2_g=ln2_g, ln2_b=ln2_b)
    m = dict(w_in=m_w_in, b_gate=m_b_gate, q_norm_b=m_q_norm_b, k_norm_b=m_k_norm_b, rpb_c=m_rpb_c,
             w_branch_a=m_w_branch_a, w_branch_b=m_w_branch_b, w_branch_c=m_w_branch_c, w_out=m_w_out,
             ln1_g=m_ln1_g, ln1_b=m_ln1_b, w_up=m_w_up, w_down=m_w_down, ln2_g=m_ln2_g, ln2_b=m_ln2_b)
    v = dict(w_in=v_w_in, b_gate=v_b_gate, q_norm_b=v_q_norm_b, k_norm_b=v_k_norm_b, rpb_c=v_rpb_c,
             w_branch_a=v_w_branch_a, w_branch_b=v_w_branch_b, w_branch_c=v_w_branch_c, w_out=v_w_out,
             ln1_g=v_ln1_g, ln1_b=v_ln1_b, w_up=v_w_up, w_down=v_w_down, ln2_g=v_ln2_g, ln2_b=v_ln2_b)
    T = x.shape[1]
    xc, yc, cc = _position()

    flat2 = lambda a: a.reshape(-1, a.shape[-1])
    shards = {n: w[n].astype(BF) for n in BIG}
    w_in0, = _all_gather([shards["w_in"][0]], "gather_w_in0")
    riding = [shards[n][0] for n in BIG[1:]] + [flat2(shards[n][1:]) for n in BIG]

    tabs = _rope_tables(T)
    gm = _group_mean_matrix()

    def in_proj(blk):
        w_in_l = _full_weight(blk, "w_in")
        return {"w_qkv": w_in_l[:, :QKV], "w_gate": w_in_l[:, QKV:]}

    Ws = [in_proj(w_in0)]

    def late(gathered):
        rest = [g.reshape(8, DEPTH - 1, -1, g.shape[2]) for g in gathered[len(BIG) - 1:]]
        for i in range(DEPTH - 1):
            W = {n: _full_weight(g[:, i], n) for n, g in zip(BIG[1:], rest[1:])}
            Ws.append({**W, **in_proj(rest[0][:, i])})
        return {n: _full_weight(g, n) for n, g in zip(BIG[1:], gathered)}

    Ps = [dict(qn=jnp.tile(q_norm_b[l][None], (1, 2)), kn=jnp.tile(k_norm_b[l][None], (1, 2)),
               rpb=rpb_c[l], bg=b_gate[l][None], ln1_g=ln1_g[l][None], ln1_b=ln1_b[l][None],
               ln2_g=ln2_g[l][None], ln2_b=ln2_b[l][None]) for l in range(DEPTH)]

    h = x[0]
    hb = h.astype(BF)
    saved = []
    for l in range(DEPTH):
        h, hb, S, Ws[l] = _layer_fwd(h, hb, Ws[l], Ps[l], tabs, gm, *((riding, late) if l == 0 else ()))
        saved.append(S)
    dy, sq = _loss_grad(h, loss_target[0])
    loss = lax.psum(0.5 / D * jnp.sum(sq), AXES)

    grads = [None] * DEPTH
    chunks, landed = [{} for _ in range(DEPTH)], [{} for _ in range(DEPTH)]
    for l in reversed(range(DEPTH)):
        above = [chunks[l + 1]["w_in"]] if l + 1 < DEPTH else []
        dy, grads[l], (arrived, landed_early, early) = _layer_bwd(dy, saved[l], Ws[l], Ps[l], tabs, gm, above)
        if above:
            landed[l + 1]["w_in"], = arrived
        chunks[l].update(early)
        landed[l].update(landed_early)
        chunks[l]["w_in"] = _chunks(grads[l]["w_in"], "w_in").astype(BF)
    landed[0]["w_in"], = _scatter([chunks[0]["w_in"]], "scatter_w_in0")
    grad_x = dy[None]
    me = (4 * xc + 2 * yc + cc).reshape(1).astype(jnp.int32)
    out_g, out_d, out_m, out_v = {}, {}, {}, {}
    for n in BIG:
        g = jnp.concatenate([_sum8(chunks[l][n], landed[l][n], me) for l in range(DEPTH)], axis=0)
        res = (g,) + tuple(_adamw(g, flat2(w[n]), flat2(m[n]), flat2(v[n])))
        out_g[n], out_d[n], out_m[n], out_v[n] = [t.reshape(w[n].shape) for t in res]

    part = _pack_small({n: jnp.stack([grads[l][n].reshape(w[n].shape[1:]) for l in range(DEPTH)]) for n in SMALL})
    parts, = _all_gather([part], "gather_small_grads")
    res = _small_sum_adamw(parts, _pack_small(w), _pack_small(m), _pack_small(v))
    for dst, packed in zip((out_g, out_d, out_m, out_v), res):
        dst.update(_unpack_small(packed, w))

    return (loss, grad_x, *[out_g[n] for n in NAMES], *[out_d[n] for n in NAMES],
            *[out_m[n] for n in NAMES], *[out_v[n] for n in NAMES])
```
